```python
import math
import jax, jax.numpy as jnp
from jax import lax
import numpy as np

D_MODEL = 1024
BATCH = 8
SEQ = 2048
DEPTH = 2

CTX_LEN = 256
GRID_W = 64
HEAD_DIM = 64
AXIS_DIM = HEAD_DIM // 2
ROPE_BASE = 10000.0
RMS_EPS = 1e-6
Q_BLOCK = 128
WINDOW = 128

A_HEADS = 4
A_QK_DIM = 2 * HEAD_DIM
A_V_DIM = 2 * HEAD_DIM
B_HEADS = 4
B_KV_HEADS = 2
B_GROUP = B_HEADS // B_KV_HEADS
C_HEADS = 4
C_KV_HEADS = 2
C_GROUP = C_HEADS // C_KV_HEADS

A_WIDTH = A_HEADS * A_V_DIM
B_WIDTH = B_HEADS * HEAD_DIM
C_WIDTH = C_HEADS * HEAD_DIM
MIX_WIDTH = A_WIDTH + B_WIDTH + C_WIDTH
Q_SIZES = (A_HEADS * A_QK_DIM, B_HEADS * HEAD_DIM, C_HEADS * HEAD_DIM)
KV_SIZES = (A_HEADS * A_QK_DIM, A_HEADS * A_V_DIM,
            B_KV_HEADS * HEAD_DIM, B_KV_HEADS * HEAD_DIM,
            C_KV_HEADS * HEAD_DIM, C_KV_HEADS * HEAD_DIM)
Q_COLS = sum(Q_SIZES)
KV_COLS = sum(KV_SIZES)
IN_COLS = Q_COLS + KV_COLS

N_EXPERTS = 16
N_GROUPS = 4
EXPERTS_PER_GROUP = N_EXPERTS // N_GROUPS
TOP_K = 2
D_EXPERT = 256

kernel_name = 'hybrid_dit_parallel_heads_moe'


def _offsets(sizes):
    out, acc = [], 0
    for s in sizes[:-1]:
        acc += s
        out.append(acc)
    return out


def rms_norm(x, g):
    xf = x.astype(jnp.float32)
    y = xf * lax.rsqrt(jnp.mean(xf * xf, axis=-1, keepdims=True) + RMS_EPS)
    return (y * g.astype(jnp.float32)).astype(x.dtype)


def rope_tables(seq_len):
    rows = seq_len // GRID_W
    row_pos = jnp.repeat(jnp.arange(rows, dtype=jnp.float32), GRID_W)
    col_pos = jnp.tile(jnp.arange(GRID_W, dtype=jnp.float32), rows)
    inv_freq = ROPE_BASE ** (-jnp.arange(0, AXIS_DIM, 2, dtype=jnp.float32) / AXIS_DIM)
    ang_r = row_pos[:, None] * inv_freq[None, :]
    ang_c = col_pos[:, None] * inv_freq[None, :]
    return (jnp.cos(ang_r), jnp.sin(ang_r), jnp.cos(ang_c), jnp.sin(ang_c))


def _rot(x, cos, sin):
    x1, x2 = jnp.split(x, 2, axis=-1)
    cos = cos.astype(x.dtype)
    sin = sin.astype(x.dtype)
    return jnp.concatenate([x1 * cos - x2 * sin, x1 * sin + x2 * cos], axis=-1)


def apply_rope_2d(x, rope):
    cos_r, sin_r, cos_c, sin_c = rope
    x_row, x_col = jnp.split(x, 2, axis=-1)
    return jnp.concatenate([_rot(x_row, cos_r, sin_r), _rot(x_col, cos_c, sin_c)], axis=-1)


def to_query_blocks(q):
    *lead, s, d = q.shape
    return jnp.moveaxis(q.reshape(*lead, s // Q_BLOCK, Q_BLOCK, d), -3, 0)


def from_query_blocks(o):
    o = jnp.moveaxis(o, 0, -3)
    *lead, nb, qb, d = o.shape
    return o.reshape(*lead, nb * qb, d)


def diff_attention(q, k, v, lam):
    kf = k.astype(jnp.float32)
    scale = HEAD_DIM ** -0.5

    def block(qb):
        s = jnp.einsum('bhiqd,bhikd->bhiqk', qb.astype(jnp.float32), kf) * scale
        p = jax.nn.softmax(s, axis=-1)
        w = p[:, :, 0] - lam * p[:, :, 1]
        return jnp.einsum('bhqk,bhkv->bhqv', w.astype(v.dtype), v)

    return from_query_blocks(lax.map(block, to_query_blocks(q)))


def gqa_attention(q, k, v, sink=None):
    kf = k.astype(jnp.float32)
    scale = HEAD_DIM ** -0.5

    def block(qb):
        s = jnp.einsum('bhgqd,bhkd->bhgqk', qb.astype(jnp.float32), kf) * scale
        if sink is not None:
            sk = jnp.broadcast_to(sink.astype(jnp.float32)[None, :, :, None, None], s.shape[:-1] + (1,))
            p = jax.nn.softmax(jnp.concatenate([s, sk], axis=-1), axis=-1)[..., :-1]
        else:
            p = jax.nn.softmax(s, axis=-1)
        return jnp.einsum('bhgqk,bhkd->bhgqd', p.astype(v.dtype), v)

    return from_query_blocks(lax.map(block, to_query_blocks(q)))


def window_attention(q, k, v, kc, vc, sink):
    B, Hkv, G, S, dh = q.shape
    nb = S // Q_BLOCK
    L = kc.shape[2]

    def band(t):
        tp = jnp.pad(t, ((0, 0), (0, 0), (WINDOW, WINDOW), (0, 0)))
        tp = tp.reshape(B, Hkv, nb + 2, Q_BLOCK, t.shape[-1])
        return jnp.concatenate([tp[:, :, :-2], tp[:, :, 1:-1], tp[:, :, 2:]], axis=3)

    kb, vb = band(k), band(v)
    qb = q.reshape(B, Hkv, G, nb, Q_BLOCK, dh).astype(jnp.float32)
    scale = HEAD_DIM ** -0.5
    s_band = jnp.einsum('bhgnqd,bhnkd->bhgnqk', qb, kb.astype(jnp.float32)) * scale
    blk = jnp.arange(nb)[:, None, None] * Q_BLOCK
    q_pos = blk + jnp.arange(Q_BLOCK)[None, :, None]
    k_pos = blk - WINDOW + jnp.arange(3 * Q_BLOCK)[None, None, :]
    valid = (jnp.abs(k_pos - q_pos) <= WINDOW) & (k_pos >= 0) & (k_pos < S)
    s_band = jnp.where(valid, s_band, -jnp.inf)
    s_ctx = jnp.einsum('bhgnqd,bhkd->bhgnqk', qb, kc.astype(jnp.float32)) * scale
    s_sink = jnp.broadcast_to(sink.astype(jnp.float32)[None, :, :, None, None, None], s_ctx.shape[:-1] + (1,))
    p = jax.nn.softmax(jnp.concatenate([s_ctx, s_band, s_sink], axis=-1), axis=-1)
    p_ctx = p[..., :L].astype(v.dtype)
    p_band = p[..., L:L + 3 * Q_BLOCK].astype(v.dtype)
    out = (jnp.einsum('bhgnqk,bhkd->bhgnqd', p_ctx, vc)
           + jnp.einsum('bhgnqk,bhnkd->bhgnqd', p_band, vb))
    return out.reshape(B, Hkv, G, S, dh)


def heads_a_qk(t):
    B, n, _ = t.shape
    return t.reshape(B, n, A_HEADS, 2, HEAD_DIM).transpose(0, 2, 3, 1, 4)


def heads_kv(t, heads, dim):
    B, n, _ = t.shape
    return t.reshape(B, n, heads, dim).transpose(0, 2, 1, 3)


def heads_gq(t, hkv, g):
    B, n, _ = t.shape
    return t.reshape(B, n, hkv, g, HEAD_DIM).transpose(0, 2, 3, 1, 4)


def merge_h(o):
    B, H, n, d = o.shape
    return o.transpose(0, 2, 1, 3).reshape(B, n, H * d)


def merge_gq(o):
    B, Hkv, G, n, d = o.shape
    return o.transpose(0, 3, 1, 2, 4).reshape(B, n, Hkv * G * d)


def split_kv(kv):
    kA, vA, kB, vB, kC, vC = jnp.split(kv, _offsets(KV_SIZES), axis=-1)
    return (heads_a_qk(kA), heads_kv(vA, A_HEADS, A_V_DIM),
            heads_kv(kB, B_KV_HEADS, HEAD_DIM), heads_kv(vB, B_KV_HEADS, HEAD_DIM),
            heads_kv(kC, C_KV_HEADS, HEAD_DIM), heads_kv(vC, C_KV_HEADS, HEAD_DIM))


def split_q(q):
    qA, qB, qC = jnp.split(q, _offsets(Q_SIZES), axis=-1)
    return heads_a_qk(qA), heads_gq(qB, B_KV_HEADS, B_GROUP), heads_gq(qC, C_KV_HEADS, C_GROUP)


def token_mixers(hx, hc, w_in, lam_q1, lam_k1, lam_q2, lam_k2, subln_g, q_norm_g, k_norm_g,
                 sink, lambda_init, rope, with_ctx):
    px = hx @ w_in
    qAx, qBx, qCx = split_q(px[..., :Q_COLS])
    kAx, vAx, kBx, vBx, kCx, vCx = split_kv(px[..., Q_COLS:])
    kAc, vAc, kBc, vBc, kCc, vCc = split_kv(hc @ w_in[:, Q_COLS:])

    f32 = jnp.float32
    lam = (jnp.exp(jnp.sum(lam_q1.astype(f32) * lam_k1.astype(f32)))
           - jnp.exp(jnp.sum(lam_q2.astype(f32) * lam_k2.astype(f32))) + lambda_init)
    sink_hg = sink.reshape(C_KV_HEADS, C_GROUP)
    kBc = rms_norm(kBc, k_norm_g)

    oA = diff_attention(apply_rope_2d(qAx, rope),
                        jnp.concatenate([kAc, apply_rope_2d(kAx, rope)], axis=-2),
                        jnp.concatenate([vAc, vAx], axis=-2), lam)
    oA = rms_norm(oA, subln_g) * (1.0 - lambda_init)
    oB = gqa_attention(apply_rope_2d(rms_norm(qBx, q_norm_g), rope),
                       jnp.concatenate([kBc, apply_rope_2d(rms_norm(kBx, k_norm_g), rope)], axis=-2),
                       jnp.concatenate([vBc, vBx], axis=-2))
    oC = window_attention(apply_rope_2d(qCx, rope), apply_rope_2d(kCx, rope), vCx, kCc, vCc, sink_hg)
    mix_x = jnp.concatenate([merge_h(oA), merge_gq(oB), merge_gq(oC)], axis=-1)

    if not with_ctx:
        return mix_x, None
    qAc, qBc, qCc = split_q(hc @ w_in[:, :Q_COLS])
    oAc = rms_norm(diff_attention(qAc, kAc, vAc, lam), subln_g) * (1.0 - lambda_init)
    oBc = gqa_attention(rms_norm(qBc, q_norm_g), kBc, vBc)
    oCc = gqa_attention(qCc, kCc, vCc, sink_hg)
    mix_c = jnp.concatenate([merge_h(oAc), merge_gq(oBc), merge_gq(oCc)], axis=-1)
    return mix_x, mix_c


def moe_ffn(h, w_router, router_bias, w_gate, w_up, w_down):
    shape = h.shape
    t = h.reshape(-1, shape[-1])
    f32 = jnp.float32
    scores = jax.nn.sigmoid(t.astype(f32) @ w_router.astype(f32))
    sel = (scores + router_bias.astype(f32)).reshape(-1, N_GROUPS, EXPERTS_PER_GROUP)
    group_score = jnp.sum(lax.top_k(sel, 2)[0], axis=-1)
    best_group = jnp.argmax(group_score, axis=-1)
    in_group = jnp.take_along_axis(sel, best_group[:, None, None], axis=1)[:, 0]
    _, local = lax.top_k(in_group, TOP_K)
    expert_idx = best_group[:, None] * EXPERTS_PER_GROUP + local
    w = jnp.take_along_axis(scores, expert_idx, axis=-1)
    w = w / jnp.sum(w, axis=-1, keepdims=True)
    gates = jnp.sum(jax.nn.one_hot(expert_idx, N_EXPERTS, dtype=f32) * w[..., None], axis=1).astype(t.dtype)
    out = jnp.zeros_like(t)
    for e in range(N_EXPERTS):
        hid = jax.nn.silu(t @ w_gate[e]) * (t @ w_up[e])
        out = out + gates[:, e:e + 1] * (hid @ w_down[e])
    return out.reshape(shape)


def setup_inputs(seed: int = 0) -> dict:
    key = jax.random.key(seed)
    ks = jax.random.split(key, 24)
    f32 = jnp.float32
    D = D_MODEL

    def nrm(k, shape, s):
        return jax.random.normal(k, shape, f32) * s

    return {
        'x': nrm(ks[0], (BATCH, SEQ, D), 1.0),
        'c': nrm(ks[1], (BATCH, D), 1.0),
        'ctx': nrm(ks[2], (BATCH, CTX_LEN, D), 1.0),
        'c_ctx': nrm(ks[3], (D,), 1.0),
        'w_ada': nrm(ks[4], (DEPTH, D, 6 * D), 0.5 * D ** -0.5),
        'b_ada': nrm(ks[5], (DEPTH, 6 * D), 0.02),
        'norm1_g': 1.0 + nrm(ks[6], (DEPTH, D), 0.02),
        'norm2_g': 1.0 + nrm(ks[7], (DEPTH, D), 0.02),
        'w_in': nrm(ks[8], (DEPTH, D, IN_COLS), D ** -0.5),
        'w_out': nrm(ks[9], (DEPTH, MIX_WIDTH, D), MIX_WIDTH ** -0.5),
        'lam_q1': nrm(ks[10], (DEPTH, HEAD_DIM), 0.1),
        'lam_k1': nrm(ks[11], (DEPTH, HEAD_DIM), 0.1),
        'lam_q2': nrm(ks[12], (DEPTH, HEAD_DIM), 0.1),
        'lam_k2': nrm(ks[13], (DEPTH, HEAD_DIM), 0.1),
        'subln_g': 1.0 + nrm(ks[14], (DEPTH, A_V_DIM), 0.02),
        'q_norm_g': 1.0 + nrm(ks[15], (DEPTH, HEAD_DIM), 0.02),
        'k_norm_g': 1.0 + nrm(ks[16], (DEPTH, HEAD_DIM), 0.02),
        'sink': nrm(ks[17], (DEPTH, C_HEADS), 0.5),
        'w_router': nrm(ks[18], (D, N_EXPERTS), D ** -0.5),
        'router_bias': nrm(ks[19], (N_EXPERTS,), 0.01),
        'w_gate': nrm(ks[20], (DEPTH, N_EXPERTS, D, D_EXPERT), D ** -0.5),
        'w_up': nrm(ks[21], (DEPTH, N_EXPERTS, D, D_EXPERT), D ** -0.5),
        'w_down': nrm(ks[22], (DEPTH, N_EXPERTS, D_EXPERT, D), D_EXPERT ** -0.5),
        'final_g': 1.0 + nrm(ks[23], (D,), 0.02),
    }


def reference(x, c, ctx, c_ctx, w_ada, b_ada, norm1_g, norm2_g, w_in, w_out,
              lam_q1, lam_k1, lam_q2, lam_k2, subln_g, q_norm_g, k_norm_g, sink,
              w_router, router_bias, w_gate, w_up, w_down, final_g):
    rope = rope_tables(x.shape[1])
    silu_c = jax.nn.silu(c)
    silu_cc = jax.nn.silu(c_ctx)
    for l in range(DEPTH):
        last = l == DEPTH - 1
        lambda_init = 0.8 - 0.6 * math.exp(-0.3 * l)
        mod = jnp.split(silu_c @ w_ada[l] + b_ada[l], 6, axis=-1)
        sh1, sc1, g1, sh2, sc2, g2 = [m[:, None, :] for m in mod]
        csh1, csc1, cg1, csh2, csc2, cg2 = jnp.split(silu_cc @ w_ada[l] + b_ada[l], 6, axis=-1)

        hx = rms_norm(x, norm1_g[l]) * (1 + sc1) + sh1
        hc = rms_norm(ctx, norm1_g[l]) * (1 + csc1) + csh1
        mix_x, mix_c = token_mixers(hx, hc, w_in[l], lam_q1[l], lam_k1[l], lam_q2[l], lam_k2[l],
                                    subln_g[l], q_norm_g[l], k_norm_g[l], sink[l],
                                    lambda_init, rope, not last)
        x = x + g1 * (mix_x @ w_out[l])
        hx = rms_norm(x, norm2_g[l]) * (1 + sc2) + sh2
        x = x + g2 * moe_ffn(hx, w_router, router_bias, w_gate[l], w_up[l], w_down[l])

        if not last:
            ctx = ctx + cg1 * (mix_c @ w_out[l])
            hc = rms_norm(ctx, norm2_g[l]) * (1 + csc2) + csh2
            ctx = ctx + cg2 * moe_ffn(hc, w_router, router_bias, w_gate[l], w_up[l], w_down[l])
    return rms_norm(x, final_g)
```

```python
import functools
import math

import numpy as np
import jax
import jax.numpy as jnp
from jax import lax
from jax.experimental import pallas as pl
from jax.experimental.pallas import tpu as pltpu

F32 = jnp.float32
BF16 = jnp.bfloat16

HEAD_DIM = 64
GRID_W = 64
ROPE_BASE = 10000.0
RMS_EPS = 1e-6
WINDOW = 128
N_EXPERTS = 16
EXPERTS_PER_GROUP = 4
LANES = 128
TILE = 256
MOE_TILE = 1024
NEG_BIG = -1e30
VMEM_LIMIT = 52 * 1024 * 1024

QA, QB, QC = 512, 256, 256
KA, KB, KC = 512, 128, 128
Q_COLS = QA + QB + QC
K_COLS = KA + KB + KC
V_COLS = 768
ROPE_COLS = Q_COLS + K_COLS


def _dot(a, b):
    return jnp.dot(a, b, preferred_element_type=F32)


def _dot_nt(a, b):
    return lax.dot_general(a, b, (((1,), (1,)), ((), ())), preferred_element_type=F32)


def _split(a):
    hi = a.astype(BF16)
    lo = (a - hi.astype(F32)).astype(BF16)
    return hi, lo


def _dot3(a, b):
    ah, al = _split(a)
    bh, bl = _split(b)
    return _dot(ah, bh) + _dot(ah, bl) + _dot(al, bh)


def _rms(x, g):
    ms = jnp.mean(x * x, axis=-1, keepdims=True)
    return x * lax.rsqrt(ms + RMS_EPS) * g


def _sigmoid(x):
    return 1.0 / (1.0 + jnp.exp(-x))


def _adaln_kernel(c_ref, w_ref, b_ref, o_ref):
    cv = c_ref[...]
    s = cv * _sigmoid(cv)
    o_ref[0] = _dot3(s, w_ref[0]) + b_ref[0]


def _adaln(cpad, w_ada, b_ada):
    depth, d, n6 = w_ada.shape
    tn = 1536
    return pl.pallas_call(
        _adaln_kernel,
        grid=(depth, n6 // tn),
        in_specs=[pl.BlockSpec((16, d), lambda l, j: (0, 0)),
                  pl.BlockSpec((1, d, tn), lambda l, j: (l, 0, j)),
                  pl.BlockSpec((1, 1, tn), lambda l, j: (l, 0, j))],
        out_specs=pl.BlockSpec((1, 16, tn), lambda l, j: (l, 0, j)),
        out_shape=jax.ShapeDtypeStruct((depth, 16, n6), F32),
        compiler_params=pltpu.CompilerParams(dimension_semantics=("parallel", "parallel"),
                                             vmem_limit_bytes=VMEM_LIMIT),
        name="adaln",
    )(cpad, w_ada, b_ada.reshape(depth, 1, n6))


def _inproj_kernel(x_ref, mod_ref, g_ref, w_ref, cos_ref, sa_ref, sb_ref, gq_ref, gk_ref, mseg_ref,
                   q_ref, k_ref, v_ref):
    x = x_ref[...]
    sh = mod_ref[0, 0, 0:1, :]
    sc = mod_ref[0, 0, 1:2, :]
    h = (_rms(x, g_ref[...]) * (1.0 + sc) + sh).astype(BF16)
    cos = cos_ref[...]
    sa = sa_ref[...]
    sb = sb_ref[...]
    mseg = mseg_ref[...]

    def rope(b):
        return b * cos + pltpu.roll(b, LANES - 16, 1) * sa + pltpu.roll(b, 16, 1) * sb

    def qknorm(b, g):
        hi, lo = _split(b * b)
        ms = (_dot(hi, mseg) + _dot(lo, mseg)) * (1.0 / HEAD_DIM)
        return b * lax.rsqrt(ms + RMS_EPS) * g

    for c in range(ROPE_COLS // 256):
        p2 = _dot(h, w_ref[:, c * 256:(c + 1) * 256])
        for half in range(2):
            j = 2 * c + half
            p = p2[:, half * LANES:(half + 1) * LANES]
            if j in (4, 5):
                p = qknorm(p, gq_ref[...])
            if j == 12:
                p = qknorm(p, gk_ref[...])
            p = rope(p)
            if j < Q_COLS // LANES:
                q_ref[0, :, j * LANES:(j + 1) * LANES] = (p * (HEAD_DIM ** -0.5)).astype(BF16)
            else:
                jk = j - Q_COLS // LANES
                k_ref[0, :, jk * LANES:(jk + 1) * LANES] = p.astype(BF16)
    for c in range(V_COLS // 256):
        lo_c = ROPE_COLS + c * 256
        v_ref[0, :, c * 256:(c + 1) * 256] = _dot(h, w_ref[:, lo_c:lo_c + 256]).astype(BF16)


def _inproj(xflat, mod, g, w, cos, sa, sb, gq, gk, mseg, batch, nt, x_tile_map):
    d = xflat.shape[1]
    t_len = nt * TILE
    row = lambda b, t: (t, 0)
    const = lambda b, t: (0, 0)
    return pl.pallas_call(
        _inproj_kernel,
        grid=(batch, nt),
        in_specs=[pl.BlockSpec((TILE, d), lambda b, t: (x_tile_map(b, t), 0)),
                  pl.BlockSpec((1, 1, 8, d), lambda b, t: (b, jnp.minimum(t, 1), 0, 0)),
                  pl.BlockSpec((1, d), const),
                  pl.BlockSpec(w.shape, const),
                  pl.BlockSpec((TILE, LANES), row),
                  pl.BlockSpec((TILE, LANES), row),
                  pl.BlockSpec((TILE, LANES), row),
                  pl.BlockSpec((1, LANES), const),
                  pl.BlockSpec((1, LANES), const),
                  pl.BlockSpec((LANES, LANES), const)],
        out_specs=[pl.BlockSpec((1, TILE, Q_COLS), lambda b, t: (b, t, 0)),
                   pl.BlockSpec((1, TILE, K_COLS), lambda b, t: (b, t, 0)),
                   pl.BlockSpec((1, TILE, V_COLS), lambda b, t: (b, t, 0))],
        out_shape=[jax.ShapeDtypeStruct((batch, t_len, Q_COLS), BF16),
                   jax.ShapeDtypeStruct((batch, t_len, K_COLS), BF16),
                   jax.ShapeDtypeStruct((batch, t_len, V_COLS), BF16)],
        compiler_params=pltpu.CompilerParams(dimension_semantics=("parallel", "parallel"),
                                             vmem_limit_bytes=VMEM_LIMIT),
        name="inproj",
    )(xflat, mod, g, w, cos, sa, sb, gq, gk, mseg)


def _exp_parts(s, extra=None):
    m = jnp.max(s, axis=-1, keepdims=True)
    if extra is not None:
        m = jnp.maximum(m, extra)
    e = jnp.exp(s - m)
    l = jnp.sum(e, axis=-1, keepdims=True)
    if extra is not None:
        l = l + jnp.exp(extra - m)
    return e, l


def _half_masks(shape):
    lane = lax.broadcasted_iota(jnp.int32, shape, 1)
    return lane < HEAD_DIM, lane >= HEAD_DIM


def _attn_a_kernel(lam_ref, subg_ref, q_ref, k_ref, v_ref, o_ref, *, t0, ctx_len, lambda_init):
    t = pl.program_id(2) + t0
    q = q_ref[0]
    lo_m, hi_m = _half_masks(q.shape)
    zero = jnp.zeros_like(q)
    qq = jnp.concatenate([jnp.where(lo_m, q, zero), jnp.where(hi_m, q, zero)], axis=0)
    lv = lam_ref[...]
    lam = (jnp.exp(jnp.sum(lv[0:1] * lv[1:2], keepdims=True))
           - jnp.exp(jnp.sum(lv[2:3] * lv[3:4], keepdims=True)) + lambda_init)

    def attend(nk):
        k = k_ref[0, :nk, :]
        v = v_ref[0, :nk, :]
        e, l = _exp_parts(_dot_nt(qq, k))
        o = _dot(e.astype(BF16), v) / l
        o = o[:TILE] - lam * o[TILE:]
        o_ref[0] = (_rms(o, subg_ref[...]) * (1.0 - lambda_init)).astype(BF16)

    if t0 == 0:
        @pl.when(t == 0)
        def _():
            attend(ctx_len)

        @pl.when(t > 0)
        def _():
            attend(k_ref.shape[1])
    else:
        attend(k_ref.shape[1])


def _attn_a(lamv, subg, q, k, v, t0, lambda_init):
    batch, t_len, _ = q.shape
    nt = t_len // TILE
    heads = QA // LANES
    kern = functools.partial(_attn_a_kernel, t0=t0, ctx_len=TILE, lambda_init=lambda_init)
    return pl.pallas_call(
        kern,
        grid=(batch, heads, nt - t0),
        in_specs=[pl.BlockSpec(lamv.shape, lambda b, h, t: (0, 0)),
                  pl.BlockSpec((1, LANES), lambda b, h, t: (0, 0)),
                  pl.BlockSpec((1, TILE, LANES), lambda b, h, t: (b, t + t0, h)),
                  pl.BlockSpec((1, t_len, LANES), lambda b, h, t: (b, 0, h)),
                  pl.BlockSpec((1, t_len, LANES), lambda b, h, t: (b, 0, h))],
        out_specs=pl.BlockSpec((1, TILE, LANES), lambda b, h, t: (b, t + t0, h)),
        out_shape=jax.ShapeDtypeStruct((batch, t_len, QA), BF16),
        compiler_params=pltpu.CompilerParams(dimension_semantics=("parallel", "parallel", "parallel"),
                                             vmem_limit_bytes=VMEM_LIMIT),
        name="attn_a",
    )(lamv, subg, q, k, v)


def _stack_group_queries(q2, kv):
    lo_m, hi_m = _half_masks((TILE, LANES))
    m = lo_m if kv == 0 else hi_m
    zero = jnp.zeros((TILE, LANES), q2.dtype)
    return jnp.concatenate([jnp.where(m, q2[:, :LANES], zero), jnp.where(m, q2[:, LANES:], zero)], axis=0)


def _merge_kv_outputs(o_kv0, o_kv1):
    lo_m, _ = _half_masks((TILE, LANES))
    g0 = jnp.where(lo_m, o_kv0[:TILE], o_kv1[:TILE])
    g1 = jnp.where(lo_m, o_kv0[TILE:], o_kv1[TILE:])
    return jnp.concatenate([g0, g1], axis=1)


def _attn_b_kernel(q_ref, k_ref, v_ref, o_ref, *, t0, ctx_len):
    t = pl.program_id(1) + t0
    q2 = q_ref[0]

    def attend(nk):
        k = k_ref[0, :nk, :]
        v = v_ref[0, :nk, :]
        outs = []
        for kv in range(2):
            e, l = _exp_parts(_dot_nt(_stack_group_queries(q2, kv), k))
            outs.append(_dot(e.astype(BF16), v) / l)
        o_ref[0] = _merge_kv_outputs(*outs).astype(BF16)

    if t0 == 0:
        @pl.when(t == 0)
        def _():
            attend(ctx_len)

        @pl.when(t > 0)
        def _():
            attend(k_ref.shape[1])
    else:
        attend(k_ref.shape[1])


def _attn_c_kernel(sink_ref, q_ref, k_ref, v_ref, o_ref, *, t0, ctx_len):
    t = pl.program_id(1) + t0
    q2 = q_ref[0]
    t_len = k_ref.shape[1]
    band = 2 * TILE
    row = lax.broadcasted_iota(jnp.int32, (2 * TILE, 1), 0)

    def sink_col(kv):
        return jnp.where(row < TILE, sink_ref[2 * kv], sink_ref[2 * kv + 1])

    def ctx_only():
        k = k_ref[0, :ctx_len, :]
        v = v_ref[0, :ctx_len, :]
        outs = []
        for kv in range(2):
            e, l = _exp_parts(_dot_nt(_stack_group_queries(q2, kv), k), sink_col(kv))
            outs.append(_dot(e.astype(BF16), v) / l)
        o_ref[0] = _merge_kv_outputs(*outs).astype(BF16)

    def windowed():
        kc = k_ref[0, :ctx_len, :]
        vc = v_ref[0, :ctx_len, :]
        q_start = (t - 1) * TILE
        u0 = pl.multiple_of(jnp.minimum(q_start + ctx_len - WINDOW, t_len - band), WINDOW)
        kb = k_ref[0, pl.ds(u0, band), :]
        vb = v_ref[0, pl.ds(u0, band), :]
        q_pos = q_start + (lax.broadcasted_iota(jnp.int32, (2 * TILE, band), 0) & (TILE - 1))
        k_pos = (u0 - ctx_len) + lax.broadcasted_iota(jnp.int32, (2 * TILE, band), 1)
        valid = (jnp.abs(k_pos - q_pos) <= WINDOW) & (k_pos >= 0)
        outs = []
        for kv in range(2):
            qq = _stack_group_queries(q2, kv)
            s_c = _dot_nt(qq, kc)
            s_b = jnp.where(valid, _dot_nt(qq, kb), NEG_BIG)
            sk = sink_col(kv)
            m = jnp.maximum(jnp.maximum(jnp.max(s_c, axis=-1, keepdims=True),
                                        jnp.max(s_b, axis=-1, keepdims=True)), sk)
            e_c = jnp.exp(s_c - m)
            e_b = jnp.exp(s_b - m)
            l = (jnp.sum(e_c, axis=-1, keepdims=True) + jnp.sum(e_b, axis=-1, keepdims=True)
                 + jnp.exp(sk - m))
            outs.append((_dot(e_c.astype(BF16), vc) + _dot(e_b.astype(BF16), vb)) / l)
        o_ref[0] = _merge_kv_outputs(*outs).astype(BF16)

    if t0 == 0:
        @pl.when(t == 0)
        def _():
            ctx_only()

        @pl.when(t > 0)
        def _():
            windowed()
    else:
        windowed()


def _attn_gqa(kernel_fn, q, k, v, t0, q_blk, kv_blk, sink=None):
    batch, t_len, _ = q.shape
    nt = t_len // TILE
    kern = functools.partial(kernel_fn, t0=t0, ctx_len=TILE)
    in_specs = [pl.BlockSpec((1, TILE, 2 * LANES), lambda b, t: (b, t + t0, q_blk)),
                pl.BlockSpec((1, t_len, LANES), lambda b, t: (b, 0, kv_blk)),
                pl.BlockSpec((1, t_len, LANES), lambda b, t: (b, 0, kv_blk))]
    args = [q, k, v]
    if sink is not None:
        in_specs = [pl.BlockSpec(memory_space=pltpu.SMEM)] + in_specs
        args = [sink] + args
    return pl.pallas_call(
        kern,
        grid=(batch, nt - t0),
        in_specs=in_specs,
        out_specs=pl.BlockSpec((1, TILE, 2 * LANES), lambda b, t: (b, t + t0, 0)),
        out_shape=jax.ShapeDtypeStruct((batch, t_len, 2 * LANES), BF16),
        compiler_params=pltpu.CompilerParams(dimension_semantics=("parallel", "parallel"),
                                             vmem_limit_bytes=VMEM_LIMIT),
        name="attn_c" if sink is not None else "attn_b",
    )(*args)


def _route(scores, bias):
    lane = lax.broadcasted_iota(jnp.int32, scores.shape, 1)
    sel = scores + bias
    in_group = lane & (EXPERTS_PER_GROUP - 1)
    group = lane >> 2

    def neighbours(x, idx, step, span):
        for d in (1, 2, 3):
            fwd = (idx + d) < 4
            y = jnp.where(fwd, pltpu.roll(x, LANES - d * step, 1), pltpu.roll(x, span - d * step, 1))
            yield y, fwd

    rank = jnp.zeros_like(sel)
    for y, fwd in neighbours(sel, in_group, 1, EXPERTS_PER_GROUP):
        beats = (y > sel) | ((y == sel) & jnp.logical_not(fwd))
        rank = rank + jnp.where(beats, 1.0, 0.0)
    top2 = jnp.where(rank < 2.0, sel, 0.0)
    gsum = top2
    for y, _ in neighbours(top2, in_group, 1, EXPERTS_PER_GROUP):
        gsum = gsum + y
    grank = jnp.zeros_like(sel)
    for y, fwd in neighbours(gsum, group, EXPERTS_PER_GROUP, N_EXPERTS):
        beats = (y > gsum) | ((y == gsum) & jnp.logical_not(fwd))
        grank = grank + jnp.where(beats, 1.0, 0.0)
    chosen = (grank < 0.5) & (rank < 2.0) & (lane < N_EXPERTS)
    w = jnp.where(chosen, scores, 0.0)
    return w / jnp.sum(w, axis=-1, keepdims=True)


def _outproj_kernel(x_ref, oa_ref, ob_ref, oc_ref, w_ref, mod_ref, g_ref, wrh_ref, wrl_ref, rb_ref,
                    x1_ref, h2_ref, gates_ref):
    acc = (_dot(oa_ref[0], w_ref[0:QA, :]) + _dot(ob_ref[0], w_ref[QA:QA + QB, :])
           + _dot(oc_ref[0], w_ref[QA + QB:, :]))
    x1 = x_ref[...] + mod_ref[0, 0, 2:3, :] * acc
    x1_ref[...] = x1
    h2 = _rms(x1, g_ref[...]) * (1.0 + mod_ref[0, 0, 4:5, :]) + mod_ref[0, 0, 3:4, :]
    h2_ref[...] = h2.astype(BF16)
    hi, lo = _split(h2)
    logits = _dot(hi, wrh_ref[...]) + _dot(hi, wrl_ref[...]) + _dot(lo, wrh_ref[...])
    gates_ref[...] = _route(_sigmoid(logits), rb_ref[...])


def _outproj(xflat, oa, ob, oc, w, mod, g, wrh, wrl, rb, t0, x_tile_map, out_tile_map, n_out):
    batch, t_len, _ = oa.shape
    nt = t_len // TILE
    d = xflat.shape[1]
    const = lambda b, t: (0, 0)
    tok = lambda b, t: (b, t + t0, 0)
    out_row = lambda b, t: (out_tile_map(b, t + t0), 0)
    return pl.pallas_call(
        _outproj_kernel,
        grid=(batch, nt - t0),
        in_specs=[pl.BlockSpec((TILE, d), lambda b, t: (x_tile_map(b, t + t0), 0)),
                  pl.BlockSpec((1, TILE, QA), tok),
                  pl.BlockSpec((1, TILE, QB), tok),
                  pl.BlockSpec((1, TILE, QC), tok),
                  pl.BlockSpec(w.shape, const),
                  pl.BlockSpec((1, 1, 8, d), lambda b, t: (b, jnp.minimum(t + t0, 1), 0, 0)),
                  pl.BlockSpec((1, d), const),
                  pl.BlockSpec((d, LANES), const),
                  pl.BlockSpec((d, LANES), const),
                  pl.BlockSpec((1, LANES), const)],
        out_specs=[pl.BlockSpec((TILE, d), out_row),
                   pl.BlockSpec((TILE, d), out_row),
                   pl.BlockSpec((TILE, LANES), out_row)],
        out_shape=[jax.ShapeDtypeStruct((n_out, d), F32),
                   jax.ShapeDtypeStruct((n_out, d), BF16),
                   jax.ShapeDtypeStruct((n_out, LANES), F32)],
        compiler_params=pltpu.CompilerParams(dimension_semantics=("parallel", "parallel"),
                                             vmem_limit_bytes=VMEM_LIMIT),
        name="outproj",
    )(xflat, oa, ob, oc, w, mod, g, wrh, wrl, rb)


def _moe_kernel(h_ref, gates_ref, x_ref, mod_ref, wg_ref, wu_ref, wd_ref, fg_ref, o_ref, acc_ref, *, final_norm):
    e = pl.program_id(1)

    @pl.when(e == 0)
    def _():
        acc_ref[...] = jnp.zeros_like(acc_ref)

    h = h_ref[...]
    gates = gates_ref[...]
    lane = lax.broadcasted_iota(jnp.int32, gates.shape, 1)
    gate = jnp.sum(jnp.where(lane == e, gates, 0.0), axis=-1, keepdims=True)
    a = _dot(h, wg_ref[0])
    u = _dot(h, wu_ref[0])
    hid = (a * _sigmoid(a)) * u * gate
    acc_ref[...] += _dot(hid.astype(BF16), wd_ref[0])

    @pl.when(e == pl.num_programs(1) - 1)
    def _():
        y = x_ref[...] + mod_ref[0, 0, 5:6, :] * acc_ref[...]
        if final_norm:
            y = _rms(y, fg_ref[...])
        o_ref[...] = y


def _moe(h2, gates, x1, mod, wg, wu, wd, fg, mod_tile_map, final_norm):
    n, d = x1.shape
    ne, _, de = wg.shape
    row = lambda i, e: (i, 0)
    return pl.pallas_call(
        functools.partial(_moe_kernel, final_norm=final_norm),
        grid=(n // MOE_TILE, ne),
        in_specs=[pl.BlockSpec((MOE_TILE, d), row),
                  pl.BlockSpec((MOE_TILE, LANES), row),
                  pl.BlockSpec((MOE_TILE, d), row),
                  pl.BlockSpec((1, 1, 8, d), lambda i, e: mod_tile_map(i) + (0, 0)),
                  pl.BlockSpec((1, d, de), lambda i, e: (e, 0, 0)),
                  pl.BlockSpec((1, d, de), lambda i, e: (e, 0, 0)),
                  pl.BlockSpec((1, de, d), lambda i, e: (e, 0, 0)),
                  pl.BlockSpec((1, d), lambda i, e: (0, 0))],
        out_specs=pl.BlockSpec((MOE_TILE, d), row),
        out_shape=jax.ShapeDtypeStruct((n, d), F32),
        scratch_shapes=[pltpu.VMEM((MOE_TILE, d), F32)],
        compiler_params=pltpu.CompilerParams(dimension_semantics=("parallel", "arbitrary"),
                                             vmem_limit_bytes=VMEM_LIMIT),
        name="moe",
    )(h2, gates, x1, mod, wg, wu, wd, fg)


def _gqa_perm():
    return np.arange(256).reshape(2, 2, HEAD_DIM).transpose(1, 0, 2).reshape(-1)


def _in_col_perm():
    g = _gqa_perm()
    q = np.concatenate([np.arange(512), 512 + g, 768 + g])
    kv0 = 1024
    k = np.concatenate([kv0 + np.arange(512), kv0 + 1024 + np.arange(128), kv0 + 1280 + np.arange(128)])
    v = np.concatenate([kv0 + 512 + np.arange(512), kv0 + 1152 + np.arange(128), kv0 + 1408 + np.arange(128)])
    return np.concatenate([q, k, v])


def _out_row_perm():
    g = _gqa_perm()
    return np.concatenate([np.arange(512), 512 + g, 768 + g])


def _rope_tables(seq, ctx_len):
    rows = seq // GRID_W
    row_pos = jnp.repeat(jnp.arange(rows, dtype=F32), GRID_W)
    col_pos = jnp.tile(jnp.arange(GRID_W, dtype=F32), rows)
    axis_dim = HEAD_DIM // 2
    inv_freq = ROPE_BASE ** (-jnp.arange(0, axis_dim, 2, dtype=F32) / axis_dim)
    ang_r = row_pos[:, None] * inv_freq[None, :]
    ang_c = col_pos[:, None] * inv_freq[None, :]
    z = jnp.zeros_like(ang_r)
    cos = jnp.concatenate([jnp.cos(ang_r)] * 2 + [jnp.cos(ang_c)] * 2, axis=-1)
    sin_a = jnp.concatenate([-jnp.sin(ang_r), z, -jnp.sin(ang_c), z], axis=-1)
    sin_b = jnp.concatenate([z, jnp.sin(ang_r), z, jnp.sin(ang_c)], axis=-1)

    def full(tab, fill):
        tab = jnp.tile(tab, (1, LANES // HEAD_DIM))
        return jnp.concatenate([jnp.full((ctx_len, LANES), fill, F32), tab], axis=0)

    return full(cos, 1.0), full(sin_a, 0.0), full(sin_b, 0.0)


def kernel(x, c, ctx, c_ctx, w_ada, b_ada, norm1_g, norm2_g, w_in, w_out, lam_q1, lam_k1, lam_q2, lam_k2,
           subln_g, q_norm_g, k_norm_g, sink, w_router, router_bias, w_gate, w_up, w_down, final_g):
    batch, seq, d = x.shape
    ctx_len = ctx.shape[1]
    depth = w_in.shape[0]
    assert ctx_len == TILE and seq % MOE_TILE == 0 and (batch * ctx_len) % MOE_TILE == 0 and batch <= 15
    nt = (ctx_len + seq) // TILE
    lt = seq // TILE
    n_ctx = batch * ctx_len

    cpad = jnp.zeros((16, d), F32).at[:batch].set(c).at[batch].set(c_ctx)
    mod_all = _adaln(cpad, w_ada, b_ada)
    cos, sin_a, sin_b = _rope_tables(seq, ctx_len)
    mseg = jnp.asarray(np.kron(np.eye(2), np.ones((HEAD_DIM, HEAD_DIM))), BF16)
    in_perm = _in_col_perm()
    out_perm = _out_row_perm()
    wr = jnp.zeros((d, LANES), F32).at[:, :N_EXPERTS].set(w_router)
    wrh = wr.astype(BF16)
    wrl = (wr - wrh.astype(F32)).astype(BF16)
    rb = jnp.zeros((1, LANES), F32).at[0, :N_EXPERTS].set(router_bias)

    xflat = jnp.concatenate([ctx.reshape(n_ctx, d), x.reshape(batch * seq, d)], axis=0)
    with_ctx_map = lambda b, t: jnp.where(t == 0, b, batch + b * lt + t - 1)
    latent_map = lambda b, t: b * lt + t - 1

    for l in range(depth):
        last = l == depth - 1
        lambda_init = 0.8 - 0.6 * math.exp(-0.3 * l)
        m6 = mod_all[l].reshape(16, 6, d)
        m8 = jnp.concatenate([m6, jnp.zeros((16, 2, d), F32)], axis=1)
        mod = jnp.stack([jnp.broadcast_to(m8[batch], (batch, 8, d)), m8[:batch]], axis=1)

        w_in_p = w_in[l][:, in_perm].astype(BF16)
        w_out_p = w_out[l][out_perm, :].astype(BF16)
        gq = jnp.tile(q_norm_g[l], 2).reshape(1, LANES)
        gk = jnp.tile(k_norm_g[l], 2).reshape(1, LANES)
        q, k, v = _inproj(xflat, mod, norm1_g[l].reshape(1, d), w_in_p, cos, sin_a, sin_b, gq, gk, mseg,
                          batch, nt, with_ctx_map)

        t0 = 1 if last else 0
        lamv = jnp.stack([lam_q1[l], lam_k1[l], lam_q2[l], lam_k2[l]])
        oa = _attn_a(lamv, subln_g[l].reshape(1, LANES), q, k, v, t0, lambda_init)
        ob = _attn_gqa(_attn_b_kernel, q, k, v, t0, q_blk=2, kv_blk=4)
        oc = _attn_gqa(_attn_c_kernel, q, k, v, t0, q_blk=3, kv_blk=5, sink=sink[l])

        if last:
            n_out = batch * seq
            out_map = latent_map
            mod_tile_map = lambda i: (i // (seq // MOE_TILE), 1)
        else:
            n_out = xflat.shape[0]
            out_map = with_ctx_map
            nc = n_ctx // MOE_TILE
            mod_tile_map = lambda i: (jnp.maximum(i - nc, 0) // (seq // MOE_TILE), jnp.minimum(i // nc, 1))
        x1, h2, gates = _outproj(xflat, oa, ob, oc, w_out_p, mod, norm2_g[l].reshape(1, d), wrh, wrl, rb,
                                 t0, with_ctx_map, out_map, n_out)
        xflat = _moe(h2, gates, x1, mod, w_gate[l].astype(BF16), w_up[l].astype(BF16), w_down[l].astype(BF16),
                     final_g.reshape(1, d), mod_tile_map, final_norm=last)
    return xflat.reshape(batch, seq, d)
```

```python
import functools
import math

import numpy as np
import jax
import jax.numpy as jnp
from jax import lax
from jax.experimental import pallas as pl
from jax.experimental.pallas import tpu as pltpu

F32 = jnp.float32
BF16 = jnp.bfloat16

HEAD_DIM = 64
GRID_W = 64
ROPE_BASE = 10000.0
RMS_EPS = 1e-6
WINDOW = 128
N_EXPERTS = 16
EXPERTS_PER_GROUP = 4
LANES = 128
TILE = 256
MOE_TILE = 1024
NEG_BIG = -1e30
KEY_CHUNK = 256
VMEM_LIMIT = 52 * 1024 * 1024

QA, QB, QC = 512, 256, 256
KA, KB, KC = 512, 128, 128
Q_COLS = QA + QB + QC
K_COLS = KA + KB + KC
V_COLS = 768
ROPE_COLS = Q_COLS + K_COLS


def _dot(a, b):
    return jnp.dot(a, b, preferred_element_type=F32)


def _dot_nt(a, b):
    return lax.dot_general(a, b, (((1,), (1,)), ((), ())), preferred_element_type=F32)


def _split(a):
    hi = a.astype(BF16)
    lo = (a - hi.astype(F32)).astype(BF16)
    return hi, lo


def _dot3(a, b):
    ah, al = _split(a)
    bh, bl = _split(b)
    return _dot(ah, bh) + _dot(ah, bl) + _dot(al, bh)


def _rms(x, g):
    ms = jnp.mean(x * x, axis=-1, keepdims=True)
    return x * lax.rsqrt(ms + RMS_EPS) * g


def _sigmoid(x):
    return 1.0 / (1.0 + jnp.exp(-x))


def _adaln_kernel(c_ref, w_ref, b_ref, o_ref):
    cv = c_ref[...]
    s = cv * _sigmoid(cv)
    o_ref[0] = _dot3(s, w_ref[0]) + b_ref[0]


def _adaln(cpad, w_ada, b_ada):
    depth, d, n6 = w_ada.shape
    tn = 1536
    return pl.pallas_call(
        _adaln_kernel,
        grid=(depth, n6 // tn),
        in_specs=[pl.BlockSpec((16, d), lambda l, j: (0, 0)),
                  pl.BlockSpec((1, d, tn), lambda l, j: (l, 0, j)),
                  pl.BlockSpec((1, 1, tn), lambda l, j: (l, 0, j))],
        out_specs=pl.BlockSpec((1, 16, tn), lambda l, j: (l, 0, j)),
        out_shape=jax.ShapeDtypeStruct((depth, 16, n6), F32),
        compiler_params=pltpu.CompilerParams(dimension_semantics=("parallel", "parallel"),
                                             vmem_limit_bytes=VMEM_LIMIT),
        name="adaln",
    )(cpad, w_ada, b_ada.reshape(depth, 1, n6))


def _inproj_kernel(x_ref, mod_ref, g_ref, w_ref, cos_ref, sa_ref, sb_ref, gq_ref, gk_ref, mseg_ref,
                   q_ref, k_ref, v_ref):
    x = x_ref[...]
    sh = mod_ref[0, 0, 0:1, :]
    sc = mod_ref[0, 0, 1:2, :]
    h = (_rms(x, g_ref[...]) * (1.0 + sc) + sh).astype(BF16)
    cos = cos_ref[...]
    sa = sa_ref[...]
    sb = sb_ref[...]
    mseg = mseg_ref[...]

    def rope(b):
        return b * cos + pltpu.roll(b, LANES - 16, 1) * sa + pltpu.roll(b, 16, 1) * sb

    def qknorm(b, g):
        hi, lo = _split(b * b)
        ms = (_dot(hi, mseg) + _dot(lo, mseg)) * (1.0 / HEAD_DIM)
        return b * lax.rsqrt(ms + RMS_EPS) * g

    for c in range(ROPE_COLS // 256):
        p2 = _dot(h, w_ref[:, c * 256:(c + 1) * 256])
        for half in range(2):
            j = 2 * c + half
            p = p2[:, half * LANES:(half + 1) * LANES]
            if j in (4, 5):
                p = qknorm(p, gq_ref[...])
            if j == 12:
                p = qknorm(p, gk_ref[...])
            p = rope(p)
            if j < Q_COLS // LANES:
                q_ref[0, :, j * LANES:(j + 1) * LANES] = (p * (HEAD_DIM ** -0.5)).astype(BF16)
            else:
                jk = j - Q_COLS // LANES
                k_ref[0, :, jk * LANES:(jk + 1) * LANES] = p.astype(BF16)
    for c in range(V_COLS // 256):
        lo_c = ROPE_COLS + c * 256
        v_ref[0, :, c * 256:(c + 1) * 256] = _dot(h, w_ref[:, lo_c:lo_c + 256]).astype(BF16)


def _inproj(xflat, mod, g, w, cos, sa, sb, gq, gk, mseg, batch, nt, x_tile_map):
    d = xflat.shape[1]
    t_len = nt * TILE
    row = lambda b, t: (t, 0)
    const = lambda b, t: (0, 0)
    return pl.pallas_call(
        _inproj_kernel,
        grid=(batch, nt),
        in_specs=[pl.BlockSpec((TILE, d), lambda b, t: (x_tile_map(b, t), 0)),
                  pl.BlockSpec((1, 1, 8, d), lambda b, t: (b, jnp.minimum(t, 1), 0, 0)),
                  pl.BlockSpec((1, d), const),
                  pl.BlockSpec(w.shape, const),
                  pl.BlockSpec((TILE, LANES), row),
                  pl.BlockSpec((TILE, LANES), row),
                  pl.BlockSpec((TILE, LANES), row),
                  pl.BlockSpec((1, LANES), const),
                  pl.BlockSpec((1, LANES), const),
                  pl.BlockSpec((LANES, LANES), const)],
        out_specs=[pl.BlockSpec((1, TILE, Q_COLS), lambda b, t: (b, t, 0)),
                   pl.BlockSpec((1, TILE, K_COLS), lambda b, t: (b, t, 0)),
                   pl.BlockSpec((1, TILE, V_COLS), lambda b, t: (b, t, 0))],
        out_shape=[jax.ShapeDtypeStruct((batch, t_len, Q_COLS), BF16),
                   jax.ShapeDtypeStruct((batch, t_len, K_COLS), BF16),
                   jax.ShapeDtypeStruct((batch, t_len, V_COLS), BF16)],
        compiler_params=pltpu.CompilerParams(dimension_semantics=("parallel", "parallel"),
                                             vmem_limit_bytes=VMEM_LIMIT),
        name="inproj",
    )(xflat, mod, g, w, cos, sa, sb, gq, gk, mseg)


def _softmax_pv(qq, k_ref, v_ref, col, starts, s_ref, mask_fn=None, extra=None):
    half = KEY_CHUNK // 2
    mrun = None
    for j, st in enumerate(starts):
        s = _dot_nt(qq, k_ref[0, pl.ds(st, KEY_CHUNK), col:col + LANES])
        if mask_fn is not None:
            s = mask_fn(j, st, s)
        s_ref[:, j * KEY_CHUNK:(j + 1) * KEY_CHUNK] = s
        mj = jnp.maximum(s[:, :half], s[:, half:])
        mrun = mj if mrun is None else jnp.maximum(mrun, mj)
    m = jnp.max(mrun, axis=-1, keepdims=True)
    if extra is not None:
        m = jnp.maximum(m, extra)
    lrun = None
    acc = None
    for j, st in enumerate(starts):
        e = jnp.exp(s_ref[:, j * KEY_CHUNK:(j + 1) * KEY_CHUNK] - m)
        lj = e[:, :half] + e[:, half:]
        lrun = lj if lrun is None else lrun + lj
        pv = _dot(e.astype(BF16), v_ref[0, pl.ds(st, KEY_CHUNK), col:col + LANES])
        acc = pv if acc is None else acc + pv
    l = jnp.sum(lrun, axis=-1, keepdims=True)
    if extra is not None:
        l = l + jnp.exp(extra - m)
    return acc / l


def _all_chunks(n_keys):
    return [j * KEY_CHUNK for j in range(n_keys // KEY_CHUNK)]


def _half_masks(shape):
    lane = lax.broadcasted_iota(jnp.int32, shape, 1)
    return lane < HEAD_DIM, lane >= HEAD_DIM


def _stack_group_queries(q2, kv):
    lo_m, hi_m = _half_masks((TILE, LANES))
    m = lo_m if kv == 0 else hi_m
    zero = jnp.zeros((TILE, LANES), q2.dtype)
    return jnp.concatenate([jnp.where(m, q2[:, :LANES], zero), jnp.where(m, q2[:, LANES:], zero)], axis=0)


def _merge_kv_outputs(o_kv0, o_kv1):
    lo_m, _ = _half_masks((TILE, LANES))
    g0 = jnp.where(lo_m, o_kv0[:TILE], o_kv1[:TILE])
    g1 = jnp.where(lo_m, o_kv0[TILE:], o_kv1[TILE:])
    return jnp.concatenate([g0, g1], axis=1)


def _attn_kernel(sink_ref, lam_ref, subg_ref, qa_ref, qb_ref, qc_ref, ka_ref, kb_ref, kc_ref, va_ref, vb_ref, vc_ref,
                 o_ref, s_ref, *, t0, ctx_len, lambda_init):
    t = pl.program_id(1) + t0
    t_len = ka_ref.shape[1]
    band = 2 * TILE
    lo_m, hi_m = _half_masks((TILE, LANES))
    zero = jnp.zeros((TILE, LANES), BF16)
    row = lax.broadcasted_iota(jnp.int32, (2 * TILE, 1), 0)
    lv = lam_ref[...]
    lam = (jnp.exp(jnp.sum(lv[0:1] * lv[1:2], keepdims=True))
           - jnp.exp(jnp.sum(lv[2:3] * lv[3:4], keepdims=True)) + lambda_init)

    def sink_col(kv):
        return jnp.where(row < TILE, sink_ref[2 * kv], sink_ref[2 * kv + 1])

    def mixers(starts_ab, starts_c, mask_c):
        n_chain = 0
        for h in range(QA // LANES):
            q = qa_ref[0, :, h * LANES:(h + 1) * LANES]
            qq = jnp.concatenate([jnp.where(lo_m, q, zero), jnp.where(hi_m, q, zero)], axis=0)
            o = _softmax_pv(qq, ka_ref, va_ref, h * LANES, starts_ab, s_ref.at[n_chain % 2])
            n_chain += 1
            o = o[:TILE] - lam * o[TILE:]
            o_ref[0, :, h * LANES:(h + 1) * LANES] = (_rms(o, subg_ref[...]) * (1.0 - lambda_init)).astype(BF16)
        outs = []
        for kv in range(2):
            outs.append(_softmax_pv(_stack_group_queries(qb_ref[0], kv), kb_ref, vb_ref, 0, starts_ab,
                                    s_ref.at[n_chain % 2]))
            n_chain += 1
        o_ref[0, :, QA:QA + QB] = _merge_kv_outputs(*outs).astype(BF16)
        outs = []
        for kv in range(2):
            outs.append(_softmax_pv(_stack_group_queries(qc_ref[0], kv), kc_ref, vc_ref, 0, starts_c,
                                    s_ref.at[n_chain % 2], mask_c, sink_col(kv)))
            n_chain += 1
        o_ref[0, :, QA + QB:] = _merge_kv_outputs(*outs).astype(BF16)

    def ctx_step():
        mixers([0], [0], None)

    def latent_step():
        q_start = (t - 1) * TILE
        u0 = pl.multiple_of(jnp.minimum(q_start + ctx_len - WINDOW, t_len - band), WINDOW)
        q_pos = q_start + (lax.broadcasted_iota(jnp.int32, (2 * TILE, KEY_CHUNK), 0) & (TILE - 1))
        key_lane = lax.broadcasted_iota(jnp.int32, (2 * TILE, KEY_CHUNK), 1)

        def mask_c(j, st, s):
            if j == 0:
                return s
            k_pos = (st - ctx_len) + key_lane
            valid = (jnp.abs(k_pos - q_pos) <= WINDOW) & (k_pos >= 0)
            return jnp.where(valid, s, NEG_BIG)

        mixers(_all_chunks(t_len), [0] + [u0 + i * KEY_CHUNK for i in range(band // KEY_CHUNK)], mask_c)

    if t0 == 0:
        pl.when(t == 0)(ctx_step)
        pl.when(t > 0)(latent_step)
    else:
        latent_step()


def _attention(sink, lamv, subg, q, k, v, t0, lambda_init):
    batch, t_len, _ = q.shape
    nt = t_len // TILE
    kern = functools.partial(_attn_kernel, t0=t0, ctx_len=TILE, lambda_init=lambda_init)
    const = lambda b, t: (0, 0)
    qspec = lambda width, blk: pl.BlockSpec((1, TILE, width), lambda b, t: (b, t + t0, blk))
    kvspec = lambda width, blk: pl.BlockSpec((1, t_len, width), lambda b, t: (b, 0, blk))
    return pl.pallas_call(
        kern,
        grid=(batch, nt - t0),
        in_specs=[pl.BlockSpec(memory_space=pltpu.SMEM),
                  pl.BlockSpec(lamv.shape, const),
                  pl.BlockSpec((1, LANES), const),
                  qspec(QA, 0), qspec(QB, QA // QB), qspec(QC, (QA + QB) // QC),
                  kvspec(KA, 0), kvspec(KB, KA // KB), kvspec(KC, (KA + KB) // KC),
                  kvspec(KA, 0), kvspec(KB, KA // KB), kvspec(KC, (KA + KB) // KC)],
        out_specs=pl.BlockSpec((1, TILE, Q_COLS), lambda b, t: (b, t + t0, 0)),
        out_shape=jax.ShapeDtypeStruct((batch, t_len, Q_COLS), BF16),
        scratch_shapes=[pltpu.VMEM((2, 2 * TILE, t_len), F32)],
        compiler_params=pltpu.CompilerParams(dimension_semantics=("parallel", "parallel"),
                                             vmem_limit_bytes=VMEM_LIMIT),
        name="attention",
    )(sink, lamv, subg, q, q, q, k, k, k, v, v, v)


def _route(scores, bias):
    lane = lax.broadcasted_iota(jnp.int32, scores.shape, 1)
    sel = scores + bias
    in_group = lane & (EXPERTS_PER_GROUP - 1)
    group = lane >> 2

    def neighbours(x, idx, step, span):
        for d in (1, 2, 3):
            fwd = (idx + d) < 4
            y = jnp.where(fwd, pltpu.roll(x, LANES - d * step, 1), pltpu.roll(x, span - d * step, 1))
            yield y, fwd

    rank = jnp.zeros_like(sel)
    for y, fwd in neighbours(sel, in_group, 1, EXPERTS_PER_GROUP):
        beats = (y > sel) | ((y == sel) & jnp.logical_not(fwd))
        rank = rank + jnp.where(beats, 1.0, 0.0)
    top2 = jnp.where(rank < 2.0, sel, 0.0)
    gsum = top2
    for y, _ in neighbours(top2, in_group, 1, EXPERTS_PER_GROUP):
        gsum = gsum + y
    grank = jnp.zeros_like(sel)
    for y, fwd in neighbours(gsum, group, EXPERTS_PER_GROUP, N_EXPERTS):
        beats = (y > gsum) | ((y == gsum) & jnp.logical_not(fwd))
        grank = grank + jnp.where(beats, 1.0, 0.0)
    chosen = (grank < 0.5) & (rank < 2.0) & (lane < N_EXPERTS)
    w = jnp.where(chosen, scores, 0.0)
    return w / jnp.sum(w, axis=-1, keepdims=True)


def _outproj_kernel(x_ref, mix_ref, w_ref, mod_ref, g_ref, wrh_ref, wrl_ref, rb_ref, x1_ref, h2_ref, gates_ref):
    x1 = x_ref[...] + mod_ref[0, 0, 2:3, :] * _dot(mix_ref[0], w_ref[...])
    x1_ref[...] = x1
    h2 = _rms(x1, g_ref[...]) * (1.0 + mod_ref[0, 0, 4:5, :]) + mod_ref[0, 0, 3:4, :]
    h2_ref[...] = h2.astype(BF16)
    hi, lo = _split(h2)
    logits = _dot(hi, wrh_ref[...]) + _dot(hi, wrl_ref[...]) + _dot(lo, wrh_ref[...])
    gates_ref[...] = _route(_sigmoid(logits), rb_ref[...])


def _outproj(xflat, mix, w, mod, g, wrh, wrl, rb, t0, x_tile_map, out_tile_map, n_out):
    batch, t_len, _ = mix.shape
    nt = t_len // TILE
    d = xflat.shape[1]
    const = lambda b, t: (0, 0)
    tok = lambda b, t: (b, t + t0, 0)
    out_row = lambda b, t: (out_tile_map(b, t + t0), 0)
    return pl.pallas_call(
        _outproj_kernel,
        grid=(batch, nt - t0),
        in_specs=[pl.BlockSpec((TILE, d), lambda b, t: (x_tile_map(b, t + t0), 0)),
                  pl.BlockSpec((1, TILE, Q_COLS), tok),
                  pl.BlockSpec(w.shape, const),
                  pl.BlockSpec((1, 1, 8, d), lambda b, t: (b, jnp.minimum(t + t0, 1), 0, 0)),
                  pl.BlockSpec((1, d), const),
                  pl.BlockSpec((d, LANES), const),
                  pl.BlockSpec((d, LANES), const),
                  pl.BlockSpec((1, LANES), const)],
        out_specs=[pl.BlockSpec((TILE, d), out_row),
                   pl.BlockSpec((TILE, d), out_row),
                   pl.BlockSpec((TILE, LANES), out_row)],
        out_shape=[jax.ShapeDtypeStruct((n_out, d), F32),
                   jax.ShapeDtypeStruct((n_out, d), BF16),
                   jax.ShapeDtypeStruct((n_out, LANES), F32)],
        compiler_params=pltpu.CompilerParams(dimension_semantics=("parallel", "parallel"),
                                             vmem_limit_bytes=VMEM_LIMIT),
        name="outproj",
    )(xflat, mix, w, mod, g, wrh, wrl, rb)


def _moe_kernel(h_ref, gates_ref, x_ref, mod_ref, wg_ref, wu_ref, wd_ref, fg_ref, o_ref, acc_ref, *, final_norm):
    e = pl.program_id(1)

    @pl.when(e == 0)
    def _():
        acc_ref[...] = jnp.zeros_like(acc_ref)

    h = h_ref[...]
    gates = gates_ref[...]
    lane = lax.broadcasted_iota(jnp.int32, gates.shape, 1)
    gate = jnp.sum(jnp.where(lane == e, gates, 0.0), axis=-1, keepdims=True)
    a = _dot(h, wg_ref[0])
    u = _dot(h, wu_ref[0])
    hid = (a * _sigmoid(a)) * u * gate
    acc_ref[...] += _dot(hid.astype(BF16), wd_ref[0])

    @pl.when(e == pl.num_programs(1) - 1)
    def _():
        y = x_ref[...] + mod_ref[0, 0, 5:6, :] * acc_ref[...]
        if final_norm:
            y = _rms(y, fg_ref[...])
        o_ref[...] = y


def _moe(h2, gates, x1, mod, wg, wu, wd, fg, mod_tile_map, final_norm):
    n, d = x1.shape
    ne, _, de = wg.shape
    row = lambda i, e: (i, 0)
    return pl.pallas_call(
        functools.partial(_moe_kernel, final_norm=final_norm),
        grid=(n // MOE_TILE, ne),
        in_specs=[pl.BlockSpec((MOE_TILE, d), row),
                  pl.BlockSpec((MOE_TILE, LANES), row),
                  pl.BlockSpec((MOE_TILE, d), row),
                  pl.BlockSpec((1, 1, 8, d), lambda i, e: mod_tile_map(i) + (0, 0)),
                  pl.BlockSpec((1, d, de), lambda i, e: (e, 0, 0)),
                  pl.BlockSpec((1, d, de), lambda i, e: (e, 0, 0)),
                  pl.BlockSpec((1, de, d), lambda i, e: (e, 0, 0)),
                  pl.BlockSpec((1, d), lambda i, e: (0, 0))],
        out_specs=pl.BlockSpec((MOE_TILE, d), row),
        out_shape=jax.ShapeDtypeStruct((n, d), F32),
        scratch_shapes=[pltpu.VMEM((MOE_TILE, d), F32)],
        compiler_params=pltpu.CompilerParams(dimension_semantics=("parallel", "arbitrary"),
                                             vmem_limit_bytes=VMEM_LIMIT),
        name="moe",
    )(h2, gates, x1, mod, wg, wu, wd, fg)


def _gqa_perm():
    return np.arange(256).reshape(2, 2, HEAD_DIM).transpose(1, 0, 2).reshape(-1)


def _in_col_perm():
    g = _gqa_perm()
    q = np.concatenate([np.arange(512), 512 + g, 768 + g])
    kv0 = 1024
    k = np.concatenate([kv0 + np.arange(512), kv0 + 1024 + np.arange(128), kv0 + 1280 + np.arange(128)])
    v = np.concatenate([kv0 + 512 + np.arange(512), kv0 + 1152 + np.arange(128), kv0 + 1408 + np.arange(128)])
    return np.concatenate([q, k, v])


def _out_row_perm():
    g = _gqa_perm()
    return np.concatenate([np.arange(512), 512 + g, 768 + g])


def _rope_tables(seq, ctx_len):
    rows = seq // GRID_W
    row_pos = jnp.repeat(jnp.arange(rows, dtype=F32), GRID_W)
    col_pos = jnp.tile(jnp.arange(GRID_W, dtype=F32), rows)
    axis_dim = HEAD_DIM // 2
    inv_freq = ROPE_BASE ** (-jnp.arange(0, axis_dim, 2, dtype=F32) / axis_dim)
    ang_r = row_pos[:, None] * inv_freq[None, :]
    ang_c = col_pos[:, None] * inv_freq[None, :]
    z = jnp.zeros_like(ang_r)
    cos = jnp.concatenate([jnp.cos(ang_r)] * 2 + [jnp.cos(ang_c)] * 2, axis=-1)
    sin_a = jnp.concatenate([-jnp.sin(ang_r), z, -jnp.sin(ang_c), z], axis=-1)
    sin_b = jnp.concatenate([z, jnp.sin(ang_r), z, jnp.sin(ang_c)], axis=-1)

    def full(tab, fill):
        tab = jnp.tile(tab, (1, LANES // HEAD_DIM))
        return jnp.concatenate([jnp.full((ctx_len, LANES), fill, F32), tab], axis=0)

    return full(cos, 1.0), full(sin_a, 0.0), full(sin_b, 0.0)


def kernel(x, c, ctx, c_ctx, w_ada, b_ada, norm1_g, norm2_g, w_in, w_out, lam_q1, lam_k1, lam_q2, lam_k2,
           subln_g, q_norm_g, k_norm_g, sink, w_router, router_bias, w_gate, w_up, w_down, final_g):
    batch, seq, d = x.shape
    ctx_len = ctx.shape[1]
    depth = w_in.shape[0]
    assert ctx_len == TILE and seq % MOE_TILE == 0 and (batch * ctx_len) % MOE_TILE == 0 and batch <= 15
    nt = (ctx_len + seq) // TILE
    lt = seq // TILE
    n_ctx = batch * ctx_len

    cpad = jnp.zeros((16, d), F32).at[:batch].set(c).at[batch].set(c_ctx)
    mod_all = _adaln(cpad, w_ada, b_ada)
    cos, sin_a, sin_b = _rope_tables(seq, ctx_len)
    mseg = jnp.asarray(np.kron(np.eye(2), np.ones((HEAD_DIM, HEAD_DIM))), BF16)
    in_perm = _in_col_perm()
    out_perm = _out_row_perm()
    wr = jnp.zeros((d, LANES), F32).at[:, :N_EXPERTS].set(w_router)
    wrh = wr.astype(BF16)
    wrl = (wr - wrh.astype(F32)).astype(BF16)
    rb = jnp.zeros((1, LANES), F32).at[0, :N_EXPERTS].set(router_bias)

    xflat = jnp.concatenate([ctx.reshape(n_ctx, d), x.reshape(batch * seq, d)], axis=0)
    with_ctx_map = lambda b, t: jnp.where(t == 0, b, batch + b * lt + t - 1)
    latent_map = lambda b, t: b * lt + t - 1

    for l in range(depth):
        last = l == depth - 1
        lambda_init = 0.8 - 0.6 * math.exp(-0.3 * l)
        m6 = mod_all[l].reshape(16, 6, d)
        m8 = jnp.concatenate([m6, jnp.zeros((16, 2, d), F32)], axis=1)
        mod = jnp.stack([jnp.broadcast_to(m8[batch], (batch, 8, d)), m8[:batch]], axis=1)

        w_in_p = w_in[l][:, in_perm].astype(BF16)
        w_out_p = w_out[l][out_perm, :].astype(BF16)
        gq = jnp.tile(q_norm_g[l], 2).reshape(1, LANES)
        gk = jnp.tile(k_norm_g[l], 2).reshape(1, LANES)
        q, k, v = _inproj(xflat, mod, norm1_g[l].reshape(1, d), w_in_p, cos, sin_a, sin_b, gq, gk, mseg,
                          batch, nt, with_ctx_map)

        t0 = 1 if last else 0
        lamv = jnp.stack([lam_q1[l], lam_k1[l], lam_q2[l], lam_k2[l]])
        mix = _attention(sink[l], lamv, subln_g[l].reshape(1, LANES), q, k, v, t0, lambda_init)

        if last:
            n_out = batch * seq
            out_map = latent_map
            mod_tile_map = lambda i: (i // (seq // MOE_TILE), 1)
        else:
            n_out = xflat.shape[0]
            out_map = with_ctx_map
            nc = n_ctx // MOE_TILE
            mod_tile_map = lambda i: (jnp.maximum(i - nc, 0) // (seq // MOE_TILE), jnp.minimum(i // nc, 1))
        x1, h2, gates = _outproj(xflat, mix, w_out_p, mod, norm2_g[l].reshape(1, d), wrh, wrl, rb,
                                 t0, with_ctx_map, out_map, n_out)
        xflat = _moe(h2, gates, x1, mod, w_gate[l].astype(BF16), w_up[l].astype(BF16), w_down[l].astype(BF16),
                     final_g.reshape(1, d), mod_tile_map, final_norm=last)
    return xflat.reshape(batch, seq, d)
```

```python
import functools
import math

import numpy as np
import jax
import jax.numpy as jnp
from jax import lax
from jax.experimental import pallas as pl
from jax.experimental.pallas import tpu as pltpu
from jax.experimental.pallas import tpu_sc as plsc

F32 = jnp.float32
BF16 = jnp.bfloat16

HEAD_DIM = 64
GRID_W = 64
ROPE_BASE = 10000.0
RMS_EPS = 1e-6
WINDOW = 128
N_EXPERTS = 16
EXPERTS_PER_GROUP = 4
LANES = 128
TILE = 256
MOE_TILE = 1024
FFN_TILE = 512
SC_WIN = 64
GROUP_LANE = 16
NEG_BIG = -1e30
KEY_CHUNK = 256
VMEM_LIMIT = 52 * 1024 * 1024

QA, QB, QC = 512, 256, 256
KA, KB, KC = 512, 128, 128
Q_COLS = QA + QB + QC
K_COLS = KA + KB + KC
V_COLS = 768
ROPE_COLS = Q_COLS + K_COLS


def _dot(a, b):
    return jnp.dot(a, b, preferred_element_type=F32)


def _dot_nt(a, b):
    return lax.dot_general(a, b, (((1,), (1,)), ((), ())), preferred_element_type=F32)


def _split(a):
    hi = a.astype(BF16)
    lo = (a - hi.astype(F32)).astype(BF16)
    return hi, lo


def _dot3(a, b):
    ah, al = _split(a)
    bh, bl = _split(b)
    return _dot(ah, bh) + _dot(ah, bl) + _dot(al, bh)


def _rms(x, g):
    ms = jnp.mean(x * x, axis=-1, keepdims=True)
    return x * lax.rsqrt(ms + RMS_EPS) * g


def _sigmoid(x):
    return 1.0 / (1.0 + jnp.exp(-x))


def _adaln_kernel(c_ref, w_ref, b_ref, o_ref):
    cv = c_ref[...]
    s = cv * _sigmoid(cv)
    o_ref[0] = _dot3(s, w_ref[0]) + b_ref[0]


def _adaln(cpad, w_ada, b_ada):
    depth, d, n6 = w_ada.shape
    tn = 1536
    return pl.pallas_call(
        _adaln_kernel,
        grid=(depth, n6 // tn),
        in_specs=[pl.BlockSpec((16, d), lambda l, j: (0, 0)),
                  pl.BlockSpec((1, d, tn), lambda l, j: (l, 0, j)),
                  pl.BlockSpec((1, 1, tn), lambda l, j: (l, 0, j))],
        out_specs=pl.BlockSpec((1, 16, tn), lambda l, j: (l, 0, j)),
        out_shape=jax.ShapeDtypeStruct((depth, 16, n6), F32),
        compiler_params=pltpu.CompilerParams(dimension_semantics=("parallel", "parallel"),
                                             vmem_limit_bytes=VMEM_LIMIT),
        name="adaln",
    )(cpad, w_ada, b_ada.reshape(depth, 1, n6))


def _inproj_kernel(x_ref, mod_ref, g_ref, w_ref, cos_ref, sa_ref, sb_ref, gq_ref, gk_ref, mseg_ref,
                   q_ref, k_ref, v_ref):
    x = x_ref[...]
    sh = mod_ref[0, 0, 0:1, :]
    sc = mod_ref[0, 0, 1:2, :]
    h = (_rms(x, g_ref[...]) * (1.0 + sc) + sh).astype(BF16)
    cos = cos_ref[...]
    sa = sa_ref[...]
    sb = sb_ref[...]
    mseg = mseg_ref[...]

    def rope(b):
        return b * cos + pltpu.roll(b, LANES - 16, 1) * sa + pltpu.roll(b, 16, 1) * sb

    def qknorm(b, g):
        hi, lo = _split(b * b)
        ms = (_dot(hi, mseg) + _dot(lo, mseg)) * (1.0 / HEAD_DIM)
        return b * lax.rsqrt(ms + RMS_EPS) * g

    for c in range(ROPE_COLS // 256):
        p2 = _dot(h, w_ref[:, c * 256:(c + 1) * 256])
        for half in range(2):
            j = 2 * c + half
            p = p2[:, half * LANES:(half + 1) * LANES]
            if j in (4, 5):
                p = qknorm(p, gq_ref[...])
            if j == 12:
                p = qknorm(p, gk_ref[...])
            p = rope(p)
            if j < Q_COLS // LANES:
                q_ref[0, :, j * LANES:(j + 1) * LANES] = (p * (HEAD_DIM ** -0.5)).astype(BF16)
            else:
                jk = j - Q_COLS // LANES
                k_ref[0, :, jk * LANES:(jk + 1) * LANES] = p.astype(BF16)
    for c in range(V_COLS // 256):
        lo_c = ROPE_COLS + c * 256
        v_ref[0, :, c * 256:(c + 1) * 256] = _dot(h, w_ref[:, lo_c:lo_c + 256]).astype(BF16)


def _inproj(xflat, mod, g, w, cos, sa, sb, gq, gk, mseg, batch, nt, x_tile_map):
    d = xflat.shape[1]
    t_len = nt * TILE
    row = lambda b, t: (t, 0)
    const = lambda b, t: (0, 0)
    return pl.pallas_call(
        _inproj_kernel,
        grid=(batch, nt),
        in_specs=[pl.BlockSpec((TILE, d), lambda b, t: (x_tile_map(b, t), 0)),
                  pl.BlockSpec((1, 1, 8, d), lambda b, t: (b, jnp.minimum(t, 1), 0, 0)),
                  pl.BlockSpec((1, d), const),
                  pl.BlockSpec(w.shape, const),
                  pl.BlockSpec((TILE, LANES), row),
                  pl.BlockSpec((TILE, LANES), row),
                  pl.BlockSpec((TILE, LANES), row),
                  pl.BlockSpec((1, LANES), const),
                  pl.BlockSpec((1, LANES), const),
                  pl.BlockSpec((LANES, LANES), const)],
        out_specs=[pl.BlockSpec((1, TILE, Q_COLS), lambda b, t: (b, t, 0)),
                   pl.BlockSpec((1, TILE, K_COLS), lambda b, t: (b, t, 0)),
                   pl.BlockSpec((1, TILE, V_COLS), lambda b, t: (b, t, 0))],
        out_shape=[jax.ShapeDtypeStruct((batch, t_len, Q_COLS), BF16),
                   jax.ShapeDtypeStruct((batch, t_len, K_COLS), BF16),
                   jax.ShapeDtypeStruct((batch, t_len, V_COLS), BF16)],
        compiler_params=pltpu.CompilerParams(dimension_semantics=("parallel", "parallel"),
                                             vmem_limit_bytes=VMEM_LIMIT),
        name="inproj",
    )(xflat, mod, g, w, cos, sa, sb, gq, gk, mseg)


def _softmax_pv(qq, k_ref, v_ref, col, starts, s_ref, mask_fn=None, extra=None):
    half = KEY_CHUNK // 2
    mrun = None
    for j, st in enumerate(starts):
        s = _dot_nt(qq, k_ref[0, pl.ds(st, KEY_CHUNK), col:col + LANES])
        if mask_fn is not None:
            s = mask_fn(j, st, s)
        s_ref[:, j * KEY_CHUNK:(j + 1) * KEY_CHUNK] = s
        mj = jnp.maximum(s[:, :half], s[:, half:])
        mrun = mj if mrun is None else jnp.maximum(mrun, mj)
    m = jnp.max(mrun, axis=-1, keepdims=True)
    if extra is not None:
        m = jnp.maximum(m, extra)
    lrun = None
    acc = None
    for j, st in enumerate(starts):
        e = jnp.exp(s_ref[:, j * KEY_CHUNK:(j + 1) * KEY_CHUNK] - m)
        lj = e[:, :half] + e[:, half:]
        lrun = lj if lrun is None else lrun + lj
        pv = _dot(e.astype(BF16), v_ref[0, pl.ds(st, KEY_CHUNK), col:col + LANES])
        acc = pv if acc is None else acc + pv
    l = jnp.sum(lrun, axis=-1, keepdims=True)
    if extra is not None:
        l = l + jnp.exp(extra - m)
    return acc / l


def _all_chunks(n_keys):
    return [j * KEY_CHUNK for j in range(n_keys // KEY_CHUNK)]


def _half_masks(shape):
    lane = lax.broadcasted_iota(jnp.int32, shape, 1)
    return lane < HEAD_DIM, lane >= HEAD_DIM


def _stack_group_queries(q2, kv):
    lo_m, hi_m = _half_masks((TILE, LANES))
    m = lo_m if kv == 0 else hi_m
    zero = jnp.zeros((TILE, LANES), q2.dtype)
    return jnp.concatenate([jnp.where(m, q2[:, :LANES], zero), jnp.where(m, q2[:, LANES:], zero)], axis=0)


def _merge_kv_outputs(o_kv0, o_kv1):
    lo_m, _ = _half_masks((TILE, LANES))
    g0 = jnp.where(lo_m, o_kv0[:TILE], o_kv1[:TILE])
    g1 = jnp.where(lo_m, o_kv0[TILE:], o_kv1[TILE:])
    return jnp.concatenate([g0, g1], axis=1)


def _attn_kernel(sink_ref, lam_ref, subg_ref, qa_ref, qb_ref, qc_ref, ka_ref, kb_ref, kc_ref, va_ref, vb_ref, vc_ref,
                 o_ref, s_ref, *, t0, ctx_len, lambda_init):
    t = pl.program_id(1) + t0
    t_len = ka_ref.shape[1]
    band = 2 * TILE
    lo_m, hi_m = _half_masks((TILE, LANES))
    zero = jnp.zeros((TILE, LANES), BF16)
    row = lax.broadcasted_iota(jnp.int32, (2 * TILE, 1), 0)
    lv = lam_ref[...]
    lam = (jnp.exp(jnp.sum(lv[0:1] * lv[1:2], keepdims=True))
           - jnp.exp(jnp.sum(lv[2:3] * lv[3:4], keepdims=True)) + lambda_init)

    def sink_col(kv):
        return jnp.where(row < TILE, sink_ref[2 * kv], sink_ref[2 * kv + 1])

    def mixers(starts_ab, starts_c, mask_c):
        n_chain = 0
        for h in range(QA // LANES):
            q = qa_ref[0, :, h * LANES:(h + 1) * LANES]
            qq = jnp.concatenate([jnp.where(lo_m, q, zero), jnp.where(hi_m, q, zero)], axis=0)
            o = _softmax_pv(qq, ka_ref, va_ref, h * LANES, starts_ab, s_ref.at[n_chain % 2])
            n_chain += 1
            o = o[:TILE] - lam * o[TILE:]
            o_ref[0, :, h * LANES:(h + 1) * LANES] = (_rms(o, subg_ref[...]) * (1.0 - lambda_init)).astype(BF16)
        outs = []
        for kv in range(2):
            outs.append(_softmax_pv(_stack_group_queries(qb_ref[0], kv), kb_ref, vb_ref, 0, starts_ab,
                                    s_ref.at[n_chain % 2]))
            n_chain += 1
        o_ref[0, :, QA:QA + QB] = _merge_kv_outputs(*outs).astype(BF16)
        outs = []
        for kv in range(2):
            outs.append(_softmax_pv(_stack_group_queries(qc_ref[0], kv), kc_ref, vc_ref, 0, starts_c,
                                    s_ref.at[n_chain % 2], mask_c, sink_col(kv)))
            n_chain += 1
        o_ref[0, :, QA + QB:] = _merge_kv_outputs(*outs).astype(BF16)

    def ctx_step():
        mixers([0], [0], None)

    def latent_step():
        q_start = (t - 1) * TILE
        u0 = pl.multiple_of(jnp.minimum(q_start + ctx_len - WINDOW, t_len - band), WINDOW)
        q_pos = q_start + (lax.broadcasted_iota(jnp.int32, (2 * TILE, KEY_CHUNK), 0) & (TILE - 1))
        key_lane = lax.broadcasted_iota(jnp.int32, (2 * TILE, KEY_CHUNK), 1)

        def mask_c(j, st, s):
            if j == 0:
                return s
            k_pos = (st - ctx_len) + key_lane
            valid = (jnp.abs(k_pos - q_pos) <= WINDOW) & (k_pos >= 0)
            return jnp.where(valid, s, NEG_BIG)

        mixers(_all_chunks(t_len), [0] + [u0 + i * KEY_CHUNK for i in range(band // KEY_CHUNK)], mask_c)

    if t0 == 0:
        pl.when(t == 0)(ctx_step)
        pl.when(t > 0)(latent_step)
    else:
        latent_step()


def _attention(sink, lamv, subg, q, k, v, t0, lambda_init):
    batch, t_len, _ = q.shape
    nt = t_len // TILE
    kern = functools.partial(_attn_kernel, t0=t0, ctx_len=TILE, lambda_init=lambda_init)
    const = lambda b, t: (0, 0)
    qspec = lambda width, blk: pl.BlockSpec((1, TILE, width), lambda b, t: (b, t + t0, blk))
    kvspec = lambda width, blk: pl.BlockSpec((1, t_len, width), lambda b, t: (b, 0, blk))
    return pl.pallas_call(
        kern,
        grid=(batch, nt - t0),
        in_specs=[pl.BlockSpec(memory_space=pltpu.SMEM),
                  pl.BlockSpec(lamv.shape, const),
                  pl.BlockSpec((1, LANES), const),
                  qspec(QA, 0), qspec(QB, QA // QB), qspec(QC, (QA + QB) // QC),
                  kvspec(KA, 0), kvspec(KB, KA // KB), kvspec(KC, (KA + KB) // KC),
                  kvspec(KA, 0), kvspec(KB, KA // KB), kvspec(KC, (KA + KB) // KC)],
        out_specs=pl.BlockSpec((1, TILE, Q_COLS), lambda b, t: (b, t + t0, 0)),
        out_shape=jax.ShapeDtypeStruct((batch, t_len, Q_COLS), BF16),
        scratch_shapes=[pltpu.VMEM((2, 2 * TILE, t_len), F32)],
        compiler_params=pltpu.CompilerParams(dimension_semantics=("parallel", "parallel"),
                                             vmem_limit_bytes=VMEM_LIMIT),
        name="attention",
    )(sink, lamv, subg, q, q, q, k, k, k, v, v, v)


def _route(scores, bias):
    lane = lax.broadcasted_iota(jnp.int32, scores.shape, 1)
    sel = scores + bias
    in_group = lane & (EXPERTS_PER_GROUP - 1)
    group = lane >> 2

    def neighbours(x, idx, step, span):
        for d in (1, 2, 3):
            fwd = (idx + d) < 4
            y = jnp.where(fwd, pltpu.roll(x, LANES - d * step, 1), pltpu.roll(x, span - d * step, 1))
            yield y, fwd

    rank = jnp.zeros_like(sel)
    for y, fwd in neighbours(sel, in_group, 1, EXPERTS_PER_GROUP):
        beats = (y > sel) | ((y == sel) & jnp.logical_not(fwd))
        rank = rank + jnp.where(beats, 1.0, 0.0)
    top2 = jnp.where(rank < 2.0, sel, 0.0)
    gsum = top2
    for y, _ in neighbours(top2, in_group, 1, EXPERTS_PER_GROUP):
        gsum = gsum + y
    grank = jnp.zeros_like(sel)
    for y, fwd in neighbours(gsum, group, EXPERTS_PER_GROUP, N_EXPERTS):
        beats = (y > gsum) | ((y == gsum) & jnp.logical_not(fwd))
        grank = grank + jnp.where(beats, 1.0, 0.0)
    best = (grank < 0.5) & (lane < N_EXPERTS)
    chosen = best & (rank < 2.0)
    w = jnp.where(chosen, scores, 0.0)
    gates = w / jnp.sum(w, axis=-1, keepdims=True)
    gid = jnp.sum(jnp.where(best & (in_group == 0), group.astype(F32), 0.0), axis=-1, keepdims=True)
    return jnp.where(lane == GROUP_LANE, gid, gates)


def _pack_bf16_pairs(y):
    n = y.shape[1] // 2
    yb = y.astype(BF16).astype(F32)
    lo = lax.bitcast_convert_type(yb[:, :n], jnp.uint32) >> 16
    hi = lax.bitcast_convert_type(yb[:, n:], jnp.uint32) & jnp.uint32(0xFFFF0000)
    return hi | lo


def _unpack_bf16_pairs(w):
    lo = lax.bitcast_convert_type(w << 16, F32)
    hi = lax.bitcast_convert_type(w & jnp.uint32(0xFFFF0000), F32)
    return lo, hi


def _outproj_kernel(x_ref, mix_ref, w_ref, mod_ref, g_ref, wrh_ref, wrl_ref, rb_ref, x1_ref, h2_ref, gates_ref):
    x1 = x_ref[...] + mod_ref[0, 0, 2:3, :] * _dot(mix_ref[0], w_ref[...])
    x1_ref[...] = x1
    h2 = _rms(x1, g_ref[...]) * (1.0 + mod_ref[0, 0, 4:5, :]) + mod_ref[0, 0, 3:4, :]
    h2_ref[...] = _pack_bf16_pairs(h2)
    hi, lo = _split(h2)
    logits = _dot(hi, wrh_ref[...]) + _dot(hi, wrl_ref[...]) + _dot(lo, wrh_ref[...])
    gates_ref[...] = _route(_sigmoid(logits), rb_ref[...])


def _outproj(xflat, mix, w, mod, g, wrh, wrl, rb, t0, x_tile_map, out_tile_map, n_out):
    batch, t_len, _ = mix.shape
    nt = t_len // TILE
    d = xflat.shape[1]
    const = lambda b, t: (0, 0)
    tok = lambda b, t: (b, t + t0, 0)
    out_row = lambda b, t: (out_tile_map(b, t + t0), 0)
    return pl.pallas_call(
        _outproj_kernel,
        grid=(batch, nt - t0),
        in_specs=[pl.BlockSpec((TILE, d), lambda b, t: (x_tile_map(b, t + t0), 0)),
                  pl.BlockSpec((1, TILE, Q_COLS), tok),
                  pl.BlockSpec(w.shape, const),
                  pl.BlockSpec((1, 1, 8, d), lambda b, t: (b, jnp.minimum(t + t0, 1), 0, 0)),
                  pl.BlockSpec((1, d), const),
                  pl.BlockSpec((d, LANES), const),
                  pl.BlockSpec((d, LANES), const),
                  pl.BlockSpec((1, LANES), const)],
        out_specs=[pl.BlockSpec((TILE, d), out_row),
                   pl.BlockSpec((TILE, d // 2), out_row),
                   pl.BlockSpec((TILE, LANES), out_row)],
        out_shape=[jax.ShapeDtypeStruct((n_out, d), F32),
                   jax.ShapeDtypeStruct((n_out, d // 2), jnp.uint32),
                   jax.ShapeDtypeStruct((n_out, LANES), F32)],
        compiler_params=pltpu.CompilerParams(dimension_semantics=("parallel", "parallel"),
                                             vmem_limit_bytes=VMEM_LIMIT),
        name="outproj",
    )(xflat, mix, w, mod, g, wrh, wrl, rb)


def _sc_workers():
    info = plsc.get_sparse_core_info()
    return info.num_cores, info.num_subcores


def _sc_dispatch(hpk, gates, pos2d, n_out):
    n, w = hpk.shape
    gw = gates.shape[1]
    nc, ns = _sc_workers()
    steps = n // (nc * ns * SC_WIN)
    mesh = plsc.VectorSubcoreMesh(core_axis_name="c", subcore_axis_name="s")

    @functools.partial(
        pl.kernel, mesh=mesh,
        out_type=(jax.ShapeDtypeStruct((n_out, w), hpk.dtype), jax.ShapeDtypeStruct((n_out, gw), gates.dtype)),
        scratch_types=[pltpu.VMEM((1, SC_WIN), jnp.int32), pltpu.VMEM((SC_WIN, w), hpk.dtype),
                       pltpu.VMEM((SC_WIN, gw), gates.dtype)],
        name="sc_dispatch")
    def k(h_hbm, g_hbm, pos_hbm, hs_hbm, gs_hbm, idx_v, rows_v, grow_v):
        wid = lax.axis_index("s") * nc + lax.axis_index("c")

        @pl.loop(0, steps)
        def _(j):
            blk = wid * steps + j
            off = pl.multiple_of(blk * SC_WIN, SC_WIN)
            pltpu.sync_copy(pos_hbm.at[pl.ds(blk, 1)], idx_v)
            pltpu.sync_copy(h_hbm.at[pl.ds(off, SC_WIN)], rows_v)
            pltpu.sync_copy(g_hbm.at[pl.ds(off, SC_WIN)], grow_v)
            pltpu.sync_copy(rows_v, hs_hbm.at[idx_v.at[0]])
            pltpu.sync_copy(grow_v, gs_hbm.at[idx_v.at[0]])

    return k(hpk, gates, pos2d)


def _sc_combine(ys, pos2d, n):
    w = ys.shape[1]
    nc, ns = _sc_workers()
    steps = n // (nc * ns * SC_WIN)
    mesh = plsc.VectorSubcoreMesh(core_axis_name="c", subcore_axis_name="s")

    @functools.partial(
        pl.kernel, mesh=mesh,
        out_type=jax.ShapeDtypeStruct((n, w), ys.dtype),
        scratch_types=[pltpu.VMEM((1, SC_WIN), jnp.int32), pltpu.VMEM((SC_WIN, w), ys.dtype)],
        name="sc_combine")
    def k(y_hbm, pos_hbm, o_hbm, idx_v, rows_v):
        wid = lax.axis_index("s") * nc + lax.axis_index("c")

        @pl.loop(0, steps)
        def _(j):
            blk = wid * steps + j
            off = pl.multiple_of(blk * SC_WIN, SC_WIN)
            pltpu.sync_copy(pos_hbm.at[pl.ds(blk, 1)], idx_v)
            pltpu.sync_copy(y_hbm.at[idx_v.at[0]], rows_v)
            pltpu.sync_copy(rows_v, o_hbm.at[pl.ds(off, SC_WIN)])

    return k(ys, pos2d)


def _bucket_plan(group, n_pad):
    n_groups = N_EXPERTS // EXPERTS_PER_GROUP
    onehot = (group[:, None] == jnp.arange(n_groups, dtype=jnp.int32)[None, :]).astype(jnp.int32)
    csum = jnp.cumsum(onehot, axis=0)
    counts = csum[-1]
    padded = ((counts + FFN_TILE - 1) // FFN_TILE) * FFN_TILE
    ends = jnp.cumsum(padded)
    starts = ends - padded
    pos = starts[group] + jnp.sum(csum * onehot, axis=1) - 1
    tile_row = jnp.arange(n_pad // FFN_TILE, dtype=jnp.int32) * FFN_TILE
    tile_group = jnp.minimum(jnp.sum((tile_row[:, None] >= ends[None, :]).astype(jnp.int32), axis=1), n_groups - 1)
    tile_used = (tile_row < (starts + counts)[tile_group]).astype(jnp.int32)
    return pos.astype(jnp.int32), tile_group.astype(jnp.int32), tile_used


def _ffn_kernel(tg_ref, tu_ref, hs_ref, gs_ref, wg_ref, wu_ref, wd_ref, ys_ref):
    i = pl.program_id(0)

    @pl.when(tu_ref[i] > 0)
    def _():
        lo, hi = _unpack_bf16_pairs(hs_ref[...])
        lo = lo.astype(BF16)
        hi = hi.astype(BF16)
        half = lo.shape[1]
        a = _dot(lo, wg_ref[0, :half, :]) + _dot(hi, wg_ref[0, half:, :])
        u = _dot(lo, wu_ref[0, :half, :]) + _dot(hi, wu_ref[0, half:, :])
        hid = (a * _sigmoid(a)) * u
        gs = gs_ref[...]
        lane = lax.broadcasted_iota(jnp.int32, gs.shape, 1)
        first = tg_ref[i] * EXPERTS_PER_GROUP
        de = hid.shape[1] // EXPERTS_PER_GROUP
        y = None
        for e in range(EXPERTS_PER_GROUP):
            gate = jnp.sum(jnp.where(lane == first + e, gs, 0.0), axis=-1, keepdims=True)
            part = _dot((hid[:, e * de:(e + 1) * de] * gate).astype(BF16), wd_ref[0, e * de:(e + 1) * de, :])
            y = part if y is None else y + part
        ys_ref[...] = _pack_bf16_pairs(y)


def _ffn(tile_group, tile_used, hs, gs, wg, wu, wd):
    n_pad, half = hs.shape
    d = 2 * half
    wspec = pl.BlockSpec((1,) + wg.shape[1:], lambda i, tg, tu: (tg[i], 0, 0))
    return pl.pallas_call(
        _ffn_kernel,
        grid_spec=pltpu.PrefetchScalarGridSpec(
            num_scalar_prefetch=2,
            grid=(n_pad // FFN_TILE,),
            in_specs=[pl.BlockSpec((FFN_TILE, half), lambda i, tg, tu: (i, 0)),
                      pl.BlockSpec((FFN_TILE, LANES), lambda i, tg, tu: (i, 0)),
                      wspec, wspec,
                      pl.BlockSpec((1,) + wd.shape[1:], lambda i, tg, tu: (tg[i], 0, 0))],
            out_specs=pl.BlockSpec((FFN_TILE, half), lambda i, tg, tu: (i, 0))),
        out_shape=jax.ShapeDtypeStruct((n_pad, half), jnp.uint32),
        compiler_params=pltpu.CompilerParams(dimension_semantics=("arbitrary",), vmem_limit_bytes=VMEM_LIMIT),
        name="ffn",
    )(tile_group, tile_used, hs, gs, wg, wu, wd)


def _residual_kernel(x_ref, y_ref, mod_ref, fg_ref, o_ref, *, final_norm):
    lo, hi = _unpack_bf16_pairs(y_ref[...])
    y = x_ref[...] + mod_ref[0, 0, 5:6, :] * jnp.concatenate([lo, hi], axis=1)
    if final_norm:
        y = _rms(y, fg_ref[...])
    o_ref[...] = y


def _residual(x1, yg, mod, fg, mod_tile_map, final_norm):
    n, d = x1.shape
    row = lambda i: (i, 0)
    return pl.pallas_call(
        functools.partial(_residual_kernel, final_norm=final_norm),
        grid=(n // MOE_TILE,),
        in_specs=[pl.BlockSpec((MOE_TILE, d), row),
                  pl.BlockSpec((MOE_TILE, d // 2), row),
                  pl.BlockSpec((1, 1, 8, d), lambda i: mod_tile_map(i) + (0, 0)),
                  pl.BlockSpec((1, d), lambda i: (0, 0))],
        out_specs=pl.BlockSpec((MOE_TILE, d), row),
        out_shape=jax.ShapeDtypeStruct((n, d), F32),
        compiler_params=pltpu.CompilerParams(dimension_semantics=("parallel",), vmem_limit_bytes=VMEM_LIMIT),
        name="residual",
    )(x1, yg, mod, fg)


def _moe(h2, gates, x1, mod, wg, wu, wd, fg, mod_tile_map, final_norm):
    n = x1.shape[0]
    n_pad = n + (N_EXPERTS // EXPERTS_PER_GROUP) * FFN_TILE
    pos, tile_group, tile_used = _bucket_plan(gates[:, GROUP_LANE].astype(jnp.int32), n_pad)
    pos2d = pos.reshape(n // SC_WIN, SC_WIN)
    hs, gs = _sc_dispatch(h2, gates, pos2d, n_pad)
    ys = _ffn(tile_group, tile_used, hs, gs, wg, wu, wd)
    yg = _sc_combine(ys, pos2d, n)
    return _residual(x1, yg, mod, fg, mod_tile_map, final_norm)


def _gqa_perm():
    return np.arange(256).reshape(2, 2, HEAD_DIM).transpose(1, 0, 2).reshape(-1)


def _in_col_perm():
    g = _gqa_perm()
    q = np.concatenate([np.arange(512), 512 + g, 768 + g])
    kv0 = 1024
    k = np.concatenate([kv0 + np.arange(512), kv0 + 1024 + np.arange(128), kv0 + 1280 + np.arange(128)])
    v = np.concatenate([kv0 + 512 + np.arange(512), kv0 + 1152 + np.arange(128), kv0 + 1408 + np.arange(128)])
    return np.concatenate([q, k, v])


def _out_row_perm():
    g = _gqa_perm()
    return np.concatenate([np.arange(512), 512 + g, 768 + g])


def _rope_tables(seq, ctx_len):
    rows = seq // GRID_W
    row_pos = jnp.repeat(jnp.arange(rows, dtype=F32), GRID_W)
    col_pos = jnp.tile(jnp.arange(GRID_W, dtype=F32), rows)
    axis_dim = HEAD_DIM // 2
    inv_freq = ROPE_BASE ** (-jnp.arange(0, axis_dim, 2, dtype=F32) / axis_dim)
    ang_r = row_pos[:, None] * inv_freq[None, :]
    ang_c = col_pos[:, None] * inv_freq[None, :]
    z = jnp.zeros_like(ang_r)
    cos = jnp.concatenate([jnp.cos(ang_r)] * 2 + [jnp.cos(ang_c)] * 2, axis=-1)
    sin_a = jnp.concatenate([-jnp.sin(ang_r), z, -jnp.sin(ang_c), z], axis=-1)
    sin_b = jnp.concatenate([z, jnp.sin(ang_r), z, jnp.sin(ang_c)], axis=-1)

    def full(tab, fill):
        tab = jnp.tile(tab, (1, LANES // HEAD_DIM))
        return jnp.concatenate([jnp.full((ctx_len, LANES), fill, F32), tab], axis=0)

    return full(cos, 1.0), full(sin_a, 0.0), full(sin_b, 0.0)


def kernel(x, c, ctx, c_ctx, w_ada, b_ada, norm1_g, norm2_g, w_in, w_out, lam_q1, lam_k1, lam_q2, lam_k2,
           subln_g, q_norm_g, k_norm_g, sink, w_router, router_bias, w_gate, w_up, w_down, final_g):
    batch, seq, d = x.shape
    ctx_len = ctx.shape[1]
    depth = w_in.shape[0]
    assert ctx_len == TILE and seq % MOE_TILE == 0 and (batch * ctx_len) % MOE_TILE == 0 and batch <= 15
    nt = (ctx_len + seq) // TILE
    lt = seq // TILE
    n_ctx = batch * ctx_len

    cpad = jnp.zeros((16, d), F32).at[:batch].set(c).at[batch].set(c_ctx)
    mod_all = _adaln(cpad, w_ada, b_ada)
    cos, sin_a, sin_b = _rope_tables(seq, ctx_len)
    mseg = jnp.asarray(np.kron(np.eye(2), np.ones((HEAD_DIM, HEAD_DIM))), BF16)
    in_perm = _in_col_perm()
    out_perm = _out_row_perm()
    wr = jnp.zeros((d, LANES), F32).at[:, :N_EXPERTS].set(w_router)
    wrh = wr.astype(BF16)
    wrl = (wr - wrh.astype(F32)).astype(BF16)
    rb = jnp.zeros((1, LANES), F32).at[0, :N_EXPERTS].set(router_bias)

    xflat = jnp.concatenate([ctx.reshape(n_ctx, d), x.reshape(batch * seq, d)], axis=0)
    with_ctx_map = lambda b, t: jnp.where(t == 0, b, batch + b * lt + t - 1)
    latent_map = lambda b, t: b * lt + t - 1

    for l in range(depth):
        last = l == depth - 1
        lambda_init = 0.8 - 0.6 * math.exp(-0.3 * l)
        m6 = mod_all[l].reshape(16, 6, d)
        m8 = jnp.concatenate([m6, jnp.zeros((16, 2, d), F32)], axis=1)
        mod = jnp.stack([jnp.broadcast_to(m8[batch], (batch, 8, d)), m8[:batch]], axis=1)

        w_in_p = w_in[l][:, in_perm].astype(BF16)
        w_out_p = w_out[l][out_perm, :].astype(BF16)
        gq = jnp.tile(q_norm_g[l], 2).reshape(1, LANES)
        gk = jnp.tile(k_norm_g[l], 2).reshape(1, LANES)
        q, k, v = _inproj(xflat, mod, norm1_g[l].reshape(1, d), w_in_p, cos, sin_a, sin_b, gq, gk, mseg,
                          batch, nt, with_ctx_map)

        t0 = 1 if last else 0
        lamv = jnp.stack([lam_q1[l], lam_k1[l], lam_q2[l], lam_k2[l]])
        mix = _attention(sink[l], lamv, subln_g[l].reshape(1, LANES), q, k, v, t0, lambda_init)

        if last:
            n_out = batch * seq
            out_map = latent_map
            mod_tile_map = lambda i: (i // (seq // MOE_TILE), 1)
        else:
            n_out = xflat.shape[0]
            out_map = with_ctx_map
            nc = n_ctx // MOE_TILE
            mod_tile_map = lambda i: (jnp.maximum(i - nc, 0) // (seq // MOE_TILE), jnp.minimum(i // nc, 1))
        x1, h2, gates = _outproj(xflat, mix, w_out_p, mod, norm2_g[l].reshape(1, d), wrh, wrl, rb,
                                 t0, with_ctx_map, out_map, n_out)
        n_groups = N_EXPERTS // EXPERTS_PER_GROUP
        de = w_gate.shape[-1]

        def by_group(w):
            w = w.reshape(n_groups, EXPERTS_PER_GROUP, d, de).transpose(0, 2, 1, 3)
            return w.reshape(n_groups, d, EXPERTS_PER_GROUP * de).astype(BF16)

        wd_g = w_down[l].reshape(n_groups, EXPERTS_PER_GROUP * de, d).astype(BF16)
        xflat = _moe(h2, gates, x1, mod, by_group(w_gate[l]), by_group(w_up[l]), wd_g,
                     final_g.reshape(1, d), mod_tile_map, final_norm=last)
    return xflat.reshape(batch, seq, d)
```

```python
import functools
import math

import numpy as np
import jax
import jax.numpy as jnp
from jax import lax
from jax.experimental import pallas as pl
from jax.experimental.pallas import tpu as pltpu
from jax.experimental.pallas import tpu_sc as plsc

F32 = jnp.float32
BF16 = jnp.bfloat16

HEAD_DIM = 64
GRID_W = 64
ROPE_BASE = 10000.0
RMS_EPS = 1e-6
WINDOW = 128
N_EXPERTS = 16
EXPERTS_PER_GROUP = 4
LANES = 128
TILE = 256
PAIR = 2
FFN_TILE = 512
SC_WIN = 64
GROUP_LANE = 16
NEG_BIG = -1e30
KEY_CHUNK = 256
VMEM_LIMIT = 52 * 1024 * 1024

QA, QB, QC = 512, 256, 256
KA, KB, KC = 512, 128, 128
Q_COLS = QA + QB + QC
K_COLS = KA + KB + KC
V_COLS = 768
ROPE_COLS = Q_COLS + K_COLS


def _dot(a, b):
    return jnp.dot(a, b, preferred_element_type=F32)


def _dot_nt(a, b):
    return lax.dot_general(a, b, (((1,), (1,)), ((), ())), preferred_element_type=F32)


def _split(a):
    hi = a.astype(BF16)
    lo = (a - hi.astype(F32)).astype(BF16)
    return hi, lo


def _dot3(a, b):
    ah, al = _split(a)
    bh, bl = _split(b)
    return _dot(ah, bh) + _dot(ah, bl) + _dot(al, bh)


def _rms(x, g):
    ms = jnp.mean(x * x, axis=-1, keepdims=True)
    return x * lax.rsqrt(ms + RMS_EPS) * g


def _sigmoid(x):
    return 1.0 / (1.0 + jnp.exp(-x))


def _adaln_kernel(c_ref, w_ref, b_ref, o_ref):
    cv = c_ref[...]
    s = cv * _sigmoid(cv)
    o_ref[0] = _dot3(s, w_ref[0]) + b_ref[0]


def _adaln(cpad, w_ada, b_ada):
    depth, d, n6 = w_ada.shape
    tn = 1536
    return pl.pallas_call(
        _adaln_kernel,
        grid=(depth, n6 // tn),
        in_specs=[pl.BlockSpec((16, d), lambda l, j: (0, 0)),
                  pl.BlockSpec((1, d, tn), lambda l, j: (l, 0, j)),
                  pl.BlockSpec((1, 1, tn), lambda l, j: (l, 0, j))],
        out_specs=pl.BlockSpec((1, 16, tn), lambda l, j: (l, 0, j)),
        out_shape=jax.ShapeDtypeStruct((depth, 16, n6), F32),
        compiler_params=pltpu.CompilerParams(dimension_semantics=("parallel", "parallel"),
                                             vmem_limit_bytes=VMEM_LIMIT),
        name="adaln",
    )(cpad, w_ada, b_ada.reshape(depth, 1, n6))


def _inproj_kernel(x_ref, mod_ref, g_ref, w_ref, cos_ref, sa_ref, sb_ref, gq_ref, gk_ref, mseg_ref,
                   q_ref, k_ref, v_ref):
    nb = x_ref.shape[0]
    h = jnp.concatenate(
        [(_rms(x_ref[i], g_ref[...]) * (1.0 + mod_ref[i, 0, 1:2, :]) + mod_ref[i, 0, 0:1, :]).astype(BF16)
         for i in range(nb)], axis=0)
    cos = jnp.concatenate([cos_ref[...]] * nb, axis=0)
    sa = jnp.concatenate([sa_ref[...]] * nb, axis=0)
    sb = jnp.concatenate([sb_ref[...]] * nb, axis=0)
    mseg = mseg_ref[...]

    def put(ref, cols, val):
        for i in range(nb):
            ref[i, :, cols] = val[i * TILE:(i + 1) * TILE]

    def rope(b):
        return b * cos + pltpu.roll(b, LANES - 16, 1) * sa + pltpu.roll(b, 16, 1) * sb

    def qknorm(b, g):
        hi, lo = _split(b * b)
        ms = (_dot(hi, mseg) + _dot(lo, mseg)) * (1.0 / HEAD_DIM)
        return b * lax.rsqrt(ms + RMS_EPS) * g

    for c in range(ROPE_COLS // 256):
        p2 = _dot(h, w_ref[:, c * 256:(c + 1) * 256])
        for half in range(2):
            j = 2 * c + half
            p = p2[:, half * LANES:(half + 1) * LANES]
            if j in (4, 5):
                p = qknorm(p, gq_ref[...])
            if j == 12:
                p = qknorm(p, gk_ref[...])
            p = rope(p)
            if j < Q_COLS // LANES:
                put(q_ref, slice(j * LANES, (j + 1) * LANES), (p * (HEAD_DIM ** -0.5)).astype(BF16))
            else:
                jk = j - Q_COLS // LANES
                put(k_ref, slice(jk * LANES, (jk + 1) * LANES), p.astype(BF16))
    for c in range(V_COLS // 256):
        lo_c = ROPE_COLS + c * 256
        put(v_ref, slice(c * 256, (c + 1) * 256), _dot(h, w_ref[:, lo_c:lo_c + 256]).astype(BF16))


def _inproj(xcat, mod, g, w, cos, sa, sb, gq, gk, mseg):
    batch, t_len, d = xcat.shape
    nt = t_len // TILE
    nb = PAIR
    row = lambda b, t: (t, 0)
    const = lambda b, t: (0, 0)
    tok = lambda b, t: (b, t, 0)
    return pl.pallas_call(
        _inproj_kernel,
        grid=(batch // nb, nt),
        in_specs=[pl.BlockSpec((nb, TILE, d), tok),
                  pl.BlockSpec((nb, 1, 8, d), lambda b, t: (b, jnp.minimum(t, 1), 0, 0)),
                  pl.BlockSpec((1, d), const),
                  pl.BlockSpec(w.shape, const),
                  pl.BlockSpec((TILE, LANES), row),
                  pl.BlockSpec((TILE, LANES), row),
                  pl.BlockSpec((TILE, LANES), row),
                  pl.BlockSpec((1, LANES), const),
                  pl.BlockSpec((1, LANES), const),
                  pl.BlockSpec((LANES, LANES), const)],
        out_specs=[pl.BlockSpec((nb, TILE, Q_COLS), tok),
                   pl.BlockSpec((nb, TILE, K_COLS), tok),
                   pl.BlockSpec((nb, TILE, V_COLS), tok)],
        out_shape=[jax.ShapeDtypeStruct((batch, t_len, Q_COLS), BF16),
                   jax.ShapeDtypeStruct((batch, t_len, K_COLS), BF16),
                   jax.ShapeDtypeStruct((batch, t_len, V_COLS), BF16)],
        compiler_params=pltpu.CompilerParams(dimension_semantics=("parallel", "parallel"),
                                             vmem_limit_bytes=VMEM_LIMIT),
        name="inproj",
    )(xcat, mod, g, w, cos, sa, sb, gq, gk, mseg)


def _softmax_pv(qq, k_ref, v_ref, col, starts, s_ref, mask_fn=None, extra=None):
    half = KEY_CHUNK // 2
    mrun = None
    for j, st in enumerate(starts):
        s = _dot_nt(qq, k_ref[0, pl.ds(st, KEY_CHUNK), col:col + LANES])
        if mask_fn is not None:
            s = mask_fn(j, st, s)
        s_ref[:, j * KEY_CHUNK:(j + 1) * KEY_CHUNK] = s
        mj = jnp.maximum(s[:, :half], s[:, half:])
        mrun = mj if mrun is None else jnp.maximum(mrun, mj)
    m = jnp.max(mrun, axis=-1, keepdims=True)
    if extra is not None:
        m = jnp.maximum(m, extra)
    lrun = None
    acc = None
    for j, st in enumerate(starts):
        e = jnp.exp(s_ref[:, j * KEY_CHUNK:(j + 1) * KEY_CHUNK] - m)
        lj = e[:, :half] + e[:, half:]
        lrun = lj if lrun is None else lrun + lj
        pv = _dot(e.astype(BF16), v_ref[0, pl.ds(st, KEY_CHUNK), col:col + LANES])
        acc = pv if acc is None else acc + pv
    l = jnp.sum(lrun, axis=-1, keepdims=True)
    if extra is not None:
        l = l + jnp.exp(extra - m)
    return acc / l


def _all_chunks(n_keys):
    return [j * KEY_CHUNK for j in range(n_keys // KEY_CHUNK)]


def _half_masks(shape):
    lane = lax.broadcasted_iota(jnp.int32, shape, 1)
    return lane < HEAD_DIM, lane >= HEAD_DIM


def _stack_group_queries(q2, kv):
    lo_m, hi_m = _half_masks((TILE, LANES))
    m = lo_m if kv == 0 else hi_m
    zero = jnp.zeros((TILE, LANES), q2.dtype)
    return jnp.concatenate([jnp.where(m, q2[:, :LANES], zero), jnp.where(m, q2[:, LANES:], zero)], axis=0)


def _merge_kv_outputs(o_kv0, o_kv1):
    lo_m, _ = _half_masks((TILE, LANES))
    g0 = jnp.where(lo_m, o_kv0[:TILE], o_kv1[:TILE])
    g1 = jnp.where(lo_m, o_kv0[TILE:], o_kv1[TILE:])
    return jnp.concatenate([g0, g1], axis=1)


def _attn_kernel(sink_ref, lam_ref, subg_ref, qa_ref, qb_ref, qc_ref, ka_ref, kb_ref, kc_ref, va_ref, vb_ref, vc_ref,
                 o_ref, s_ref, *, t0, ctx_len, lambda_init):
    t = pl.program_id(1) + t0
    t_len = ka_ref.shape[1]
    band = 2 * TILE
    lo_m, hi_m = _half_masks((TILE, LANES))
    zero = jnp.zeros((TILE, LANES), BF16)
    row = lax.broadcasted_iota(jnp.int32, (2 * TILE, 1), 0)
    lv = lam_ref[...]
    lam = (jnp.exp(jnp.sum(lv[0:1] * lv[1:2], keepdims=True))
           - jnp.exp(jnp.sum(lv[2:3] * lv[3:4], keepdims=True)) + lambda_init)

    def sink_col(kv):
        return jnp.where(row < TILE, sink_ref[2 * kv], sink_ref[2 * kv + 1])

    def mixers(starts_ab, starts_c, mask_c):
        n_chain = 0
        for h in range(QA // LANES):
            q = qa_ref[0, :, h * LANES:(h + 1) * LANES]
            qq = jnp.concatenate([jnp.where(lo_m, q, zero), jnp.where(hi_m, q, zero)], axis=0)
            o = _softmax_pv(qq, ka_ref, va_ref, h * LANES, starts_ab, s_ref.at[n_chain % 2])
            n_chain += 1
            o = o[:TILE] - lam * o[TILE:]
            o_ref[0, :, h * LANES:(h + 1) * LANES] = (_rms(o, subg_ref[...]) * (1.0 - lambda_init)).astype(BF16)
        outs = []
        for kv in range(2):
            outs.append(_softmax_pv(_stack_group_queries(qb_ref[0], kv), kb_ref, vb_ref, 0, starts_ab,
                                    s_ref.at[n_chain % 2]))
            n_chain += 1
        o_ref[0, :, QA:QA + QB] = _merge_kv_outputs(*outs).astype(BF16)
        outs = []
        for kv in range(2):
            outs.append(_softmax_pv(_stack_group_queries(qc_ref[0], kv), kc_ref, vc_ref, 0, starts_c,
                                    s_ref.at[n_chain % 2], mask_c, sink_col(kv)))
            n_chain += 1
        o_ref[0, :, QA + QB:] = _merge_kv_outputs(*outs).astype(BF16)

    def ctx_step():
        mixers([0], [0], None)

    def latent_step():
        q_start = (t - 1) * TILE
        u0 = pl.multiple_of(jnp.minimum(q_start + ctx_len - WINDOW, t_len - band), WINDOW)
        q_pos = q_start + (lax.broadcasted_iota(jnp.int32, (2 * TILE, KEY_CHUNK), 0) & (TILE - 1))
        key_lane = lax.broadcasted_iota(jnp.int32, (2 * TILE, KEY_CHUNK), 1)

        def mask_c(j, st, s):
            if j == 0:
                return s
            k_pos = (st - ctx_len) + key_lane
            valid = (jnp.abs(k_pos - q_pos) <= WINDOW) & (k_pos >= 0)
            return jnp.where(valid, s, NEG_BIG)

        mixers(_all_chunks(t_len), [0] + [u0 + i * KEY_CHUNK for i in range(band // KEY_CHUNK)], mask_c)

    if t0 == 0:
        pl.when(t == 0)(ctx_step)
        pl.when(t > 0)(latent_step)
    else:
        latent_step()


def _attention(sink, lamv, subg, q, k, v, t0, lambda_init):
    batch, t_len, _ = q.shape
    nt = t_len // TILE
    kern = functools.partial(_attn_kernel, t0=t0, ctx_len=TILE, lambda_init=lambda_init)
    const = lambda b, t: (0, 0)
    qspec = lambda width, blk: pl.BlockSpec((1, TILE, width), lambda b, t: (b, t + t0, blk))
    kvspec = lambda width, blk: pl.BlockSpec((1, t_len, width), lambda b, t: (b, 0, blk))
    return pl.pallas_call(
        kern,
        grid=(batch, nt - t0),
        in_specs=[pl.BlockSpec(memory_space=pltpu.SMEM),
                  pl.BlockSpec(lamv.shape, const),
                  pl.BlockSpec((1, LANES), const),
                  qspec(QA, 0), qspec(QB, QA // QB), qspec(QC, (QA + QB) // QC),
                  kvspec(KA, 0), kvspec(KB, KA // KB), kvspec(KC, (KA + KB) // KC),
                  kvspec(KA, 0), kvspec(KB, KA // KB), kvspec(KC, (KA + KB) // KC)],
        out_specs=pl.BlockSpec((1, TILE, Q_COLS), lambda b, t: (b, t + t0, 0)),
        out_shape=jax.ShapeDtypeStruct((batch, t_len, Q_COLS), BF16),
        scratch_shapes=[pltpu.VMEM((2, 2 * TILE, t_len), F32)],
        compiler_params=pltpu.CompilerParams(dimension_semantics=("parallel", "parallel"),
                                             vmem_limit_bytes=VMEM_LIMIT),
        name="attention",
    )(sink, lamv, subg, q, q, q, k, k, k, v, v, v)


def _route(scores, bias):
    lane = lax.broadcasted_iota(jnp.int32, scores.shape, 1)
    sel = scores + bias
    in_group = lane & (EXPERTS_PER_GROUP - 1)
    group = lane >> 2

    def neighbours(x, idx, step, span):
        for d in (1, 2, 3):
            fwd = (idx + d) < 4
            y = jnp.where(fwd, pltpu.roll(x, LANES - d * step, 1), pltpu.roll(x, span - d * step, 1))
            yield y, fwd

    rank = jnp.zeros_like(sel)
    for y, fwd in neighbours(sel, in_group, 1, EXPERTS_PER_GROUP):
        beats = (y > sel) | ((y == sel) & jnp.logical_not(fwd))
        rank = rank + jnp.where(beats, 1.0, 0.0)
    top2 = jnp.where(rank < 2.0, sel, 0.0)
    gsum = top2
    for y, _ in neighbours(top2, in_group, 1, EXPERTS_PER_GROUP):
        gsum = gsum + y
    grank = jnp.zeros_like(sel)
    for y, fwd in neighbours(gsum, group, EXPERTS_PER_GROUP, N_EXPERTS):
        beats = (y > gsum) | ((y == gsum) & jnp.logical_not(fwd))
        grank = grank + jnp.where(beats, 1.0, 0.0)
    best = (grank < 0.5) & (lane < N_EXPERTS)
    chosen = best & (rank < 2.0)
    w = jnp.where(chosen, scores, 0.0)
    gates = w / jnp.sum(w, axis=-1, keepdims=True)
    gid = jnp.sum(jnp.where(best & (in_group == 0), group.astype(F32), 0.0), axis=-1, keepdims=True)
    return jnp.where(lane == GROUP_LANE, gid, gates)


def _pack_bf16_pairs(y):
    n = y.shape[1] // 2
    yb = y.astype(BF16).astype(F32)
    lo = lax.bitcast_convert_type(yb[:, :n], jnp.uint32) >> 16
    hi = lax.bitcast_convert_type(yb[:, n:], jnp.uint32) & jnp.uint32(0xFFFF0000)
    return hi | lo


def _unpack_bf16_pairs(w):
    lo = lax.bitcast_convert_type(w << 16, F32)
    hi = lax.bitcast_convert_type(w & jnp.uint32(0xFFFF0000), F32)
    return lo, hi


def _outproj_kernel(x_ref, mix_ref, w_ref, mod_ref, g_ref, wrh_ref, wrl_ref, rb_ref, x1_ref, h2_ref, gates_ref):
    nb = x_ref.shape[0]
    acc = _dot(jnp.concatenate([mix_ref[i] for i in range(nb)], axis=0), w_ref[...])
    h2s = []
    for i in range(nb):
        x1 = x_ref[i] + mod_ref[i, 0, 2:3, :] * acc[i * TILE:(i + 1) * TILE]
        x1_ref[i] = x1
        h2 = _rms(x1, g_ref[...]) * (1.0 + mod_ref[i, 0, 4:5, :]) + mod_ref[i, 0, 3:4, :]
        h2_ref[i] = _pack_bf16_pairs(h2)
        h2s.append(h2)
    hi, lo = _split(jnp.concatenate(h2s, axis=0))
    logits = _dot(hi, wrh_ref[...]) + _dot(hi, wrl_ref[...]) + _dot(lo, wrh_ref[...])
    gates = _route(_sigmoid(logits), rb_ref[...])
    for i in range(nb):
        gates_ref[i] = gates[i * TILE:(i + 1) * TILE]


def _outproj(xcat, mix, w, mod, g, wrh, wrl, rb, t0):
    batch, t_len, d = xcat.shape
    nt = t_len // TILE
    nb = 1
    t_out = (nt - t0) * TILE
    const = lambda b, t: (0, 0)
    tok_in = lambda b, t: (b, t + t0, 0)
    tok_out = lambda b, t: (b, t, 0)
    return pl.pallas_call(
        _outproj_kernel,
        grid=(batch // nb, nt - t0),
        in_specs=[pl.BlockSpec((nb, TILE, d), tok_in),
                  pl.BlockSpec((nb, TILE, Q_COLS), tok_in),
                  pl.BlockSpec(w.shape, const),
                  pl.BlockSpec((nb, 1, 8, d), lambda b, t: (b, jnp.minimum(t + t0, 1), 0, 0)),
                  pl.BlockSpec((1, d), const),
                  pl.BlockSpec((d, LANES), const),
                  pl.BlockSpec((d, LANES), const),
                  pl.BlockSpec((1, LANES), const)],
        out_specs=[pl.BlockSpec((nb, TILE, d), tok_out),
                   pl.BlockSpec((nb, TILE, d // 2), tok_out),
                   pl.BlockSpec((nb, TILE, LANES), tok_out)],
        out_shape=[jax.ShapeDtypeStruct((batch, t_out, d), F32),
                   jax.ShapeDtypeStruct((batch, t_out, d // 2), jnp.uint32),
                   jax.ShapeDtypeStruct((batch, t_out, LANES), F32)],
        compiler_params=pltpu.CompilerParams(dimension_semantics=("parallel", "parallel"),
                                             vmem_limit_bytes=VMEM_LIMIT),
        name="outproj",
    )(xcat, mix, w, mod, g, wrh, wrl, rb)


def _sc_workers():
    info = plsc.get_sparse_core_info()
    return info.num_cores, info.num_subcores


def _sc_dispatch(hpk, gates, pos2d, n_out):
    n, w = hpk.shape
    gw = gates.shape[1]
    nc, ns = _sc_workers()
    steps = n // (nc * ns * SC_WIN)
    mesh = plsc.VectorSubcoreMesh(core_axis_name="c", subcore_axis_name="s")

    @functools.partial(
        pl.kernel, mesh=mesh,
        out_type=(jax.ShapeDtypeStruct((n_out, w), hpk.dtype), jax.ShapeDtypeStruct((n_out, gw), gates.dtype)),
        scratch_types=[pltpu.VMEM((1, SC_WIN), jnp.int32), pltpu.VMEM((SC_WIN, w), hpk.dtype),
                       pltpu.VMEM((SC_WIN, gw), gates.dtype)],
        name="sc_dispatch")
    def k(h_hbm, g_hbm, pos_hbm, hs_hbm, gs_hbm, idx_v, rows_v, grow_v):
        wid = lax.axis_index("s") * nc + lax.axis_index("c")

        @pl.loop(0, steps)
        def _(j):
            blk = wid * steps + j
            off = pl.multiple_of(blk * SC_WIN, SC_WIN)
            pltpu.sync_copy(pos_hbm.at[pl.ds(blk, 1)], idx_v)
            pltpu.sync_copy(h_hbm.at[pl.ds(off, SC_WIN)], rows_v)
            pltpu.sync_copy(g_hbm.at[pl.ds(off, SC_WIN)], grow_v)
            pltpu.sync_copy(rows_v, hs_hbm.at[idx_v.at[0]])
            pltpu.sync_copy(grow_v, gs_hbm.at[idx_v.at[0]])

    return k(hpk, gates, pos2d)


def _sc_combine(ys, pos2d, n):
    w = ys.shape[1]
    nc, ns = _sc_workers()
    steps = n // (nc * ns * SC_WIN)
    mesh = plsc.VectorSubcoreMesh(core_axis_name="c", subcore_axis_name="s")

    @functools.partial(
        pl.kernel, mesh=mesh,
        out_type=jax.ShapeDtypeStruct((n, w), ys.dtype),
        scratch_types=[pltpu.VMEM((1, SC_WIN), jnp.int32), pltpu.VMEM((SC_WIN, w), ys.dtype)],
        name="sc_combine")
    def k(y_hbm, pos_hbm, o_hbm, idx_v, rows_v):
        wid = lax.axis_index("s") * nc + lax.axis_index("c")

        @pl.loop(0, steps)
        def _(j):
            blk = wid * steps + j
            off = pl.multiple_of(blk * SC_WIN, SC_WIN)
            pltpu.sync_copy(pos_hbm.at[pl.ds(blk, 1)], idx_v)
            pltpu.sync_copy(y_hbm.at[idx_v.at[0]], rows_v)
            pltpu.sync_copy(rows_v, o_hbm.at[pl.ds(off, SC_WIN)])

    return k(ys, pos2d)


def _bucket_plan(group, n_pad):
    n_groups = N_EXPERTS // EXPERTS_PER_GROUP
    onehot = (group[:, None] == jnp.arange(n_groups, dtype=jnp.int32)[None, :]).astype(jnp.int32)
    csum = jnp.cumsum(onehot, axis=0)
    counts = csum[-1]
    padded = ((counts + FFN_TILE - 1) // FFN_TILE) * FFN_TILE
    ends = jnp.cumsum(padded)
    starts = ends - padded
    pos = starts[group] + jnp.sum(csum * onehot, axis=1) - 1
    tile_row = jnp.arange(n_pad // FFN_TILE, dtype=jnp.int32) * FFN_TILE
    tile_group = jnp.minimum(jnp.sum((tile_row[:, None] >= ends[None, :]).astype(jnp.int32), axis=1), n_groups - 1)
    tile_used = (tile_row < (starts + counts)[tile_group]).astype(jnp.int32)
    return pos.astype(jnp.int32), tile_group.astype(jnp.int32), tile_used


def _ffn_kernel(tg_ref, tu_ref, hs_ref, gs_ref, wg_ref, wu_ref, wd_ref, ys_ref, wgb_ref, wub_ref, wdb_ref):
    i = pl.program_id(0)

    @pl.when((i == 0) | (tg_ref[i] != tg_ref[jnp.maximum(i - 1, 0)]))
    def _():
        wgb_ref[...] = wg_ref[...].astype(BF16)
        wub_ref[...] = wu_ref[...].astype(BF16)
        wdb_ref[...] = wd_ref[...].astype(BF16)

    @pl.when(tu_ref[i] > 0)
    def _():
        lo, hi = _unpack_bf16_pairs(hs_ref[...])
        lo = lo.astype(BF16)
        hi = hi.astype(BF16)
        half = lo.shape[1]
        gs = gs_ref[...]
        lane = lax.broadcasted_iota(jnp.int32, gs.shape, 1)
        first = tg_ref[i] * EXPERTS_PER_GROUP
        y = None
        for e in range(EXPERTS_PER_GROUP):
            a = _dot(lo, wgb_ref[e, :half, :]) + _dot(hi, wgb_ref[e, half:, :])
            u = _dot(lo, wub_ref[e, :half, :]) + _dot(hi, wub_ref[e, half:, :])
            gate = jnp.sum(jnp.where(lane == first + e, gs, 0.0), axis=-1, keepdims=True)
            part = _dot(((a * _sigmoid(a)) * u * gate).astype(BF16), wdb_ref[e])
            y = part if y is None else y + part
        ys_ref[...] = _pack_bf16_pairs(y)


def _ffn(tile_group, tile_used, hs, gs, wg, wu, wd):
    n_pad, half = hs.shape
    grp = lambda i, tg, tu: (tg[i], 0, 0)
    gshape = lambda w: (EXPERTS_PER_GROUP,) + w.shape[1:]
    return pl.pallas_call(
        _ffn_kernel,
        grid_spec=pltpu.PrefetchScalarGridSpec(
            num_scalar_prefetch=2,
            grid=(n_pad // FFN_TILE,),
            in_specs=[pl.BlockSpec((FFN_TILE, half), lambda i, tg, tu: (i, 0)),
                      pl.BlockSpec((FFN_TILE, LANES), lambda i, tg, tu: (i, 0)),
                      pl.BlockSpec(gshape(wg), grp), pl.BlockSpec(gshape(wu), grp), pl.BlockSpec(gshape(wd), grp)],
            out_specs=pl.BlockSpec((FFN_TILE, half), lambda i, tg, tu: (i, 0)),
            scratch_shapes=[pltpu.VMEM(gshape(wg), BF16), pltpu.VMEM(gshape(wu), BF16),
                            pltpu.VMEM(gshape(wd), BF16)]),
        out_shape=jax.ShapeDtypeStruct((n_pad, half), jnp.uint32),
        compiler_params=pltpu.CompilerParams(dimension_semantics=("arbitrary",), vmem_limit_bytes=VMEM_LIMIT),
        name="ffn",
    )(tile_group, tile_used, hs, gs, wg, wu, wd)


def _residual_kernel(x_ref, y_ref, mod_ref, fg_ref, o_ref, *, final_norm):
    for i in range(x_ref.shape[0]):
        lo, hi = _unpack_bf16_pairs(y_ref[i])
        y = x_ref[i] + mod_ref[i, 0, 5:6, :] * jnp.concatenate([lo, hi], axis=1)
        if final_norm:
            y = _rms(y, fg_ref[...])
        o_ref[i] = y


def _residual(x1, yg, mod, fg, t0, final_norm):
    batch, t_out, d = x1.shape
    nb = PAIR
    tok = lambda b, t: (b, t, 0)
    return pl.pallas_call(
        functools.partial(_residual_kernel, final_norm=final_norm),
        grid=(batch // nb, t_out // TILE),
        in_specs=[pl.BlockSpec((nb, TILE, d), tok),
                  pl.BlockSpec((nb, TILE, d // 2), tok),
                  pl.BlockSpec((nb, 1, 8, d), lambda b, t: (b, jnp.minimum(t + t0, 1), 0, 0)),
                  pl.BlockSpec((1, d), lambda b, t: (0, 0))],
        out_specs=pl.BlockSpec((nb, TILE, d), tok),
        out_shape=jax.ShapeDtypeStruct((batch, t_out, d), F32),
        compiler_params=pltpu.CompilerParams(dimension_semantics=("parallel", "parallel"),
                                             vmem_limit_bytes=VMEM_LIMIT),
        name="residual",
    )(x1, yg, mod, fg)


def _moe(h2, gates, x1, mod, wg, wu, wd, fg, t0, final_norm):
    batch, t_out, d = x1.shape
    n = batch * t_out
    n_pad = n + (N_EXPERTS // EXPERTS_PER_GROUP) * FFN_TILE
    gates = gates.reshape(n, LANES)
    pos, tile_group, tile_used = _bucket_plan(gates[:, GROUP_LANE].astype(jnp.int32), n_pad)
    pos2d = pos.reshape(n // SC_WIN, SC_WIN)
    hs, gs = _sc_dispatch(h2.reshape(n, d // 2), gates, pos2d, n_pad)
    ys = _ffn(tile_group, tile_used, hs, gs, wg, wu, wd)
    yg = _sc_combine(ys, pos2d, n).reshape(batch, t_out, d // 2)
    return _residual(x1, yg, mod, fg, t0, final_norm)


def _gqa_perm():
    return np.arange(256).reshape(2, 2, HEAD_DIM).transpose(1, 0, 2).reshape(-1)


def _in_col_perm():
    g = _gqa_perm()
    q = np.concatenate([np.arange(512), 512 + g, 768 + g])
    kv0 = 1024
    k = np.concatenate([kv0 + np.arange(512), kv0 + 1024 + np.arange(128), kv0 + 1280 + np.arange(128)])
    v = np.concatenate([kv0 + 512 + np.arange(512), kv0 + 1152 + np.arange(128), kv0 + 1408 + np.arange(128)])
    return np.concatenate([q, k, v])


def _out_row_perm():
    g = _gqa_perm()
    return np.concatenate([np.arange(512), 512 + g, 768 + g])


def _rope_tables(seq, ctx_len):
    rows = seq // GRID_W
    row_pos = jnp.repeat(jnp.arange(rows, dtype=F32), GRID_W)
    col_pos = jnp.tile(jnp.arange(GRID_W, dtype=F32), rows)
    axis_dim = HEAD_DIM // 2
    inv_freq = ROPE_BASE ** (-jnp.arange(0, axis_dim, 2, dtype=F32) / axis_dim)
    ang_r = row_pos[:, None] * inv_freq[None, :]
    ang_c = col_pos[:, None] * inv_freq[None, :]
    z = jnp.zeros_like(ang_r)
    cos = jnp.concatenate([jnp.cos(ang_r)] * 2 + [jnp.cos(ang_c)] * 2, axis=-1)
    sin_a = jnp.concatenate([-jnp.sin(ang_r), z, -jnp.sin(ang_c), z], axis=-1)
    sin_b = jnp.concatenate([z, jnp.sin(ang_r), z, jnp.sin(ang_c)], axis=-1)

    def full(tab, fill):
        tab = jnp.tile(tab, (1, LANES // HEAD_DIM))
        return jnp.concatenate([jnp.full((ctx_len, LANES), fill, F32), tab], axis=0)

    return full(cos, 1.0), full(sin_a, 0.0), full(sin_b, 0.0)


def kernel(x, c, ctx, c_ctx, w_ada, b_ada, norm1_g, norm2_g, w_in, w_out, lam_q1, lam_k1, lam_q2, lam_k2,
           subln_g, q_norm_g, k_norm_g, sink, w_router, router_bias, w_gate, w_up, w_down, final_g):
    batch, seq, d = x.shape
    ctx_len = ctx.shape[1]
    depth = w_in.shape[0]
    sc_rows = SC_WIN * math.prod(_sc_workers())
    assert ctx_len == TILE and seq % TILE == 0 and batch % PAIR == 0 and batch <= 15
    assert (batch * seq) % sc_rows == 0 and (batch * (seq + ctx_len)) % sc_rows == 0

    cpad = jnp.zeros((16, d), F32).at[:batch].set(c).at[batch].set(c_ctx)
    mod_all = _adaln(cpad, w_ada, b_ada)
    cos, sin_a, sin_b = _rope_tables(seq, ctx_len)
    mseg = jnp.asarray(np.kron(np.eye(2), np.ones((HEAD_DIM, HEAD_DIM))), BF16)
    in_perm = _in_col_perm()
    out_perm = _out_row_perm()
    wr = jnp.zeros((d, LANES), F32).at[:, :N_EXPERTS].set(w_router)
    wrh = wr.astype(BF16)
    wrl = (wr - wrh.astype(F32)).astype(BF16)
    rb = jnp.zeros((1, LANES), F32).at[0, :N_EXPERTS].set(router_bias)

    xcat = jnp.concatenate([ctx, x], axis=1)

    for l in range(depth):
        last = l == depth - 1
        lambda_init = 0.8 - 0.6 * math.exp(-0.3 * l)
        m6 = mod_all[l].reshape(16, 6, d)
        m8 = jnp.concatenate([m6, jnp.zeros((16, 2, d), F32)], axis=1)
        mod = jnp.stack([jnp.broadcast_to(m8[batch], (batch, 8, d)), m8[:batch]], axis=1)

        w_in_p = w_in[l][:, in_perm].astype(BF16)
        w_out_p = w_out[l][out_perm, :].astype(BF16)
        gq = jnp.tile(q_norm_g[l], 2).reshape(1, LANES)
        gk = jnp.tile(k_norm_g[l], 2).reshape(1, LANES)
        q, k, v = _inproj(xcat, mod, norm1_g[l].reshape(1, d), w_in_p, cos, sin_a, sin_b, gq, gk, mseg)

        t0 = 1 if last else 0
        lamv = jnp.stack([lam_q1[l], lam_k1[l], lam_q2[l], lam_k2[l]])
        mix = _attention(sink[l], lamv, subln_g[l].reshape(1, LANES), q, k, v, t0, lambda_init)
        x1, h2, gates = _outproj(xcat, mix, w_out_p, mod, norm2_g[l].reshape(1, d), wrh, wrl, rb, t0)
        xcat = _moe(h2, gates, x1, mod, w_gate[l], w_up[l], w_down[l], final_g.reshape(1, d), t0, final_norm=last)
    return xcat
```

```python
import functools
import math

import numpy as np
import jax
import jax.numpy as jnp
from jax import lax
from jax.experimental import pallas as pl
from jax.experimental.pallas import tpu as pltpu
from jax.experimental.pallas import tpu_sc as plsc

F32 = jnp.float32
BF16 = jnp.bfloat16

HEAD_DIM = 64
GRID_W = 64
ROPE_BASE = 10000.0
RMS_EPS = 1e-6
WINDOW = 128
N_EXPERTS = 16
EXPERTS_PER_GROUP = 4
LANES = 128
TILE = 256
PAIR = 2
FFN_TILE = 512
SC_WIN = 64
GROUP_LANE = 16
NEG_BIG = -1e30
KEY_CHUNK = 256
VMEM_LIMIT = 52 * 1024 * 1024

QA, QB, QC = 512, 256, 256
KA, KB, KC = 512, 128, 128
Q_COLS = QA + QB + QC
K_COLS = KA + KB + KC
V_COLS = 768
ROPE_COLS = Q_COLS + K_COLS


def _dot(a, b):
    return jnp.dot(a, b, preferred_element_type=F32)


def _dot_nt(a, b):
    return lax.dot_general(a, b, (((1,), (1,)), ((), ())), preferred_element_type=F32)


def _split(a):
    hi = a.astype(BF16)
    lo = (a - hi.astype(F32)).astype(BF16)
    return hi, lo


def _dot3(a, b):
    ah, al = _split(a)
    bh, bl = _split(b)
    return _dot(ah, bh) + _dot(ah, bl) + _dot(al, bh)


def _rms(x, g):
    ms = jnp.mean(x * x, axis=-1, keepdims=True)
    return x * lax.rsqrt(ms + RMS_EPS) * g


def _sigmoid(x):
    return 1.0 / (1.0 + jnp.exp(-x))


def _adaln_kernel(c_ref, w_ref, b_ref, o_ref):
    cv = c_ref[...]
    s = cv * _sigmoid(cv)
    o_ref[0] = _dot3(s, w_ref[0]) + b_ref[0]


def _adaln(cpad, w_ada, b_ada):
    depth, d, n6 = w_ada.shape
    tn = 1536
    return pl.pallas_call(
        _adaln_kernel,
        grid=(depth, n6 // tn),
        in_specs=[pl.BlockSpec((16, d), lambda l, j: (0, 0)),
                  pl.BlockSpec((1, d, tn), lambda l, j: (l, 0, j)),
                  pl.BlockSpec((1, 1, tn), lambda l, j: (l, 0, j))],
        out_specs=pl.BlockSpec((1, 16, tn), lambda l, j: (l, 0, j)),
        out_shape=jax.ShapeDtypeStruct((depth, 16, n6), F32),
        compiler_params=pltpu.CompilerParams(dimension_semantics=("parallel", "parallel"),
                                             vmem_limit_bytes=VMEM_LIMIT),
        name="adaln",
    )(cpad, w_ada, b_ada.reshape(depth, 1, n6))


def _inproj_kernel(x_ref, mod_ref, g_ref, w_ref, cos_ref, sa_ref, sb_ref, gq_ref, gk_ref, mseg_ref,
                   q_ref, k_ref, v_ref):
    for base in range(0, x_ref.shape[0], PAIR):
        _inproj_rows(base, x_ref, mod_ref, g_ref, w_ref, cos_ref, sa_ref, sb_ref, gq_ref, gk_ref, mseg_ref,
                     q_ref, k_ref, v_ref)


def _inproj_rows(base, x_ref, mod_ref, g_ref, w_ref, cos_ref, sa_ref, sb_ref, gq_ref, gk_ref, mseg_ref,
                 q_ref, k_ref, v_ref):
    members = range(base, base + PAIR)
    h = jnp.concatenate(
        [(_rms(x_ref[i], g_ref[...]) * (1.0 + mod_ref[i, 0, 1:2, :]) + mod_ref[i, 0, 0:1, :]).astype(BF16)
         for i in members], axis=0)
    cos = jnp.concatenate([cos_ref[...]] * PAIR, axis=0)
    sa = jnp.concatenate([sa_ref[...]] * PAIR, axis=0)
    sb = jnp.concatenate([sb_ref[...]] * PAIR, axis=0)
    mseg = mseg_ref[...]

    def put(ref, cols, val):
        for n, i in enumerate(members):
            ref[i, :, cols] = val[n * TILE:(n + 1) * TILE]

    def rope(b):
        return b * cos + pltpu.roll(b, LANES - 16, 1) * sa + pltpu.roll(b, 16, 1) * sb

    def qknorm(b, g):
        hi, lo = _split(b * b)
        ms = (_dot(hi, mseg) + _dot(lo, mseg)) * (1.0 / HEAD_DIM)
        return b * lax.rsqrt(ms + RMS_EPS) * g

    for c in range(ROPE_COLS // 256):
        p2 = _dot(h, w_ref[:, c * 256:(c + 1) * 256])
        for half in range(2):
            j = 2 * c + half
            p = p2[:, half * LANES:(half + 1) * LANES]
            if j in (4, 5):
                p = qknorm(p, gq_ref[...])
            if j == 12:
                p = qknorm(p, gk_ref[...])
            p = rope(p)
            if j < Q_COLS // LANES:
                put(q_ref, slice(j * LANES, (j + 1) * LANES), (p * (HEAD_DIM ** -0.5)).astype(BF16))
            else:
                jk = j - Q_COLS // LANES
                put(k_ref, slice(jk * LANES, (jk + 1) * LANES), p.astype(BF16))
    for c in range(V_COLS // 256):
        lo_c = ROPE_COLS + c * 256
        put(v_ref, slice(c * 256, (c + 1) * 256), _dot(h, w_ref[:, lo_c:lo_c + 256]).astype(BF16))


def _inproj(xcat, mod, g, w, cos, sa, sb, gq, gk, mseg):
    batch, t_len, d = xcat.shape
    nt = t_len // TILE
    nb = 2 * PAIR
    row = lambda b, t: (t, 0)
    const = lambda b, t: (0, 0)
    tok = lambda b, t: (b, t, 0)
    return pl.pallas_call(
        _inproj_kernel,
        grid=(batch // nb, nt),
        in_specs=[pl.BlockSpec((nb, TILE, d), tok),
                  pl.BlockSpec((nb, 1, 8, d), lambda b, t: (b, jnp.minimum(t, 1), 0, 0)),
                  pl.BlockSpec((1, d), const),
                  pl.BlockSpec(w.shape, const),
                  pl.BlockSpec((TILE, LANES), row),
                  pl.BlockSpec((TILE, LANES), row),
                  pl.BlockSpec((TILE, LANES), row),
                  pl.BlockSpec((1, LANES), const),
                  pl.BlockSpec((1, LANES), const),
                  pl.BlockSpec((LANES, LANES), const)],
        out_specs=[pl.BlockSpec((nb, TILE, Q_COLS), tok),
                   pl.BlockSpec((nb, TILE, K_COLS), tok),
                   pl.BlockSpec((nb, TILE, V_COLS), tok)],
        out_shape=[jax.ShapeDtypeStruct((batch, t_len, Q_COLS), BF16),
                   jax.ShapeDtypeStruct((batch, t_len, K_COLS), BF16),
                   jax.ShapeDtypeStruct((batch, t_len, V_COLS), BF16)],
        compiler_params=pltpu.CompilerParams(dimension_semantics=("parallel", "parallel"),
                                             vmem_limit_bytes=VMEM_LIMIT),
        name="inproj",
    )(xcat, mod, g, w, cos, sa, sb, gq, gk, mseg)


def _softmax_pv(qq, k_ref, v_ref, col, starts, s_ref, mask_fn=None, extra=None):
    half = KEY_CHUNK // 2
    mrun = None
    for j, st in enumerate(starts):
        s = _dot_nt(qq, k_ref[0, pl.ds(st, KEY_CHUNK), col:col + LANES])
        if mask_fn is not None:
            s = mask_fn(j, st, s)
        s_ref[:, j * KEY_CHUNK:(j + 1) * KEY_CHUNK] = s
        mj = jnp.maximum(s[:, :half], s[:, half:])
        mrun = mj if mrun is None else jnp.maximum(mrun, mj)
    m = jnp.max(mrun, axis=-1, keepdims=True)
    if extra is not None:
        m = jnp.maximum(m, extra)
    lrun = None
    acc = None
    for j, st in enumerate(starts):
        e = jnp.exp(s_ref[:, j * KEY_CHUNK:(j + 1) * KEY_CHUNK] - m)
        lj = e[:, :half] + e[:, half:]
        lrun = lj if lrun is None else lrun + lj
        pv = _dot(e.astype(BF16), v_ref[0, pl.ds(st, KEY_CHUNK), col:col + LANES])
        acc = pv if acc is None else acc + pv
    l = jnp.sum(lrun, axis=-1, keepdims=True)
    if extra is not None:
        l = l + jnp.exp(extra - m)
    return acc / l


def _all_chunks(n_keys):
    return [j * KEY_CHUNK for j in range(n_keys // KEY_CHUNK)]


def _half_masks(shape):
    lane = lax.broadcasted_iota(jnp.int32, shape, 1)
    return lane < HEAD_DIM, lane >= HEAD_DIM


def _stack_group_queries(q2, kv):
    lo_m, hi_m = _half_masks((TILE, LANES))
    m = lo_m if kv == 0 else hi_m
    zero = jnp.zeros((TILE, LANES), q2.dtype)
    return jnp.concatenate([jnp.where(m, q2[:, :LANES], zero), jnp.where(m, q2[:, LANES:], zero)], axis=0)


def _merge_kv_outputs(o_kv0, o_kv1):
    lo_m, _ = _half_masks((TILE, LANES))
    g0 = jnp.where(lo_m, o_kv0[:TILE], o_kv1[:TILE])
    g1 = jnp.where(lo_m, o_kv0[TILE:], o_kv1[TILE:])
    return jnp.concatenate([g0, g1], axis=1)


def _attn_kernel(sink_ref, lam_ref, subg_ref, qa_ref, qb_ref, qc_ref, ka_ref, kb_ref, kc_ref, va_ref, vb_ref, vc_ref,
                 o_ref, s_ref, *, t0, ctx_len, lambda_init):
    t = pl.program_id(1) + t0
    t_len = ka_ref.shape[1]
    band = 2 * TILE
    lo_m, hi_m = _half_masks((TILE, LANES))
    zero = jnp.zeros((TILE, LANES), BF16)
    row = lax.broadcasted_iota(jnp.int32, (2 * TILE, 1), 0)
    lv = lam_ref[...]
    lam = (jnp.exp(jnp.sum(lv[0:1] * lv[1:2], keepdims=True))
           - jnp.exp(jnp.sum(lv[2:3] * lv[3:4], keepdims=True)) + lambda_init)

    def sink_col(kv):
        return jnp.where(row < TILE, sink_ref[2 * kv], sink_ref[2 * kv + 1])

    def mixers(starts_ab, starts_c, mask_c):
        n_chain = 0
        for h in range(QA // LANES):
            q = qa_ref[0, :, h * LANES:(h + 1) * LANES]
            qq = jnp.concatenate([jnp.where(lo_m, q, zero), jnp.where(hi_m, q, zero)], axis=0)
            o = _softmax_pv(qq, ka_ref, va_ref, h * LANES, starts_ab, s_ref.at[n_chain % 2])
            n_chain += 1
            o = o[:TILE] - lam * o[TILE:]
            o_ref[0, :, h * LANES:(h + 1) * LANES] = (_rms(o, subg_ref[...]) * (1.0 - lambda_init)).astype(BF16)
        outs = []
        for kv in range(2):
            outs.append(_softmax_pv(_stack_group_queries(qb_ref[0], kv), kb_ref, vb_ref, 0, starts_ab,
                                    s_ref.at[n_chain % 2]))
            n_chain += 1
        o_ref[0, :, QA:QA + QB] = _merge_kv_outputs(*outs).astype(BF16)
        outs = []
        for kv in range(2):
            outs.append(_softmax_pv(_stack_group_queries(qc_ref[0], kv), kc_ref, vc_ref, 0, starts_c,
                                    s_ref.at[n_chain % 2], mask_c, sink_col(kv)))
            n_chain += 1
        o_ref[0, :, QA + QB:] = _merge_kv_outputs(*outs).astype(BF16)

    def ctx_step():
        mixers([0], [0], None)

    def latent_step():
        q_start = (t - 1) * TILE
        u0 = pl.multiple_of(jnp.minimum(q_start + ctx_len - WINDOW, t_len - band), WINDOW)
        q_pos = q_start + (lax.broadcasted_iota(jnp.int32, (2 * TILE, KEY_CHUNK), 0) & (TILE - 1))
        key_lane = lax.broadcasted_iota(jnp.int32, (2 * TILE, KEY_CHUNK), 1)

        def mask_c(j, st, s):
            if j == 0:
                return s
            k_pos = (st - ctx_len) + key_lane
            valid = (jnp.abs(k_pos - q_pos) <= WINDOW) & (k_pos >= 0)
            return jnp.where(valid, s, NEG_BIG)

        mixers(_all_chunks(t_len), [0] + [u0 + i * KEY_CHUNK for i in range(band // KEY_CHUNK)], mask_c)

    if t0 == 0:
        pl.when(t == 0)(ctx_step)
        pl.when(t > 0)(latent_step)
    else:
        latent_step()


def _attention(sink, lamv, subg, q, k, v, t0, lambda_init):
    batch, t_len, _ = q.shape
    nt = t_len // TILE
    kern = functools.partial(_attn_kernel, t0=t0, ctx_len=TILE, lambda_init=lambda_init)
    const = lambda b, t: (0, 0)
    qspec = lambda width, blk: pl.BlockSpec((1, TILE, width), lambda b, t: (b, t + t0, blk))
    kvspec = lambda width, blk: pl.BlockSpec((1, t_len, width), lambda b, t: (b, 0, blk))
    return pl.pallas_call(
        kern,
        grid=(batch, nt - t0),
        in_specs=[pl.BlockSpec(memory_space=pltpu.SMEM),
                  pl.BlockSpec(lamv.shape, const),
                  pl.BlockSpec((1, LANES), const),
                  qspec(QA, 0), qspec(QB, QA // QB), qspec(QC, (QA + QB) // QC),
                  kvspec(KA, 0), kvspec(KB, KA // KB), kvspec(KC, (KA + KB) // KC),
                  kvspec(KA, 0), kvspec(KB, KA // KB), kvspec(KC, (KA + KB) // KC)],
        out_specs=pl.BlockSpec((1, TILE, Q_COLS), lambda b, t: (b, t + t0, 0)),
        out_shape=jax.ShapeDtypeStruct((batch, t_len, Q_COLS), BF16),
        scratch_shapes=[pltpu.VMEM((2, 2 * TILE, t_len), F32)],
        compiler_params=pltpu.CompilerParams(dimension_semantics=("parallel", "parallel"),
                                             vmem_limit_bytes=VMEM_LIMIT),
        name="attention",
    )(sink, lamv, subg, q, q, q, k, k, k, v, v, v)


def _route(scores, bias):
    lane = lax.broadcasted_iota(jnp.int32, scores.shape, 1)
    sel = scores + bias
    in_group = lane & (EXPERTS_PER_GROUP - 1)
    group = lane >> 2

    def neighbours(x, idx, step, span):
        for d in (1, 2, 3):
            fwd = (idx + d) < 4
            y = jnp.where(fwd, pltpu.roll(x, LANES - d * step, 1), pltpu.roll(x, span - d * step, 1))
            yield y, fwd

    rank = jnp.zeros_like(sel)
    for y, fwd in neighbours(sel, in_group, 1, EXPERTS_PER_GROUP):
        beats = (y > sel) | ((y == sel) & jnp.logical_not(fwd))
        rank = rank + jnp.where(beats, 1.0, 0.0)
    top2 = jnp.where(rank < 2.0, sel, 0.0)
    gsum = top2
    for y, _ in neighbours(top2, in_group, 1, EXPERTS_PER_GROUP):
        gsum = gsum + y
    grank = jnp.zeros_like(sel)
    for y, fwd in neighbours(gsum, group, EXPERTS_PER_GROUP, N_EXPERTS):
        beats = (y > gsum) | ((y == gsum) & jnp.logical_not(fwd))
        grank = grank + jnp.where(beats, 1.0, 0.0)
    best = (grank < 0.5) & (lane < N_EXPERTS)
    chosen = best & (rank < 2.0)
    w = jnp.where(chosen, scores, 0.0)
    gates = w / jnp.sum(w, axis=-1, keepdims=True)
    gid = jnp.sum(jnp.where(best & (in_group == 0), group.astype(F32), 0.0), axis=-1, keepdims=True)
    return jnp.where(lane == GROUP_LANE, gid, gates)


def _pack_bf16_pairs(y):
    n = y.shape[1] // 2
    yb = y.astype(BF16).astype(F32)
    lo = lax.bitcast_convert_type(yb[:, :n], jnp.uint32) >> 16
    hi = lax.bitcast_convert_type(yb[:, n:], jnp.uint32) & jnp.uint32(0xFFFF0000)
    return hi | lo


def _unpack_bf16_pairs(w):
    lo = lax.bitcast_convert_type(w << 16, F32)
    hi = lax.bitcast_convert_type(w & jnp.uint32(0xFFFF0000), F32)
    return lo, hi


def _outproj_kernel(x_ref, mix_ref, w_ref, mod_ref, g_ref, wrh_ref, wrl_ref, rb_ref, x1_ref, h2_ref, gates_ref):
    for i in range(x_ref.shape[0]):
        x1 = x_ref[i] + mod_ref[i, 0, 2:3, :] * _dot(mix_ref[i], w_ref[...])
        x1_ref[i] = x1
        h2 = _rms(x1, g_ref[...]) * (1.0 + mod_ref[i, 0, 4:5, :]) + mod_ref[i, 0, 3:4, :]
        h2_ref[i] = _pack_bf16_pairs(h2)
        hi, lo = _split(h2)
        logits = _dot(hi, wrh_ref[...]) + _dot(hi, wrl_ref[...]) + _dot(lo, wrh_ref[...])
        gates_ref[i] = _route(_sigmoid(logits), rb_ref[...])


def _outproj(xcat, mix, w, mod, g, wrh, wrl, rb, t0):
    batch, t_len, d = xcat.shape
    nt = t_len // TILE
    nb = 2 * PAIR
    t_out = (nt - t0) * TILE
    const = lambda b, t: (0, 0)
    tok_in = lambda b, t: (b, t + t0, 0)
    tok_out = lambda b, t: (b, t, 0)
    return pl.pallas_call(
        _outproj_kernel,
        grid=(batch // nb, nt - t0),
        in_specs=[pl.BlockSpec((nb, TILE, d), tok_in),
                  pl.BlockSpec((nb, TILE, Q_COLS), tok_in),
                  pl.BlockSpec(w.shape, const),
                  pl.BlockSpec((nb, 1, 8, d), lambda b, t: (b, jnp.minimum(t + t0, 1), 0, 0)),
                  pl.BlockSpec((1, d), const),
                  pl.BlockSpec((d, LANES), const),
                  pl.BlockSpec((d, LANES), const),
                  pl.BlockSpec((1, LANES), const)],
        out_specs=[pl.BlockSpec((nb, TILE, d), tok_out),
                   pl.BlockSpec((nb, TILE, d // 2), tok_out),
                   pl.BlockSpec((nb, TILE, LANES), tok_out)],
        out_shape=[jax.ShapeDtypeStruct((batch, t_out, d), F32),
                   jax.ShapeDtypeStruct((batch, t_out, d // 2), jnp.uint32),
                   jax.ShapeDtypeStruct((batch, t_out, LANES), F32)],
        compiler_params=pltpu.CompilerParams(dimension_semantics=("parallel", "parallel"),
                                             vmem_limit_bytes=VMEM_LIMIT),
        name="outproj",
    )(xcat, mix, w, mod, g, wrh, wrl, rb)


def _sc_workers():
    info = plsc.get_sparse_core_info()
    return info.num_cores, info.num_subcores


def _sc_dispatch(hpk, gates, pos2d, n_out):
    n, w = hpk.shape
    gw = gates.shape[1]
    nc, ns = _sc_workers()
    steps = n // (nc * ns * SC_WIN)
    mesh = plsc.VectorSubcoreMesh(core_axis_name="c", subcore_axis_name="s")

    @functools.partial(
        pl.kernel, mesh=mesh,
        out_type=(jax.ShapeDtypeStruct((n_out, w), hpk.dtype), jax.ShapeDtypeStruct((n_out, gw), gates.dtype)),
        scratch_types=[pltpu.VMEM((1, SC_WIN), jnp.int32), pltpu.VMEM((SC_WIN, w), hpk.dtype),
                       pltpu.VMEM((SC_WIN, gw), gates.dtype)],
        name="sc_dispatch")
    def k(h_hbm, g_hbm, pos_hbm, hs_hbm, gs_hbm, idx_v, rows_v, grow_v):
        wid = lax.axis_index("s") * nc + lax.axis_index("c")

        @pl.loop(0, steps)
        def _(j):
            blk = wid * steps + j
            off = pl.multiple_of(blk * SC_WIN, SC_WIN)
            pltpu.sync_copy(pos_hbm.at[pl.ds(blk, 1)], idx_v)
            pltpu.sync_copy(h_hbm.at[pl.ds(off, SC_WIN)], rows_v)
            pltpu.sync_copy(g_hbm.at[pl.ds(off, SC_WIN)], grow_v)
            pltpu.sync_copy(rows_v, hs_hbm.at[idx_v.at[0]])
            pltpu.sync_copy(grow_v, gs_hbm.at[idx_v.at[0]])

    return k(hpk, gates, pos2d)


def _sc_combine(ys, pos2d, n):
    w = ys.shape[1]
    nc, ns = _sc_workers()
    steps = n // (nc * ns * SC_WIN)
    mesh = plsc.VectorSubcoreMesh(core_axis_name="c", subcore_axis_name="s")

    @functools.partial(
        pl.kernel, mesh=mesh,
        out_type=jax.ShapeDtypeStruct((n, w), ys.dtype),
        scratch_types=[pltpu.VMEM((1, SC_WIN), jnp.int32), pltpu.VMEM((SC_WIN, w), ys.dtype)],
        name="sc_combine")
    def k(y_hbm, pos_hbm, o_hbm, idx_v, rows_v):
        wid = lax.axis_index("s") * nc + lax.axis_index("c")

        @pl.loop(0, steps)
        def _(j):
            blk = wid * steps + j
            off = pl.multiple_of(blk * SC_WIN, SC_WIN)
            pltpu.sync_copy(pos_hbm.at[pl.ds(blk, 1)], idx_v)
            pltpu.sync_copy(y_hbm.at[idx_v.at[0]], rows_v)
            pltpu.sync_copy(rows_v, o_hbm.at[pl.ds(off, SC_WIN)])

    return k(ys, pos2d)


def _bucket_plan(group, n_pad):
    n_groups = N_EXPERTS // EXPERTS_PER_GROUP
    onehot = (group[:, None] == jnp.arange(n_groups, dtype=jnp.int32)[None, :]).astype(jnp.int32)
    csum = jnp.cumsum(onehot, axis=0)
    counts = csum[-1]
    padded = ((counts + FFN_TILE - 1) // FFN_TILE) * FFN_TILE
    ends = jnp.cumsum(padded)
    starts = ends - padded
    pos = starts[group] + jnp.sum(csum * onehot, axis=1) - 1
    tile_row = jnp.arange(n_pad // FFN_TILE, dtype=jnp.int32) * FFN_TILE
    tile_group = jnp.minimum(jnp.sum((tile_row[:, None] >= ends[None, :]).astype(jnp.int32), axis=1), n_groups - 1)
    tile_used = (tile_row < (starts + counts)[tile_group]).astype(jnp.int32)
    return pos.astype(jnp.int32), tile_group.astype(jnp.int32), tile_used


def _ffn_kernel(tg_ref, tu_ref, hs_ref, gs_ref, wg_ref, wu_ref, wd_ref, ys_ref, wgb_ref, wub_ref, wdb_ref):
    i = pl.program_id(0)

    @pl.when((i == 0) | (tg_ref[i] != tg_ref[jnp.maximum(i - 1, 0)]))
    def _():
        wgb_ref[...] = wg_ref[0].astype(BF16)
        wub_ref[...] = wu_ref[0].astype(BF16)
        wdb_ref[...] = wd_ref[0].astype(BF16)

    @pl.when(tu_ref[i] > 0)
    def _():
        lo, hi = _unpack_bf16_pairs(hs_ref[...])
        lo = lo.astype(BF16)
        hi = hi.astype(BF16)
        half = lo.shape[1]
        gs = gs_ref[...]
        lane = lax.broadcasted_iota(jnp.int32, gs.shape, 1)
        first = tg_ref[i] * EXPERTS_PER_GROUP
        y = None
        for e in range(EXPERTS_PER_GROUP):
            a = _dot(lo, wgb_ref[e, :half, :]) + _dot(hi, wgb_ref[e, half:, :])
            u = _dot(lo, wub_ref[e, :half, :]) + _dot(hi, wub_ref[e, half:, :])
            gate = jnp.sum(jnp.where(lane == first + e, gs, 0.0), axis=-1, keepdims=True)
            part = _dot(((a * _sigmoid(a)) * u * gate).astype(BF16), wdb_ref[e])
            y = part if y is None else y + part
        ys_ref[...] = _pack_bf16_pairs(y)


def _ffn(tile_group, tile_used, hs, gs, wg, wu, wd, layer):
    n_pad, half = hs.shape
    grp = lambda i, tg, tu: (layer, tg[i], 0, 0)
    gshape = lambda w: (EXPERTS_PER_GROUP,) + w.shape[2:]
    wspec = lambda w: pl.BlockSpec((1,) + gshape(w), grp)
    return pl.pallas_call(
        _ffn_kernel,
        grid_spec=pltpu.PrefetchScalarGridSpec(
            num_scalar_prefetch=2,
            grid=(n_pad // FFN_TILE,),
            in_specs=[pl.BlockSpec((FFN_TILE, half), lambda i, tg, tu: (i, 0)),
                      pl.BlockSpec((FFN_TILE, LANES), lambda i, tg, tu: (i, 0)),
                      wspec(wg), wspec(wu), wspec(wd)],
            out_specs=pl.BlockSpec((FFN_TILE, half), lambda i, tg, tu: (i, 0)),
            scratch_shapes=[pltpu.VMEM(gshape(wg), BF16), pltpu.VMEM(gshape(wu), BF16),
                            pltpu.VMEM(gshape(wd), BF16)]),
        out_shape=jax.ShapeDtypeStruct((n_pad, half), jnp.uint32),
        compiler_params=pltpu.CompilerParams(dimension_semantics=("arbitrary",), vmem_limit_bytes=VMEM_LIMIT),
        name="ffn",
    )(tile_group, tile_used, hs, gs, wg, wu, wd)


def _residual_kernel(x_ref, y_ref, mod_ref, fg_ref, o_ref, *, final_norm):
    for i in range(x_ref.shape[0]):
        lo, hi = _unpack_bf16_pairs(y_ref[i])
        y = x_ref[i] + mod_ref[i, 0, 5:6, :] * jnp.concatenate([lo, hi], axis=1)
        if final_norm:
            y = _rms(y, fg_ref[...])
        o_ref[i] = y


def _residual(x1, yg, mod, fg, t0, final_norm):
    batch, t_out, d = x1.shape
    nb = PAIR
    tok = lambda b, t: (b, t, 0)
    return pl.pallas_call(
        functools.partial(_residual_kernel, final_norm=final_norm),
        grid=(batch // nb, t_out // TILE),
        in_specs=[pl.BlockSpec((nb, TILE, d), tok),
                  pl.BlockSpec((nb, TILE, d // 2), tok),
                  pl.BlockSpec((nb, 1, 8, d), lambda b, t: (b, jnp.minimum(t + t0, 1), 0, 0)),
                  pl.BlockSpec((1, d), lambda b, t: (0, 0))],
        out_specs=pl.BlockSpec((nb, TILE, d), tok),
        out_shape=jax.ShapeDtypeStruct((batch, t_out, d), F32),
        compiler_params=pltpu.CompilerParams(dimension_semantics=("parallel", "parallel"),
                                             vmem_limit_bytes=VMEM_LIMIT),
        name="residual",
    )(x1, yg, mod, fg)


def _moe(h2, gates, x1, mod, wg, wu, wd, layer, fg, t0, final_norm):
    batch, t_out, d = x1.shape
    n = batch * t_out
    n_pad = n + (N_EXPERTS // EXPERTS_PER_GROUP) * FFN_TILE
    gates = gates.reshape(n, LANES)
    pos, tile_group, tile_used = _bucket_plan(gates[:, GROUP_LANE].astype(jnp.int32), n_pad)
    pos2d = pos.reshape(n // SC_WIN, SC_WIN)
    hs, gs = _sc_dispatch(h2.reshape(n, d // 2), gates, pos2d, n_pad)
    ys = _ffn(tile_group, tile_used, hs, gs, wg, wu, wd, layer)
    yg = _sc_combine(ys, pos2d, n).reshape(batch, t_out, d // 2)
    return _residual(x1, yg, mod, fg, t0, final_norm)


def _gqa_perm():
    return np.arange(256).reshape(2, 2, HEAD_DIM).transpose(1, 0, 2).reshape(-1)


def _in_col_perm():
    g = _gqa_perm()
    q = np.concatenate([np.arange(512), 512 + g, 768 + g])
    kv0 = 1024
    k = np.concatenate([kv0 + np.arange(512), kv0 + 1024 + np.arange(128), kv0 + 1280 + np.arange(128)])
    v = np.concatenate([kv0 + 512 + np.arange(512), kv0 + 1152 + np.arange(128), kv0 + 1408 + np.arange(128)])
    return np.concatenate([q, k, v])


def _out_row_perm():
    g = _gqa_perm()
    return np.concatenate([np.arange(512), 512 + g, 768 + g])


def _rope_tables(seq, ctx_len):
    rows = seq // GRID_W
    row_pos = jnp.repeat(jnp.arange(rows, dtype=F32), GRID_W)
    col_pos = jnp.tile(jnp.arange(GRID_W, dtype=F32), rows)
    axis_dim = HEAD_DIM // 2
    inv_freq = ROPE_BASE ** (-jnp.arange(0, axis_dim, 2, dtype=F32) / axis_dim)
    ang_r = row_pos[:, None] * inv_freq[None, :]
    ang_c = col_pos[:, None] * inv_freq[None, :]
    z = jnp.zeros_like(ang_r)
    cos = jnp.concatenate([jnp.cos(ang_r)] * 2 + [jnp.cos(ang_c)] * 2, axis=-1)
    sin_a = jnp.concatenate([-jnp.sin(ang_r), z, -jnp.sin(ang_c), z], axis=-1)
    sin_b = jnp.concatenate([z, jnp.sin(ang_r), z, jnp.sin(ang_c)], axis=-1)

    def full(tab, fill):
        tab = jnp.tile(tab, (1, LANES // HEAD_DIM))
        return jnp.concatenate([jnp.full((ctx_len, LANES), fill, F32), tab], axis=0)

    return full(cos, 1.0), full(sin_a, 0.0), full(sin_b, 0.0)


def kernel(x, c, ctx, c_ctx, w_ada, b_ada, norm1_g, norm2_g, w_in, w_out, lam_q1, lam_k1, lam_q2, lam_k2,
           subln_g, q_norm_g, k_norm_g, sink, w_router, router_bias, w_gate, w_up, w_down, final_g):
    batch, seq, d = x.shape
    ctx_len = ctx.shape[1]
    depth = w_in.shape[0]
    sc_rows = SC_WIN * math.prod(_sc_workers())
    assert ctx_len == TILE and seq % TILE == 0 and batch % PAIR == 0 and batch <= 15
    assert (batch * seq) % sc_rows == 0 and (batch * (seq + ctx_len)) % sc_rows == 0

    cpad = jnp.zeros((16, d), F32).at[:batch].set(c).at[batch].set(c_ctx)
    mod_all = _adaln(cpad, w_ada, b_ada)
    cos, sin_a, sin_b = _rope_tables(seq, ctx_len)
    mseg = jnp.asarray(np.kron(np.eye(2), np.ones((HEAD_DIM, HEAD_DIM))), BF16)
    in_perm = _in_col_perm()
    out_perm = _out_row_perm()
    wr = jnp.zeros((d, LANES), F32).at[:, :N_EXPERTS].set(w_router)
    wrh = wr.astype(BF16)
    wrl = (wr - wrh.astype(F32)).astype(BF16)
    rb = jnp.zeros((1, LANES), F32).at[0, :N_EXPERTS].set(router_bias)

    xcat = jnp.concatenate([ctx, x], axis=1)

    for l in range(depth):
        last = l == depth - 1
        lambda_init = 0.8 - 0.6 * math.exp(-0.3 * l)
        m6 = mod_all[l].reshape(16, 6, d)
        m8 = jnp.concatenate([m6, jnp.zeros((16, 2, d), F32)], axis=1)
        mod = jnp.stack([jnp.broadcast_to(m8[batch], (batch, 8, d)), m8[:batch]], axis=1)

        w_in_p = w_in[l][:, in_perm].astype(BF16)
        w_out_p = w_out[l][out_perm, :].astype(BF16)
        gq = jnp.tile(q_norm_g[l], 2).reshape(1, LANES)
        gk = jnp.tile(k_norm_g[l], 2).reshape(1, LANES)
        q, k, v = _inproj(xcat, mod, norm1_g[l].reshape(1, d), w_in_p, cos, sin_a, sin_b, gq, gk, mseg)

        t0 = 1 if last else 0
        lamv = jnp.stack([lam_q1[l], lam_k1[l], lam_q2[l], lam_k2[l]])
        mix = _attention(sink[l], lamv, subln_g[l].reshape(1, LANES), q, k, v, t0, lambda_init)
        x1, h2, gates = _outproj(xcat, mix, w_out_p, mod, norm2_g[l].reshape(1, d), wrh, wrl, rb, t0)
        xcat = _moe(h2, gates, x1, mod, w_gate, w_up, w_down, l, final_g.reshape(1, d), t0, final_norm=last)
    return xcat
```

```python
import functools
import math

import numpy as np
import jax
import jax.numpy as jnp
from jax import lax
from jax.experimental import pallas as pl
from jax.experimental.pallas import tpu as pltpu
from jax.experimental.pallas import tpu_sc as plsc

F32 = jnp.float32
BF16 = jnp.bfloat16

HEAD_DIM = 64
GRID_W = 64
ROPE_BASE = 10000.0
RMS_EPS = 1e-6
WINDOW = 128
N_EXPERTS = 16
EXPERTS_PER_GROUP = 4
LANES = 128
TILE = 256
PAIR = 2
FFN_TILE = 512
SC_WIN = 64
GROUP_LANE = 16
NEG_BIG = -1e30
KEY_CHUNK = 256
VMEM_LIMIT = 52 * 1024 * 1024

QA, QB, QC = 512, 256, 256
KA, KB, KC = 512, 128, 128
Q_COLS = QA + QB + QC
K_COLS = KA + KB + KC
V_COLS = 768
ROPE_COLS = Q_COLS + K_COLS


def _dot(a, b):
    return jnp.dot(a, b, preferred_element_type=F32)


def _dot_nt(a, b):
    return lax.dot_general(a, b, (((1,), (1,)), ((), ())), preferred_element_type=F32)


def _split(a):
    hi = a.astype(BF16)
    lo = (a - hi.astype(F32)).astype(BF16)
    return hi, lo


def _dot3(a, b):
    ah, al = _split(a)
    bh, bl = _split(b)
    return _dot(ah, bh) + _dot(ah, bl) + _dot(al, bh)


def _rms(x, g):
    ms = jnp.mean(x * x, axis=-1, keepdims=True)
    return x * lax.rsqrt(ms + RMS_EPS) * g


def _sigmoid(x):
    return 1.0 / (1.0 + jnp.exp(-x))


def _pack_bf16_pairs(y):
    n = y.shape[1] // 2
    yb = y.astype(BF16).astype(F32)
    lo = lax.bitcast_convert_type(yb[:, :n], jnp.uint32) >> 16
    hi = lax.bitcast_convert_type(yb[:, n:], jnp.uint32) & jnp.uint32(0xFFFF0000)
    return hi | lo


def _unpack_bf16_pairs(w):
    lo = lax.bitcast_convert_type(w << 16, F32)
    hi = lax.bitcast_convert_type(w & jnp.uint32(0xFFFF0000), F32)
    return lo, hi


def _stream_rows(stream_refs, i):
    if len(stream_refs) == 1:
        return stream_refs[0][i]
    x_ref, y_ref, mod_ref = stream_refs
    lo, hi = _unpack_bf16_pairs(y_ref[i])
    return x_ref[i] + mod_ref[i, 0, 5:6, :] * jnp.concatenate([lo, hi], axis=1)


def _stream_specs(stream, nb, tile_of):
    d = stream[0].shape[-1]
    tok = lambda b, t: (b, tile_of(t), 0)
    specs = [pl.BlockSpec((nb, TILE, d), tok)]
    if len(stream) == 3:
        specs += [pl.BlockSpec((nb, TILE, d // 2), tok),
                  pl.BlockSpec((nb, 1, 8, d), lambda b, t: (b, jnp.minimum(tile_of(t), 1), 0, 0))]
    return specs


def _adaln_kernel(c_ref, w_ref, b_ref, o_ref):
    cv = c_ref[...]
    s = cv * _sigmoid(cv)
    o_ref[0] = _dot3(s, w_ref[0]) + b_ref[0]


def _adaln(cpad, w_ada, b_ada):
    depth, d, n6 = w_ada.shape
    tn = 1536
    return pl.pallas_call(
        _adaln_kernel,
        grid=(depth, n6 // tn),
        in_specs=[pl.BlockSpec((16, d), lambda l, j: (0, 0)),
                  pl.BlockSpec((1, d, tn), lambda l, j: (l, 0, j)),
                  pl.BlockSpec((1, 1, tn), lambda l, j: (l, 0, j))],
        out_specs=pl.BlockSpec((1, 16, tn), lambda l, j: (l, 0, j)),
        out_shape=jax.ShapeDtypeStruct((depth, 16, n6), F32),
        compiler_params=pltpu.CompilerParams(dimension_semantics=("parallel", "parallel"),
                                             vmem_limit_bytes=VMEM_LIMIT),
        name="adaln",
    )(cpad, w_ada, b_ada.reshape(depth, 1, n6))


def _inproj_kernel(*refs, n_stream):
    for base in range(0, refs[0].shape[0], PAIR):
        _inproj_rows(base, refs[:n_stream], *refs[n_stream:])


def _inproj_rows(base, stream_refs, mod_ref, g_ref, w_ref, cos_ref, sa_ref, sb_ref, gq_ref, gk_ref, mseg_ref,
                 q_ref, k_ref, v_ref):
    members = range(base, base + PAIR)
    h = jnp.concatenate(
        [(_rms(_stream_rows(stream_refs, i), g_ref[...]) * (1.0 + mod_ref[i, 0, 1:2, :])
          + mod_ref[i, 0, 0:1, :]).astype(BF16) for i in members], axis=0)
    cos = jnp.concatenate([cos_ref[...]] * PAIR, axis=0)
    sa = jnp.concatenate([sa_ref[...]] * PAIR, axis=0)
    sb = jnp.concatenate([sb_ref[...]] * PAIR, axis=0)
    mseg = mseg_ref[...]

    def put(ref, cols, val):
        for n, i in enumerate(members):
            ref[i, :, cols] = val[n * TILE:(n + 1) * TILE]

    def rope(b):
        return b * cos + pltpu.roll(b, LANES - 16, 1) * sa + pltpu.roll(b, 16, 1) * sb

    def qknorm(b, g):
        hi, lo = _split(b * b)
        ms = (_dot(hi, mseg) + _dot(lo, mseg)) * (1.0 / HEAD_DIM)
        return b * lax.rsqrt(ms + RMS_EPS) * g

    for c in range(ROPE_COLS // 256):
        p2 = _dot(h, w_ref[:, c * 256:(c + 1) * 256])
        for half in range(2):
            j = 2 * c + half
            p = p2[:, half * LANES:(half + 1) * LANES]
            if j in (4, 5):
                p = qknorm(p, gq_ref[...])
            if j == 12:
                p = qknorm(p, gk_ref[...])
            p = rope(p)
            if j < Q_COLS // LANES:
                put(q_ref, slice(j * LANES, (j + 1) * LANES), (p * (HEAD_DIM ** -0.5)).astype(BF16))
            else:
                jk = j - Q_COLS // LANES
                put(k_ref, slice(jk * LANES, (jk + 1) * LANES), p.astype(BF16))
    for c in range(V_COLS // 256):
        lo_c = ROPE_COLS + c * 256
        put(v_ref, slice(c * 256, (c + 1) * 256), _dot(h, w_ref[:, lo_c:lo_c + 256]).astype(BF16))


def _inproj(stream, mod, g, w, cos, sa, sb, gq, gk, mseg):
    batch, t_len, d = stream[0].shape
    nt = t_len // TILE
    nb = 2 * PAIR
    row = lambda b, t: (t, 0)
    const = lambda b, t: (0, 0)
    tok = lambda b, t: (b, t, 0)
    return pl.pallas_call(
        functools.partial(_inproj_kernel, n_stream=len(stream)),
        grid=(batch // nb, nt),
        in_specs=_stream_specs(stream, nb, lambda t: t) + [
                  pl.BlockSpec((nb, 1, 8, d), lambda b, t: (b, jnp.minimum(t, 1), 0, 0)),
                  pl.BlockSpec((1, d), const),
                  pl.BlockSpec(w.shape, const),
                  pl.BlockSpec((TILE, LANES), row),
                  pl.BlockSpec((TILE, LANES), row),
                  pl.BlockSpec((TILE, LANES), row),
                  pl.BlockSpec((1, LANES), const),
                  pl.BlockSpec((1, LANES), const),
                  pl.BlockSpec((LANES, LANES), const)],
        out_specs=[pl.BlockSpec((nb, TILE, Q_COLS), tok),
                   pl.BlockSpec((nb, TILE, K_COLS), tok),
                   pl.BlockSpec((nb, TILE, V_COLS), tok)],
        out_shape=[jax.ShapeDtypeStruct((batch, t_len, Q_COLS), BF16),
                   jax.ShapeDtypeStruct((batch, t_len, K_COLS), BF16),
                   jax.ShapeDtypeStruct((batch, t_len, V_COLS), BF16)],
        compiler_params=pltpu.CompilerParams(dimension_semantics=("parallel", "parallel"),
                                             vmem_limit_bytes=VMEM_LIMIT),
        name="inproj",
    )(*stream, mod, g, w, cos, sa, sb, gq, gk, mseg)


def _softmax_pv(qq, k_ref, v_ref, bi, col, starts, s_ref, mask_fn=None, extra=None):
    half = KEY_CHUNK // 2
    rows = qq.shape[0]
    mrun = None
    for j, st in enumerate(starts):
        s = _dot_nt(qq, k_ref[bi, pl.ds(st, KEY_CHUNK), col:col + LANES])
        if mask_fn is not None:
            s = mask_fn(j, st, s)
        s_ref[0:rows, j * KEY_CHUNK:(j + 1) * KEY_CHUNK] = s
        mj = jnp.maximum(s[:, :half], s[:, half:])
        mrun = mj if mrun is None else jnp.maximum(mrun, mj)
    m = jnp.max(mrun, axis=-1, keepdims=True)
    if extra is not None:
        m = jnp.maximum(m, extra)
    lrun = None
    acc = None
    for j, st in enumerate(starts):
        e = jnp.exp(s_ref[0:rows, j * KEY_CHUNK:(j + 1) * KEY_CHUNK] - m)
        lj = e[:, :half] + e[:, half:]
        lrun = lj if lrun is None else lrun + lj
        pv = _dot(e.astype(BF16), v_ref[bi, pl.ds(st, KEY_CHUNK), col:col + LANES])
        acc = pv if acc is None else acc + pv
    l = jnp.sum(lrun, axis=-1, keepdims=True)
    if extra is not None:
        l = l + jnp.exp(extra - m)
    return acc / l


def _all_chunks(n_keys):
    return [j * KEY_CHUNK for j in range(n_keys // KEY_CHUNK)]


def _half_masks(shape):
    lane = lax.broadcasted_iota(jnp.int32, shape, 1)
    return lane < HEAD_DIM, lane >= HEAD_DIM


def _stack_group_queries(q2, kv):
    lo_m, hi_m = _half_masks((TILE, LANES))
    m = lo_m if kv == 0 else hi_m
    zero = jnp.zeros((TILE, LANES), q2.dtype)
    return jnp.concatenate([jnp.where(m, q2[:, :LANES], zero), jnp.where(m, q2[:, LANES:], zero)], axis=0)


def _merge_kv_outputs(o_kv0, o_kv1):
    lo_m, _ = _half_masks((TILE, LANES))
    g0 = jnp.where(lo_m, o_kv0[:TILE], o_kv1[:TILE])
    g1 = jnp.where(lo_m, o_kv0[TILE:], o_kv1[TILE:])
    return jnp.concatenate([g0, g1], axis=1)


def _attn_kernel(sink_ref, lam_ref, subg_ref, qa_ref, qb_ref, qc_ref, ka_ref, kb_ref, kc_ref, va_ref, vb_ref, vc_ref,
                 o_ref, s_ref, *, t0, ctx_len, lambda_init):
    t = pl.program_id(1) + t0
    t_len = ka_ref.shape[1]
    band = 2 * TILE
    lo_m, hi_m = _half_masks((TILE, LANES))
    zero = jnp.zeros((TILE, LANES), BF16)
    row = lax.broadcasted_iota(jnp.int32, (2 * TILE, 1), 0)
    lv = lam_ref[...]
    lam = (jnp.exp(jnp.sum(lv[0:1] * lv[1:2], keepdims=True))
           - jnp.exp(jnp.sum(lv[2:3] * lv[3:4], keepdims=True)) + lambda_init)

    def sink_col(kv):
        return jnp.where(row < TILE, sink_ref[2 * kv], sink_ref[2 * kv + 1])

    def mixers(starts_ab, starts_c, mask_c):
        n_chain = 0
        for bi in range(qa_ref.shape[0]):
            for h in range(QA // LANES):
                q = qa_ref[bi, :, h * LANES:(h + 1) * LANES]
                qq = jnp.concatenate([jnp.where(lo_m, q, zero), jnp.where(hi_m, q, zero)], axis=0)
                o = _softmax_pv(qq, ka_ref, va_ref, bi, h * LANES, starts_ab, s_ref.at[n_chain % 2])
                n_chain += 1
                o = o[:TILE] - lam * o[TILE:]
                o_ref[bi, :, h * LANES:(h + 1) * LANES] = (
                    _rms(o, subg_ref[...]) * (1.0 - lambda_init)).astype(BF16)
            outs = []
            for kv in range(2):
                outs.append(_softmax_pv(_stack_group_queries(qb_ref[bi], kv), kb_ref, vb_ref, bi, 0, starts_ab,
                                        s_ref.at[n_chain % 2]))
                n_chain += 1
            o_ref[bi, :, QA:QA + QB] = _merge_kv_outputs(*outs).astype(BF16)
            outs = []
            for kv in range(2):
                outs.append(_softmax_pv(_stack_group_queries(qc_ref[bi], kv), kc_ref, vc_ref, bi, 0, starts_c,
                                        s_ref.at[n_chain % 2], mask_c, sink_col(kv)))
                n_chain += 1
            o_ref[bi, :, QA + QB:] = _merge_kv_outputs(*outs).astype(BF16)

    def ctx_step():
        mixers([0], [0], None)

    def latent_step():
        q_start = (t - 1) * TILE
        u0 = pl.multiple_of(jnp.minimum(q_start + ctx_len - WINDOW, t_len - band), WINDOW)
        q_pos = q_start + (lax.broadcasted_iota(jnp.int32, (2 * TILE, KEY_CHUNK), 0) & (TILE - 1))
        key_lane = lax.broadcasted_iota(jnp.int32, (2 * TILE, KEY_CHUNK), 1)

        def mask_c(j, st, s):
            if j == 0:
                return s
            k_pos = (st - ctx_len) + key_lane
            valid = (jnp.abs(k_pos - q_pos) <= WINDOW) & (k_pos >= 0)
            return jnp.where(valid, s, NEG_BIG)

        mixers(_all_chunks(t_len), [0] + [u0 + i * KEY_CHUNK for i in range(band // KEY_CHUNK)], mask_c)

    if t0 == 0:
        pl.when(t == 0)(ctx_step)
        pl.when(t > 0)(latent_step)
    else:
        latent_step()


def _attention(sink, lamv, subg, q, k, v, t0, lambda_init):
    batch, t_len, _ = q.shape
    nt = t_len // TILE
    kern = functools.partial(_attn_kernel, t0=t0, ctx_len=TILE, lambda_init=lambda_init)
    const = lambda b, t: (0, 0)
    nb = 1
    qspec = lambda width, blk: pl.BlockSpec((nb, TILE, width), lambda b, t: (b, t + t0, blk))
    kvspec = lambda width, blk: pl.BlockSpec((nb, t_len, width), lambda b, t: (b, 0, blk))
    return pl.pallas_call(
        kern,
        grid=(batch // nb, nt - t0),
        in_specs=[pl.BlockSpec(memory_space=pltpu.SMEM),
                  pl.BlockSpec(lamv.shape, const),
                  pl.BlockSpec((1, LANES), const),
                  qspec(QA, 0), qspec(QB, QA // QB), qspec(QC, (QA + QB) // QC),
                  kvspec(KA, 0), kvspec(KB, KA // KB), kvspec(KC, (KA + KB) // KC),
                  kvspec(KA, 0), kvspec(KB, KA // KB), kvspec(KC, (KA + KB) // KC)],
        out_specs=pl.BlockSpec((nb, TILE, Q_COLS), lambda b, t: (b, t + t0, 0)),
        out_shape=jax.ShapeDtypeStruct((batch, t_len, Q_COLS), BF16),
        scratch_shapes=[pltpu.VMEM((2, 2 * TILE, t_len), F32)],
        compiler_params=pltpu.CompilerParams(dimension_semantics=("parallel", "parallel"),
                                             vmem_limit_bytes=VMEM_LIMIT),
        name="attention",
    )(sink, lamv, subg, q, q, q, k, k, k, v, v, v)


def _route(scores, bias):
    lane = lax.broadcasted_iota(jnp.int32, scores.shape, 1)
    sel = scores + bias
    in_group = lane & (EXPERTS_PER_GROUP - 1)
    group = lane >> 2

    def neighbours(x, idx, step, span):
        for d in (1, 2, 3):
            fwd = (idx + d) < 4
            y = jnp.where(fwd, pltpu.roll(x, LANES - d * step, 1), pltpu.roll(x, span - d * step, 1))
            yield y, fwd

    rank = jnp.zeros_like(sel)
    for y, fwd in neighbours(sel, in_group, 1, EXPERTS_PER_GROUP):
        beats = (y > sel) | ((y == sel) & jnp.logical_not(fwd))
        rank = rank + jnp.where(beats, 1.0, 0.0)
    top2 = jnp.where(rank < 2.0, sel, 0.0)
    gsum = top2
    for y, _ in neighbours(top2, in_group, 1, EXPERTS_PER_GROUP):
        gsum = gsum + y
    grank = jnp.zeros_like(sel)
    for y, fwd in neighbours(gsum, group, EXPERTS_PER_GROUP, N_EXPERTS):
        beats = (y > gsum) | ((y == gsum) & jnp.logical_not(fwd))
        grank = grank + jnp.where(beats, 1.0, 0.0)
    best = (grank < 0.5) & (lane < N_EXPERTS)
    chosen = best & (rank < 2.0)
    w = jnp.where(chosen, scores, 0.0)
    gates = w / jnp.sum(w, axis=-1, keepdims=True)
    gid = jnp.sum(jnp.where(best & (in_group == 0), group.astype(F32), 0.0), axis=-1, keepdims=True)
    return jnp.where(lane == GROUP_LANE, gid, gates)


def _outproj_kernel(*refs, n_stream):
    stream_refs = refs[:n_stream]
    mix_ref, w_ref, mod_ref, g_ref, wrh_ref, wrl_ref, rb_ref, x1_ref, h2_ref, gates_ref = refs[n_stream:]
    for i in range(mix_ref.shape[0]):
        x1 = _stream_rows(stream_refs, i) + mod_ref[i, 0, 2:3, :] * _dot(mix_ref[i], w_ref[...])
        x1_ref[i] = x1
        h2 = _rms(x1, g_ref[...]) * (1.0 + mod_ref[i, 0, 4:5, :]) + mod_ref[i, 0, 3:4, :]
        h2_ref[i] = _pack_bf16_pairs(h2)
        hi, lo = _split(h2)
        logits = _dot(hi, wrh_ref[...]) + _dot(hi, wrl_ref[...]) + _dot(lo, wrh_ref[...])
        gates_ref[i] = _route(_sigmoid(logits), rb_ref[...])


def _outproj(stream, mix, w, mod, g, wrh, wrl, rb, t0):
    batch, t_len, d = stream[0].shape
    nt = t_len // TILE
    nb = 2 * PAIR
    t_out = (nt - t0) * TILE
    const = lambda b, t: (0, 0)
    tok_in = lambda b, t: (b, t + t0, 0)
    tok_out = lambda b, t: (b, t, 0)
    return pl.pallas_call(
        functools.partial(_outproj_kernel, n_stream=len(stream)),
        grid=(batch // nb, nt - t0),
        in_specs=_stream_specs(stream, nb, lambda t: t + t0) + [
                  pl.BlockSpec((nb, TILE, Q_COLS), tok_in),
                  pl.BlockSpec(w.shape, const),
                  pl.BlockSpec((nb, 1, 8, d), lambda b, t: (b, jnp.minimum(t + t0, 1), 0, 0)),
                  pl.BlockSpec((1, d), const),
                  pl.BlockSpec((d, LANES), const),
                  pl.BlockSpec((d, LANES), const),
                  pl.BlockSpec((1, LANES), const)],
        out_specs=[pl.BlockSpec((nb, TILE, d), tok_out),
                   pl.BlockSpec((nb, TILE, d // 2), tok_out),
                   pl.BlockSpec((nb, TILE, LANES), tok_out)],
        out_shape=[jax.ShapeDtypeStruct((batch, t_out, d), F32),
                   jax.ShapeDtypeStruct((batch, t_out, d // 2), jnp.uint32),
                   jax.ShapeDtypeStruct((batch, t_out, LANES), F32)],
        compiler_params=pltpu.CompilerParams(dimension_semantics=("parallel", "parallel"),
                                             vmem_limit_bytes=VMEM_LIMIT),
        name="outproj",
    )(*stream, mix, w, mod, g, wrh, wrl, rb)


def _sc_workers():
    info = plsc.get_sparse_core_info()
    return info.num_cores, info.num_subcores


def _sc_dispatch(hpk, gates, pos2d, n_out):
    n, w = hpk.shape
    gw = gates.shape[1]
    nc, ns = _sc_workers()
    steps = n // (nc * ns * SC_WIN)
    mesh = plsc.VectorSubcoreMesh(core_axis_name="c", subcore_axis_name="s")

    @functools.partial(
        pl.kernel, mesh=mesh,
        out_type=(jax.ShapeDtypeStruct((n_out, w), hpk.dtype), jax.ShapeDtypeStruct((n_out, gw), gates.dtype)),
        scratch_types=[pltpu.VMEM((1, SC_WIN), jnp.int32), pltpu.VMEM((SC_WIN, w), hpk.dtype),
                       pltpu.VMEM((SC_WIN, gw), gates.dtype)],
        name="sc_dispatch")
    def k(h_hbm, g_hbm, pos_hbm, hs_hbm, gs_hbm, idx_v, rows_v, grow_v):
        wid = lax.axis_index("s") * nc + lax.axis_index("c")

        @pl.loop(0, steps)
        def _(j):
            blk = wid * steps + j
            off = pl.multiple_of(blk * SC_WIN, SC_WIN)
            pltpu.sync_copy(pos_hbm.at[pl.ds(blk, 1)], idx_v)
            pltpu.sync_copy(h_hbm.at[pl.ds(off, SC_WIN)], rows_v)
            pltpu.sync_copy(g_hbm.at[pl.ds(off, SC_WIN)], grow_v)
            pltpu.sync_copy(rows_v, hs_hbm.at[idx_v.at[0]])
            pltpu.sync_copy(grow_v, gs_hbm.at[idx_v.at[0]])

    return k(hpk, gates, pos2d)


def _sc_combine(ys, pos2d, n):
    w = ys.shape[1]
    nc, ns = _sc_workers()
    steps = n // (nc * ns * SC_WIN)
    mesh = plsc.VectorSubcoreMesh(core_axis_name="c", subcore_axis_name="s")

    @functools.partial(
        pl.kernel, mesh=mesh,
        out_type=jax.ShapeDtypeStruct((n, w), ys.dtype),
        scratch_types=[pltpu.VMEM((1, SC_WIN), jnp.int32), pltpu.VMEM((SC_WIN, w), ys.dtype)],
        name="sc_combine")
    def k(y_hbm, pos_hbm, o_hbm, idx_v, rows_v):
        wid = lax.axis_index("s") * nc + lax.axis_index("c")

        @pl.loop(0, steps)
        def _(j):
            blk = wid * steps + j
            off = pl.multiple_of(blk * SC_WIN, SC_WIN)
            pltpu.sync_copy(pos_hbm.at[pl.ds(blk, 1)], idx_v)
            pltpu.sync_copy(y_hbm.at[idx_v.at[0]], rows_v)
            pltpu.sync_copy(rows_v, o_hbm.at[pl.ds(off, SC_WIN)])

    return k(ys, pos2d)


def _bucket_plan(group, n_pad):
    n_groups = N_EXPERTS // EXPERTS_PER_GROUP
    onehot = (group[:, None] == jnp.arange(n_groups, dtype=jnp.int32)[None, :]).astype(jnp.int32)
    csum = jnp.cumsum(onehot, axis=0)
    counts = csum[-1]
    padded = ((counts + FFN_TILE - 1) // FFN_TILE) * FFN_TILE
    ends = jnp.cumsum(padded)
    starts = ends - padded
    pos = starts[group] + jnp.sum(csum * onehot, axis=1) - 1
    tile_row = jnp.arange(n_pad // FFN_TILE, dtype=jnp.int32) * FFN_TILE
    tile_group = jnp.minimum(jnp.sum((tile_row[:, None] >= ends[None, :]).astype(jnp.int32), axis=1), n_groups - 1)
    tile_used = (tile_row < (starts + counts)[tile_group]).astype(jnp.int32)
    return pos.astype(jnp.int32), tile_group.astype(jnp.int32), tile_used


def _ffn_kernel(tg_ref, tu_ref, hs_ref, gs_ref, wg_ref, wu_ref, wd_ref, ys_ref, wgb_ref, wub_ref, wdb_ref):
    i = pl.program_id(0)

    @pl.when((i == 0) | (tg_ref[i] != tg_ref[jnp.maximum(i - 1, 0)]))
    def _():
        wgb_ref[...] = wg_ref[0].astype(BF16)
        wub_ref[...] = wu_ref[0].astype(BF16)
        wdb_ref[...] = wd_ref[0].astype(BF16)

    @pl.when(tu_ref[i] > 0)
    def _():
        lo, hi = _unpack_bf16_pairs(hs_ref[...])
        lo = lo.astype(BF16)
        hi = hi.astype(BF16)
        half = lo.shape[1]
        gs = gs_ref[...]
        lane = lax.broadcasted_iota(jnp.int32, gs.shape, 1)
        first = tg_ref[i] * EXPERTS_PER_GROUP
        y = None
        for e in range(EXPERTS_PER_GROUP):
            a = _dot(lo, wgb_ref[e, :half, :]) + _dot(hi, wgb_ref[e, half:, :])
            u = _dot(lo, wub_ref[e, :half, :]) + _dot(hi, wub_ref[e, half:, :])
            gate = jnp.sum(jnp.where(lane == first + e, gs, 0.0), axis=-1, keepdims=True)
            part = _dot(((a * _sigmoid(a)) * u * gate).astype(BF16), wdb_ref[e])
            y = part if y is None else y + part
        ys_ref[...] = _pack_bf16_pairs(y)


def _ffn(tile_group, tile_used, hs, gs, wg, wu, wd, layer):
    n_pad, half = hs.shape
    grp = lambda i, tg, tu: (layer, tg[i], 0, 0)
    gshape = lambda w: (EXPERTS_PER_GROUP,) + w.shape[2:]
    wspec = lambda w: pl.BlockSpec((1,) + gshape(w), grp)
    return pl.pallas_call(
        _ffn_kernel,
        grid_spec=pltpu.PrefetchScalarGridSpec(
            num_scalar_prefetch=2,
            grid=(n_pad // FFN_TILE,),
            in_specs=[pl.BlockSpec((FFN_TILE, half), lambda i, tg, tu: (i, 0)),
                      pl.BlockSpec((FFN_TILE, LANES), lambda i, tg, tu: (i, 0)),
                      wspec(wg), wspec(wu), wspec(wd)],
            out_specs=pl.BlockSpec((FFN_TILE, half), lambda i, tg, tu: (i, 0)),
            scratch_shapes=[pltpu.VMEM(gshape(wg), BF16), pltpu.VMEM(gshape(wu), BF16),
                            pltpu.VMEM(gshape(wd), BF16)]),
        out_shape=jax.ShapeDtypeStruct((n_pad, half), jnp.uint32),
        compiler_params=pltpu.CompilerParams(dimension_semantics=("arbitrary",), vmem_limit_bytes=VMEM_LIMIT),
        name="ffn",
    )(tile_group, tile_used, hs, gs, wg, wu, wd)


def _residual_kernel(x_ref, y_ref, mod_ref, fg_ref, o_ref, *, final_norm):
    for i in range(x_ref.shape[0]):
        lo, hi = _unpack_bf16_pairs(y_ref[i])
        y = x_ref[i] + mod_ref[i, 0, 5:6, :] * jnp.concatenate([lo, hi], axis=1)
        if final_norm:
            y = _rms(y, fg_ref[...])
        o_ref[i] = y


def _residual(x1, yg, mod, fg, t0, final_norm):
    batch, t_out, d = x1.shape
    nb = PAIR
    tok = lambda b, t: (b, t, 0)
    return pl.pallas_call(
        functools.partial(_residual_kernel, final_norm=final_norm),
        grid=(batch // nb, t_out // TILE),
        in_specs=[pl.BlockSpec((nb, TILE, d), tok),
                  pl.BlockSpec((nb, TILE, d // 2), tok),
                  pl.BlockSpec((nb, 1, 8, d), lambda b, t: (b, jnp.minimum(t + t0, 1), 0, 0)),
                  pl.BlockSpec((1, d), lambda b, t: (0, 0))],
        out_specs=pl.BlockSpec((nb, TILE, d), tok),
        out_shape=jax.ShapeDtypeStruct((batch, t_out, d), F32),
        compiler_params=pltpu.CompilerParams(dimension_semantics=("parallel", "parallel"),
                                             vmem_limit_bytes=VMEM_LIMIT),
        name="residual",
    )(x1, yg, mod, fg)


def _moe(h2, gates, wg, wu, wd, layer):
    batch, t_out, half = h2.shape
    n = batch * t_out
    n_pad = n + (N_EXPERTS // EXPERTS_PER_GROUP) * FFN_TILE
    gates = gates.reshape(n, LANES)
    pos, tile_group, tile_used = _bucket_plan(gates[:, GROUP_LANE].astype(jnp.int32), n_pad)
    pos2d = pos.reshape(n // SC_WIN, SC_WIN)
    hs, gs = _sc_dispatch(h2.reshape(n, half), gates, pos2d, n_pad)
    ys = _ffn(tile_group, tile_used, hs, gs, wg, wu, wd, layer)
    return _sc_combine(ys, pos2d, n).reshape(batch, t_out, half)


def _gqa_perm():
    return np.arange(256).reshape(2, 2, HEAD_DIM).transpose(1, 0, 2).reshape(-1)


def _in_col_perm():
    g = _gqa_perm()
    q = np.concatenate([np.arange(512), 512 + g, 768 + g])
    kv0 = 1024
    k = np.concatenate([kv0 + np.arange(512), kv0 + 1024 + np.arange(128), kv0 + 1280 + np.arange(128)])
    v = np.concatenate([kv0 + 512 + np.arange(512), kv0 + 1152 + np.arange(128), kv0 + 1408 + np.arange(128)])
    return np.concatenate([q, k, v])


def _out_row_perm():
    g = _gqa_perm()
    return np.concatenate([np.arange(512), 512 + g, 768 + g])


def _rope_tables(seq, ctx_len):
    rows = seq // GRID_W
    row_pos = jnp.repeat(jnp.arange(rows, dtype=F32), GRID_W)
    col_pos = jnp.tile(jnp.arange(GRID_W, dtype=F32), rows)
    axis_dim = HEAD_DIM // 2
    inv_freq = ROPE_BASE ** (-jnp.arange(0, axis_dim, 2, dtype=F32) / axis_dim)
    ang_r = row_pos[:, None] * inv_freq[None, :]
    ang_c = col_pos[:, None] * inv_freq[None, :]
    z = jnp.zeros_like(ang_r)
    cos = jnp.concatenate([jnp.cos(ang_r)] * 2 + [jnp.cos(ang_c)] * 2, axis=-1)
    sin_a = jnp.concatenate([-jnp.sin(ang_r), z, -jnp.sin(ang_c), z], axis=-1)
    sin_b = jnp.concatenate([z, jnp.sin(ang_r), z, jnp.sin(ang_c)], axis=-1)

    def full(tab, fill):
        tab = jnp.tile(tab, (1, LANES // HEAD_DIM))
        return jnp.concatenate([jnp.full((ctx_len, LANES), fill, F32), tab], axis=0)

    return full(cos, 1.0), full(sin_a, 0.0), full(sin_b, 0.0)


def kernel(x, c, ctx, c_ctx, w_ada, b_ada, norm1_g, norm2_g, w_in, w_out, lam_q1, lam_k1, lam_q2, lam_k2,
           subln_g, q_norm_g, k_norm_g, sink, w_router, router_bias, w_gate, w_up, w_down, final_g):
    batch, seq, d = x.shape
    ctx_len = ctx.shape[1]
    depth = w_in.shape[0]
    sc_rows = SC_WIN * math.prod(_sc_workers())
    assert ctx_len == TILE and seq % TILE == 0 and batch % PAIR == 0 and batch <= 15
    assert (batch * seq) % sc_rows == 0 and (batch * (seq + ctx_len)) % sc_rows == 0

    cpad = jnp.zeros((16, d), F32).at[:batch].set(c).at[batch].set(c_ctx)
    mod_all = _adaln(cpad, w_ada, b_ada)
    cos, sin_a, sin_b = _rope_tables(seq, ctx_len)
    mseg = jnp.asarray(np.kron(np.eye(2), np.ones((HEAD_DIM, HEAD_DIM))), BF16)
    in_perm = _in_col_perm()
    out_perm = _out_row_perm()
    wr = jnp.zeros((d, LANES), F32).at[:, :N_EXPERTS].set(w_router)
    wrh = wr.astype(BF16)
    wrl = (wr - wrh.astype(F32)).astype(BF16)
    rb = jnp.zeros((1, LANES), F32).at[0, :N_EXPERTS].set(router_bias)

    stream = (jnp.concatenate([ctx, x], axis=1),)

    for l in range(depth):
        last = l == depth - 1
        lambda_init = 0.8 - 0.6 * math.exp(-0.3 * l)
        m6 = mod_all[l].reshape(16, 6, d)
        m8 = jnp.concatenate([m6, jnp.zeros((16, 2, d), F32)], axis=1)
        mod = jnp.stack([jnp.broadcast_to(m8[batch], (batch, 8, d)), m8[:batch]], axis=1)

        w_in_p = w_in[l][:, in_perm].astype(BF16)
        w_out_p = w_out[l][out_perm, :].astype(BF16)
        gq = jnp.tile(q_norm_g[l], 2).reshape(1, LANES)
        gk = jnp.tile(k_norm_g[l], 2).reshape(1, LANES)
        q, k, v = _inproj(stream, mod, norm1_g[l].reshape(1, d), w_in_p, cos, sin_a, sin_b, gq, gk, mseg)

        t0 = 1 if last else 0
        lamv = jnp.stack([lam_q1[l], lam_k1[l], lam_q2[l], lam_k2[l]])
        mix = _attention(sink[l], lamv, subln_g[l].reshape(1, LANES), q, k, v, t0, lambda_init)
        x1, h2, gates = _outproj(stream, mix, w_out_p, mod, norm2_g[l].reshape(1, d), wrh, wrl, rb, t0)
        stream = (x1, _moe(h2, gates, w_gate, w_up, w_down, l), mod)
    return _residual(*stream, final_g.reshape(1, d), 1, final_norm=True)
```

```python
import functools
import math

import numpy as np
import jax
import jax.numpy as jnp
from jax import lax
from jax.experimental import pallas as pl
from jax.experimental.pallas import tpu as pltpu
from jax.experimental.pallas import tpu_sc as plsc

F32 = jnp.float32
BF16 = jnp.bfloat16

HEAD_DIM = 64
GRID_W = 64
ROPE_BASE = 10000.0
RMS_EPS = 1e-6
WINDOW = 128
N_EXPERTS = 16
EXPERTS_PER_GROUP = 4
LANES = 128
TILE = 256
PAIR = 2
FFN_TILE = 512
SC_WIN = 64
GROUP_LANE = 16
NEG_BIG = -1e30
KEY_CHUNK = 256
VMEM_LIMIT = 52 * 1024 * 1024

QA, QB, QC = 512, 256, 256
KA, KB, KC = 512, 128, 128
Q_COLS = QA + QB + QC
K_COLS = KA + KB + KC
V_COLS = 768
ROPE_COLS = Q_COLS + K_COLS


def _dot(a, b):
    return jnp.dot(a, b, preferred_element_type=F32)


def _dot_nt(a, b):
    return lax.dot_general(a, b, (((1,), (1,)), ((), ())), preferred_element_type=F32)


def _split(a):
    hi = a.astype(BF16)
    lo = (a - hi.astype(F32)).astype(BF16)
    return hi, lo


def _dot3(a, b):
    ah, al = _split(a)
    bh, bl = _split(b)
    return _dot(ah, bh) + _dot(ah, bl) + _dot(al, bh)


def _rms(x, g):
    ms = jnp.mean(x * x, axis=-1, keepdims=True)
    return x * lax.rsqrt(ms + RMS_EPS) * g


def _sigmoid(x):
    return 1.0 / (1.0 + jnp.exp(-x))


def _pack_bf16_pairs(y):
    n = y.shape[1] // 2
    yb = y.astype(BF16).astype(F32)
    lo = lax.bitcast_convert_type(yb[:, :n], jnp.uint32) >> 16
    hi = lax.bitcast_convert_type(yb[:, n:], jnp.uint32) & jnp.uint32(0xFFFF0000)
    return hi | lo


def _unpack_bf16_pairs(w):
    lo = lax.bitcast_convert_type(w << 16, F32)
    hi = lax.bitcast_convert_type(w & jnp.uint32(0xFFFF0000), F32)
    return lo, hi


def _stream_dims(stream):
    batch, t_len, d = stream[0].shape
    return batch, (t_len + stream[1].shape[1] if len(stream) == 2 else t_len), d


def _stream_rows(stream_refs, i, tile):
    if len(stream_refs) == 2:
        ctx_ref, x_ref = stream_refs
        return jnp.where(tile == 0, ctx_ref[i], x_ref[i])
    x_ref, y_ref, mod_ref = stream_refs
    lo, hi = _unpack_bf16_pairs(y_ref[i])
    return x_ref[i] + mod_ref[i, 0, 5:6, :] * jnp.concatenate([lo, hi], axis=1)


def _stream_specs(stream, nb, tile_of):
    d = stream[0].shape[-1]
    if len(stream) == 2:
        return [pl.BlockSpec((nb, TILE, d), lambda b, t: (b, 0, 0)),
                pl.BlockSpec((nb, TILE, d), lambda b, t: (b, jnp.maximum(tile_of(t) - 1, 0), 0))]
    tok = lambda b, t: (b, tile_of(t), 0)
    return [pl.BlockSpec((nb, TILE, d), tok),
            pl.BlockSpec((nb, TILE, d // 2), tok),
            pl.BlockSpec((nb, 1, 8, d), lambda b, t: (b, jnp.minimum(tile_of(t), 1), 0, 0))]


def _adaln_kernel(c_ref, w_ref, b_ref, o_ref):
    cv = c_ref[...]
    s = cv * _sigmoid(cv)
    o_ref[0] = _dot3(s, w_ref[0]) + b_ref[0]


def _adaln(cpad, w_ada, b_ada):
    depth, d, n6 = w_ada.shape
    tn = 1536
    return pl.pallas_call(
        _adaln_kernel,
        grid=(depth, n6 // tn),
        in_specs=[pl.BlockSpec((16, d), lambda l, j: (0, 0)),
                  pl.BlockSpec((1, d, tn), lambda l, j: (l, 0, j)),
                  pl.BlockSpec((1, 1, tn), lambda l, j: (l, 0, j))],
        out_specs=pl.BlockSpec((1, 16, tn), lambda l, j: (l, 0, j)),
        out_shape=jax.ShapeDtypeStruct((depth, 16, n6), F32),
        compiler_params=pltpu.CompilerParams(dimension_semantics=("parallel", "parallel"),
                                             vmem_limit_bytes=VMEM_LIMIT),
        name="adaln",
    )(cpad, w_ada, b_ada.reshape(depth, 1, n6))


def _inproj_kernel(*refs, n_stream):
    for base in range(0, refs[0].shape[0], PAIR):
        _inproj_rows(base, pl.program_id(1), refs[:n_stream], *refs[n_stream:])


def _inproj_rows(base, tile, stream_refs, mod_ref, g_ref, w_ref, cos_ref, sa_ref, sb_ref, gq_ref, gk_ref, mseg_ref,
                 q_ref, k_ref, v_ref):
    members = range(base, base + PAIR)
    h = jnp.concatenate(
        [(_rms(_stream_rows(stream_refs, i, tile), g_ref[...]) * (1.0 + mod_ref[i, 0, 1:2, :])
          + mod_ref[i, 0, 0:1, :]).astype(BF16) for i in members], axis=0)
    cos = jnp.concatenate([cos_ref[...]] * PAIR, axis=0)
    sa = jnp.concatenate([sa_ref[...]] * PAIR, axis=0)
    sb = jnp.concatenate([sb_ref[...]] * PAIR, axis=0)
    mseg = mseg_ref[...]

    def put(ref, cols, val):
        for n, i in enumerate(members):
            ref[i, :, cols] = val[n * TILE:(n + 1) * TILE]

    def rope(b):
        return b * cos + pltpu.roll(b, LANES - 16, 1) * sa + pltpu.roll(b, 16, 1) * sb

    def qknorm(b, g):
        hi, lo = _split(b * b)
        ms = (_dot(hi, mseg) + _dot(lo, mseg)) * (1.0 / HEAD_DIM)
        return b * lax.rsqrt(ms + RMS_EPS) * g

    for c in range(ROPE_COLS // 256):
        p2 = _dot(h, w_ref[:, c * 256:(c + 1) * 256])
        for half in range(2):
            j = 2 * c + half
            p = p2[:, half * LANES:(half + 1) * LANES]
            if j in (4, 5):
                p = qknorm(p, gq_ref[...])
            if j == 12:
                p = qknorm(p, gk_ref[...])
            p = rope(p)
            if j < Q_COLS // LANES:
                put(q_ref, slice(j * LANES, (j + 1) * LANES), (p * (HEAD_DIM ** -0.5)).astype(BF16))
            else:
                jk = j - Q_COLS // LANES
                put(k_ref, slice(jk * LANES, (jk + 1) * LANES), p.astype(BF16))
    for c in range(V_COLS // 256):
        lo_c = ROPE_COLS + c * 256
        put(v_ref, slice(c * 256, (c + 1) * 256), _dot(h, w_ref[:, lo_c:lo_c + 256]).astype(BF16))


def _inproj(stream, mod, g, w, cos, sa, sb, gq, gk, mseg):
    batch, t_len, d = _stream_dims(stream)
    nt = t_len // TILE
    nb = 2 * PAIR
    row = lambda b, t: (t, 0)
    const = lambda b, t: (0, 0)
    tok = lambda b, t: (b, t, 0)
    return pl.pallas_call(
        functools.partial(_inproj_kernel, n_stream=len(stream)),
        grid=(batch // nb, nt),
        in_specs=_stream_specs(stream, nb, lambda t: t) + [
                  pl.BlockSpec((nb, 1, 8, d), lambda b, t: (b, jnp.minimum(t, 1), 0, 0)),
                  pl.BlockSpec((1, d), const),
                  pl.BlockSpec(w.shape, const),
                  pl.BlockSpec((TILE, LANES), row),
                  pl.BlockSpec((TILE, LANES), row),
                  pl.BlockSpec((TILE, LANES), row),
                  pl.BlockSpec((1, LANES), const),
                  pl.BlockSpec((1, LANES), const),
                  pl.BlockSpec((LANES, LANES), const)],
        out_specs=[pl.BlockSpec((nb, TILE, Q_COLS), tok),
                   pl.BlockSpec((nb, TILE, K_COLS), tok),
                   pl.BlockSpec((nb, TILE, V_COLS), tok)],
        out_shape=[jax.ShapeDtypeStruct((batch, t_len, Q_COLS), BF16),
                   jax.ShapeDtypeStruct((batch, t_len, K_COLS), BF16),
                   jax.ShapeDtypeStruct((batch, t_len, V_COLS), BF16)],
        compiler_params=pltpu.CompilerParams(dimension_semantics=("parallel", "parallel"),
                                             vmem_limit_bytes=VMEM_LIMIT),
        name="inproj",
    )(*stream, mod, g, w, cos, sa, sb, gq, gk, mseg)


def _softmax_pv(qq, k_ref, v_ref, bi, col, starts, s_ref, mask_fn=None, extra=None):
    half = KEY_CHUNK // 2
    rows = qq.shape[0]
    mrun = None
    for j, st in enumerate(starts):
        s = _dot_nt(qq, k_ref[bi, pl.ds(st, KEY_CHUNK), col:col + LANES])
        if mask_fn is not None:
            s = mask_fn(j, st, s)
        s_ref[0:rows, j * KEY_CHUNK:(j + 1) * KEY_CHUNK] = s
        mj = jnp.maximum(s[:, :half], s[:, half:])
        mrun = mj if mrun is None else jnp.maximum(mrun, mj)
    m = jnp.max(mrun, axis=-1, keepdims=True)
    if extra is not None:
        m = jnp.maximum(m, extra)
    lrun = None
    acc = None
    for j, st in enumerate(starts):
        e = jnp.exp(s_ref[0:rows, j * KEY_CHUNK:(j + 1) * KEY_CHUNK] - m)
        lj = e[:, :half] + e[:, half:]
        lrun = lj if lrun is None else lrun + lj
        pv = _dot(e.astype(BF16), v_ref[bi, pl.ds(st, KEY_CHUNK), col:col + LANES])
        acc = pv if acc is None else acc + pv
    l = jnp.sum(lrun, axis=-1, keepdims=True)
    if extra is not None:
        l = l + jnp.exp(extra - m)
    return acc / l


def _all_chunks(n_keys):
    return [j * KEY_CHUNK for j in range(n_keys // KEY_CHUNK)]


def _half_masks(shape):
    lane = lax.broadcasted_iota(jnp.int32, shape, 1)
    return lane < HEAD_DIM, lane >= HEAD_DIM


def _stack_group_queries(q2, kv):
    lo_m, hi_m = _half_masks((TILE, LANES))
    m = lo_m if kv == 0 else hi_m
    zero = jnp.zeros((TILE, LANES), q2.dtype)
    return jnp.concatenate([jnp.where(m, q2[:, :LANES], zero), jnp.where(m, q2[:, LANES:], zero)], axis=0)


def _merge_kv_outputs(o_kv0, o_kv1):
    lo_m, _ = _half_masks((TILE, LANES))
    g0 = jnp.where(lo_m, o_kv0[:TILE], o_kv1[:TILE])
    g1 = jnp.where(lo_m, o_kv0[TILE:], o_kv1[TILE:])
    return jnp.concatenate([g0, g1], axis=1)


def _attn_kernel(sink_ref, lam_ref, subg_ref, qa_ref, qb_ref, qc_ref, ka_ref, kb_ref, kc_ref, va_ref, vb_ref, vc_ref,
                 o_ref, s_ref, *, t0, ctx_len, lambda_init):
    t = pl.program_id(1) + t0
    t_len = ka_ref.shape[1]
    band = 2 * TILE
    lo_m, hi_m = _half_masks((TILE, LANES))
    zero = jnp.zeros((TILE, LANES), BF16)
    row = lax.broadcasted_iota(jnp.int32, (2 * TILE, 1), 0)
    lv = lam_ref[...]
    lam = (jnp.exp(jnp.sum(lv[0:1] * lv[1:2], keepdims=True))
           - jnp.exp(jnp.sum(lv[2:3] * lv[3:4], keepdims=True)) + lambda_init)

    def sink_col(kv):
        return jnp.where(row < TILE, sink_ref[2 * kv], sink_ref[2 * kv + 1])

    def mixers(starts_ab, starts_c, mask_c):
        n_chain = 0
        for bi in range(qa_ref.shape[0]):
            for h in range(QA // LANES):
                q = qa_ref[bi, :, h * LANES:(h + 1) * LANES]
                qq = jnp.concatenate([jnp.where(lo_m, q, zero), jnp.where(hi_m, q, zero)], axis=0)
                o = _softmax_pv(qq, ka_ref, va_ref, bi, h * LANES, starts_ab, s_ref.at[n_chain % 2])
                n_chain += 1
                o = o[:TILE] - lam * o[TILE:]
                o_ref[bi, :, h * LANES:(h + 1) * LANES] = (
                    _rms(o, subg_ref[...]) * (1.0 - lambda_init)).astype(BF16)
            outs = []
            for kv in range(2):
                outs.append(_softmax_pv(_stack_group_queries(qb_ref[bi], kv), kb_ref, vb_ref, bi, 0, starts_ab,
                                        s_ref.at[n_chain % 2]))
                n_chain += 1
            o_ref[bi, :, QA:QA + QB] = _merge_kv_outputs(*outs).astype(BF16)
            outs = []
            for kv in range(2):
                outs.append(_softmax_pv(_stack_group_queries(qc_ref[bi], kv), kc_ref, vc_ref, bi, 0, starts_c,
                                        s_ref.at[n_chain % 2], mask_c, sink_col(kv)))
                n_chain += 1
            o_ref[bi, :, QA + QB:] = _merge_kv_outputs(*outs).astype(BF16)

    def ctx_step():
        mixers([0], [0], None)

    def latent_step():
        q_start = (t - 1) * TILE
        u0 = pl.multiple_of(jnp.minimum(q_start + ctx_len - WINDOW, t_len - band), WINDOW)
        q_pos = q_start + (lax.broadcasted_iota(jnp.int32, (2 * TILE, KEY_CHUNK), 0) & (TILE - 1))
        key_lane = lax.broadcasted_iota(jnp.int32, (2 * TILE, KEY_CHUNK), 1)

        def mask_c(j, st, s):
            if j == 0:
                return s
            k_pos = (st - ctx_len) + key_lane
            valid = (jnp.abs(k_pos - q_pos) <= WINDOW) & (k_pos >= 0)
            return jnp.where(valid, s, NEG_BIG)

        mixers(_all_chunks(t_len), [0] + [u0 + i * KEY_CHUNK for i in range(band // KEY_CHUNK)], mask_c)

    if t0 == 0:
        pl.when(t == 0)(ctx_step)
        pl.when(t > 0)(latent_step)
    else:
        latent_step()


def _attention(sink, lamv, subg, q, k, v, t0, lambda_init):
    batch, t_len, _ = q.shape
    nt = t_len // TILE
    kern = functools.partial(_attn_kernel, t0=t0, ctx_len=TILE, lambda_init=lambda_init)
    const = lambda b, t: (0, 0)
    nb = 1
    qspec = lambda width, blk: pl.BlockSpec((nb, TILE, width), lambda b, t: (b, t + t0, blk))
    kvspec = lambda width, blk: pl.BlockSpec((nb, t_len, width), lambda b, t: (b, 0, blk))
    return pl.pallas_call(
        kern,
        grid=(batch // nb, nt - t0),
        in_specs=[pl.BlockSpec(memory_space=pltpu.SMEM),
                  pl.BlockSpec(lamv.shape, const),
                  pl.BlockSpec((1, LANES), const),
                  qspec(QA, 0), qspec(QB, QA // QB), qspec(QC, (QA + QB) // QC),
                  kvspec(KA, 0), kvspec(KB, KA // KB), kvspec(KC, (KA + KB) // KC),
                  kvspec(KA, 0), kvspec(KB, KA // KB), kvspec(KC, (KA + KB) // KC)],
        out_specs=pl.BlockSpec((nb, TILE, Q_COLS), lambda b, t: (b, t + t0, 0)),
        out_shape=jax.ShapeDtypeStruct((batch, t_len, Q_COLS), BF16),
        scratch_shapes=[pltpu.VMEM((2, 2 * TILE, t_len), F32)],
        compiler_params=pltpu.CompilerParams(dimension_semantics=("parallel", "parallel"),
                                             vmem_limit_bytes=VMEM_LIMIT),
        name="attention",
    )(sink, lamv, subg, q, q, q, k, k, k, v, v, v)


def _route(scores, bias):
    lane = lax.broadcasted_iota(jnp.int32, scores.shape, 1)
    sel = scores + bias
    in_group = lane & (EXPERTS_PER_GROUP - 1)
    group = lane >> 2

    def neighbours(x, idx, step, span):
        for d in (1, 2, 3):
            fwd = (idx + d) < 4
            y = jnp.where(fwd, pltpu.roll(x, LANES - d * step, 1), pltpu.roll(x, span - d * step, 1))
            yield y, fwd

    rank = jnp.zeros_like(sel)
    for y, fwd in neighbours(sel, in_group, 1, EXPERTS_PER_GROUP):
        beats = (y > sel) | ((y == sel) & jnp.logical_not(fwd))
        rank = rank + jnp.where(beats, 1.0, 0.0)
    top2 = jnp.where(rank < 2.0, sel, 0.0)
    gsum = top2
    for y, _ in neighbours(top2, in_group, 1, EXPERTS_PER_GROUP):
        gsum = gsum + y
    grank = jnp.zeros_like(sel)
    for y, fwd in neighbours(gsum, group, EXPERTS_PER_GROUP, N_EXPERTS):
        beats = (y > gsum) | ((y == gsum) & jnp.logical_not(fwd))
        grank = grank + jnp.where(beats, 1.0, 0.0)
    best = (grank < 0.5) & (lane < N_EXPERTS)
    chosen = best & (rank < 2.0)
    w = jnp.where(chosen, scores, 0.0)
    gates = w / jnp.sum(w, axis=-1, keepdims=True)
    gid = jnp.sum(jnp.where(best & (in_group == 0), group.astype(F32), 0.0), axis=-1, keepdims=True)
    return jnp.where(lane == GROUP_LANE, gid, gates)


def _outproj_kernel(*refs, n_stream, t0):
    stream_refs = refs[:n_stream]
    tile = pl.program_id(1) + t0
    mix_ref, w_ref, mod_ref, g_ref, wrh_ref, wrl_ref, rb_ref, x1_ref, h2_ref, gates_ref = refs[n_stream:]
    for i in range(mix_ref.shape[0]):
        x1 = _stream_rows(stream_refs, i, tile) + mod_ref[i, 0, 2:3, :] * _dot(mix_ref[i], w_ref[...])
        x1_ref[i] = x1
        h2 = _rms(x1, g_ref[...]) * (1.0 + mod_ref[i, 0, 4:5, :]) + mod_ref[i, 0, 3:4, :]
        h2_ref[i] = _pack_bf16_pairs(h2)
        hi, lo = _split(h2)
        logits = _dot(hi, wrh_ref[...]) + _dot(hi, wrl_ref[...]) + _dot(lo, wrh_ref[...])
        gates_ref[i] = _route(_sigmoid(logits), rb_ref[...])


def _outproj(stream, mix, w, mod, g, wrh, wrl, rb, t0):
    batch, t_len, d = _stream_dims(stream)
    nt = t_len // TILE
    nb = 2 * PAIR
    t_out = (nt - t0) * TILE
    const = lambda b, t: (0, 0)
    tok_in = lambda b, t: (b, t + t0, 0)
    tok_out = lambda b, t: (b, t, 0)
    return pl.pallas_call(
        functools.partial(_outproj_kernel, n_stream=len(stream), t0=t0),
        grid=(batch // nb, nt - t0),
        in_specs=_stream_specs(stream, nb, lambda t: t + t0) + [
                  pl.BlockSpec((nb, TILE, Q_COLS), tok_in),
                  pl.BlockSpec(w.shape, const),
                  pl.BlockSpec((nb, 1, 8, d), lambda b, t: (b, jnp.minimum(t + t0, 1), 0, 0)),
                  pl.BlockSpec((1, d), const),
                  pl.BlockSpec((d, LANES), const),
                  pl.BlockSpec((d, LANES), const),
                  pl.BlockSpec((1, LANES), const)],
        out_specs=[pl.BlockSpec((nb, TILE, d), tok_out),
                   pl.BlockSpec((nb, TILE, d // 2), tok_out),
                   pl.BlockSpec((nb, TILE, LANES), tok_out)],
        out_shape=[jax.ShapeDtypeStruct((batch, t_out, d), F32),
                   jax.ShapeDtypeStruct((batch, t_out, d // 2), jnp.uint32),
                   jax.ShapeDtypeStruct((batch, t_out, LANES), F32)],
        compiler_params=pltpu.CompilerParams(dimension_semantics=("parallel", "parallel"),
                                             vmem_limit_bytes=VMEM_LIMIT),
        name="outproj",
    )(*stream, mix, w, mod, g, wrh, wrl, rb)


def _sc_workers():
    info = plsc.get_sparse_core_info()
    return info.num_cores, info.num_subcores


def _sc_dispatch(hpk, gates, pos2d, n_out):
    n, w = hpk.shape
    gw = gates.shape[1]
    nc, ns = _sc_workers()
    steps = n // (nc * ns * SC_WIN)
    mesh = plsc.VectorSubcoreMesh(core_axis_name="c", subcore_axis_name="s")

    @functools.partial(
        pl.kernel, mesh=mesh,
        out_type=(jax.ShapeDtypeStruct((n_out, w), hpk.dtype), jax.ShapeDtypeStruct((n_out, gw), gates.dtype)),
        scratch_types=[pltpu.VMEM((1, SC_WIN), jnp.int32), pltpu.VMEM((SC_WIN, w), hpk.dtype),
                       pltpu.VMEM((SC_WIN, gw), gates.dtype), pltpu.SemaphoreType.DMA, pltpu.SemaphoreType.DMA],
        name="sc_dispatch")
    def k(h_hbm, g_hbm, pos_hbm, hs_hbm, gs_hbm, idx_v, rows_v, grow_v, sem_in, sem_out):
        wid = lax.axis_index("s") * nc + lax.axis_index("c")

        @pl.loop(0, steps)
        def _(j):
            blk = wid * steps + j
            off = pl.multiple_of(blk * SC_WIN, SC_WIN)
            loads = [pltpu.async_copy(pos_hbm.at[pl.ds(blk, 1)], idx_v, sem_in),
                     pltpu.async_copy(h_hbm.at[pl.ds(off, SC_WIN)], rows_v, sem_in),
                     pltpu.async_copy(g_hbm.at[pl.ds(off, SC_WIN)], grow_v, sem_in)]
            for cp in loads:
                cp.wait()
            stores = [pltpu.async_copy(rows_v, hs_hbm.at[idx_v.at[0]], sem_out),
                      pltpu.async_copy(grow_v, gs_hbm.at[idx_v.at[0]], sem_out)]
            for cp in stores:
                cp.wait()

    return k(hpk, gates, pos2d)


def _sc_combine(ys, pos2d, n):
    w = ys.shape[1]
    nc, ns = _sc_workers()
    steps = n // (nc * ns * SC_WIN)
    mesh = plsc.VectorSubcoreMesh(core_axis_name="c", subcore_axis_name="s")

    @functools.partial(
        pl.kernel, mesh=mesh,
        out_type=jax.ShapeDtypeStruct((n, w), ys.dtype),
        scratch_types=[pltpu.VMEM((1, SC_WIN), jnp.int32), pltpu.VMEM((SC_WIN, w), ys.dtype)],
        name="sc_combine")
    def k(y_hbm, pos_hbm, o_hbm, idx_v, rows_v):
        wid = lax.axis_index("s") * nc + lax.axis_index("c")

        @pl.loop(0, steps)
        def _(j):
            blk = wid * steps + j
            off = pl.multiple_of(blk * SC_WIN, SC_WIN)
            pltpu.sync_copy(pos_hbm.at[pl.ds(blk, 1)], idx_v)
            pltpu.sync_copy(y_hbm.at[idx_v.at[0]], rows_v)
            pltpu.sync_copy(rows_v, o_hbm.at[pl.ds(off, SC_WIN)])

    return k(ys, pos2d)


def _bucket_plan(group, n_pad):
    n_groups = N_EXPERTS // EXPERTS_PER_GROUP
    onehot = (group[:, None] == jnp.arange(n_groups, dtype=jnp.int32)[None, :]).astype(jnp.int32)
    csum = jnp.cumsum(onehot, axis=0)
    counts = csum[-1]
    padded = ((counts + FFN_TILE - 1) // FFN_TILE) * FFN_TILE
    ends = jnp.cumsum(padded)
    starts = ends - padded
    pos = starts[group] + jnp.sum(csum * onehot, axis=1) - 1
    tile_row = jnp.arange(n_pad // FFN_TILE, dtype=jnp.int32) * FFN_TILE
    tile_group = jnp.minimum(jnp.sum((tile_row[:, None] >= ends[None, :]).astype(jnp.int32), axis=1), n_groups - 1)
    tile_used = (tile_row < (starts + counts)[tile_group]).astype(jnp.int32)
    return pos.astype(jnp.int32), tile_group.astype(jnp.int32), tile_used


def _ffn_kernel(tg_ref, tu_ref, hs_ref, gs_ref, wg_ref, wu_ref, wd_ref, ys_ref, wgb_ref, wub_ref, wdb_ref):
    i = pl.program_id(0)

    @pl.when((i == 0) | (tg_ref[i] != tg_ref[jnp.maximum(i - 1, 0)]))
    def _():
        wgb_ref[...] = wg_ref[0].astype(BF16)
        wub_ref[...] = wu_ref[0].astype(BF16)
        wdb_ref[...] = wd_ref[0].astype(BF16)

    @pl.when(tu_ref[i] > 0)
    def _():
        lo, hi = _unpack_bf16_pairs(hs_ref[...])
        lo = lo.astype(BF16)
        hi = hi.astype(BF16)
        half = lo.shape[1]
        gs = gs_ref[...]
        lane = lax.broadcasted_iota(jnp.int32, gs.shape, 1)
        first = tg_ref[i] * EXPERTS_PER_GROUP
        y = None
        for e in range(EXPERTS_PER_GROUP):
            a = _dot(lo, wgb_ref[e, :half, :]) + _dot(hi, wgb_ref[e, half:, :])
            u = _dot(lo, wub_ref[e, :half, :]) + _dot(hi, wub_ref[e, half:, :])
            gate = jnp.sum(jnp.where(lane == first + e, gs, 0.0), axis=-1, keepdims=True)
            part = _dot(((a * _sigmoid(a)) * u * gate).astype(BF16), wdb_ref[e])
            y = part if y is None else y + part
        ys_ref[...] = _pack_bf16_pairs(y)


def _ffn(tile_group, tile_used, hs, gs, wg, wu, wd, layer):
    n_pad, half = hs.shape
    grp = lambda i, tg, tu: (layer, tg[i], 0, 0)
    gshape = lambda w: (EXPERTS_PER_GROUP,) + w.shape[2:]
    wspec = lambda w: pl.BlockSpec((1,) + gshape(w), grp)
    return pl.pallas_call(
        _ffn_kernel,
        grid_spec=pltpu.PrefetchScalarGridSpec(
            num_scalar_prefetch=2,
            grid=(n_pad // FFN_TILE,),
            in_specs=[pl.BlockSpec((FFN_TILE, half), lambda i, tg, tu: (i, 0)),
                      pl.BlockSpec((FFN_TILE, LANES), lambda i, tg, tu: (i, 0)),
                      wspec(wg), wspec(wu), wspec(wd)],
            out_specs=pl.BlockSpec((FFN_TILE, half), lambda i, tg, tu: (i, 0)),
            scratch_shapes=[pltpu.VMEM(gshape(wg), BF16), pltpu.VMEM(gshape(wu), BF16),
                            pltpu.VMEM(gshape(wd), BF16)]),
        out_shape=jax.ShapeDtypeStruct((n_pad, half), jnp.uint32),
        compiler_params=pltpu.CompilerParams(dimension_semantics=("arbitrary",), vmem_limit_bytes=VMEM_LIMIT),
        name="ffn",
    )(tile_group, tile_used, hs, gs, wg, wu, wd)


def _residual_kernel(x_ref, y_ref, mod_ref, fg_ref, o_ref, *, final_norm):
    for i in range(x_ref.shape[0]):
        lo, hi = _unpack_bf16_pairs(y_ref[i])
        y = x_ref[i] + mod_ref[i, 0, 5:6, :] * jnp.concatenate([lo, hi], axis=1)
        if final_norm:
            y = _rms(y, fg_ref[...])
        o_ref[i] = y


def _residual(x1, yg, mod, fg, t0, final_norm):
    batch, t_out, d = x1.shape
    nb = PAIR
    tok = lambda b, t: (b, t, 0)
    return pl.pallas_call(
        functools.partial(_residual_kernel, final_norm=final_norm),
        grid=(batch // nb, t_out // TILE),
        in_specs=[pl.BlockSpec((nb, TILE, d), tok),
                  pl.BlockSpec((nb, TILE, d // 2), tok),
                  pl.BlockSpec((nb, 1, 8, d), lambda b, t: (b, jnp.minimum(t + t0, 1), 0, 0)),
                  pl.BlockSpec((1, d), lambda b, t: (0, 0))],
        out_specs=pl.BlockSpec((nb, TILE, d), tok),
        out_shape=jax.ShapeDtypeStruct((batch, t_out, d), F32),
        compiler_params=pltpu.CompilerParams(dimension_semantics=("parallel", "parallel"),
                                             vmem_limit_bytes=VMEM_LIMIT),
        name="residual",
    )(x1, yg, mod, fg)


def _moe(h2, gates, wg, wu, wd, layer):
    batch, t_out, half = h2.shape
    n = batch * t_out
    n_pad = n + (N_EXPERTS // EXPERTS_PER_GROUP) * FFN_TILE
    gates = gates.reshape(n, LANES)
    pos, tile_group, tile_used = _bucket_plan(gates[:, GROUP_LANE].astype(jnp.int32), n_pad)
    pos2d = pos.reshape(n // SC_WIN, SC_WIN)
    hs, gs = _sc_dispatch(h2.reshape(n, half), gates, pos2d, n_pad)
    ys = _ffn(tile_group, tile_used, hs, gs, wg, wu, wd, layer)
    return _sc_combine(ys, pos2d, n).reshape(batch, t_out, half)


def _gqa_perm():
    return np.arange(256).reshape(2, 2, HEAD_DIM).transpose(1, 0, 2).reshape(-1)


def _in_col_perm():
    g = _gqa_perm()
    q = np.concatenate([np.arange(512), 512 + g, 768 + g])
    kv0 = 1024
    k = np.concatenate([kv0 + np.arange(512), kv0 + 1024 + np.arange(128), kv0 + 1280 + np.arange(128)])
    v = np.concatenate([kv0 + 512 + np.arange(512), kv0 + 1152 + np.arange(128), kv0 + 1408 + np.arange(128)])
    return np.concatenate([q, k, v])


def _out_row_perm():
    g = _gqa_perm()
    return np.concatenate([np.arange(512), 512 + g, 768 + g])


def _rope_tables(seq, ctx_len):
    rows = seq // GRID_W
    row_pos = jnp.repeat(jnp.arange(rows, dtype=F32), GRID_W)
    col_pos = jnp.tile(jnp.arange(GRID_W, dtype=F32), rows)
    axis_dim = HEAD_DIM // 2
    inv_freq = ROPE_BASE ** (-jnp.arange(0, axis_dim, 2, dtype=F32) / axis_dim)
    ang_r = row_pos[:, None] * inv_freq[None, :]
    ang_c = col_pos[:, None] * inv_freq[None, :]
    z = jnp.zeros_like(ang_r)
    cos = jnp.concatenate([jnp.cos(ang_r)] * 2 + [jnp.cos(ang_c)] * 2, axis=-1)
    sin_a = jnp.concatenate([-jnp.sin(ang_r), z, -jnp.sin(ang_c), z], axis=-1)
    sin_b = jnp.concatenate([z, jnp.sin(ang_r), z, jnp.sin(ang_c)], axis=-1)

    def full(tab, fill):
        tab = jnp.tile(tab, (1, LANES // HEAD_DIM))
        return jnp.concatenate([jnp.full((ctx_len, LANES), fill, F32), tab], axis=0)

    return full(cos, 1.0), full(sin_a, 0.0), full(sin_b, 0.0)


def kernel(x, c, ctx, c_ctx, w_ada, b_ada, norm1_g, norm2_g, w_in, w_out, lam_q1, lam_k1, lam_q2, lam_k2,
           subln_g, q_norm_g, k_norm_g, sink, w_router, router_bias, w_gate, w_up, w_down, final_g):
    batch, seq, d = x.shape
    ctx_len = ctx.shape[1]
    depth = w_in.shape[0]
    sc_rows = SC_WIN * math.prod(_sc_workers())
    assert ctx_len == TILE and seq % TILE == 0 and batch % PAIR == 0 and batch <= 15
    assert (batch * seq) % sc_rows == 0 and (batch * (seq + ctx_len)) % sc_rows == 0

    cpad = jnp.zeros((16, d), F32).at[:batch].set(c).at[batch].set(c_ctx)
    mod_all = _adaln(cpad, w_ada, b_ada)
    cos, sin_a, sin_b = _rope_tables(seq, ctx_len)
    mseg = jnp.asarray(np.kron(np.eye(2), np.ones((HEAD_DIM, HEAD_DIM))), BF16)
    in_perm = _in_col_perm()
    out_perm = _out_row_perm()
    wr = jnp.zeros((d, LANES), F32).at[:, :N_EXPERTS].set(w_router)
    wrh = wr.astype(BF16)
    wrl = (wr - wrh.astype(F32)).astype(BF16)
    rb = jnp.zeros((1, LANES), F32).at[0, :N_EXPERTS].set(router_bias)

    stream = (ctx, x)

    for l in range(depth):
        last = l == depth - 1
        lambda_init = 0.8 - 0.6 * math.exp(-0.3 * l)
        m6 = mod_all[l].reshape(16, 6, d)
        m8 = jnp.concatenate([m6, jnp.zeros((16, 2, d), F32)], axis=1)
        mod = jnp.stack([jnp.broadcast_to(m8[batch], (batch, 8, d)), m8[:batch]], axis=1)

        w_in_p = w_in[l][:, in_perm].astype(BF16)
        w_out_p = w_out[l][out_perm, :].astype(BF16)
        gq = jnp.tile(q_norm_g[l], 2).reshape(1, LANES)
        gk = jnp.tile(k_norm_g[l], 2).reshape(1, LANES)
        q, k, v = _inproj(stream, mod, norm1_g[l].reshape(1, d), w_in_p, cos, sin_a, sin_b, gq, gk, mseg)

        t0 = 1 if last else 0
        lamv = jnp.stack([lam_q1[l], lam_k1[l], lam_q2[l], lam_k2[l]])
        mix = _attention(sink[l], lamv, subln_g[l].reshape(1, LANES), q, k, v, t0, lambda_init)
        x1, h2, gates = _outproj(stream, mix, w_out_p, mod, norm2_g[l].reshape(1, d), wrh, wrl, rb, t0)
        stream = (x1, _moe(h2, gates, w_gate, w_up, w_down, l), mod)
    return _residual(*stream, final_g.reshape(1, d), 1, final_norm=True)
```

```python
import functools
import math

import numpy as np
import jax
import jax.numpy as jnp
from jax import lax
from jax.experimental import pallas as pl
from jax.experimental.pallas import tpu as pltpu
from jax.experimental.pallas import tpu_sc as plsc

F32 = jnp.float32
BF16 = jnp.bfloat16

HEAD_DIM = 64
GRID_W = 64
ROPE_BASE = 10000.0
RMS_EPS = 1e-6
WINDOW = 128
N_EXPERTS = 16
EXPERTS_PER_GROUP = 4
LANES = 128
TILE = 256
PAIR = 2
FFN_TILE = 512
SC_WIN = 64
GROUP_LANE = 16
NEG_BIG = -1e30
KEY_CHUNK = 256
VMEM_LIMIT = 52 * 1024 * 1024

QA, QB, QC = 512, 256, 256
KA, KB, KC = 512, 128, 128
Q_COLS = QA + QB + QC
K_COLS = KA + KB + KC
V_COLS = 768
ROPE_COLS = Q_COLS + K_COLS


def _dot(a, b):
    return jnp.dot(a, b, preferred_element_type=F32)


def _dot_nt(a, b):
    return lax.dot_general(a, b, (((1,), (1,)), ((), ())), preferred_element_type=F32)


def _split(a):
    hi = a.astype(BF16)
    lo = (a - hi.astype(F32)).astype(BF16)
    return hi, lo


def _dot3(a, b):
    ah, al = _split(a)
    bh, bl = _split(b)
    return _dot(ah, bh) + _dot(ah, bl) + _dot(al, bh)


def _rms(x, g):
    ms = jnp.mean(x * x, axis=-1, keepdims=True)
    return x * lax.rsqrt(ms + RMS_EPS) * g


def _sigmoid(x):
    return 1.0 / (1.0 + jnp.exp(-x))


def _pack_bf16_pairs(y):
    n = y.shape[1] // 2
    yb = y.astype(BF16).astype(F32)
    lo = lax.bitcast_convert_type(yb[:, :n], jnp.uint32) >> 16
    hi = lax.bitcast_convert_type(yb[:, n:], jnp.uint32) & jnp.uint32(0xFFFF0000)
    return hi | lo


def _unpack_bf16_pairs(w):
    lo = lax.bitcast_convert_type(w << 16, F32)
    hi = lax.bitcast_convert_type(w & jnp.uint32(0xFFFF0000), F32)
    return lo, hi


def _stream_dims(stream):
    batch, t_len, d = stream[0].shape
    return batch, (t_len + stream[1].shape[1] if len(stream) == 2 else t_len), d


def _stream_rows(stream_refs, i, tile):
    if len(stream_refs) == 2:
        ctx_ref, x_ref = stream_refs
        return jnp.where(tile == 0, ctx_ref[i], x_ref[i])
    x_ref, y_ref, mod_ref = stream_refs
    lo, hi = _unpack_bf16_pairs(y_ref[i])
    return x_ref[i] + mod_ref[i, 0, 5:6, :] * jnp.concatenate([lo, hi], axis=1)


def _stream_specs(stream, nb, tile_of):
    d = stream[0].shape[-1]
    if len(stream) == 2:
        return [pl.BlockSpec((nb, TILE, d), lambda b, t: (b, 0, 0)),
                pl.BlockSpec((nb, TILE, d), lambda b, t: (b, jnp.maximum(tile_of(t) - 1, 0), 0))]
    tok = lambda b, t: (b, tile_of(t), 0)
    return [pl.BlockSpec((nb, TILE, d), tok),
            pl.BlockSpec((nb, TILE, d // 2), tok),
            pl.BlockSpec((nb, 1, 8, d), lambda b, t: (b, jnp.minimum(tile_of(t), 1), 0, 0))]


def _adaln_kernel(c_ref, w_ref, b_ref, o_ref):
    cv = c_ref[...]
    s = cv * _sigmoid(cv)
    o_ref[0] = _dot3(s, w_ref[0]) + b_ref[0]


def _adaln(cpad, w_ada, b_ada):
    depth, d, n6 = w_ada.shape
    tn = 1536
    return pl.pallas_call(
        _adaln_kernel,
        grid=(depth, n6 // tn),
        in_specs=[pl.BlockSpec((16, d), lambda l, j: (0, 0)),
                  pl.BlockSpec((1, d, tn), lambda l, j: (l, 0, j)),
                  pl.BlockSpec((1, 1, tn), lambda l, j: (l, 0, j))],
        out_specs=pl.BlockSpec((1, 16, tn), lambda l, j: (l, 0, j)),
        out_shape=jax.ShapeDtypeStruct((depth, 16, n6), F32),
        compiler_params=pltpu.CompilerParams(dimension_semantics=("parallel", "parallel"),
                                             vmem_limit_bytes=VMEM_LIMIT),
        name="adaln",
    )(cpad, w_ada, b_ada.reshape(depth, 1, n6))


def _inproj_kernel(*refs, n_stream):
    for base in range(0, refs[0].shape[0], PAIR):
        _inproj_rows(base, pl.program_id(1), refs[:n_stream], *refs[n_stream:])


def _inproj_rows(base, tile, stream_refs, mod_ref, g_ref, w_ref, cos_ref, sa_ref, sb_ref, gq_ref, gk_ref, mseg_ref,
                 q_ref, k_ref, v_ref):
    members = range(base, base + PAIR)
    h = jnp.concatenate(
        [(_rms(_stream_rows(stream_refs, i, tile), g_ref[...]) * (1.0 + mod_ref[i, 0, 1:2, :])
          + mod_ref[i, 0, 0:1, :]).astype(BF16) for i in members], axis=0)
    cos = jnp.concatenate([cos_ref[...]] * PAIR, axis=0)
    sa = jnp.concatenate([sa_ref[...]] * PAIR, axis=0)
    sb = jnp.concatenate([sb_ref[...]] * PAIR, axis=0)
    mseg = mseg_ref[...]

    def put(ref, cols, val):
        for n, i in enumerate(members):
            ref[i, :, cols] = val[n * TILE:(n + 1) * TILE]

    def rope(b):
        return b * cos + pltpu.roll(b, LANES - 16, 1) * sa + pltpu.roll(b, 16, 1) * sb

    def qknorm(b, g):
        hi, lo = _split(b * b)
        ms = (_dot(hi, mseg) + _dot(lo, mseg)) * (1.0 / HEAD_DIM)
        return b * lax.rsqrt(ms + RMS_EPS) * g

    for c in range(ROPE_COLS // 256):
        p2 = _dot(h, w_ref[:, c * 256:(c + 1) * 256])
        for half in range(2):
            j = 2 * c + half
            p = p2[:, half * LANES:(half + 1) * LANES]
            if j in (4, 5):
                p = qknorm(p, gq_ref[...])
            if j == 12:
                p = qknorm(p, gk_ref[...])
            p = rope(p)
            if j < Q_COLS // LANES:
                put(q_ref, slice(j * LANES, (j + 1) * LANES), (p * (HEAD_DIM ** -0.5)).astype(BF16))
            else:
                jk = j - Q_COLS // LANES
                put(k_ref, slice(jk * LANES, (jk + 1) * LANES), p.astype(BF16))
    for c in range(V_COLS // 256):
        lo_c = ROPE_COLS + c * 256
        put(v_ref, slice(c * 256, (c + 1) * 256), _dot(h, w_ref[:, lo_c:lo_c + 256]).astype(BF16))


def _inproj(stream, mod, g, w, cos, sa, sb, gq, gk, mseg):
    batch, t_len, d = _stream_dims(stream)
    nt = t_len // TILE
    nb = 2 * PAIR
    row = lambda b, t: (t, 0)
    const = lambda b, t: (0, 0)
    tok = lambda b, t: (b, t, 0)
    return pl.pallas_call(
        functools.partial(_inproj_kernel, n_stream=len(stream)),
        grid=(batch // nb, nt),
        in_specs=_stream_specs(stream, nb, lambda t: t) + [
                  pl.BlockSpec((nb, 1, 8, d), lambda b, t: (b, jnp.minimum(t, 1), 0, 0)),
                  pl.BlockSpec((1, d), const),
                  pl.BlockSpec(w.shape, const),
                  pl.BlockSpec((TILE, LANES), row),
                  pl.BlockSpec((TILE, LANES), row),
                  pl.BlockSpec((TILE, LANES), row),
                  pl.BlockSpec((1, LANES), const),
                  pl.BlockSpec((1, LANES), const),
                  pl.BlockSpec((LANES, LANES), const)],
        out_specs=[pl.BlockSpec((nb, TILE, Q_COLS), tok),
                   pl.BlockSpec((nb, TILE, K_COLS), tok),
                   pl.BlockSpec((nb, TILE, V_COLS), tok)],
        out_shape=[jax.ShapeDtypeStruct((batch, t_len, Q_COLS), BF16),
                   jax.ShapeDtypeStruct((batch, t_len, K_COLS), BF16),
                   jax.ShapeDtypeStruct((batch, t_len, V_COLS), BF16)],
        compiler_params=pltpu.CompilerParams(dimension_semantics=("parallel", "parallel"),
                                             vmem_limit_bytes=VMEM_LIMIT),
        name="inproj",
    )(*stream, mod, g, w, cos, sa, sb, gq, gk, mseg)


def _scores_pass(qq, k_ref, bi, col, starts, s_ref, mask_fn=None, extra=None):
    half = KEY_CHUNK // 2
    rows = qq.shape[0]
    mrun = None
    for j, st in enumerate(starts):
        s = _dot_nt(qq, k_ref[bi, pl.ds(st, KEY_CHUNK), col:col + LANES])
        if mask_fn is not None:
            s = mask_fn(j, st, s)
        s_ref[0:rows, j * KEY_CHUNK:(j + 1) * KEY_CHUNK] = s
        mj = jnp.maximum(s[:, :half], s[:, half:])
        mrun = mj if mrun is None else jnp.maximum(mrun, mj)
    m = jnp.max(mrun, axis=-1, keepdims=True)
    return m if extra is None else jnp.maximum(m, extra)


def _values_pass(m, rows, v_ref, bi, col, starts, s_ref, extra=None):
    half = KEY_CHUNK // 2
    lrun = None
    acc = None
    for j, st in enumerate(starts):
        e = jnp.exp(s_ref[0:rows, j * KEY_CHUNK:(j + 1) * KEY_CHUNK] - m)
        lj = e[:, :half] + e[:, half:]
        lrun = lj if lrun is None else lrun + lj
        pv = _dot(e.astype(BF16), v_ref[bi, pl.ds(st, KEY_CHUNK), col:col + LANES])
        acc = pv if acc is None else acc + pv
    l = jnp.sum(lrun, axis=-1, keepdims=True)
    if extra is not None:
        l = l + jnp.exp(extra - m)
    return acc / l


def _softmax_pv(qq, k_ref, v_ref, bi, col, starts, s_ref, mask_fn=None, extra=None):
    m = _scores_pass(qq, k_ref, bi, col, starts, s_ref, mask_fn, extra)
    return _values_pass(m, qq.shape[0], v_ref, bi, col, starts, s_ref, extra)


def _all_chunks(n_keys):
    return [j * KEY_CHUNK for j in range(n_keys // KEY_CHUNK)]


def _half_masks(shape):
    lane = lax.broadcasted_iota(jnp.int32, shape, 1)
    return lane < HEAD_DIM, lane >= HEAD_DIM


def _stack_group_queries(q2, kv):
    lo_m, hi_m = _half_masks((TILE, LANES))
    m = lo_m if kv == 0 else hi_m
    zero = jnp.zeros((TILE, LANES), q2.dtype)
    return jnp.concatenate([jnp.where(m, q2[:, :LANES], zero), jnp.where(m, q2[:, LANES:], zero)], axis=0)


def _merge_kv_outputs(o_kv0, o_kv1):
    lo_m, _ = _half_masks((TILE, LANES))
    g0 = jnp.where(lo_m, o_kv0[:TILE], o_kv1[:TILE])
    g1 = jnp.where(lo_m, o_kv0[TILE:], o_kv1[TILE:])
    return jnp.concatenate([g0, g1], axis=1)


def _attn_kernel(sink_ref, lam_ref, subg_ref, qa_ref, qb_ref, qc_ref, ka_ref, kb_ref, kc_ref, va_ref, vb_ref, vc_ref,
                 o_ref, s_ref, *, t0, ctx_len, lambda_init):
    t = pl.program_id(1) + t0
    t_len = ka_ref.shape[1]
    band = 2 * TILE
    lo_m, hi_m = _half_masks((TILE, LANES))
    zero = jnp.zeros((TILE, LANES), BF16)
    row = lax.broadcasted_iota(jnp.int32, (2 * TILE, 1), 0)
    lv = lam_ref[...]
    lam = (jnp.exp(jnp.sum(lv[0:1] * lv[1:2], keepdims=True))
           - jnp.exp(jnp.sum(lv[2:3] * lv[3:4], keepdims=True)) + lambda_init)

    def sink_col(kv):
        return jnp.where(row < TILE, sink_ref[2 * kv], sink_ref[2 * kv + 1])

    def mixers(starts_ab, starts_c, mask_c):
        n_chain = 0
        for bi in range(qa_ref.shape[0]):
            for h in range(QA // LANES):
                q = qa_ref[bi, :, h * LANES:(h + 1) * LANES]
                qq = jnp.concatenate([jnp.where(lo_m, q, zero), jnp.where(hi_m, q, zero)], axis=0)
                o = _softmax_pv(qq, ka_ref, va_ref, bi, h * LANES, starts_ab, s_ref.at[n_chain % 2])
                n_chain += 1
                o = o[:TILE] - lam * o[TILE:]
                o_ref[bi, :, h * LANES:(h + 1) * LANES] = (
                    _rms(o, subg_ref[...]) * (1.0 - lambda_init)).astype(BF16)
            outs = []
            for kv in range(2):
                outs.append(_softmax_pv(_stack_group_queries(qb_ref[bi], kv), kb_ref, vb_ref, bi, 0, starts_ab,
                                        s_ref.at[n_chain % 2]))
                n_chain += 1
            o_ref[bi, :, QA:QA + QB] = _merge_kv_outputs(*outs).astype(BF16)
            outs = []
            for kv in range(2):
                outs.append(_softmax_pv(_stack_group_queries(qc_ref[bi], kv), kc_ref, vc_ref, bi, 0, starts_c,
                                        s_ref.at[n_chain % 2], mask_c, sink_col(kv)))
                n_chain += 1
            o_ref[bi, :, QA + QB:] = _merge_kv_outputs(*outs).astype(BF16)

    def ctx_step():
        mixers([0], [0], None)

    def latent_step():
        q_start = (t - 1) * TILE
        u0 = pl.multiple_of(jnp.minimum(q_start + ctx_len - WINDOW, t_len - band), WINDOW)
        q_pos = q_start + (lax.broadcasted_iota(jnp.int32, (2 * TILE, KEY_CHUNK), 0) & (TILE - 1))
        key_lane = lax.broadcasted_iota(jnp.int32, (2 * TILE, KEY_CHUNK), 1)

        def mask_c(j, st, s):
            if j == 0:
                return s
            k_pos = (st - ctx_len) + key_lane
            valid = (jnp.abs(k_pos - q_pos) <= WINDOW) & (k_pos >= 0)
            return jnp.where(valid, s, NEG_BIG)

        mixers(_all_chunks(t_len), [0] + [u0 + i * KEY_CHUNK for i in range(band // KEY_CHUNK)], mask_c)

    if t0 == 0:
        pl.when(t == 0)(ctx_step)
        pl.when(t > 0)(latent_step)
    else:
        latent_step()


def _attention(sink, lamv, subg, q, k, v, t0, lambda_init):
    batch, t_len, _ = q.shape
    nt = t_len // TILE
    kern = functools.partial(_attn_kernel, t0=t0, ctx_len=TILE, lambda_init=lambda_init)
    const = lambda b, t: (0, 0)
    nb = 1
    qspec = lambda width, blk: pl.BlockSpec((nb, TILE, width), lambda b, t: (b, t + t0, blk))
    kvspec = lambda width, blk: pl.BlockSpec((nb, t_len, width), lambda b, t: (b, 0, blk))
    return pl.pallas_call(
        kern,
        grid=(batch // nb, nt - t0),
        in_specs=[pl.BlockSpec(memory_space=pltpu.SMEM),
                  pl.BlockSpec(lamv.shape, const),
                  pl.BlockSpec((1, LANES), const),
                  qspec(QA, 0), qspec(QB, QA // QB), qspec(QC, (QA + QB) // QC),
                  kvspec(KA, 0), kvspec(KB, KA // KB), kvspec(KC, (KA + KB) // KC),
                  kvspec(KA, 0), kvspec(KB, KA // KB), kvspec(KC, (KA + KB) // KC)],
        out_specs=pl.BlockSpec((nb, TILE, Q_COLS), lambda b, t: (b, t + t0, 0)),
        out_shape=jax.ShapeDtypeStruct((batch, t_len, Q_COLS), BF16),
        scratch_shapes=[pltpu.VMEM((2, 2 * TILE, t_len), F32)],
        compiler_params=pltpu.CompilerParams(dimension_semantics=("parallel", "parallel"),
                                             vmem_limit_bytes=VMEM_LIMIT),
        name="attention",
    )(sink, lamv, subg, q, q, q, k, k, k, v, v, v)


def _route(scores, bias):
    lane = lax.broadcasted_iota(jnp.int32, scores.shape, 1)
    sel = scores + bias
    in_group = lane & (EXPERTS_PER_GROUP - 1)
    group = lane >> 2

    def neighbours(x, idx, step, span):
        for d in (1, 2, 3):
            fwd = (idx + d) < 4
            y = jnp.where(fwd, pltpu.roll(x, LANES - d * step, 1), pltpu.roll(x, span - d * step, 1))
            yield y, fwd

    rank = jnp.zeros_like(sel)
    for y, fwd in neighbours(sel, in_group, 1, EXPERTS_PER_GROUP):
        beats = (y > sel) | ((y == sel) & jnp.logical_not(fwd))
        rank = rank + jnp.where(beats, 1.0, 0.0)
    top2 = jnp.where(rank < 2.0, sel, 0.0)
    gsum = top2
    for y, _ in neighbours(top2, in_group, 1, EXPERTS_PER_GROUP):
        gsum = gsum + y
    grank = jnp.zeros_like(sel)
    for y, fwd in neighbours(gsum, group, EXPERTS_PER_GROUP, N_EXPERTS):
        beats = (y > gsum) | ((y == gsum) & jnp.logical_not(fwd))
        grank = grank + jnp.where(beats, 1.0, 0.0)
    best = (grank < 0.5) & (lane < N_EXPERTS)
    chosen = best & (rank < 2.0)
    w = jnp.where(chosen, scores, 0.0)
    gates = w / jnp.sum(w, axis=-1, keepdims=True)
    gid = jnp.sum(jnp.where(best & (in_group == 0), group.astype(F32), 0.0), axis=-1, keepdims=True)
    return jnp.where(lane == GROUP_LANE, gid, gates)


def _outproj_kernel(*refs, n_stream, t0):
    stream_refs = refs[:n_stream]
    tile = pl.program_id(1) + t0
    mix_ref, w_ref, mod_ref, g_ref, wrh_ref, wrl_ref, rb_ref, x1_ref, h2_ref, gates_ref = refs[n_stream:]
    for i in range(mix_ref.shape[0]):
        x1 = _stream_rows(stream_refs, i, tile) + mod_ref[i, 0, 2:3, :] * _dot(mix_ref[i], w_ref[...])
        x1_ref[i] = x1
        h2 = _rms(x1, g_ref[...]) * (1.0 + mod_ref[i, 0, 4:5, :]) + mod_ref[i, 0, 3:4, :]
        h2_ref[i] = _pack_bf16_pairs(h2)
        hi, lo = _split(h2)
        logits = _dot(hi, wrh_ref[...]) + _dot(hi, wrl_ref[...]) + _dot(lo, wrh_ref[...])
        gates_ref[i] = _route(_sigmoid(logits), rb_ref[...])


def _outproj(stream, mix, w, mod, g, wrh, wrl, rb, t0):
    batch, t_len, d = _stream_dims(stream)
    nt = t_len // TILE
    nb = 2 * PAIR
    t_out = (nt - t0) * TILE
    const = lambda b, t: (0, 0)
    tok_in = lambda b, t: (b, t + t0, 0)
    tok_out = lambda b, t: (b, t, 0)
    return pl.pallas_call(
        functools.partial(_outproj_kernel, n_stream=len(stream), t0=t0),
        grid=(batch // nb, nt - t0),
        in_specs=_stream_specs(stream, nb, lambda t: t + t0) + [
                  pl.BlockSpec((nb, TILE, Q_COLS), tok_in),
                  pl.BlockSpec(w.shape, const),
                  pl.BlockSpec((nb, 1, 8, d), lambda b, t: (b, jnp.minimum(t + t0, 1), 0, 0)),
                  pl.BlockSpec((1, d), const),
                  pl.BlockSpec((d, LANES), const),
                  pl.BlockSpec((d, LANES), const),
                  pl.BlockSpec((1, LANES), const)],
        out_specs=[pl.BlockSpec((nb, TILE, d), tok_out),
                   pl.BlockSpec((nb, TILE, d // 2), tok_out),
                   pl.BlockSpec((nb, TILE, LANES), tok_out)],
        out_shape=[jax.ShapeDtypeStruct((batch, t_out, d), F32),
                   jax.ShapeDtypeStruct((batch, t_out, d // 2), jnp.uint32),
                   jax.ShapeDtypeStruct((batch, t_out, LANES), F32)],
        compiler_params=pltpu.CompilerParams(dimension_semantics=("parallel", "parallel"),
                                             vmem_limit_bytes=VMEM_LIMIT),
        name="outproj",
    )(*stream, mix, w, mod, g, wrh, wrl, rb)


def _sc_workers():
    info = plsc.get_sparse_core_info()
    return info.num_cores, info.num_subcores


def _sc_dispatch(hpk, gates, pos2d, n_out):
    n, w = hpk.shape
    gw = gates.shape[1]
    nc, ns = _sc_workers()
    steps = n // (nc * ns * SC_WIN)
    mesh = plsc.VectorSubcoreMesh(core_axis_name="c", subcore_axis_name="s")

    @functools.partial(
        pl.kernel, mesh=mesh,
        out_type=(jax.ShapeDtypeStruct((n_out, w), hpk.dtype), jax.ShapeDtypeStruct((n_out, gw), gates.dtype)),
        scratch_types=[pltpu.VMEM((1, SC_WIN), jnp.int32), pltpu.VMEM((SC_WIN, w), hpk.dtype),
                       pltpu.VMEM((SC_WIN, gw), gates.dtype), pltpu.SemaphoreType.DMA, pltpu.SemaphoreType.DMA],
        name="sc_dispatch")
    def k(h_hbm, g_hbm, pos_hbm, hs_hbm, gs_hbm, idx_v, rows_v, grow_v, sem_in, sem_out):
        wid = lax.axis_index("s") * nc + lax.axis_index("c")

        @pl.loop(0, steps)
        def _(j):
            blk = wid * steps + j
            off = pl.multiple_of(blk * SC_WIN, SC_WIN)
            loads = [pltpu.async_copy(pos_hbm.at[pl.ds(blk, 1)], idx_v, sem_in),
                     pltpu.async_copy(h_hbm.at[pl.ds(off, SC_WIN)], rows_v, sem_in),
                     pltpu.async_copy(g_hbm.at[pl.ds(off, SC_WIN)], grow_v, sem_in)]
            for cp in loads:
                cp.wait()
            stores = [pltpu.async_copy(rows_v, hs_hbm.at[idx_v.at[0]], sem_out),
                      pltpu.async_copy(grow_v, gs_hbm.at[idx_v.at[0]], sem_out)]
            for cp in stores:
                cp.wait()

    return k(hpk, gates, pos2d)


def _sc_combine(ys, pos2d, n):
    w = ys.shape[1]
    nc, ns = _sc_workers()
    steps = n // (nc * ns * SC_WIN)
    mesh = plsc.VectorSubcoreMesh(core_axis_name="c", subcore_axis_name="s")

    @functools.partial(
        pl.kernel, mesh=mesh,
        out_type=jax.ShapeDtypeStruct((n, w), ys.dtype),
        scratch_types=[pltpu.VMEM((1, SC_WIN), jnp.int32), pltpu.VMEM((SC_WIN, w), ys.dtype)],
        name="sc_combine")
    def k(y_hbm, pos_hbm, o_hbm, idx_v, rows_v):
        wid = lax.axis_index("s") * nc + lax.axis_index("c")

        @pl.loop(0, steps)
        def _(j):
            blk = wid * steps + j
            off = pl.multiple_of(blk * SC_WIN, SC_WIN)
            pltpu.sync_copy(pos_hbm.at[pl.ds(blk, 1)], idx_v)
            pltpu.sync_copy(y_hbm.at[idx_v.at[0]], rows_v)
            pltpu.sync_copy(rows_v, o_hbm.at[pl.ds(off, SC_WIN)])

    return k(ys, pos2d)


PLAN_ROWS = 256
PLAN_BLOCK = 2048


def _plan_kernel(gates_ref, pos_ref, tg_ref, tu_ref, g_ref):
    step = pl.program_id(0)

    @pl.when(step == 0)
    def _():
        g_ref[...] = jnp.full(g_ref.shape, -1.0, F32)

    gates = gates_ref[...]
    lane = lax.broadcasted_iota(jnp.int32, gates.shape, 1)
    along_lanes = _dot_nt(jnp.ones((8, LANES), BF16), jnp.where(lane == GROUP_LANE, gates, 0.0).astype(BF16))
    for r in range(PLAN_BLOCK // LANES):
        g_ref[pl.ds(step * (PLAN_BLOCK // LANES) + r, 1), :] = along_lanes[0:1, r * LANES:(r + 1) * LANES]

    pl.when(step == pl.num_programs(0) - 1)(functools.partial(_plan_finish, g_ref, pos_ref, tg_ref, tu_ref))


def _plan_finish(g_ref, pos_ref, tg_ref, tu_ref):
    g = g_ref[...]
    rows = g.shape[0]
    n_groups = N_EXPERTS // EXPERTS_PER_GROUP
    before_lane = (lax.broadcasted_iota(jnp.int32, (LANES, LANES), 0)
                   < lax.broadcasted_iota(jnp.int32, (LANES, LANES), 1)).astype(BF16)
    before_row = (lax.broadcasted_iota(jnp.int32, (rows, rows), 1)
                  < lax.broadcasted_iota(jnp.int32, (rows, rows), 0)).astype(BF16)
    tile_row = lax.broadcasted_iota(jnp.int32, (1, LANES), 1).astype(F32) * FFN_TILE
    pos = jnp.zeros_like(g)
    start = jnp.zeros((1, 1), F32)
    tile_group = jnp.zeros((1, LANES), F32)
    used_end = []
    for grp in range(n_groups):
        m = jnp.where(g == grp, 1.0, 0.0)
        in_row = _dot(m.astype(BF16), before_lane)
        row_total = jnp.sum(m, axis=1, keepdims=True)
        rows_before = _dot(before_row, jnp.broadcast_to(row_total, m.shape).astype(BF16))
        pos = pos + m * (start + in_row + rows_before)
        count = jnp.sum(row_total, axis=0, keepdims=True)
        used_end.append(start + count)
        start = start + jnp.floor((count + (FFN_TILE - 1)) * (1.0 / FFN_TILE)) * FFN_TILE
        if grp < n_groups - 1:
            tile_group = tile_group + jnp.where(tile_row >= start, 1.0, 0.0)
    pos_ref[...] = pos.astype(jnp.int32)
    tg_ref[...] = tile_group.astype(jnp.int32)
    end_of_tile_group = sum(jnp.where(tile_group == grp, used_end[grp], 0.0) for grp in range(n_groups))
    tu_ref[...] = jnp.where(tile_row < end_of_tile_group, 1, 0).astype(jnp.int32)


def _bucket_plan(gates, n_pad):
    n = gates.shape[0]
    whole = lambda shape: pl.BlockSpec(shape, lambda i: (0, 0))
    pos, tile_group, tile_used = pl.pallas_call(
        _plan_kernel,
        grid=(n // PLAN_BLOCK,),
        in_specs=[pl.BlockSpec((PLAN_BLOCK, LANES), lambda i: (i, 0))],
        out_specs=[whole((PLAN_ROWS, LANES)), whole((1, LANES)), whole((1, LANES))],
        out_shape=[jax.ShapeDtypeStruct((PLAN_ROWS, LANES), jnp.int32),
                   jax.ShapeDtypeStruct((1, LANES), jnp.int32),
                   jax.ShapeDtypeStruct((1, LANES), jnp.int32)],
        scratch_shapes=[pltpu.VMEM((PLAN_ROWS, LANES), F32)],
        compiler_params=pltpu.CompilerParams(dimension_semantics=("arbitrary",)),
        name="plan",
    )(gates)
    n_tiles = n_pad // FFN_TILE
    return pos[:n // LANES].reshape(n // SC_WIN, SC_WIN), tile_group[0, :n_tiles], tile_used[0, :n_tiles]


def _ffn_kernel(tg_ref, tu_ref, hs_ref, gs_ref, wg_ref, wu_ref, wd_ref, ys_ref, wgb_ref, wub_ref, wdb_ref):
    i = pl.program_id(0)

    @pl.when((i == 0) | (tg_ref[i] != tg_ref[jnp.maximum(i - 1, 0)]))
    def _():
        wgb_ref[...] = wg_ref[0].astype(BF16)
        wub_ref[...] = wu_ref[0].astype(BF16)
        wdb_ref[...] = wd_ref[0].astype(BF16)

    @pl.when(tu_ref[i] > 0)
    def _():
        lo, hi = _unpack_bf16_pairs(hs_ref[...])
        lo = lo.astype(BF16)
        hi = hi.astype(BF16)
        half = lo.shape[1]
        gs = gs_ref[...]
        lane = lax.broadcasted_iota(jnp.int32, gs.shape, 1)
        first = tg_ref[i] * EXPERTS_PER_GROUP
        y = None
        for e in range(EXPERTS_PER_GROUP):
            a = _dot(lo, wgb_ref[e, :half, :]) + _dot(hi, wgb_ref[e, half:, :])
            u = _dot(lo, wub_ref[e, :half, :]) + _dot(hi, wub_ref[e, half:, :])
            gate = jnp.sum(jnp.where(lane == first + e, gs, 0.0), axis=-1, keepdims=True)
            part = _dot(((a * _sigmoid(a)) * u * gate).astype(BF16), wdb_ref[e])
            y = part if y is None else y + part
        ys_ref[...] = _pack_bf16_pairs(y)


def _ffn(tile_group, tile_used, hs, gs, wg, wu, wd, layer):
    n_pad, half = hs.shape
    grp = lambda i, tg, tu: (layer, tg[i], 0, 0)
    gshape = lambda w: (EXPERTS_PER_GROUP,) + w.shape[2:]
    wspec = lambda w: pl.BlockSpec((1,) + gshape(w), grp)
    return pl.pallas_call(
        _ffn_kernel,
        grid_spec=pltpu.PrefetchScalarGridSpec(
            num_scalar_prefetch=2,
            grid=(n_pad // FFN_TILE,),
            in_specs=[pl.BlockSpec((FFN_TILE, half), lambda i, tg, tu: (i, 0)),
                      pl.BlockSpec((FFN_TILE, LANES), lambda i, tg, tu: (i, 0)),
                      wspec(wg), wspec(wu), wspec(wd)],
            out_specs=pl.BlockSpec((FFN_TILE, half), lambda i, tg, tu: (i, 0)),
            scratch_shapes=[pltpu.VMEM(gshape(wg), BF16), pltpu.VMEM(gshape(wu), BF16),
                            pltpu.VMEM(gshape(wd), BF16)]),
        out_shape=jax.ShapeDtypeStruct((n_pad, half), jnp.uint32),
        compiler_params=pltpu.CompilerParams(dimension_semantics=("arbitrary",), vmem_limit_bytes=VMEM_LIMIT),
        name="ffn",
    )(tile_group, tile_used, hs, gs, wg, wu, wd)


def _residual_kernel(x_ref, y_ref, mod_ref, fg_ref, o_ref, *, final_norm):
    for i in range(x_ref.shape[0]):
        lo, hi = _unpack_bf16_pairs(y_ref[i])
        y = x_ref[i] + mod_ref[i, 0, 5:6, :] * jnp.concatenate([lo, hi], axis=1)
        if final_norm:
            y = _rms(y, fg_ref[...])
        o_ref[i] = y


def _residual(x1, yg, mod, fg, t0, final_norm):
    batch, t_out, d = x1.shape
    nb = PAIR
    tok = lambda b, t: (b, t, 0)
    return pl.pallas_call(
        functools.partial(_residual_kernel, final_norm=final_norm),
        grid=(batch // nb, t_out // TILE),
        in_specs=[pl.BlockSpec((nb, TILE, d), tok),
                  pl.BlockSpec((nb, TILE, d // 2), tok),
                  pl.BlockSpec((nb, 1, 8, d), lambda b, t: (b, jnp.minimum(t + t0, 1), 0, 0)),
                  pl.BlockSpec((1, d), lambda b, t: (0, 0))],
        out_specs=pl.BlockSpec((nb, TILE, d), tok),
        out_shape=jax.ShapeDtypeStruct((batch, t_out, d), F32),
        compiler_params=pltpu.CompilerParams(dimension_semantics=("parallel", "parallel"),
                                             vmem_limit_bytes=VMEM_LIMIT),
        name="residual",
    )(x1, yg, mod, fg)


def _moe(h2, gates, wg, wu, wd, layer):
    batch, t_out, half = h2.shape
    n = batch * t_out
    n_pad = n + (N_EXPERTS // EXPERTS_PER_GROUP) * FFN_TILE
    assert n % PLAN_BLOCK == 0 and n // LANES <= PLAN_ROWS and n_pad // FFN_TILE <= LANES
    gates = gates.reshape(n, LANES)
    pos2d, tile_group, tile_used = _bucket_plan(gates, n_pad)
    hs, gs = _sc_dispatch(h2.reshape(n, half), gates, pos2d, n_pad)
    ys = _ffn(tile_group, tile_used, hs, gs, wg, wu, wd, layer)
    return _sc_combine(ys, pos2d, n).reshape(batch, t_out, half)


def _gqa_perm():
    return np.arange(256).reshape(2, 2, HEAD_DIM).transpose(1, 0, 2).reshape(-1)


def _in_col_perm():
    g = _gqa_perm()
    q = np.concatenate([np.arange(512), 512 + g, 768 + g])
    kv0 = 1024
    k = np.concatenate([kv0 + np.arange(512), kv0 + 1024 + np.arange(128), kv0 + 1280 + np.arange(128)])
    v = np.concatenate([kv0 + 512 + np.arange(512), kv0 + 1152 + np.arange(128), kv0 + 1408 + np.arange(128)])
    return np.concatenate([q, k, v])


def _out_row_perm():
    g = _gqa_perm()
    return np.concatenate([np.arange(512), 512 + g, 768 + g])


def _rope_tables(seq, ctx_len):
    rows = seq // GRID_W
    row_pos = jnp.repeat(jnp.arange(rows, dtype=F32), GRID_W)
    col_pos = jnp.tile(jnp.arange(GRID_W, dtype=F32), rows)
    axis_dim = HEAD_DIM // 2
    inv_freq = ROPE_BASE ** (-jnp.arange(0, axis_dim, 2, dtype=F32) / axis_dim)
    ang_r = row_pos[:, None] * inv_freq[None, :]
    ang_c = col_pos[:, None] * inv_freq[None, :]
    z = jnp.zeros_like(ang_r)
    cos = jnp.concatenate([jnp.cos(ang_r)] * 2 + [jnp.cos(ang_c)] * 2, axis=-1)
    sin_a = jnp.concatenate([-jnp.sin(ang_r), z, -jnp.sin(ang_c), z], axis=-1)
    sin_b = jnp.concatenate([z, jnp.sin(ang_r), z, jnp.sin(ang_c)], axis=-1)

    def full(tab, fill):
        tab = jnp.tile(tab, (1, LANES // HEAD_DIM))
        return jnp.concatenate([jnp.full((ctx_len, LANES), fill, F32), tab], axis=0)

    return full(cos, 1.0), full(sin_a, 0.0), full(sin_b, 0.0)


def kernel(x, c, ctx, c_ctx, w_ada, b_ada, norm1_g, norm2_g, w_in, w_out, lam_q1, lam_k1, lam_q2, lam_k2,
           subln_g, q_norm_g, k_norm_g, sink, w_router, router_bias, w_gate, w_up, w_down, final_g):
    batch, seq, d = x.shape
    ctx_len = ctx.shape[1]
    depth = w_in.shape[0]
    sc_rows = SC_WIN * math.prod(_sc_workers())
    assert ctx_len == TILE and seq % TILE == 0 and batch % PAIR == 0 and batch <= 15
    assert (batch * seq) % sc_rows == 0 and (batch * (seq + ctx_len)) % sc_rows == 0

    cpad = jnp.zeros((16, d), F32).at[:batch].set(c).at[batch].set(c_ctx)
    mod_all = _adaln(cpad, w_ada, b_ada)
    cos, sin_a, sin_b = _rope_tables(seq, ctx_len)
    mseg = jnp.asarray(np.kron(np.eye(2), np.ones((HEAD_DIM, HEAD_DIM))), BF16)
    in_perm = _in_col_perm()
    out_perm = _out_row_perm()
    wr = jnp.zeros((d, LANES), F32).at[:, :N_EXPERTS].set(w_router)
    wrh = wr.astype(BF16)
    wrl = (wr - wrh.astype(F32)).astype(BF16)
    rb = jnp.zeros((1, LANES), F32).at[0, :N_EXPERTS].set(router_bias)

    stream = (ctx, x)

    for l in range(depth):
        last = l == depth - 1
        lambda_init = 0.8 - 0.6 * math.exp(-0.3 * l)
        m6 = mod_all[l].reshape(16, 6, d)
        m8 = jnp.concatenate([m6, jnp.zeros((16, 2, d), F32)], axis=1)
        mod = jnp.stack([jnp.broadcast_to(m8[batch], (batch, 8, d)), m8[:batch]], axis=1)

        w_in_p = w_in[l][:, in_perm].astype(BF16)
        w_out_p = w_out[l][out_perm, :].astype(BF16)
        gq = jnp.tile(q_norm_g[l], 2).reshape(1, LANES)
        gk = jnp.tile(k_norm_g[l], 2).reshape(1, LANES)
        q, k, v = _inproj(stream, mod, norm1_g[l].reshape(1, d), w_in_p, cos, sin_a, sin_b, gq, gk, mseg)

        t0 = 1 if last else 0
        lamv = jnp.stack([lam_q1[l], lam_k1[l], lam_q2[l], lam_k2[l]])
        mix = _attention(sink[l], lamv, subln_g[l].reshape(1, LANES), q, k, v, t0, lambda_init)
        x1, h2, gates = _outproj(stream, mix, w_out_p, mod, norm2_g[l].reshape(1, d), wrh, wrl, rb, t0)
        stream = (x1, _moe(h2, gates, w_gate, w_up, w_down, l), mod)
    return _residual(*stream, final_g.reshape(1, d), 1, final_norm=True)
```

```python
import functools
import math

import numpy as np
import jax
import jax.numpy as jnp
from jax import lax
from jax.experimental import pallas as pl
from jax.experimental.pallas import tpu as pltpu
from jax.experimental.pallas import tpu_sc as plsc

F32 = jnp.float32
BF16 = jnp.bfloat16

HEAD_DIM = 64
GRID_W = 64
ROPE_BASE = 10000.0
RMS_EPS = 1e-6
WINDOW = 128
N_EXPERTS = 16
EXPERTS_PER_GROUP = 4
LANES = 128
TILE = 256
PAIR = 2
FFN_TILE = 512
SC_WIN = 64
GROUP_LANE = 16
NEG_BIG = -1e30
KEY_CHUNK = 256
VMEM_LIMIT = 52 * 1024 * 1024

QA, QB, QC = 512, 256, 256
KA, KB, KC = 512, 128, 128
Q_COLS = QA + QB + QC
K_COLS = KA + KB + KC
V_COLS = 768
ROPE_COLS = Q_COLS + K_COLS


def _dot(a, b):
    return jnp.dot(a, b, preferred_element_type=F32)


def _dot_nt(a, b):
    return lax.dot_general(a, b, (((1,), (1,)), ((), ())), preferred_element_type=F32)


def _split(a):
    hi = a.astype(BF16)
    lo = (a - hi.astype(F32)).astype(BF16)
    return hi, lo


def _dot3(a, b):
    ah, al = _split(a)
    bh, bl = _split(b)
    return _dot(ah, bh) + _dot(ah, bl) + _dot(al, bh)


def _rms(x, g):
    ms = jnp.mean(x * x, axis=-1, keepdims=True)
    return x * lax.rsqrt(ms + RMS_EPS) * g


def _sigmoid(x):
    return 1.0 / (1.0 + jnp.exp(-x))


def _pack_bf16_pairs(y):
    n = y.shape[1] // 2
    yb = y.astype(BF16).astype(F32)
    lo = lax.bitcast_convert_type(yb[:, :n], jnp.uint32) >> 16
    hi = lax.bitcast_convert_type(yb[:, n:], jnp.uint32) & jnp.uint32(0xFFFF0000)
    return hi | lo


def _unpack_bf16_pairs(w):
    lo = lax.bitcast_convert_type(w << 16, F32)
    hi = lax.bitcast_convert_type(w & jnp.uint32(0xFFFF0000), F32)
    return lo, hi


def _stream_dims(stream):
    batch, t_len, d = stream[0].shape
    return batch, (t_len + stream[1].shape[1] if len(stream) == 2 else t_len), d


def _stream_rows(stream_refs, i, tile):
    if len(stream_refs) == 2:
        ctx_ref, x_ref = stream_refs
        return jnp.where(tile == 0, ctx_ref[i], x_ref[i])
    x_ref, y_ref, mod_ref = stream_refs
    lo, hi = _unpack_bf16_pairs(y_ref[i])
    return x_ref[i] + mod_ref[i, 0, 5:6, :] * jnp.concatenate([lo, hi], axis=1)


def _stream_specs(stream, nb, tile_of):
    d = stream[0].shape[-1]
    if len(stream) == 2:
        return [pl.BlockSpec((nb, TILE, d), lambda b, t: (b, 0, 0)),
                pl.BlockSpec((nb, TILE, d), lambda b, t: (b, jnp.maximum(tile_of(t) - 1, 0), 0))]
    tok = lambda b, t: (b, tile_of(t), 0)
    return [pl.BlockSpec((nb, TILE, d), tok),
            pl.BlockSpec((nb, TILE, d // 2), tok),
            pl.BlockSpec((nb, 1, 8, d), lambda b, t: (b, jnp.minimum(tile_of(t), 1), 0, 0))]


def _adaln_kernel(c_ref, w_ref, b_ref, o_ref):
    cv = c_ref[...]
    s = cv * _sigmoid(cv)
    o_ref[0] = _dot3(s, w_ref[0]) + b_ref[0]


def _adaln(cpad, w_ada, b_ada):
    depth, d, n6 = w_ada.shape
    tn = 1536
    return pl.pallas_call(
        _adaln_kernel,
        grid=(depth, n6 // tn),
        in_specs=[pl.BlockSpec((16, d), lambda l, j: (0, 0)),
                  pl.BlockSpec((1, d, tn), lambda l, j: (l, 0, j)),
                  pl.BlockSpec((1, 1, tn), lambda l, j: (l, 0, j))],
        out_specs=pl.BlockSpec((1, 16, tn), lambda l, j: (l, 0, j)),
        out_shape=jax.ShapeDtypeStruct((depth, 16, n6), F32),
        compiler_params=pltpu.CompilerParams(dimension_semantics=("parallel", "parallel"),
                                             vmem_limit_bytes=VMEM_LIMIT),
        name="adaln",
    )(cpad, w_ada, b_ada.reshape(depth, 1, n6))


def _inproj_kernel(*refs, n_stream):
    for base in range(0, refs[0].shape[0], PAIR):
        _inproj_rows(base, pl.program_id(1), refs[:n_stream], *refs[n_stream:])


def _inproj_rows(base, tile, stream_refs, mod_ref, g_ref, w_ref, cos_ref, sa_ref, sb_ref, gq_ref, gk_ref, mseg_ref,
                 q_ref, k_ref, v_ref):
    members = range(base, base + PAIR)
    h = jnp.concatenate(
        [(_rms(_stream_rows(stream_refs, i, tile), g_ref[...]) * (1.0 + mod_ref[i, 0, 1:2, :])
          + mod_ref[i, 0, 0:1, :]).astype(BF16) for i in members], axis=0)
    cos = jnp.concatenate([cos_ref[...]] * PAIR, axis=0)
    sa = jnp.concatenate([sa_ref[...]] * PAIR, axis=0)
    sb = jnp.concatenate([sb_ref[...]] * PAIR, axis=0)
    mseg = mseg_ref[...]

    def put(ref, cols, val):
        for n, i in enumerate(members):
            ref[i, :, cols] = val[n * TILE:(n + 1) * TILE]

    def rope(b):
        return b * cos + pltpu.roll(b, LANES - 16, 1) * sa + pltpu.roll(b, 16, 1) * sb

    def qknorm(b, g):
        hi, lo = _split(b * b)
        ms = (_dot(hi, mseg) + _dot(lo, mseg)) * (1.0 / HEAD_DIM)
        return b * lax.rsqrt(ms + RMS_EPS) * g

    def wcols(*pieces):
        parts = [w_ref[:, a:a + n] for a, n in pieces]
        return parts[0] if len(parts) == 1 else jnp.concatenate(parts, axis=1)

    lane = lax.broadcasted_iota(jnp.int32, (PAIR * TILE, LANES), 1)

    def regroup(p2):
        h0, h1 = p2[:, :LANES], p2[:, LANES:]
        return (jnp.where(lane < HEAD_DIM, h0, pltpu.roll(h1, HEAD_DIM, 1)),
                jnp.where(lane < HEAD_DIM, pltpu.roll(h0, HEAD_DIM, 1), h1))

    q0, kv0 = 0, Q_COLS
    rope_chunks = [wcols((q0, 256)), wcols((q0 + 256, 256)), wcols((q0 + QA, QB)), wcols((q0 + QA + QB, QC)),
                   wcols((kv0, 256)), wcols((kv0 + 256, 256)),
                   wcols((kv0 + 2 * KA, KB), (kv0 + 2 * KA + 2 * KB, KC))]
    v_chunks = [wcols((kv0 + KA, 256)), wcols((kv0 + KA + 256, 256)),
                wcols((kv0 + 2 * KA + KB, KB), (kv0 + 2 * KA + 2 * KB + KC, KC))]
    for c, wc in enumerate(rope_chunks):
        p2 = _dot(h, wc)
        halves = regroup(p2) if c in (2, 3) else (p2[:, :LANES], p2[:, LANES:])
        for half in range(2):
            j = 2 * c + half
            p = halves[half]
            if j in (4, 5):
                p = qknorm(p, gq_ref[...])
            if j == 12:
                p = qknorm(p, gk_ref[...])
            p = rope(p)
            if j < Q_COLS // LANES:
                put(q_ref, slice(j * LANES, (j + 1) * LANES), (p * (HEAD_DIM ** -0.5)).astype(BF16))
            else:
                jk = j - Q_COLS // LANES
                put(k_ref, slice(jk * LANES, (jk + 1) * LANES), p.astype(BF16))
    for c, wc in enumerate(v_chunks):
        put(v_ref, slice(c * 256, (c + 1) * 256), _dot(h, wc).astype(BF16))


def _inproj(stream, mod, g, w, cos, sa, sb, gq, gk, mseg):
    batch, t_len, d = _stream_dims(stream)
    nt = t_len // TILE
    nb = 2 * PAIR
    row = lambda b, t: (t, 0)
    const = lambda b, t: (0, 0)
    tok = lambda b, t: (b, t, 0)
    return pl.pallas_call(
        functools.partial(_inproj_kernel, n_stream=len(stream)),
        grid=(batch // nb, nt),
        in_specs=_stream_specs(stream, nb, lambda t: t) + [
                  pl.BlockSpec((nb, 1, 8, d), lambda b, t: (b, jnp.minimum(t, 1), 0, 0)),
                  pl.BlockSpec((1, d), const),
                  pl.BlockSpec(w.shape, const),
                  pl.BlockSpec((TILE, LANES), row),
                  pl.BlockSpec((TILE, LANES), row),
                  pl.BlockSpec((TILE, LANES), row),
                  pl.BlockSpec((1, LANES), const),
                  pl.BlockSpec((1, LANES), const),
                  pl.BlockSpec((LANES, LANES), const)],
        out_specs=[pl.BlockSpec((nb, TILE, Q_COLS), tok),
                   pl.BlockSpec((nb, TILE, K_COLS), tok),
                   pl.BlockSpec((nb, TILE, V_COLS), tok)],
        out_shape=[jax.ShapeDtypeStruct((batch, t_len, Q_COLS), BF16),
                   jax.ShapeDtypeStruct((batch, t_len, K_COLS), BF16),
                   jax.ShapeDtypeStruct((batch, t_len, V_COLS), BF16)],
        compiler_params=pltpu.CompilerParams(dimension_semantics=("parallel", "parallel"),
                                             vmem_limit_bytes=VMEM_LIMIT),
        name="inproj",
    )(*stream, mod, g, w, cos, sa, sb, gq, gk, mseg)


def _scores_pass(qq, k_ref, bi, col, starts, s_ref, mask_fn=None, extra=None):
    half = KEY_CHUNK // 2
    rows = qq.shape[0]
    mrun = None
    for j, st in enumerate(starts):
        s = _dot_nt(qq, k_ref[bi, pl.ds(st, KEY_CHUNK), col:col + LANES])
        if mask_fn is not None:
            s = mask_fn(j, st, s)
        s_ref[0:rows, j * KEY_CHUNK:(j + 1) * KEY_CHUNK] = s
        mj = jnp.maximum(s[:, :half], s[:, half:])
        mrun = mj if mrun is None else jnp.maximum(mrun, mj)
    m = jnp.max(mrun, axis=-1, keepdims=True)
    return m if extra is None else jnp.maximum(m, extra)


def _values_pass(m, rows, v_ref, bi, col, starts, s_ref, extra=None):
    half = KEY_CHUNK // 2
    lrun = None
    acc = None
    for j, st in enumerate(starts):
        e = jnp.exp(s_ref[0:rows, j * KEY_CHUNK:(j + 1) * KEY_CHUNK] - m)
        lj = e[:, :half] + e[:, half:]
        lrun = lj if lrun is None else lrun + lj
        pv = _dot(e.astype(BF16), v_ref[bi, pl.ds(st, KEY_CHUNK), col:col + LANES])
        acc = pv if acc is None else acc + pv
    l = jnp.sum(lrun, axis=-1, keepdims=True)
    if extra is not None:
        l = l + jnp.exp(extra - m)
    return acc / l


def _softmax_pv(qq, k_ref, v_ref, bi, col, starts, s_ref, mask_fn=None, extra=None):
    m = _scores_pass(qq, k_ref, bi, col, starts, s_ref, mask_fn, extra)
    return _values_pass(m, qq.shape[0], v_ref, bi, col, starts, s_ref, extra)


def _all_chunks(n_keys):
    return [j * KEY_CHUNK for j in range(n_keys // KEY_CHUNK)]


def _half_masks(shape):
    lane = lax.broadcasted_iota(jnp.int32, shape, 1)
    return lane < HEAD_DIM, lane >= HEAD_DIM


def _stack_group_queries(q2, kv):
    lo_m, hi_m = _half_masks((TILE, LANES))
    m = lo_m if kv == 0 else hi_m
    zero = jnp.zeros((TILE, LANES), q2.dtype)
    return jnp.concatenate([jnp.where(m, q2[:, :LANES], zero), jnp.where(m, q2[:, LANES:], zero)], axis=0)


def _merge_kv_outputs(o_kv0, o_kv1):
    lo_m, _ = _half_masks((TILE, LANES))
    kv0 = jnp.where(lo_m, o_kv0[:TILE], pltpu.roll(o_kv0[TILE:], HEAD_DIM, 1))
    kv1 = jnp.where(lo_m, pltpu.roll(o_kv1[:TILE], HEAD_DIM, 1), o_kv1[TILE:])
    return jnp.concatenate([kv0, kv1], axis=1)


def _attn_kernel(sink_ref, lam_ref, subg_ref, qa_ref, qb_ref, qc_ref, ka_ref, kb_ref, kc_ref, va_ref, vb_ref, vc_ref,
                 o_ref, s_ref, *, t0, ctx_len, lambda_init):
    t = pl.program_id(1) + t0
    t_len = ka_ref.shape[1]
    band = 2 * TILE
    lo_m, hi_m = _half_masks((TILE, LANES))
    zero = jnp.zeros((TILE, LANES), BF16)
    row = lax.broadcasted_iota(jnp.int32, (2 * TILE, 1), 0)
    lv = lam_ref[...]
    lam = (jnp.exp(jnp.sum(lv[0:1] * lv[1:2], keepdims=True))
           - jnp.exp(jnp.sum(lv[2:3] * lv[3:4], keepdims=True)) + lambda_init)

    def sink_col(kv):
        return jnp.where(row < TILE, sink_ref[2 * kv], sink_ref[2 * kv + 1])

    def mixers(starts_ab, starts_c, mask_c):
        n_chain = 0
        for bi in range(qa_ref.shape[0]):
            for h in range(QA // LANES):
                q = qa_ref[bi, :, h * LANES:(h + 1) * LANES]
                qq = jnp.concatenate([jnp.where(lo_m, q, zero), jnp.where(hi_m, q, zero)], axis=0)
                o = _softmax_pv(qq, ka_ref, va_ref, bi, h * LANES, starts_ab, s_ref.at[n_chain % 2])
                n_chain += 1
                o = o[:TILE] - lam * o[TILE:]
                o_ref[bi, :, h * LANES:(h + 1) * LANES] = (
                    _rms(o, subg_ref[...]) * (1.0 - lambda_init)).astype(BF16)
            outs = []
            for kv in range(2):
                outs.append(_softmax_pv(_stack_group_queries(qb_ref[bi], kv), kb_ref, vb_ref, bi, 0, starts_ab,
                                        s_ref.at[n_chain % 2]))
                n_chain += 1
            o_ref[bi, :, QA:QA + QB] = _merge_kv_outputs(*outs).astype(BF16)
            outs = []
            for kv in range(2):
                outs.append(_softmax_pv(_stack_group_queries(qc_ref[bi], kv), kc_ref, vc_ref, bi, 0, starts_c,
                                        s_ref.at[n_chain % 2], mask_c, sink_col(kv)))
                n_chain += 1
            o_ref[bi, :, QA + QB:] = _merge_kv_outputs(*outs).astype(BF16)

    def ctx_step():
        mixers([0], [0], None)

    def latent_step():
        q_start = (t - 1) * TILE
        u0 = pl.multiple_of(jnp.minimum(q_start + ctx_len - WINDOW, t_len - band), WINDOW)
        q_pos = q_start + (lax.broadcasted_iota(jnp.int32, (2 * TILE, KEY_CHUNK), 0) & (TILE - 1))
        key_lane = lax.broadcasted_iota(jnp.int32, (2 * TILE, KEY_CHUNK), 1)

        def mask_c(j, st, s):
            if j == 0:
                return s
            k_pos = (st - ctx_len) + key_lane
            valid = (jnp.abs(k_pos - q_pos) <= WINDOW) & (k_pos >= 0)
            return jnp.where(valid, s, NEG_BIG)

        mixers(_all_chunks(t_len), [0] + [u0 + i * KEY_CHUNK for i in range(band // KEY_CHUNK)], mask_c)

    if t0 == 0:
        pl.when(t == 0)(ctx_step)
        pl.when(t > 0)(latent_step)
    else:
        latent_step()


def _attention(sink, lamv, subg, q, k, v, t0, lambda_init):
    batch, t_len, _ = q.shape
    nt = t_len // TILE
    kern = functools.partial(_attn_kernel, t0=t0, ctx_len=TILE, lambda_init=lambda_init)
    const = lambda b, t: (0, 0)
    nb = 1
    qspec = lambda width, blk: pl.BlockSpec((nb, TILE, width), lambda b, t: (b, t + t0, blk))
    kvspec = lambda width, blk: pl.BlockSpec((nb, t_len, width), lambda b, t: (b, 0, blk))
    return pl.pallas_call(
        kern,
        grid=(batch // nb, nt - t0),
        in_specs=[pl.BlockSpec(memory_space=pltpu.SMEM),
                  pl.BlockSpec(lamv.shape, const),
                  pl.BlockSpec((1, LANES), const),
                  qspec(QA, 0), qspec(QB, QA // QB), qspec(QC, (QA + QB) // QC),
                  kvspec(KA, 0), kvspec(KB, KA // KB), kvspec(KC, (KA + KB) // KC),
                  kvspec(KA, 0), kvspec(KB, KA // KB), kvspec(KC, (KA + KB) // KC)],
        out_specs=pl.BlockSpec((nb, TILE, Q_COLS), lambda b, t: (b, t + t0, 0)),
        out_shape=jax.ShapeDtypeStruct((batch, t_len, Q_COLS), BF16),
        scratch_shapes=[pltpu.VMEM((2, 2 * TILE, t_len), F32)],
        compiler_params=pltpu.CompilerParams(dimension_semantics=("parallel", "parallel"),
                                             vmem_limit_bytes=VMEM_LIMIT),
        name="attention",
    )(sink, lamv, subg, q, q, q, k, k, k, v, v, v)


def _route(scores, bias):
    lane = lax.broadcasted_iota(jnp.int32, scores.shape, 1)
    sel = scores + bias
    in_group = lane & (EXPERTS_PER_GROUP - 1)
    group = lane >> 2

    def neighbours(x, idx, step, span):
        for d in (1, 2, 3):
            fwd = (idx + d) < 4
            y = jnp.where(fwd, pltpu.roll(x, LANES - d * step, 1), pltpu.roll(x, span - d * step, 1))
            yield y, fwd

    rank = jnp.zeros_like(sel)
    for y, fwd in neighbours(sel, in_group, 1, EXPERTS_PER_GROUP):
        beats = (y > sel) | ((y == sel) & jnp.logical_not(fwd))
        rank = rank + jnp.where(beats, 1.0, 0.0)
    top2 = jnp.where(rank < 2.0, sel, 0.0)
    gsum = top2
    for y, _ in neighbours(top2, in_group, 1, EXPERTS_PER_GROUP):
        gsum = gsum + y
    grank = jnp.zeros_like(sel)
    for y, fwd in neighbours(gsum, group, EXPERTS_PER_GROUP, N_EXPERTS):
        beats = (y > gsum) | ((y == gsum) & jnp.logical_not(fwd))
        grank = grank + jnp.where(beats, 1.0, 0.0)
    best = (grank < 0.5) & (lane < N_EXPERTS)
    chosen = best & (rank < 2.0)
    w = jnp.where(chosen, scores, 0.0)
    gates = w / jnp.sum(w, axis=-1, keepdims=True)
    gid = jnp.sum(jnp.where(best & (in_group == 0), group.astype(F32), 0.0), axis=-1, keepdims=True)
    return jnp.where(lane == GROUP_LANE, gid, gates)


def _outproj_kernel(*refs, n_stream, t0):
    stream_refs = refs[:n_stream]
    tile = pl.program_id(1) + t0
    mix_ref, w_ref, mod_ref, g_ref, wrh_ref, wrl_ref, rb_ref, x1_ref, h2_ref, gates_ref = refs[n_stream:]
    for i in range(mix_ref.shape[0]):
        x1 = _stream_rows(stream_refs, i, tile) + mod_ref[i, 0, 2:3, :] * _dot(mix_ref[i], w_ref[...])
        x1_ref[i] = x1
        h2 = _rms(x1, g_ref[...]) * (1.0 + mod_ref[i, 0, 4:5, :]) + mod_ref[i, 0, 3:4, :]
        h2_ref[i] = _pack_bf16_pairs(h2)
        hi, lo = _split(h2)
        logits = _dot(hi, wrh_ref[...]) + _dot(hi, wrl_ref[...]) + _dot(lo, wrh_ref[...])
        gates_ref[i] = _route(_sigmoid(logits), rb_ref[...])


def _outproj(stream, mix, w, mod, g, wrh, wrl, rb, t0):
    batch, t_len, d = _stream_dims(stream)
    nt = t_len // TILE
    nb = 2 * PAIR
    t_out = (nt - t0) * TILE
    const = lambda b, t: (0, 0)
    tok_in = lambda b, t: (b, t + t0, 0)
    tok_out = lambda b, t: (b, t, 0)
    return pl.pallas_call(
        functools.partial(_outproj_kernel, n_stream=len(stream), t0=t0),
        grid=(batch // nb, nt - t0),
        in_specs=_stream_specs(stream, nb, lambda t: t + t0) + [
                  pl.BlockSpec((nb, TILE, Q_COLS), tok_in),
                  pl.BlockSpec(w.shape, const),
                  pl.BlockSpec((nb, 1, 8, d), lambda b, t: (b, jnp.minimum(t + t0, 1), 0, 0)),
                  pl.BlockSpec((1, d), const),
                  pl.BlockSpec((d, LANES), const),
                  pl.BlockSpec((d, LANES), const),
                  pl.BlockSpec((1, LANES), const)],
        out_specs=[pl.BlockSpec((nb, TILE, d), tok_out),
                   pl.BlockSpec((nb, TILE, d // 2), tok_out),
                   pl.BlockSpec((nb, TILE, LANES), tok_out)],
        out_shape=[jax.ShapeDtypeStruct((batch, t_out, d), F32),
                   jax.ShapeDtypeStruct((batch, t_out, d // 2), jnp.uint32),
                   jax.ShapeDtypeStruct((batch, t_out, LANES), F32)],
        compiler_params=pltpu.CompilerParams(dimension_semantics=("parallel", "parallel"),
                                             vmem_limit_bytes=VMEM_LIMIT),
        name="outproj",
    )(*stream, mix, w, mod, g, wrh, wrl, rb)


def _sc_workers():
    info = plsc.get_sparse_core_info()
    return info.num_cores, info.num_subcores


def _sc_dispatch(hpk, gates, pos2d, n_out):
    n, w = hpk.shape
    gw = gates.shape[1]
    nc, ns = _sc_workers()
    steps = n // (nc * ns * SC_WIN)
    mesh = plsc.VectorSubcoreMesh(core_axis_name="c", subcore_axis_name="s")

    @functools.partial(
        pl.kernel, mesh=mesh,
        out_type=(jax.ShapeDtypeStruct((n_out, w), hpk.dtype), jax.ShapeDtypeStruct((n_out, gw), gates.dtype)),
        scratch_types=[pltpu.VMEM((1, SC_WIN), jnp.int32), pltpu.VMEM((SC_WIN, w), hpk.dtype),
                       pltpu.VMEM((SC_WIN, gw), gates.dtype), pltpu.SemaphoreType.DMA, pltpu.SemaphoreType.DMA],
        name="sc_dispatch")
    def k(h_hbm, g_hbm, pos_hbm, hs_hbm, gs_hbm, idx_v, rows_v, grow_v, sem_in, sem_out):
        wid = lax.axis_index("s") * nc + lax.axis_index("c")

        @pl.loop(0, steps)
        def _(j):
            blk = wid * steps + j
            off = pl.multiple_of(blk * SC_WIN, SC_WIN)
            loads = [pltpu.async_copy(pos_hbm.at[pl.ds(blk, 1)], idx_v, sem_in),
                     pltpu.async_copy(h_hbm.at[pl.ds(off, SC_WIN)], rows_v, sem_in),
                     pltpu.async_copy(g_hbm.at[pl.ds(off, SC_WIN)], grow_v, sem_in)]
            for cp in loads:
                cp.wait()
            stores = [pltpu.async_copy(rows_v, hs_hbm.at[idx_v.at[0]], sem_out),
                      pltpu.async_copy(grow_v, gs_hbm.at[idx_v.at[0]], sem_out)]
            for cp in stores:
                cp.wait()

    return k(hpk, gates, pos2d)


def _sc_combine(ys, pos2d, n):
    w = ys.shape[1]
    nc, ns = _sc_workers()
    steps = n // (nc * ns * SC_WIN)
    mesh = plsc.VectorSubcoreMesh(core_axis_name="c", subcore_axis_name="s")

    @functools.partial(
        pl.kernel, mesh=mesh,
        out_type=jax.ShapeDtypeStruct((n, w), ys.dtype),
        scratch_types=[pltpu.VMEM((1, SC_WIN), jnp.int32), pltpu.VMEM((SC_WIN, w), ys.dtype)],
        name="sc_combine")
    def k(y_hbm, pos_hbm, o_hbm, idx_v, rows_v):
        wid = lax.axis_index("s") * nc + lax.axis_index("c")

        @pl.loop(0, steps)
        def _(j):
            blk = wid * steps + j
            off = pl.multiple_of(blk * SC_WIN, SC_WIN)
            pltpu.sync_copy(pos_hbm.at[pl.ds(blk, 1)], idx_v)
            pltpu.sync_copy(y_hbm.at[idx_v.at[0]], rows_v)
            pltpu.sync_copy(rows_v, o_hbm.at[pl.ds(off, SC_WIN)])

    return k(ys, pos2d)


PLAN_ROWS = 256
PLAN_BLOCK = 2048


def _plan_kernel(gates_ref, pos_ref, tg_ref, tu_ref, g_ref):
    step = pl.program_id(0)

    @pl.when(step == 0)
    def _():
        g_ref[...] = jnp.full(g_ref.shape, -1.0, F32)

    gates = gates_ref[...]
    lane = lax.broadcasted_iota(jnp.int32, gates.shape, 1)
    along_lanes = _dot_nt(jnp.ones((8, LANES), BF16), jnp.where(lane == GROUP_LANE, gates, 0.0).astype(BF16))
    for r in range(PLAN_BLOCK // LANES):
        g_ref[pl.ds(step * (PLAN_BLOCK // LANES) + r, 1), :] = along_lanes[0:1, r * LANES:(r + 1) * LANES]

    pl.when(step == pl.num_programs(0) - 1)(functools.partial(_plan_finish, g_ref, pos_ref, tg_ref, tu_ref))


def _plan_finish(g_ref, pos_ref, tg_ref, tu_ref):
    g = g_ref[...]
    rows = g.shape[0]
    n_groups = N_EXPERTS // EXPERTS_PER_GROUP
    before_lane = (lax.broadcasted_iota(jnp.int32, (LANES, LANES), 0)
                   < lax.broadcasted_iota(jnp.int32, (LANES, LANES), 1)).astype(BF16)
    before_row = (lax.broadcasted_iota(jnp.int32, (rows, rows), 1)
                  < lax.broadcasted_iota(jnp.int32, (rows, rows), 0)).astype(BF16)
    tile_row = lax.broadcasted_iota(jnp.int32, (1, LANES), 1).astype(F32) * FFN_TILE
    pos = jnp.zeros_like(g)
    start = jnp.zeros((1, 1), F32)
    tile_group = jnp.zeros((1, LANES), F32)
    used_end = []
    for grp in range(n_groups):
        m = jnp.where(g == grp, 1.0, 0.0)
        in_row = _dot(m.astype(BF16), before_lane)
        row_total = jnp.sum(m, axis=1, keepdims=True)
        rows_before = _dot(before_row, jnp.broadcast_to(row_total, m.shape).astype(BF16))
        pos = pos + m * (start + in_row + rows_before)
        count = jnp.sum(row_total, axis=0, keepdims=True)
        used_end.append(start + count)
        start = start + jnp.floor((count + (FFN_TILE - 1)) * (1.0 / FFN_TILE)) * FFN_TILE
        if grp < n_groups - 1:
            tile_group = tile_group + jnp.where(tile_row >= start, 1.0, 0.0)
    pos_ref[...] = pos.astype(jnp.int32)
    tg_ref[...] = tile_group.astype(jnp.int32)
    end_of_tile_group = sum(jnp.where(tile_group == grp, used_end[grp], 0.0) for grp in range(n_groups))
    tu_ref[...] = jnp.where(tile_row < end_of_tile_group, 1, 0).astype(jnp.int32)


def _bucket_plan(gates, n_pad):
    n = gates.shape[0]
    whole = lambda shape: pl.BlockSpec(shape, lambda i: (0, 0))
    pos, tile_group, tile_used = pl.pallas_call(
        _plan_kernel,
        grid=(n // PLAN_BLOCK,),
        in_specs=[pl.BlockSpec((PLAN_BLOCK, LANES), lambda i: (i, 0))],
        out_specs=[whole((PLAN_ROWS, LANES)), whole((1, LANES)), whole((1, LANES))],
        out_shape=[jax.ShapeDtypeStruct((PLAN_ROWS, LANES), jnp.int32),
                   jax.ShapeDtypeStruct((1, LANES), jnp.int32),
                   jax.ShapeDtypeStruct((1, LANES), jnp.int32)],
        scratch_shapes=[pltpu.VMEM((PLAN_ROWS, LANES), F32)],
        compiler_params=pltpu.CompilerParams(dimension_semantics=("arbitrary",)),
        name="plan",
    )(gates)
    n_tiles = n_pad // FFN_TILE
    return pos[:n // LANES].reshape(n // SC_WIN, SC_WIN), tile_group[0, :n_tiles], tile_used[0, :n_tiles]


def _ffn_kernel(tg_ref, tu_ref, hs_ref, gs_ref, wg_ref, wu_ref, wd_ref, ys_ref, wgb_ref, wub_ref, wdb_ref):
    i = pl.program_id(0)

    @pl.when((i == 0) | (tg_ref[i] != tg_ref[jnp.maximum(i - 1, 0)]))
    def _():
        wgb_ref[...] = wg_ref[0].astype(BF16)
        wub_ref[...] = wu_ref[0].astype(BF16)
        wdb_ref[...] = wd_ref[0].astype(BF16)

    @pl.when(tu_ref[i] > 0)
    def _():
        lo, hi = _unpack_bf16_pairs(hs_ref[...])
        lo = lo.astype(BF16)
        hi = hi.astype(BF16)
        half = lo.shape[1]
        gs = gs_ref[...]
        lane = lax.broadcasted_iota(jnp.int32, gs.shape, 1)
        first = tg_ref[i] * EXPERTS_PER_GROUP
        y = None
        for e in range(EXPERTS_PER_GROUP):
            a = _dot(lo, wgb_ref[e, :half, :]) + _dot(hi, wgb_ref[e, half:, :])
            u = _dot(lo, wub_ref[e, :half, :]) + _dot(hi, wub_ref[e, half:, :])
            gate = jnp.sum(jnp.where(lane == first + e, gs, 0.0), axis=-1, keepdims=True)
            part = _dot(((a * _sigmoid(a)) * u * gate).astype(BF16), wdb_ref[e])
            y = part if y is None else y + part
        ys_ref[...] = _pack_bf16_pairs(y)


def _ffn(tile_group, tile_used, hs, gs, wg, wu, wd, layer):
    n_pad, half = hs.shape
    grp = lambda i, tg, tu: (layer, tg[i], 0, 0)
    gshape = lambda w: (EXPERTS_PER_GROUP,) + w.shape[2:]
    wspec = lambda w: pl.BlockSpec((1,) + gshape(w), grp)
    return pl.pallas_call(
        _ffn_kernel,
        grid_spec=pltpu.PrefetchScalarGridSpec(
            num_scalar_prefetch=2,
            grid=(n_pad // FFN_TILE,),
            in_specs=[pl.BlockSpec((FFN_TILE, half), lambda i, tg, tu: (i, 0)),
                      pl.BlockSpec((FFN_TILE, LANES), lambda i, tg, tu: (i, 0)),
                      wspec(wg), wspec(wu), wspec(wd)],
            out_specs=pl.BlockSpec((FFN_TILE, half), lambda i, tg, tu: (i, 0)),
            scratch_shapes=[pltpu.VMEM(gshape(wg), BF16), pltpu.VMEM(gshape(wu), BF16),
                            pltpu.VMEM(gshape(wd), BF16)]),
        out_shape=jax.ShapeDtypeStruct((n_pad, half), jnp.uint32),
        compiler_params=pltpu.CompilerParams(dimension_semantics=("arbitrary",), vmem_limit_bytes=VMEM_LIMIT),
        name="ffn",
    )(tile_group, tile_used, hs, gs, wg, wu, wd)


def _residual_kernel(x_ref, y_ref, mod_ref, fg_ref, o_ref, *, final_norm):
    for i in range(x_ref.shape[0]):
        lo, hi = _unpack_bf16_pairs(y_ref[i])
        y = x_ref[i] + mod_ref[i, 0, 5:6, :] * jnp.concatenate([lo, hi], axis=1)
        if final_norm:
            y = _rms(y, fg_ref[...])
        o_ref[i] = y


def _residual(x1, yg, mod, fg, t0, final_norm):
    batch, t_out, d = x1.shape
    nb = PAIR
    tok = lambda b, t: (b, t, 0)
    return pl.pallas_call(
        functools.partial(_residual_kernel, final_norm=final_norm),
        grid=(batch // nb, t_out // TILE),
        in_specs=[pl.BlockSpec((nb, TILE, d), tok),
                  pl.BlockSpec((nb, TILE, d // 2), tok),
                  pl.BlockSpec((nb, 1, 8, d), lambda b, t: (b, jnp.minimum(t + t0, 1), 0, 0)),
                  pl.BlockSpec((1, d), lambda b, t: (0, 0))],
        out_specs=pl.BlockSpec((nb, TILE, d), tok),
        out_shape=jax.ShapeDtypeStruct((batch, t_out, d), F32),
        compiler_params=pltpu.CompilerParams(dimension_semantics=("parallel", "parallel"),
                                             vmem_limit_bytes=VMEM_LIMIT),
        name="residual",
    )(x1, yg, mod, fg)


def _moe(h2, gates, wg, wu, wd, layer):
    batch, t_out, half = h2.shape
    n = batch * t_out
    n_pad = n + (N_EXPERTS // EXPERTS_PER_GROUP) * FFN_TILE
    assert n % PLAN_BLOCK == 0 and n // LANES <= PLAN_ROWS and n_pad // FFN_TILE <= LANES
    gates = gates.reshape(n, LANES)
    pos2d, tile_group, tile_used = _bucket_plan(gates, n_pad)
    hs, gs = _sc_dispatch(h2.reshape(n, half), gates, pos2d, n_pad)
    ys = _ffn(tile_group, tile_used, hs, gs, wg, wu, wd, layer)
    return _sc_combine(ys, pos2d, n).reshape(batch, t_out, half)


def _rope_tables(seq, ctx_len):
    rows = seq // GRID_W
    row_pos = jnp.repeat(jnp.arange(rows, dtype=F32), GRID_W)
    col_pos = jnp.tile(jnp.arange(GRID_W, dtype=F32), rows)
    axis_dim = HEAD_DIM // 2
    inv_freq = ROPE_BASE ** (-jnp.arange(0, axis_dim, 2, dtype=F32) / axis_dim)
    ang_r = row_pos[:, None] * inv_freq[None, :]
    ang_c = col_pos[:, None] * inv_freq[None, :]
    z = jnp.zeros_like(ang_r)
    cos = jnp.concatenate([jnp.cos(ang_r)] * 2 + [jnp.cos(ang_c)] * 2, axis=-1)
    sin_a = jnp.concatenate([-jnp.sin(ang_r), z, -jnp.sin(ang_c), z], axis=-1)
    sin_b = jnp.concatenate([z, jnp.sin(ang_r), z, jnp.sin(ang_c)], axis=-1)

    def full(tab, fill):
        tab = jnp.tile(tab, (1, LANES // HEAD_DIM))
        return jnp.concatenate([jnp.full((ctx_len, LANES), fill, F32), tab], axis=0)

    return full(cos, 1.0), full(sin_a, 0.0), full(sin_b, 0.0)


def kernel(x, c, ctx, c_ctx, w_ada, b_ada, norm1_g, norm2_g, w_in, w_out, lam_q1, lam_k1, lam_q2, lam_k2,
           subln_g, q_norm_g, k_norm_g, sink, w_router, router_bias, w_gate, w_up, w_down, final_g):
    batch, seq, d = x.shape
    ctx_len = ctx.shape[1]
    depth = w_in.shape[0]
    sc_rows = SC_WIN * math.prod(_sc_workers())
    assert ctx_len == TILE and seq % TILE == 0 and batch % PAIR == 0 and batch <= 15
    assert (batch * seq) % sc_rows == 0 and (batch * (seq + ctx_len)) % sc_rows == 0

    cpad = jnp.zeros((16, d), F32).at[:batch].set(c).at[batch].set(c_ctx)
    mod_all = _adaln(cpad, w_ada, b_ada)
    cos, sin_a, sin_b = _rope_tables(seq, ctx_len)
    mseg = jnp.asarray(np.kron(np.eye(2), np.ones((HEAD_DIM, HEAD_DIM))), BF16)
    wr = jnp.zeros((d, LANES), F32).at[:, :N_EXPERTS].set(w_router)
    wrh = wr.astype(BF16)
    wrl = (wr - wrh.astype(F32)).astype(BF16)
    rb = jnp.zeros((1, LANES), F32).at[0, :N_EXPERTS].set(router_bias)

    stream = (ctx, x)

    for l in range(depth):
        last = l == depth - 1
        lambda_init = 0.8 - 0.6 * math.exp(-0.3 * l)
        m6 = mod_all[l].reshape(16, 6, d)
        m8 = jnp.concatenate([m6, jnp.zeros((16, 2, d), F32)], axis=1)
        mod = jnp.stack([jnp.broadcast_to(m8[batch], (batch, 8, d)), m8[:batch]], axis=1)

        w_in_p = w_in[l].astype(BF16)
        w_out_p = w_out[l].astype(BF16)
        gq = jnp.tile(q_norm_g[l], 2).reshape(1, LANES)
        gk = jnp.tile(k_norm_g[l], 2).reshape(1, LANES)
        q, k, v = _inproj(stream, mod, norm1_g[l].reshape(1, d), w_in_p, cos, sin_a, sin_b, gq, gk, mseg)

        t0 = 1 if last else 0
        lamv = jnp.stack([lam_q1[l], lam_k1[l], lam_q2[l], lam_k2[l]])
        mix = _attention(sink[l], lamv, subln_g[l].reshape(1, LANES), q, k, v, t0, lambda_init)
        x1, h2, gates = _outproj(stream, mix, w_out_p, mod, norm2_g[l].reshape(1, d), wrh, wrl, rb, t0)
        stream = (x1, _moe(h2, gates, w_gate, w_up, w_down, l), mod)
    return _residual(*stream, final_g.reshape(1, d), 1, final_norm=True)
```

```python
import functools
import math

import numpy as np
import jax
import jax.numpy as jnp
from jax import lax
from jax.experimental import pallas as pl
from jax.experimental.pallas import tpu as pltpu
from jax.experimental.pallas import tpu_sc as plsc

F32 = jnp.float32
BF16 = jnp.bfloat16

HEAD_DIM = 64
GRID_W = 64
ROPE_BASE = 10000.0
RMS_EPS = 1e-6
WINDOW = 128
N_EXPERTS = 16
EXPERTS_PER_GROUP = 4
LANES = 128
TILE = 256
PAIR = 2
FFN_TILE = 512
SC_WIN = 64
GROUP_LANE = 16
NEG_BIG = -1e30
KEY_CHUNK = 256
VMEM_LIMIT = 52 * 1024 * 1024

QA, QB, QC = 512, 256, 256
KA, KB, KC = 512, 128, 128
Q_COLS = QA + QB + QC
K_COLS = KA + KB + KC
V_COLS = 768
ROPE_COLS = Q_COLS + K_COLS


def _dot(a, b):
    return jnp.dot(a, b, preferred_element_type=F32)


def _dot_nt(a, b):
    return lax.dot_general(a, b, (((1,), (1,)), ((), ())), preferred_element_type=F32)


def _split(a):
    hi = a.astype(BF16)
    lo = (a - hi.astype(F32)).astype(BF16)
    return hi, lo


def _dot3(a, b):
    ah, al = _split(a)
    bh, bl = _split(b)
    return _dot(ah, bh) + _dot(ah, bl) + _dot(al, bh)


def _rms(x, g):
    ms = jnp.mean(x * x, axis=-1, keepdims=True)
    return x * lax.rsqrt(ms + RMS_EPS) * g


def _sigmoid(x):
    return 1.0 / (1.0 + jnp.exp(-x))


def _pack_bf16_pairs(y):
    n = y.shape[1] // 2
    yb = y.astype(BF16).astype(F32)
    lo = lax.bitcast_convert_type(yb[:, :n], jnp.uint32) >> 16
    hi = lax.bitcast_convert_type(yb[:, n:], jnp.uint32) & jnp.uint32(0xFFFF0000)
    return hi | lo


def _unpack_bf16_pairs(w):
    lo = lax.bitcast_convert_type(w << 16, F32)
    hi = lax.bitcast_convert_type(w & jnp.uint32(0xFFFF0000), F32)
    return lo, hi


def _stream_dims(stream):
    batch, t_len, d = stream[0].shape
    return batch, (t_len + stream[1].shape[1] if len(stream) == 2 else t_len), d


def _stream_rows(stream_refs, i, tile):
    if len(stream_refs) == 2:
        ctx_ref, x_ref = stream_refs
        return jnp.where(tile == 0, ctx_ref[i], x_ref[i])
    x_ref, y_ref, mod_ref = stream_refs
    lo, hi = _unpack_bf16_pairs(y_ref[i])
    return x_ref[i] + mod_ref[i, 0, 5:6, :] * jnp.concatenate([lo, hi], axis=1)


def _stream_specs(stream, nb, tile_of):
    d = stream[0].shape[-1]
    if len(stream) == 2:
        return [pl.BlockSpec((nb, TILE, d), lambda b, t: (b, 0, 0)),
                pl.BlockSpec((nb, TILE, d), lambda b, t: (b, jnp.maximum(tile_of(t) - 1, 0), 0))]
    tok = lambda b, t: (b, tile_of(t), 0)
    return [pl.BlockSpec((nb, TILE, d), tok),
            pl.BlockSpec((nb, TILE, d // 2), tok),
            pl.BlockSpec((nb, 1, 8, d), lambda b, t: (b, jnp.minimum(tile_of(t), 1), 0, 0))]


def _adaln_kernel(c_ref, w_ref, b_ref, o_ref):
    cv = c_ref[...]
    s = cv * _sigmoid(cv)
    o_ref[0] = _dot3(s, w_ref[0]) + b_ref[0]


def _adaln(cpad, w_ada, b_ada):
    depth, d, n6 = w_ada.shape
    tn = 1536
    return pl.pallas_call(
        _adaln_kernel,
        grid=(depth, n6 // tn),
        in_specs=[pl.BlockSpec((16, d), lambda l, j: (0, 0)),
                  pl.BlockSpec((1, d, tn), lambda l, j: (l, 0, j)),
                  pl.BlockSpec((1, 1, tn), lambda l, j: (l, 0, j))],
        out_specs=pl.BlockSpec((1, 16, tn), lambda l, j: (l, 0, j)),
        out_shape=jax.ShapeDtypeStruct((depth, 16, n6), F32),
        compiler_params=pltpu.CompilerParams(dimension_semantics=("parallel", "parallel"),
                                             vmem_limit_bytes=VMEM_LIMIT),
        name="adaln",
    )(cpad, w_ada, b_ada.reshape(depth, 1, n6))


def _inproj_kernel(*refs, n_stream):
    for base in range(0, refs[0].shape[0], PAIR):
        _inproj_rows(base, pl.program_id(1), refs[:n_stream], *refs[n_stream:])


def _inproj_rows(base, tile, stream_refs, mod_ref, g_ref, w_ref, cos_ref, sa_ref, sb_ref, gq_ref, gk_ref, mseg_ref,
                 q_ref, k_ref, v_ref):
    members = range(base, base + PAIR)
    h = jnp.concatenate(
        [(_rms(_stream_rows(stream_refs, i, tile), g_ref[...]) * (1.0 + mod_ref[i, 0, 1:2, :])
          + mod_ref[i, 0, 0:1, :]).astype(BF16) for i in members], axis=0)
    cos = jnp.concatenate([cos_ref[...]] * PAIR, axis=0)
    sa = jnp.concatenate([sa_ref[...]] * PAIR, axis=0)
    sb = jnp.concatenate([sb_ref[...]] * PAIR, axis=0)
    mseg = mseg_ref[...]

    def put(ref, cols, val):
        for n, i in enumerate(members):
            ref[i, :, cols] = val[n * TILE:(n + 1) * TILE]

    def rope(b):
        return b * cos + pltpu.roll(b, LANES - 16, 1) * sa + pltpu.roll(b, 16, 1) * sb

    def qknorm(b, g):
        hi, lo = _split(b * b)
        ms = (_dot(hi, mseg) + _dot(lo, mseg)) * (1.0 / HEAD_DIM)
        return b * lax.rsqrt(ms + RMS_EPS) * g

    def wcols(*pieces):
        parts = [w_ref[:, a:a + n] for a, n in pieces]
        return parts[0] if len(parts) == 1 else jnp.concatenate(parts, axis=1)

    lane = lax.broadcasted_iota(jnp.int32, (PAIR * TILE, LANES), 1)

    def regroup(p2):
        h0, h1 = p2[:, :LANES], p2[:, LANES:]
        return (jnp.where(lane < HEAD_DIM, h0, pltpu.roll(h1, HEAD_DIM, 1)),
                jnp.where(lane < HEAD_DIM, pltpu.roll(h0, HEAD_DIM, 1), h1))

    q0, kv0 = 0, Q_COLS
    rope_chunks = [wcols((q0, 256)), wcols((q0 + 256, 256)), wcols((q0 + QA, QB)), wcols((q0 + QA + QB, QC)),
                   wcols((kv0, 256)), wcols((kv0 + 256, 256)),
                   wcols((kv0 + 2 * KA, KB), (kv0 + 2 * KA + 2 * KB, KC))]
    v_chunks = [wcols((kv0 + KA, 256)), wcols((kv0 + KA + 256, 256)),
                wcols((kv0 + 2 * KA + KB, KB), (kv0 + 2 * KA + 2 * KB + KC, KC))]
    for c, wc in enumerate(rope_chunks):
        p2 = _dot(h, wc)
        halves = regroup(p2) if c in (2, 3) else (p2[:, :LANES], p2[:, LANES:])
        for half in range(2):
            j = 2 * c + half
            p = halves[half]
            if j in (4, 5):
                p = qknorm(p, gq_ref[...])
            if j == 12:
                p = qknorm(p, gk_ref[...])
            p = rope(p)
            if j < Q_COLS // LANES:
                put(q_ref, slice(j * LANES, (j + 1) * LANES), (p * (HEAD_DIM ** -0.5)).astype(BF16))
            else:
                jk = j - Q_COLS // LANES
                put(k_ref, slice(jk * LANES, (jk + 1) * LANES), p.astype(BF16))
    for c, wc in enumerate(v_chunks):
        put(v_ref, slice(c * 256, (c + 1) * 256), _dot(h, wc).astype(BF16))


def _inproj(stream, mod, g, w, cos, sa, sb, gq, gk, mseg):
    batch, t_len, d = _stream_dims(stream)
    nt = t_len // TILE
    nb = 2 * PAIR
    row = lambda b, t: (t, 0)
    const = lambda b, t: (0, 0)
    tok = lambda b, t: (b, t, 0)
    return pl.pallas_call(
        functools.partial(_inproj_kernel, n_stream=len(stream)),
        grid=(batch // nb, nt),
        in_specs=_stream_specs(stream, nb, lambda t: t) + [
                  pl.BlockSpec((nb, 1, 8, d), lambda b, t: (b, jnp.minimum(t, 1), 0, 0)),
                  pl.BlockSpec((1, d), const),
                  pl.BlockSpec(w.shape, const),
                  pl.BlockSpec((TILE, LANES), row),
                  pl.BlockSpec((TILE, LANES), row),
                  pl.BlockSpec((TILE, LANES), row),
                  pl.BlockSpec((1, LANES), const),
                  pl.BlockSpec((1, LANES), const),
                  pl.BlockSpec((LANES, LANES), const)],
        out_specs=[pl.BlockSpec((nb, TILE, Q_COLS), tok),
                   pl.BlockSpec((nb, TILE, K_COLS), tok),
                   pl.BlockSpec((nb, TILE, V_COLS), tok)],
        out_shape=[jax.ShapeDtypeStruct((batch, t_len, Q_COLS), BF16),
                   jax.ShapeDtypeStruct((batch, t_len, K_COLS), BF16),
                   jax.ShapeDtypeStruct((batch, t_len, V_COLS), BF16)],
        compiler_params=pltpu.CompilerParams(dimension_semantics=("parallel", "parallel"),
                                             vmem_limit_bytes=VMEM_LIMIT),
        name="inproj",
    )(*stream, mod, g, w, cos, sa, sb, gq, gk, mseg)


def _scores_pass(qq, k_ref, bi, col, starts, s_ref, mask_fn=None, extra=None):
    half = KEY_CHUNK // 2
    rows = qq.shape[0]
    mrun = None
    for j, st in enumerate(starts):
        s = _dot_nt(qq, k_ref[bi, pl.ds(st, KEY_CHUNK), col:col + LANES])
        if mask_fn is not None:
            s = mask_fn(j, st, s)
        s_ref[0:rows, j * KEY_CHUNK:(j + 1) * KEY_CHUNK] = s
        mj = jnp.maximum(s[:, :half], s[:, half:])
        mrun = mj if mrun is None else jnp.maximum(mrun, mj)
    m = jnp.max(mrun, axis=-1, keepdims=True)
    return m if extra is None else jnp.maximum(m, extra)


def _values_pass(m, rows, v_ref, bi, col, starts, s_ref, extra=None):
    half = KEY_CHUNK // 2
    lrun = None
    acc = None
    for j, st in enumerate(starts):
        e = jnp.exp(s_ref[0:rows, j * KEY_CHUNK:(j + 1) * KEY_CHUNK] - m)
        lj = e[:, :half] + e[:, half:]
        lrun = lj if lrun is None else lrun + lj
        pv = _dot(e.astype(BF16), v_ref[bi, pl.ds(st, KEY_CHUNK), col:col + LANES])
        acc = pv if acc is None else acc + pv
    l = jnp.sum(lrun, axis=-1, keepdims=True)
    if extra is not None:
        l = l + jnp.exp(extra - m)
    return acc / l


def _softmax_pv(qq, k_ref, v_ref, bi, col, starts, s_ref, mask_fn=None, extra=None):
    m = _scores_pass(qq, k_ref, bi, col, starts, s_ref, mask_fn, extra)
    return _values_pass(m, qq.shape[0], v_ref, bi, col, starts, s_ref, extra)


def _all_chunks(n_keys):
    return [j * KEY_CHUNK for j in range(n_keys // KEY_CHUNK)]


def _half_masks(shape):
    lane = lax.broadcasted_iota(jnp.int32, shape, 1)
    return lane < HEAD_DIM, lane >= HEAD_DIM


def _stack_group_queries(q2, kv):
    lo_m, hi_m = _half_masks((TILE, LANES))
    m = lo_m if kv == 0 else hi_m
    zero = jnp.zeros((TILE, LANES), q2.dtype)
    return jnp.concatenate([jnp.where(m, q2[:, :LANES], zero), jnp.where(m, q2[:, LANES:], zero)], axis=0)


def _merge_kv_outputs(o_kv0, o_kv1):
    lo_m, _ = _half_masks((TILE, LANES))
    kv0 = jnp.where(lo_m, o_kv0[:TILE], pltpu.roll(o_kv0[TILE:], HEAD_DIM, 1))
    kv1 = jnp.where(lo_m, pltpu.roll(o_kv1[:TILE], HEAD_DIM, 1), o_kv1[TILE:])
    return jnp.concatenate([kv0, kv1], axis=1)


def _attn_kernel(sink_ref, lam_ref, subg_ref, qa_ref, qb_ref, qc_ref, ka_ref, kb_ref, kc_ref, va_ref, vb_ref, vc_ref,
                 o_ref, s_ref, *, t0, ctx_len, lambda_init):
    t = pl.program_id(1) + t0
    t_len = ka_ref.shape[1]
    band = 2 * TILE
    lo_m, hi_m = _half_masks((TILE, LANES))
    zero = jnp.zeros((TILE, LANES), BF16)
    row = lax.broadcasted_iota(jnp.int32, (2 * TILE, 1), 0)
    lv = lam_ref[...]
    lam = (jnp.exp(jnp.sum(lv[0:1] * lv[1:2], keepdims=True))
           - jnp.exp(jnp.sum(lv[2:3] * lv[3:4], keepdims=True)) + lambda_init)

    def sink_col(kv):
        return jnp.where(row < TILE, sink_ref[2 * kv], sink_ref[2 * kv + 1])

    def mixers(starts_ab, starts_c, mask_c):
        n_chain = 0
        for bi in range(qa_ref.shape[0]):
            for h in range(QA // LANES):
                q = qa_ref[bi, :, h * LANES:(h + 1) * LANES]
                qq = jnp.concatenate([jnp.where(lo_m, q, zero), jnp.where(hi_m, q, zero)], axis=0)
                o = _softmax_pv(qq, ka_ref, va_ref, bi, h * LANES, starts_ab, s_ref.at[n_chain % 2])
                n_chain += 1
                o = o[:TILE] - lam * o[TILE:]
                o_ref[bi, :, h * LANES:(h + 1) * LANES] = (
                    _rms(o, subg_ref[...]) * (1.0 - lambda_init)).astype(BF16)
            outs = []
            for kv in range(2):
                outs.append(_softmax_pv(_stack_group_queries(qb_ref[bi], kv), kb_ref, vb_ref, bi, 0, starts_ab,
                                        s_ref.at[n_chain % 2]))
                n_chain += 1
            o_ref[bi, :, QA:QA + QB] = _merge_kv_outputs(*outs).astype(BF16)
            outs = []
            for kv in range(2):
                outs.append(_softmax_pv(_stack_group_queries(qc_ref[bi], kv), kc_ref, vc_ref, bi, 0, starts_c,
                                        s_ref.at[n_chain % 2], mask_c, sink_col(kv)))
                n_chain += 1
            o_ref[bi, :, QA + QB:] = _merge_kv_outputs(*outs).astype(BF16)

    def ctx_step():
        mixers([0], [0], None)

    def latent_step():
        q_start = (t - 1) * TILE
        u0 = pl.multiple_of(jnp.minimum(q_start + ctx_len - WINDOW, t_len - band), WINDOW)
        q_pos = q_start + (lax.broadcasted_iota(jnp.int32, (2 * TILE, KEY_CHUNK), 0) & (TILE - 1))
        key_lane = lax.broadcasted_iota(jnp.int32, (2 * TILE, KEY_CHUNK), 1)

        def mask_c(j, st, s):
            if j == 0:
                return s
            k_pos = (st - ctx_len) + key_lane
            valid = (jnp.abs(k_pos - q_pos) <= WINDOW) & (k_pos >= 0)
            return jnp.where(valid, s, NEG_BIG)

        mixers(_all_chunks(t_len), [0] + [u0 + i * KEY_CHUNK for i in range(band // KEY_CHUNK)], mask_c)

    if t0 == 0:
        pl.when(t == 0)(ctx_step)
        pl.when(t > 0)(latent_step)
    else:
        latent_step()


def _attention(sink, lamv, subg, q, k, v, t0, lambda_init):
    batch, t_len, _ = q.shape
    nt = t_len // TILE
    kern = functools.partial(_attn_kernel, t0=t0, ctx_len=TILE, lambda_init=lambda_init)
    const = lambda b, t: (0, 0)
    nb = 1
    qspec = lambda width, blk: pl.BlockSpec((nb, TILE, width), lambda b, t: (b, t + t0, blk))
    kvspec = lambda width, blk: pl.BlockSpec((nb, t_len, width), lambda b, t: (b, 0, blk))
    return pl.pallas_call(
        kern,
        grid=(batch // nb, nt - t0),
        in_specs=[pl.BlockSpec(memory_space=pltpu.SMEM),
                  pl.BlockSpec(lamv.shape, const),
                  pl.BlockSpec((1, LANES), const),
                  qspec(QA, 0), qspec(QB, QA // QB), qspec(QC, (QA + QB) // QC),
                  kvspec(KA, 0), kvspec(KB, KA // KB), kvspec(KC, (KA + KB) // KC),
                  kvspec(KA, 0), kvspec(KB, KA // KB), kvspec(KC, (KA + KB) // KC)],
        out_specs=pl.BlockSpec((nb, TILE, Q_COLS), lambda b, t: (b, t + t0, 0)),
        out_shape=jax.ShapeDtypeStruct((batch, t_len, Q_COLS), BF16),
        scratch_shapes=[pltpu.VMEM((2, 2 * TILE, t_len), F32)],
        compiler_params=pltpu.CompilerParams(dimension_semantics=("parallel", "parallel"),
                                             vmem_limit_bytes=VMEM_LIMIT),
        name="attention",
    )(sink, lamv, subg, q, q, q, k, k, k, v, v, v)


def _route(scores, bias):
    lane = lax.broadcasted_iota(jnp.int32, scores.shape, 1)
    sel = scores + bias
    in_group = lane & (EXPERTS_PER_GROUP - 1)
    group = lane >> 2

    def neighbours(x, idx, step, span):
        for d in (1, 2, 3):
            fwd = (idx + d) < 4
            y = jnp.where(fwd, pltpu.roll(x, LANES - d * step, 1), pltpu.roll(x, span - d * step, 1))
            yield y, fwd

    rank = jnp.zeros_like(sel)
    for y, fwd in neighbours(sel, in_group, 1, EXPERTS_PER_GROUP):
        beats = (y > sel) | ((y == sel) & jnp.logical_not(fwd))
        rank = rank + jnp.where(beats, 1.0, 0.0)
    top2 = jnp.where(rank < 2.0, sel, 0.0)
    gsum = top2
    for y, _ in neighbours(top2, in_group, 1, EXPERTS_PER_GROUP):
        gsum = gsum + y
    grank = jnp.zeros_like(sel)
    for y, fwd in neighbours(gsum, group, EXPERTS_PER_GROUP, N_EXPERTS):
        beats = (y > gsum) | ((y == gsum) & jnp.logical_not(fwd))
        grank = grank + jnp.where(beats, 1.0, 0.0)
    best = (grank < 0.5) & (lane < N_EXPERTS)
    chosen = best & (rank < 2.0)
    w = jnp.where(chosen, scores, 0.0)
    gates = w / jnp.sum(w, axis=-1, keepdims=True)
    gid = jnp.sum(jnp.where(best & (in_group == 0), group.astype(F32), 0.0), axis=-1, keepdims=True)
    return jnp.where(lane == GROUP_LANE, gid, gates)


def _outproj_kernel(*refs, n_stream, t0):
    stream_refs = refs[:n_stream]
    tile = pl.program_id(1) + t0
    mix_ref, w_ref, mod_ref, g_ref, wrh_ref, wrl_ref, rb_ref, x1_ref, h2_ref, gates_ref = refs[n_stream:]
    for i in range(mix_ref.shape[0]):
        x1 = _stream_rows(stream_refs, i, tile) + mod_ref[i, 0, 2:3, :] * _dot(mix_ref[i], w_ref[...])
        x1_ref[i] = x1
        h2 = _rms(x1, g_ref[...]) * (1.0 + mod_ref[i, 0, 4:5, :]) + mod_ref[i, 0, 3:4, :]
        h2_ref[i] = _pack_bf16_pairs(h2)
        hi, lo = _split(h2)
        logits = _dot(hi, wrh_ref[...]) + _dot(hi, wrl_ref[...]) + _dot(lo, wrh_ref[...])
        gates_ref[i] = _route(_sigmoid(logits), rb_ref[...])


def _outproj(stream, mix, w, mod, g, wrh, wrl, rb, t0):
    batch, t_len, d = _stream_dims(stream)
    nt = t_len // TILE
    nb = 2 * PAIR
    t_out = (nt - t0) * TILE
    const = lambda b, t: (0, 0)
    tok_in = lambda b, t: (b, t + t0, 0)
    tok_out = lambda b, t: (b, t, 0)
    return pl.pallas_call(
        functools.partial(_outproj_kernel, n_stream=len(stream), t0=t0),
        grid=(batch // nb, nt - t0),
        in_specs=_stream_specs(stream, nb, lambda t: t + t0) + [
                  pl.BlockSpec((nb, TILE, Q_COLS), tok_in),
                  pl.BlockSpec(w.shape, const),
                  pl.BlockSpec((nb, 1, 8, d), lambda b, t: (b, jnp.minimum(t + t0, 1), 0, 0)),
                  pl.BlockSpec((1, d), const),
                  pl.BlockSpec((d, LANES), const),
                  pl.BlockSpec((d, LANES), const),
                  pl.BlockSpec((1, LANES), const)],
        out_specs=[pl.BlockSpec((nb, TILE, d), tok_out),
                   pl.BlockSpec((nb, TILE, d // 2), tok_out),
                   pl.BlockSpec((nb, TILE, LANES), tok_out)],
        out_shape=[jax.ShapeDtypeStruct((batch, t_out, d), F32),
                   jax.ShapeDtypeStruct((batch, t_out, d // 2), jnp.uint32),
                   jax.ShapeDtypeStruct((batch, t_out, LANES), F32)],
        compiler_params=pltpu.CompilerParams(dimension_semantics=("parallel", "parallel"),
                                             vmem_limit_bytes=VMEM_LIMIT),
        name="outproj",
    )(*stream, mix, w, mod, g, wrh, wrl, rb)


def _sc_workers():
    info = plsc.get_sparse_core_info()
    return info.num_cores, info.num_subcores


def _sc_dispatch(hpk, gates, pos2d, n_out):
    n, w = hpk.shape
    gw = gates.shape[1]
    nc, ns = _sc_workers()
    steps = n // (nc * ns * SC_WIN)
    mesh = plsc.VectorSubcoreMesh(core_axis_name="c", subcore_axis_name="s")

    @functools.partial(
        pl.kernel, mesh=mesh,
        out_type=(jax.ShapeDtypeStruct((n_out, w), hpk.dtype), jax.ShapeDtypeStruct((n_out, gw), gates.dtype)),
        scratch_types=[pltpu.VMEM((2, 1, SC_WIN), jnp.int32), pltpu.VMEM((2, SC_WIN, w), hpk.dtype),
                       pltpu.VMEM((2, SC_WIN, gw), gates.dtype),
                       pltpu.SemaphoreType.DMA, pltpu.SemaphoreType.DMA, pltpu.SemaphoreType.DMA],
        name="sc_dispatch")
    def k(h_hbm, g_hbm, pos_hbm, hs_hbm, gs_hbm, idx_v, rows_v, grow_v, sem_in0, sem_in1, sem_out):
        wid = lax.axis_index("s") * nc + lax.axis_index("c")
        sem_in = (sem_in0, sem_in1)

        def start_loads(j):
            slot = j % 2
            blk = wid * steps + j
            off = pl.multiple_of(blk * SC_WIN, SC_WIN)
            return [pltpu.async_copy(pos_hbm.at[pl.ds(blk, 1)], idx_v.at[slot], sem_in[slot]),
                    pltpu.async_copy(h_hbm.at[pl.ds(off, SC_WIN)], rows_v.at[slot], sem_in[slot]),
                    pltpu.async_copy(g_hbm.at[pl.ds(off, SC_WIN)], grow_v.at[slot], sem_in[slot])]

        loads = start_loads(0)
        for j in range(steps):
            slot = j % 2
            for cp in loads:
                cp.wait()
            if j + 1 < steps:
                loads = start_loads(j + 1)
            idx = idx_v.at[slot].at[0]
            stores = [pltpu.async_copy(rows_v.at[slot], hs_hbm.at[idx], sem_out),
                      pltpu.async_copy(grow_v.at[slot], gs_hbm.at[idx], sem_out)]
            for cp in stores:
                cp.wait()

    return k(hpk, gates, pos2d)


def _sc_combine(ys, pos2d, n):
    w = ys.shape[1]
    nc, ns = _sc_workers()
    steps = n // (nc * ns * SC_WIN)
    mesh = plsc.VectorSubcoreMesh(core_axis_name="c", subcore_axis_name="s")

    @functools.partial(
        pl.kernel, mesh=mesh,
        out_type=jax.ShapeDtypeStruct((n, w), ys.dtype),
        scratch_types=[pltpu.VMEM((2, 1, SC_WIN), jnp.int32), pltpu.VMEM((2, SC_WIN, w), ys.dtype),
                       pltpu.SemaphoreType.DMA, pltpu.SemaphoreType.DMA],
        name="sc_combine")
    def k(y_hbm, pos_hbm, o_hbm, idx_v, rows_v, sem0, sem1):
        wid = lax.axis_index("s") * nc + lax.axis_index("c")
        sem = (sem0, sem1)

        def start_gather(j):
            slot = j % 2
            pltpu.sync_copy(pos_hbm.at[pl.ds(wid * steps + j, 1)], idx_v.at[slot])
            return pltpu.async_copy(y_hbm.at[idx_v.at[slot].at[0]], rows_v.at[slot], sem[slot])

        gather = start_gather(0)
        for j in range(steps):
            slot = j % 2
            gather.wait()
            if j + 1 < steps:
                gather = start_gather(j + 1)
            off = pl.multiple_of((wid * steps + j) * SC_WIN, SC_WIN)
            pltpu.sync_copy(rows_v.at[slot], o_hbm.at[pl.ds(off, SC_WIN)])

    return k(ys, pos2d)


PLAN_ROWS = 256
PLAN_BLOCK = 2048


def _plan_kernel(gates_ref, pos_ref, tg_ref, tu_ref, g_ref):
    step = pl.program_id(0)

    @pl.when(step == 0)
    def _():
        g_ref[...] = jnp.full(g_ref.shape, -1.0, F32)

    gates = gates_ref[...]
    lane = lax.broadcasted_iota(jnp.int32, gates.shape, 1)
    along_lanes = _dot_nt(jnp.ones((8, LANES), BF16), jnp.where(lane == GROUP_LANE, gates, 0.0).astype(BF16))
    for r in range(PLAN_BLOCK // LANES):
        g_ref[pl.ds(step * (PLAN_BLOCK // LANES) + r, 1), :] = along_lanes[0:1, r * LANES:(r + 1) * LANES]

    pl.when(step == pl.num_programs(0) - 1)(functools.partial(_plan_finish, g_ref, pos_ref, tg_ref, tu_ref))


def _plan_finish(g_ref, pos_ref, tg_ref, tu_ref):
    g = g_ref[...]
    rows = g.shape[0]
    n_groups = N_EXPERTS // EXPERTS_PER_GROUP
    before_lane = (lax.broadcasted_iota(jnp.int32, (LANES, LANES), 0)
                   < lax.broadcasted_iota(jnp.int32, (LANES, LANES), 1)).astype(BF16)
    before_row = (lax.broadcasted_iota(jnp.int32, (rows, rows), 1)
                  < lax.broadcasted_iota(jnp.int32, (rows, rows), 0)).astype(BF16)
    tile_row = lax.broadcasted_iota(jnp.int32, (1, LANES), 1).astype(F32) * FFN_TILE
    pos = jnp.zeros_like(g)
    start = jnp.zeros((1, 1), F32)
    tile_group = jnp.zeros((1, LANES), F32)
    used_end = []
    for grp in range(n_groups):
        m = jnp.where(g == grp, 1.0, 0.0)
        in_row = _dot(m.astype(BF16), before_lane)
        row_total = jnp.sum(m, axis=1, keepdims=True)
        rows_before = _dot(before_row, jnp.broadcast_to(row_total, m.shape).astype(BF16))
        pos = pos + m * (start + in_row + rows_before)
        count = jnp.sum(row_total, axis=0, keepdims=True)
        used_end.append(start + count)
        start = start + jnp.floor((count + (FFN_TILE - 1)) * (1.0 / FFN_TILE)) * FFN_TILE
        if grp < n_groups - 1:
            tile_group = tile_group + jnp.where(tile_row >= start, 1.0, 0.0)
    pos_ref[...] = pos.astype(jnp.int32)
    tg_ref[...] = tile_group.astype(jnp.int32)
    end_of_tile_group = sum(jnp.where(tile_group == grp, used_end[grp], 0.0) for grp in range(n_groups))
    tu_ref[...] = jnp.where(tile_row < end_of_tile_group, 1, 0).astype(jnp.int32)


def _bucket_plan(gates, n_pad):
    n = gates.shape[0]
    whole = lambda shape: pl.BlockSpec(shape, lambda i: (0, 0))
    pos, tile_group, tile_used = pl.pallas_call(
        _plan_kernel,
        grid=(n // PLAN_BLOCK,),
        in_specs=[pl.BlockSpec((PLAN_BLOCK, LANES), lambda i: (i, 0))],
        out_specs=[whole((PLAN_ROWS, LANES)), whole((1, LANES)), whole((1, LANES))],
        out_shape=[jax.ShapeDtypeStruct((PLAN_ROWS, LANES), jnp.int32),
                   jax.ShapeDtypeStruct((1, LANES), jnp.int32),
                   jax.ShapeDtypeStruct((1, LANES), jnp.int32)],
        scratch_shapes=[pltpu.VMEM((PLAN_ROWS, LANES), F32)],
        compiler_params=pltpu.CompilerParams(dimension_semantics=("arbitrary",)),
        name="plan",
    )(gates)
    n_tiles = n_pad // FFN_TILE
    return pos[:n // LANES].reshape(n // SC_WIN, SC_WIN), tile_group[0, :n_tiles], tile_used[0, :n_tiles]


def _ffn_kernel(tg_ref, tu_ref, hs_ref, gs_ref, wg_ref, wu_ref, wd_ref, ys_ref, wgb_ref, wub_ref, wdb_ref):
    i = pl.program_id(0)

    @pl.when((i == 0) | (tg_ref[i] != tg_ref[jnp.maximum(i - 1, 0)]))
    def _():
        wgb_ref[...] = wg_ref[0].astype(BF16)
        wub_ref[...] = wu_ref[0].astype(BF16)
        wdb_ref[...] = wd_ref[0].astype(BF16)

    @pl.when(tu_ref[i] > 0)
    def _():
        lo, hi = _unpack_bf16_pairs(hs_ref[...])
        lo = lo.astype(BF16)
        hi = hi.astype(BF16)
        half = lo.shape[1]
        gs = gs_ref[...]
        lane = lax.broadcasted_iota(jnp.int32, gs.shape, 1)
        first = tg_ref[i] * EXPERTS_PER_GROUP
        y = None
        for e in range(EXPERTS_PER_GROUP):
            a = _dot(lo, wgb_ref[e, :half, :]) + _dot(hi, wgb_ref[e, half:, :])
            u = _dot(lo, wub_ref[e, :half, :]) + _dot(hi, wub_ref[e, half:, :])
            gate = jnp.sum(jnp.where(lane == first + e, gs, 0.0), axis=-1, keepdims=True)
            part = _dot(((a * _sigmoid(a)) * u * gate).astype(BF16), wdb_ref[e])
            y = part if y is None else y + part
        ys_ref[...] = _pack_bf16_pairs(y)


def _ffn(tile_group, tile_used, hs, gs, wg, wu, wd, layer):
    n_pad, half = hs.shape
    grp = lambda i, tg, tu: (layer, tg[i], 0, 0)
    gshape = lambda w: (EXPERTS_PER_GROUP,) + w.shape[2:]
    wspec = lambda w: pl.BlockSpec((1,) + gshape(w), grp)
    return pl.pallas_call(
        _ffn_kernel,
        grid_spec=pltpu.PrefetchScalarGridSpec(
            num_scalar_prefetch=2,
            grid=(n_pad // FFN_TILE,),
            in_specs=[pl.BlockSpec((FFN_TILE, half), lambda i, tg, tu: (i, 0)),
                      pl.BlockSpec((FFN_TILE, LANES), lambda i, tg, tu: (i, 0)),
                      wspec(wg), wspec(wu), wspec(wd)],
            out_specs=pl.BlockSpec((FFN_TILE, half), lambda i, tg, tu: (i, 0)),
            scratch_shapes=[pltpu.VMEM(gshape(wg), BF16), pltpu.VMEM(gshape(wu), BF16),
                            pltpu.VMEM(gshape(wd), BF16)]),
        out_shape=jax.ShapeDtypeStruct((n_pad, half), jnp.uint32),
        compiler_params=pltpu.CompilerParams(dimension_semantics=("arbitrary",), vmem_limit_bytes=VMEM_LIMIT),
        name="ffn",
    )(tile_group, tile_used, hs, gs, wg, wu, wd)


def _residual_kernel(x_ref, y_ref, mod_ref, fg_ref, o_ref, *, final_norm):
    for i in range(x_ref.shape[0]):
        lo, hi = _unpack_bf16_pairs(y_ref[i])
        y = x_ref[i] + mod_ref[i, 0, 5:6, :] * jnp.concatenate([lo, hi], axis=1)
        if final_norm:
            y = _rms(y, fg_ref[...])
        o_ref[i] = y


def _residual(x1, yg, mod, fg, t0, final_norm):
    batch, t_out, d = x1.shape
    nb = PAIR
    tok = lambda b, t: (b, t, 0)
    return pl.pallas_call(
        functools.partial(_residual_kernel, final_norm=final_norm),
        grid=(batch // nb, t_out // TILE),
        in_specs=[pl.BlockSpec((nb, TILE, d), tok),
                  pl.BlockSpec((nb, TILE, d // 2), tok),
                  pl.BlockSpec((nb, 1, 8, d), lambda b, t: (b, jnp.minimum(t + t0, 1), 0, 0)),
                  pl.BlockSpec((1, d), lambda b, t: (0, 0))],
        out_specs=pl.BlockSpec((nb, TILE, d), tok),
        out_shape=jax.ShapeDtypeStruct((batch, t_out, d), F32),
        compiler_params=pltpu.CompilerParams(dimension_semantics=("parallel", "parallel"),
                                             vmem_limit_bytes=VMEM_LIMIT),
        name="residual",
    )(x1, yg, mod, fg)


def _moe(h2, gates, wg, wu, wd, layer):
    batch, t_out, half = h2.shape
    n = batch * t_out
    n_pad = n + (N_EXPERTS // EXPERTS_PER_GROUP) * FFN_TILE
    assert n % PLAN_BLOCK == 0 and n // LANES <= PLAN_ROWS and n_pad // FFN_TILE <= LANES
    gates = gates.reshape(n, LANES)
    pos2d, tile_group, tile_used = _bucket_plan(gates, n_pad)
    hs, gs = _sc_dispatch(h2.reshape(n, half), gates, pos2d, n_pad)
    ys = _ffn(tile_group, tile_used, hs, gs, wg, wu, wd, layer)
    return _sc_combine(ys, pos2d, n).reshape(batch, t_out, half)


def _rope_tables(seq, ctx_len):
    rows = seq // GRID_W
    row_pos = jnp.repeat(jnp.arange(rows, dtype=F32), GRID_W)
    col_pos = jnp.tile(jnp.arange(GRID_W, dtype=F32), rows)
    axis_dim = HEAD_DIM // 2
    inv_freq = ROPE_BASE ** (-jnp.arange(0, axis_dim, 2, dtype=F32) / axis_dim)
    ang_r = row_pos[:, None] * inv_freq[None, :]
    ang_c = col_pos[:, None] * inv_freq[None, :]
    z = jnp.zeros_like(ang_r)
    cos = jnp.concatenate([jnp.cos(ang_r)] * 2 + [jnp.cos(ang_c)] * 2, axis=-1)
    sin_a = jnp.concatenate([-jnp.sin(ang_r), z, -jnp.sin(ang_c), z], axis=-1)
    sin_b = jnp.concatenate([z, jnp.sin(ang_r), z, jnp.sin(ang_c)], axis=-1)

    def full(tab, fill):
        tab = jnp.tile(tab, (1, LANES // HEAD_DIM))
        return jnp.concatenate([jnp.full((ctx_len, LANES), fill, F32), tab], axis=0)

    return full(cos, 1.0), full(sin_a, 0.0), full(sin_b, 0.0)


def kernel(x, c, ctx, c_ctx, w_ada, b_ada, norm1_g, norm2_g, w_in, w_out, lam_q1, lam_k1, lam_q2, lam_k2,
           subln_g, q_norm_g, k_norm_g, sink, w_router, router_bias, w_gate, w_up, w_down, final_g):
    batch, seq, d = x.shape
    ctx_len = ctx.shape[1]
    depth = w_in.shape[0]
    sc_rows = SC_WIN * math.prod(_sc_workers())
    assert ctx_len == TILE and seq % TILE == 0 and batch % PAIR == 0 and batch <= 15
    assert (batch * seq) % sc_rows == 0 and (batch * (seq + ctx_len)) % sc_rows == 0

    cpad = jnp.zeros((16, d), F32).at[:batch].set(c).at[batch].set(c_ctx)
    mod_all = _adaln(cpad, w_ada, b_ada)
    cos, sin_a, sin_b = _rope_tables(seq, ctx_len)
    mseg = jnp.asarray(np.kron(np.eye(2), np.ones((HEAD_DIM, HEAD_DIM))), BF16)
    wr = jnp.zeros((d, LANES), F32).at[:, :N_EXPERTS].set(w_router)
    wrh = wr.astype(BF16)
    wrl = (wr - wrh.astype(F32)).astype(BF16)
    rb = jnp.zeros((1, LANES), F32).at[0, :N_EXPERTS].set(router_bias)

    stream = (ctx, x)

    for l in range(depth):
        last = l == depth - 1
        lambda_init = 0.8 - 0.6 * math.exp(-0.3 * l)
        m6 = mod_all[l].reshape(16, 6, d)
        m8 = jnp.concatenate([m6, jnp.zeros((16, 2, d), F32)], axis=1)
        mod = jnp.stack([jnp.broadcast_to(m8[batch], (batch, 8, d)), m8[:batch]], axis=1)

        w_in_p = w_in[l].astype(BF16)
        w_out_p = w_out[l].astype(BF16)
        gq = jnp.tile(q_norm_g[l], 2).reshape(1, LANES)
        gk = jnp.tile(k_norm_g[l], 2).reshape(1, LANES)
        q, k, v = _inproj(stream, mod, norm1_g[l].reshape(1, d), w_in_p, cos, sin_a, sin_b, gq, gk, mseg)

        t0 = 1 if last else 0
        lamv = jnp.stack([lam_q1[l], lam_k1[l], lam_q2[l], lam_k2[l]])
        mix = _attention(sink[l], lamv, subln_g[l].reshape(1, LANES), q, k, v, t0, lambda_init)
        x1, h2, gates = _outproj(stream, mix, w_out_p, mod, norm2_g[l].reshape(1, d), wrh, wrl, rb, t0)
        stream = (x1, _moe(h2, gates, w_gate, w_up, w_down, l), mod)
    return _residual(*stream, final_g.reshape(1, d), 1, final_norm=True)
```

```python
import functools
import math

import numpy as np
import jax
import jax.numpy as jnp
from jax import lax
from jax.experimental import pallas as pl
from jax.experimental.pallas import tpu as pltpu
from jax.experimental.pallas import tpu_sc as plsc

F32 = jnp.float32
BF16 = jnp.bfloat16

HEAD_DIM = 64
GRID_W = 64
ROPE_BASE = 10000.0
RMS_EPS = 1e-6
WINDOW = 128
N_EXPERTS = 16
EXPERTS_PER_GROUP = 4
LANES = 128
TILE = 256
PAIR = 2
FFN_TILE = 512
SC_WIN = 64
GROUP_LANE = 16
NEG_BIG = -1e30
KEY_CHUNK = 256
VMEM_LIMIT = 52 * 1024 * 1024

QA, QB, QC = 512, 256, 256
KA, KB, KC = 512, 128, 128
Q_COLS = QA + QB + QC
K_COLS = KA + KB + KC
V_COLS = 768
ROPE_COLS = Q_COLS + K_COLS


def _dot(a, b):
    return jnp.dot(a, b, preferred_element_type=F32)


def _dot_nt(a, b):
    return lax.dot_general(a, b, (((1,), (1,)), ((), ())), preferred_element_type=F32)


def _split(a):
    hi = a.astype(BF16)
    lo = (a - hi.astype(F32)).astype(BF16)
    return hi, lo


def _dot3(a, b):
    ah, al = _split(a)
    bh, bl = _split(b)
    return _dot(ah, bh) + _dot(ah, bl) + _dot(al, bh)


def _rms(x, g):
    ms = jnp.mean(x * x, axis=-1, keepdims=True)
    return x * lax.rsqrt(ms + RMS_EPS) * g


def _sigmoid(x):
    return 1.0 / (1.0 + jnp.exp(-x))


def _pack_bf16_pairs(y):
    n = y.shape[1] // 2
    yb = y.astype(BF16).astype(F32)
    lo = lax.bitcast_convert_type(yb[:, :n], jnp.uint32) >> 16
    hi = lax.bitcast_convert_type(yb[:, n:], jnp.uint32) & jnp.uint32(0xFFFF0000)
    return hi | lo


def _unpack_bf16_pairs(w):
    lo = lax.bitcast_convert_type(w << 16, F32)
    hi = lax.bitcast_convert_type(w & jnp.uint32(0xFFFF0000), F32)
    return lo, hi


def _stream_dims(stream):
    batch, t_len, d = stream[0].shape
    return batch, (t_len + stream[1].shape[1] if len(stream) == 2 else t_len), d


def _stream_rows(stream_refs, i, tile):
    if len(stream_refs) == 2:
        ctx_ref, x_ref = stream_refs
        return jnp.where(tile == 0, ctx_ref[i], x_ref[i])
    x_ref, y_ref, mod_ref = stream_refs
    lo, hi = _unpack_bf16_pairs(y_ref[i])
    return x_ref[i] + mod_ref[i, 0, 5:6, :] * jnp.concatenate([lo, hi], axis=1)


def _stream_specs(stream, nb, tile_of):
    d = stream[0].shape[-1]
    if len(stream) == 2:
        return [pl.BlockSpec((nb, TILE, d), lambda b, t: (b, 0, 0)),
                pl.BlockSpec((nb, TILE, d), lambda b, t: (b, jnp.maximum(tile_of(t) - 1, 0), 0))]
    tok = lambda b, t: (b, tile_of(t), 0)
    return [pl.BlockSpec((nb, TILE, d), tok),
            pl.BlockSpec((nb, TILE, d // 2), tok),
            pl.BlockSpec((nb, 1, 8, d), lambda b, t: (b, jnp.minimum(tile_of(t), 1), 0, 0))]


def _adaln_kernel(c_ref, w_ref, b_ref, o_ref):
    cv = c_ref[...]
    s = cv * _sigmoid(cv)
    o_ref[0] = _dot3(s, w_ref[0]) + b_ref[0]


def _adaln(cpad, w_ada, b_ada):
    depth, d, n6 = w_ada.shape
    tn = 1536
    return pl.pallas_call(
        _adaln_kernel,
        grid=(depth, n6 // tn),
        in_specs=[pl.BlockSpec((16, d), lambda l, j: (0, 0)),
                  pl.BlockSpec((1, d, tn), lambda l, j: (l, 0, j)),
                  pl.BlockSpec((1, 1, tn), lambda l, j: (l, 0, j))],
        out_specs=pl.BlockSpec((1, 16, tn), lambda l, j: (l, 0, j)),
        out_shape=jax.ShapeDtypeStruct((depth, 16, n6), F32),
        compiler_params=pltpu.CompilerParams(dimension_semantics=("parallel", "parallel"),
                                             vmem_limit_bytes=VMEM_LIMIT),
        name="adaln",
    )(cpad, w_ada, b_ada.reshape(depth, 1, n6))


def _inproj_kernel(*refs, n_stream):
    for base in range(0, refs[0].shape[0], PAIR):
        _inproj_rows(base, pl.program_id(1), refs[:n_stream], *refs[n_stream:])


def _inproj_rows(base, tile, stream_refs, mod_ref, g_ref, w_ref, cos_ref, sa_ref, sb_ref, gq_ref, gk_ref, mseg_ref,
                 q_ref, k_ref, v_ref):
    members = range(base, base + PAIR)
    h = jnp.concatenate(
        [(_rms(_stream_rows(stream_refs, i, tile), g_ref[...]) * (1.0 + mod_ref[i, 0, 1:2, :])
          + mod_ref[i, 0, 0:1, :]).astype(BF16) for i in members], axis=0)
    cos = jnp.concatenate([cos_ref[...]] * PAIR, axis=0)
    sa = jnp.concatenate([sa_ref[...]] * PAIR, axis=0)
    sb = jnp.concatenate([sb_ref[...]] * PAIR, axis=0)
    mseg = mseg_ref[...]

    def put(ref, cols, val):
        for n, i in enumerate(members):
            ref[i, :, cols] = val[n * TILE:(n + 1) * TILE]

    def rope(b):
        return b * cos + pltpu.roll(b, LANES - 16, 1) * sa + pltpu.roll(b, 16, 1) * sb

    def qknorm(b, g):
        hi, lo = _split(b * b)
        ms = (_dot(hi, mseg) + _dot(lo, mseg)) * (1.0 / HEAD_DIM)
        return b * lax.rsqrt(ms + RMS_EPS) * g

    def wcols(*pieces):
        parts = [w_ref[:, a:a + n] for a, n in pieces]
        return parts[0] if len(parts) == 1 else jnp.concatenate(parts, axis=1)

    lane = lax.broadcasted_iota(jnp.int32, (PAIR * TILE, LANES), 1)

    def regroup(p2):
        h0, h1 = p2[:, :LANES], p2[:, LANES:]
        return (jnp.where(lane < HEAD_DIM, h0, pltpu.roll(h1, HEAD_DIM, 1)),
                jnp.where(lane < HEAD_DIM, pltpu.roll(h0, HEAD_DIM, 1), h1))

    q0, kv0 = 0, Q_COLS
    rope_chunks = [wcols((q0, 256)), wcols((q0 + 256, 256)), wcols((q0 + QA, QB)), wcols((q0 + QA + QB, QC)),
                   wcols((kv0, 256)), wcols((kv0 + 256, 256)),
                   wcols((kv0 + 2 * KA, KB), (kv0 + 2 * KA + 2 * KB, KC))]
    v_chunks = [wcols((kv0 + KA, 256)), wcols((kv0 + KA + 256, 256)),
                wcols((kv0 + 2 * KA + KB, KB), (kv0 + 2 * KA + 2 * KB + KC, KC))]
    for c, wc in enumerate(rope_chunks):
        p2 = _dot(h, wc)
        halves = regroup(p2) if c in (2, 3) else (p2[:, :LANES], p2[:, LANES:])
        for half in range(2):
            j = 2 * c + half
            p = halves[half]
            if j in (4, 5):
                p = qknorm(p, gq_ref[...])
            if j == 12:
                p = qknorm(p, gk_ref[...])
            p = rope(p)
            if j < Q_COLS // LANES:
                put(q_ref, slice(j * LANES, (j + 1) * LANES), (p * (HEAD_DIM ** -0.5)).astype(BF16))
            else:
                jk = j - Q_COLS // LANES
                put(k_ref, slice(jk * LANES, (jk + 1) * LANES), p.astype(BF16))
    for c, wc in enumerate(v_chunks):
        put(v_ref, slice(c * 256, (c + 1) * 256), _dot(h, wc).astype(BF16))


def _inproj(stream, mod, g, w, cos, sa, sb, gq, gk, mseg):
    batch, t_len, d = _stream_dims(stream)
    nt = t_len // TILE
    nb = 2 * PAIR
    row = lambda b, t: (t, 0)
    const = lambda b, t: (0, 0)
    tok = lambda b, t: (b, t, 0)
    return pl.pallas_call(
        functools.partial(_inproj_kernel, n_stream=len(stream)),
        grid=(batch // nb, nt),
        in_specs=_stream_specs(stream, nb, lambda t: t) + [
                  pl.BlockSpec((nb, 1, 8, d), lambda b, t: (b, jnp.minimum(t, 1), 0, 0)),
                  pl.BlockSpec((1, d), const),
                  pl.BlockSpec(w.shape, const),
                  pl.BlockSpec((TILE, LANES), row),
                  pl.BlockSpec((TILE, LANES), row),
                  pl.BlockSpec((TILE, LANES), row),
                  pl.BlockSpec((1, LANES), const),
                  pl.BlockSpec((1, LANES), const),
                  pl.BlockSpec((LANES, LANES), const)],
        out_specs=[pl.BlockSpec((nb, TILE, Q_COLS), tok),
                   pl.BlockSpec((nb, TILE, K_COLS), tok),
                   pl.BlockSpec((nb, TILE, V_COLS), tok)],
        out_shape=[jax.ShapeDtypeStruct((batch, t_len, Q_COLS), BF16),
                   jax.ShapeDtypeStruct((batch, t_len, K_COLS), BF16),
                   jax.ShapeDtypeStruct((batch, t_len, V_COLS), BF16)],
        compiler_params=pltpu.CompilerParams(dimension_semantics=("parallel", "parallel"),
                                             vmem_limit_bytes=VMEM_LIMIT),
        name="inproj",
    )(*stream, mod, g, w, cos, sa, sb, gq, gk, mseg)


def _scores_pass(qq, k_ref, bi, col, starts, s_ref, mask_fn=None, extra=None):
    half = KEY_CHUNK // 2
    rows = qq.shape[0]
    mrun = None
    for j, st in enumerate(starts):
        s = _dot_nt(qq, k_ref[bi, pl.ds(st, KEY_CHUNK), col:col + LANES])
        if mask_fn is not None:
            s = mask_fn(j, st, s)
        s_ref[0:rows, j * KEY_CHUNK:(j + 1) * KEY_CHUNK] = s
        mj = jnp.maximum(s[:, :half], s[:, half:])
        mrun = mj if mrun is None else jnp.maximum(mrun, mj)
    m = jnp.max(mrun, axis=-1, keepdims=True)
    return m if extra is None else jnp.maximum(m, extra)


def _values_pass(m, rows, v_ref, bi, col, starts, s_ref, extra=None):
    half = KEY_CHUNK // 2
    lrun = None
    acc = None
    for j, st in enumerate(starts):
        e = jnp.exp(s_ref[0:rows, j * KEY_CHUNK:(j + 1) * KEY_CHUNK] - m)
        lj = e[:, :half] + e[:, half:]
        lrun = lj if lrun is None else lrun + lj
        pv = _dot(e.astype(BF16), v_ref[bi, pl.ds(st, KEY_CHUNK), col:col + LANES])
        acc = pv if acc is None else acc + pv
    l = jnp.sum(lrun, axis=-1, keepdims=True)
    if extra is not None:
        l = l + jnp.exp(extra - m)
    return acc / l


def _softmax_pv(qq, k_ref, v_ref, bi, col, starts, s_ref, mask_fn=None, extra=None):
    m = _scores_pass(qq, k_ref, bi, col, starts, s_ref, mask_fn, extra)
    return _values_pass(m, qq.shape[0], v_ref, bi, col, starts, s_ref, extra)


def _all_chunks(n_keys):
    return [j * KEY_CHUNK for j in range(n_keys // KEY_CHUNK)]


def _half_masks(shape):
    lane = lax.broadcasted_iota(jnp.int32, shape, 1)
    return lane < HEAD_DIM, lane >= HEAD_DIM


def _stack_group_queries(q2, kv):
    lo_m, hi_m = _half_masks((TILE, LANES))
    m = lo_m if kv == 0 else hi_m
    zero = jnp.zeros((TILE, LANES), q2.dtype)
    return jnp.concatenate([jnp.where(m, q2[:, :LANES], zero), jnp.where(m, q2[:, LANES:], zero)], axis=0)


def _merge_kv_outputs(o_kv0, o_kv1):
    lo_m, _ = _half_masks((TILE, LANES))
    kv0 = jnp.where(lo_m, o_kv0[:TILE], pltpu.roll(o_kv0[TILE:], HEAD_DIM, 1))
    kv1 = jnp.where(lo_m, pltpu.roll(o_kv1[:TILE], HEAD_DIM, 1), o_kv1[TILE:])
    return jnp.concatenate([kv0, kv1], axis=1)


def _attn_kernel(sink_ref, lam_ref, subg_ref, qa_ref, qb_ref, qc_ref, ka_ref, kb_ref, kc_ref, va_ref, vb_ref, vc_ref,
                 o_ref, s_ref, *, t0, ctx_len, lambda_init):
    t = pl.program_id(1) + t0
    t_len = ka_ref.shape[1]
    band = 2 * TILE
    lo_m, hi_m = _half_masks((TILE, LANES))
    zero = jnp.zeros((TILE, LANES), BF16)
    row = lax.broadcasted_iota(jnp.int32, (2 * TILE, 1), 0)
    lv = lam_ref[...]
    lam = (jnp.exp(jnp.sum(lv[0:1] * lv[1:2], keepdims=True))
           - jnp.exp(jnp.sum(lv[2:3] * lv[3:4], keepdims=True)) + lambda_init)

    def sink_col(kv):
        return jnp.where(row < TILE, sink_ref[2 * kv], sink_ref[2 * kv + 1])

    def mixers(starts_ab, starts_c, mask_c):
        n_chain = 0
        for bi in range(qa_ref.shape[0]):
            for h in range(QA // LANES):
                q = qa_ref[bi, :, h * LANES:(h + 1) * LANES]
                qq = jnp.concatenate([jnp.where(lo_m, q, zero), jnp.where(hi_m, q, zero)], axis=0)
                o = _softmax_pv(qq, ka_ref, va_ref, bi, h * LANES, starts_ab, s_ref.at[n_chain % 2])
                n_chain += 1
                o = o[:TILE] - lam * o[TILE:]
                o_ref[bi, :, h * LANES:(h + 1) * LANES] = (
                    _rms(o, subg_ref[...]) * (1.0 - lambda_init)).astype(BF16)
            outs = []
            for kv in range(2):
                outs.append(_softmax_pv(_stack_group_queries(qb_ref[bi], kv), kb_ref, vb_ref, bi, 0, starts_ab,
                                        s_ref.at[n_chain % 2]))
                n_chain += 1
            o_ref[bi, :, QA:QA + QB] = _merge_kv_outputs(*outs).astype(BF16)
            outs = []
            for kv in range(2):
                outs.append(_softmax_pv(_stack_group_queries(qc_ref[bi], kv), kc_ref, vc_ref, bi, 0, starts_c,
                                        s_ref.at[n_chain % 2], mask_c, sink_col(kv)))
                n_chain += 1
            o_ref[bi, :, QA + QB:] = _merge_kv_outputs(*outs).astype(BF16)

    def ctx_step():
        mixers([0], [0], None)

    def latent_step():
        q_start = (t - 1) * TILE
        u0 = pl.multiple_of(jnp.minimum(q_start + ctx_len - WINDOW, t_len - band), WINDOW)
        q_pos = q_start + (lax.broadcasted_iota(jnp.int32, (2 * TILE, KEY_CHUNK), 0) & (TILE - 1))
        key_lane = lax.broadcasted_iota(jnp.int32, (2 * TILE, KEY_CHUNK), 1)

        def mask_c(j, st, s):
            if j == 0:
                return s
            k_pos = (st - ctx_len) + key_lane
            valid = (jnp.abs(k_pos - q_pos) <= WINDOW) & (k_pos >= 0)
            return jnp.where(valid, s, NEG_BIG)

        mixers(_all_chunks(t_len), [0] + [u0 + i * KEY_CHUNK for i in range(band // KEY_CHUNK)], mask_c)

    if t0 == 0:
        pl.when(t == 0)(ctx_step)
        pl.when(t > 0)(latent_step)
    else:
        latent_step()


def _attention(sink, lamv, subg, q, k, v, t0, lambda_init):
    batch, t_len, _ = q.shape
    nt = t_len // TILE
    kern = functools.partial(_attn_kernel, t0=t0, ctx_len=TILE, lambda_init=lambda_init)
    const = lambda b, t: (0, 0)
    nb = 1
    qspec = lambda width, blk: pl.BlockSpec((nb, TILE, width), lambda b, t: (b, t + t0, blk))
    kvspec = lambda width, blk: pl.BlockSpec((nb, t_len, width), lambda b, t: (b, 0, blk))
    return pl.pallas_call(
        kern,
        grid=(batch // nb, nt - t0),
        in_specs=[pl.BlockSpec(memory_space=pltpu.SMEM),
                  pl.BlockSpec(lamv.shape, const),
                  pl.BlockSpec((1, LANES), const),
                  qspec(QA, 0), qspec(QB, QA // QB), qspec(QC, (QA + QB) // QC),
                  kvspec(KA, 0), kvspec(KB, KA // KB), kvspec(KC, (KA + KB) // KC),
                  kvspec(KA, 0), kvspec(KB, KA // KB), kvspec(KC, (KA + KB) // KC)],
        out_specs=pl.BlockSpec((nb, TILE, Q_COLS), lambda b, t: (b, t + t0, 0)),
        out_shape=jax.ShapeDtypeStruct((batch, t_len, Q_COLS), BF16),
        scratch_shapes=[pltpu.VMEM((2, 2 * TILE, t_len), F32)],
        compiler_params=pltpu.CompilerParams(dimension_semantics=("parallel", "parallel"),
                                             vmem_limit_bytes=VMEM_LIMIT),
        name="attention",
    )(sink, lamv, subg, q, q, q, k, k, k, v, v, v)


def _route(scores, bias):
    expert = lax.broadcasted_iota(jnp.int32, scores.shape, 0)
    sel = scores + bias
    in_group = expert & (EXPERTS_PER_GROUP - 1)
    group = expert >> 2

    def neighbours(x, idx, step, span):
        for d in (1, 2, 3):
            fwd = (idx + d) < 4
            y = jnp.where(fwd, pltpu.roll(x, N_EXPERTS - d * step, 0), pltpu.roll(x, span - d * step, 0))
            yield y, fwd

    rank = jnp.zeros_like(sel)
    for y, fwd in neighbours(sel, in_group, 1, EXPERTS_PER_GROUP):
        beats = (y > sel) | ((y == sel) & jnp.logical_not(fwd))
        rank = rank + jnp.where(beats, 1.0, 0.0)
    top2 = jnp.where(rank < 2.0, sel, 0.0)
    gsum = top2
    for y, _ in neighbours(top2, in_group, 1, EXPERTS_PER_GROUP):
        gsum = gsum + y
    grank = jnp.zeros_like(sel)
    for y, fwd in neighbours(gsum, group, EXPERTS_PER_GROUP, N_EXPERTS):
        beats = (y > gsum) | ((y == gsum) & jnp.logical_not(fwd))
        grank = grank + jnp.where(beats, 1.0, 0.0)
    best = grank < 0.5
    w = jnp.where(best & (rank < 2.0), scores, 0.0)
    gates = w / jnp.sum(w, axis=0, keepdims=True)
    gid = jnp.sum(jnp.where(best & (in_group == 0), group.astype(F32), 0.0), axis=0, keepdims=True)
    tokens = scores.shape[1]
    return jnp.concatenate([gates, jnp.broadcast_to(gid, (8, tokens)),
                            jnp.zeros((LANES - N_EXPERTS - 8, tokens), F32)], axis=0)


def _outproj_kernel(*refs, n_stream, t0):
    stream_refs = refs[:n_stream]
    tile = pl.program_id(1) + t0
    mix_ref, w_ref, mod_ref, g_ref, wrh_ref, wrl_ref, rb_ref, x1_ref, h2_ref, gates_ref = refs[n_stream:]
    for i in range(mix_ref.shape[0]):
        x1 = _stream_rows(stream_refs, i, tile) + mod_ref[i, 0, 2:3, :] * _dot(mix_ref[i], w_ref[...])
        x1_ref[i] = x1
        h2 = _rms(x1, g_ref[...]) * (1.0 + mod_ref[i, 0, 4:5, :]) + mod_ref[i, 0, 3:4, :]
        h2_ref[i] = _pack_bf16_pairs(h2)
        hi, lo = _split(h2)
        logits = _dot_nt(wrh_ref[...], hi) + _dot_nt(wrl_ref[...], hi) + _dot_nt(wrh_ref[...], lo)
        gates_ref[i] = _route(_sigmoid(logits), rb_ref[...]).T


def _outproj(stream, mix, w, mod, g, wrh, wrl, rb, t0):
    batch, t_len, d = _stream_dims(stream)
    nt = t_len // TILE
    nb = 2 * PAIR
    t_out = (nt - t0) * TILE
    const = lambda b, t: (0, 0)
    tok_in = lambda b, t: (b, t + t0, 0)
    tok_out = lambda b, t: (b, t, 0)
    return pl.pallas_call(
        functools.partial(_outproj_kernel, n_stream=len(stream), t0=t0),
        grid=(batch // nb, nt - t0),
        in_specs=_stream_specs(stream, nb, lambda t: t + t0) + [
                  pl.BlockSpec((nb, TILE, Q_COLS), tok_in),
                  pl.BlockSpec(w.shape, const),
                  pl.BlockSpec((nb, 1, 8, d), lambda b, t: (b, jnp.minimum(t + t0, 1), 0, 0)),
                  pl.BlockSpec((1, d), const),
                  pl.BlockSpec((N_EXPERTS, d), const),
                  pl.BlockSpec((N_EXPERTS, d), const),
                  pl.BlockSpec((N_EXPERTS, 1), const)],
        out_specs=[pl.BlockSpec((nb, TILE, d), tok_out),
                   pl.BlockSpec((nb, TILE, d // 2), tok_out),
                   pl.BlockSpec((nb, TILE, LANES), tok_out)],
        out_shape=[jax.ShapeDtypeStruct((batch, t_out, d), F32),
                   jax.ShapeDtypeStruct((batch, t_out, d // 2), jnp.uint32),
                   jax.ShapeDtypeStruct((batch, t_out, LANES), F32)],
        compiler_params=pltpu.CompilerParams(dimension_semantics=("parallel", "parallel"),
                                             vmem_limit_bytes=VMEM_LIMIT),
        name="outproj",
    )(*stream, mix, w, mod, g, wrh, wrl, rb)


def _sc_workers():
    info = plsc.get_sparse_core_info()
    return info.num_cores, info.num_subcores


def _sc_dispatch(hpk, gates, pos2d, n_out):
    n, w = hpk.shape
    gw = gates.shape[1]
    nc, ns = _sc_workers()
    steps = n // (nc * ns * SC_WIN)
    mesh = plsc.VectorSubcoreMesh(core_axis_name="c", subcore_axis_name="s")

    @functools.partial(
        pl.kernel, mesh=mesh,
        out_type=(jax.ShapeDtypeStruct((n_out, w), hpk.dtype), jax.ShapeDtypeStruct((n_out, gw), gates.dtype)),
        scratch_types=[pltpu.VMEM((1, SC_WIN), jnp.int32), pltpu.VMEM((SC_WIN, w), hpk.dtype),
                       pltpu.VMEM((SC_WIN, gw), gates.dtype), pltpu.SemaphoreType.DMA, pltpu.SemaphoreType.DMA],
        name="sc_dispatch")
    def k(h_hbm, g_hbm, pos_hbm, hs_hbm, gs_hbm, idx_v, rows_v, grow_v, sem_in, sem_out):
        wid = lax.axis_index("s") * nc + lax.axis_index("c")

        @pl.loop(0, steps)
        def _(j):
            blk = wid * steps + j
            off = pl.multiple_of(blk * SC_WIN, SC_WIN)
            loads = [pltpu.async_copy(pos_hbm.at[pl.ds(blk, 1)], idx_v, sem_in),
                     pltpu.async_copy(h_hbm.at[pl.ds(off, SC_WIN)], rows_v, sem_in),
                     pltpu.async_copy(g_hbm.at[pl.ds(off, SC_WIN)], grow_v, sem_in)]
            for cp in loads:
                cp.wait()
            stores = [pltpu.async_copy(rows_v, hs_hbm.at[idx_v.at[0]], sem_out),
                      pltpu.async_copy(grow_v, gs_hbm.at[idx_v.at[0]], sem_out)]
            for cp in stores:
                cp.wait()

    return k(hpk, gates, pos2d)


def _sc_combine(ys, pos2d, n):
    w = ys.shape[1]
    nc, ns = _sc_workers()
    steps = n // (nc * ns * SC_WIN)
    mesh = plsc.VectorSubcoreMesh(core_axis_name="c", subcore_axis_name="s")

    @functools.partial(
        pl.kernel, mesh=mesh,
        out_type=jax.ShapeDtypeStruct((n, w), ys.dtype),
        scratch_types=[pltpu.VMEM((1, SC_WIN), jnp.int32), pltpu.VMEM((SC_WIN, w), ys.dtype)],
        name="sc_combine")
    def k(y_hbm, pos_hbm, o_hbm, idx_v, rows_v):
        wid = lax.axis_index("s") * nc + lax.axis_index("c")

        @pl.loop(0, steps)
        def _(j):
            blk = wid * steps + j
            off = pl.multiple_of(blk * SC_WIN, SC_WIN)
            pltpu.sync_copy(pos_hbm.at[pl.ds(blk, 1)], idx_v)
            pltpu.sync_copy(y_hbm.at[idx_v.at[0]], rows_v)
            pltpu.sync_copy(rows_v, o_hbm.at[pl.ds(off, SC_WIN)])

    return k(ys, pos2d)


PLAN_ROWS = 256
PLAN_BLOCK = 2048


def _plan_kernel(gates_ref, pos_ref, tg_ref, tu_ref, g_ref):
    step = pl.program_id(0)

    @pl.when(step == 0)
    def _():
        g_ref[...] = jnp.full(g_ref.shape, -1.0, F32)

    gates = gates_ref[...]
    lane = lax.broadcasted_iota(jnp.int32, gates.shape, 1)
    along_lanes = _dot_nt(jnp.ones((8, LANES), BF16), jnp.where(lane == GROUP_LANE, gates, 0.0).astype(BF16))
    for r in range(PLAN_BLOCK // LANES):
        g_ref[pl.ds(step * (PLAN_BLOCK // LANES) + r, 1), :] = along_lanes[0:1, r * LANES:(r + 1) * LANES]

    pl.when(step == pl.num_programs(0) - 1)(functools.partial(_plan_finish, g_ref, pos_ref, tg_ref, tu_ref))


def _plan_finish(g_ref, pos_ref, tg_ref, tu_ref):
    g = g_ref[...]
    rows = g.shape[0]
    n_groups = N_EXPERTS // EXPERTS_PER_GROUP
    before_lane = (lax.broadcasted_iota(jnp.int32, (LANES, LANES), 0)
                   < lax.broadcasted_iota(jnp.int32, (LANES, LANES), 1)).astype(BF16)
    before_row = (lax.broadcasted_iota(jnp.int32, (rows, rows), 1)
                  < lax.broadcasted_iota(jnp.int32, (rows, rows), 0)).astype(BF16)
    tile_row = lax.broadcasted_iota(jnp.int32, (1, LANES), 1).astype(F32) * FFN_TILE
    pos = jnp.zeros_like(g)
    start = jnp.zeros((1, 1), F32)
    tile_group = jnp.zeros((1, LANES), F32)
    used_end = []
    for grp in range(n_groups):
        m = jnp.where(g == grp, 1.0, 0.0)
        in_row = _dot(m.astype(BF16), before_lane)
        row_total = jnp.sum(m, axis=1, keepdims=True)
        rows_before = _dot(before_row, jnp.broadcast_to(row_total, m.shape).astype(BF16))
        pos = pos + m * (start + in_row + rows_before)
        count = jnp.sum(row_total, axis=0, keepdims=True)
        used_end.append(start + count)
        start = start + jnp.floor((count + (FFN_TILE - 1)) * (1.0 / FFN_TILE)) * FFN_TILE
        if grp < n_groups - 1:
            tile_group = tile_group + jnp.where(tile_row >= start, 1.0, 0.0)
    pos_ref[...] = pos.astype(jnp.int32)
    tg_ref[...] = tile_group.astype(jnp.int32)
    end_of_tile_group = sum(jnp.where(tile_group == grp, used_end[grp], 0.0) for grp in range(n_groups))
    tu_ref[...] = jnp.where(tile_row < end_of_tile_group, 1, 0).astype(jnp.int32)


def _bucket_plan(gates, n_pad):
    n = gates.shape[0]
    whole = lambda shape: pl.BlockSpec(shape, lambda i: (0, 0))
    pos, tile_group, tile_used = pl.pallas_call(
        _plan_kernel,
        grid=(n // PLAN_BLOCK,),
        in_specs=[pl.BlockSpec((PLAN_BLOCK, LANES), lambda i: (i, 0))],
        out_specs=[whole((PLAN_ROWS, LANES)), whole((1, LANES)), whole((1, LANES))],
        out_shape=[jax.ShapeDtypeStruct((PLAN_ROWS, LANES), jnp.int32),
                   jax.ShapeDtypeStruct((1, LANES), jnp.int32),
                   jax.ShapeDtypeStruct((1, LANES), jnp.int32)],
        scratch_shapes=[pltpu.VMEM((PLAN_ROWS, LANES), F32)],
        compiler_params=pltpu.CompilerParams(dimension_semantics=("arbitrary",)),
        name="plan",
    )(gates)
    n_tiles = n_pad // FFN_TILE
    return pos[:n // LANES].reshape(n // SC_WIN, SC_WIN), tile_group[0, :n_tiles], tile_used[0, :n_tiles]


def _ffn_kernel(tg_ref, tu_ref, hs_ref, gs_ref, wg_ref, wu_ref, wd_ref, ys_ref, wgb_ref, wub_ref, wdb_ref):
    i = pl.program_id(0)

    @pl.when((i == 0) | (tg_ref[i] != tg_ref[jnp.maximum(i - 1, 0)]))
    def _():
        wgb_ref[...] = wg_ref[0].astype(BF16)
        wub_ref[...] = wu_ref[0].astype(BF16)
        wdb_ref[...] = wd_ref[0].astype(BF16)

    @pl.when(tu_ref[i] > 0)
    def _():
        lo, hi = _unpack_bf16_pairs(hs_ref[...])
        lo = lo.astype(BF16)
        hi = hi.astype(BF16)
        half = lo.shape[1]
        gs = gs_ref[...]
        lane = lax.broadcasted_iota(jnp.int32, gs.shape, 1)
        first = tg_ref[i] * EXPERTS_PER_GROUP
        y = None
        for e in range(EXPERTS_PER_GROUP):
            a = _dot(lo, wgb_ref[e, :half, :]) + _dot(hi, wgb_ref[e, half:, :])
            u = _dot(lo, wub_ref[e, :half, :]) + _dot(hi, wub_ref[e, half:, :])
            gate = jnp.sum(jnp.where(lane == first + e, gs, 0.0), axis=-1, keepdims=True)
            part = _dot(((a * _sigmoid(a)) * u * gate).astype(BF16), wdb_ref[e])
            y = part if y is None else y + part
        ys_ref[...] = _pack_bf16_pairs(y)


def _ffn(tile_group, tile_used, hs, gs, wg, wu, wd, layer):
    n_pad, half = hs.shape
    grp = lambda i, tg, tu: (layer, tg[i], 0, 0)
    gshape = lambda w: (EXPERTS_PER_GROUP,) + w.shape[2:]
    wspec = lambda w: pl.BlockSpec((1,) + gshape(w), grp)
    return pl.pallas_call(
        _ffn_kernel,
        grid_spec=pltpu.PrefetchScalarGridSpec(
            num_scalar_prefetch=2,
            grid=(n_pad // FFN_TILE,),
            in_specs=[pl.BlockSpec((FFN_TILE, half), lambda i, tg, tu: (i, 0)),
                      pl.BlockSpec((FFN_TILE, LANES), lambda i, tg, tu: (i, 0)),
                      wspec(wg), wspec(wu), wspec(wd)],
            out_specs=pl.BlockSpec((FFN_TILE, half), lambda i, tg, tu: (i, 0)),
            scratch_shapes=[pltpu.VMEM(gshape(wg), BF16), pltpu.VMEM(gshape(wu), BF16),
                            pltpu.VMEM(gshape(wd), BF16)]),
        out_shape=jax.ShapeDtypeStruct((n_pad, half), jnp.uint32),
        compiler_params=pltpu.CompilerParams(dimension_semantics=("arbitrary",), vmem_limit_bytes=VMEM_LIMIT),
        name="ffn",
    )(tile_group, tile_used, hs, gs, wg, wu, wd)


def _residual_kernel(x_ref, y_ref, mod_ref, fg_ref, o_ref, *, final_norm):
    for i in range(x_ref.shape[0]):
        lo, hi = _unpack_bf16_pairs(y_ref[i])
        y = x_ref[i] + mod_ref[i, 0, 5:6, :] * jnp.concatenate([lo, hi], axis=1)
        if final_norm:
            y = _rms(y, fg_ref[...])
        o_ref[i] = y


def _residual(x1, yg, mod, fg, t0, final_norm):
    batch, t_out, d = x1.shape
    nb = PAIR
    tok = lambda b, t: (b, t, 0)
    return pl.pallas_call(
        functools.partial(_residual_kernel, final_norm=final_norm),
        grid=(batch // nb, t_out // TILE),
        in_specs=[pl.BlockSpec((nb, TILE, d), tok),
                  pl.BlockSpec((nb, TILE, d // 2), tok),
                  pl.BlockSpec((nb, 1, 8, d), lambda b, t: (b, jnp.minimum(t + t0, 1), 0, 0)),
                  pl.BlockSpec((1, d), lambda b, t: (0, 0))],
        out_specs=pl.BlockSpec((nb, TILE, d), tok),
        out_shape=jax.ShapeDtypeStruct((batch, t_out, d), F32),
        compiler_params=pltpu.CompilerParams(dimension_semantics=("parallel", "parallel"),
                                             vmem_limit_bytes=VMEM_LIMIT),
        name="residual",
    )(x1, yg, mod, fg)


def _moe(h2, gates, wg, wu, wd, layer):
    batch, t_out, half = h2.shape
    n = batch * t_out
    n_pad = n + (N_EXPERTS // EXPERTS_PER_GROUP) * FFN_TILE
    assert n % PLAN_BLOCK == 0 and n // LANES <= PLAN_ROWS and n_pad // FFN_TILE <= LANES
    gates = gates.reshape(n, LANES)
    pos2d, tile_group, tile_used = _bucket_plan(gates, n_pad)
    hs, gs = _sc_dispatch(h2.reshape(n, half), gates, pos2d, n_pad)
    ys = _ffn(tile_group, tile_used, hs, gs, wg, wu, wd, layer)
    return _sc_combine(ys, pos2d, n).reshape(batch, t_out, half)


def _rope_tables(seq, ctx_len):
    rows = seq // GRID_W
    row_pos = jnp.repeat(jnp.arange(rows, dtype=F32), GRID_W)
    col_pos = jnp.tile(jnp.arange(GRID_W, dtype=F32), rows)
    axis_dim = HEAD_DIM // 2
    inv_freq = ROPE_BASE ** (-jnp.arange(0, axis_dim, 2, dtype=F32) / axis_dim)
    ang_r = row_pos[:, None] * inv_freq[None, :]
    ang_c = col_pos[:, None] * inv_freq[None, :]
    z = jnp.zeros_like(ang_r)
    cos = jnp.concatenate([jnp.cos(ang_r)] * 2 + [jnp.cos(ang_c)] * 2, axis=-1)
    sin_a = jnp.concatenate([-jnp.sin(ang_r), z, -jnp.sin(ang_c), z], axis=-1)
    sin_b = jnp.concatenate([z, jnp.sin(ang_r), z, jnp.sin(ang_c)], axis=-1)

    def full(tab, fill):
        tab = jnp.tile(tab, (1, LANES // HEAD_DIM))
        return jnp.concatenate([jnp.full((ctx_len, LANES), fill, F32), tab], axis=0)

    return full(cos, 1.0), full(sin_a, 0.0), full(sin_b, 0.0)


def kernel(x, c, ctx, c_ctx, w_ada, b_ada, norm1_g, norm2_g, w_in, w_out, lam_q1, lam_k1, lam_q2, lam_k2,
           subln_g, q_norm_g, k_norm_g, sink, w_router, router_bias, w_gate, w_up, w_down, final_g):
    batch, seq, d = x.shape
    ctx_len = ctx.shape[1]
    depth = w_in.shape[0]
    sc_rows = SC_WIN * math.prod(_sc_workers())
    assert ctx_len == TILE and seq % TILE == 0 and batch % PAIR == 0 and batch <= 15
    assert (batch * seq) % sc_rows == 0 and (batch * (seq + ctx_len)) % sc_rows == 0

    cpad = jnp.zeros((16, d), F32).at[:batch].set(c).at[batch].set(c_ctx)
    mod_all = _adaln(cpad, w_ada, b_ada)
    cos, sin_a, sin_b = _rope_tables(seq, ctx_len)
    mseg = jnp.asarray(np.kron(np.eye(2), np.ones((HEAD_DIM, HEAD_DIM))), BF16)
    wr = w_router.T
    wrh = wr.astype(BF16)
    wrl = (wr - wrh.astype(F32)).astype(BF16)
    rb = router_bias.reshape(N_EXPERTS, 1)

    stream = (ctx, x)

    for l in range(depth):
        last = l == depth - 1
        lambda_init = 0.8 - 0.6 * math.exp(-0.3 * l)
        m6 = mod_all[l].reshape(16, 6, d)
        m8 = jnp.concatenate([m6, jnp.zeros((16, 2, d), F32)], axis=1)
        mod = jnp.stack([jnp.broadcast_to(m8[batch], (batch, 8, d)), m8[:batch]], axis=1)

        w_in_p = w_in[l].astype(BF16)
        w_out_p = w_out[l].astype(BF16)
        gq = jnp.tile(q_norm_g[l], 2).reshape(1, LANES)
        gk = jnp.tile(k_norm_g[l], 2).reshape(1, LANES)
        q, k, v = _inproj(stream, mod, norm1_g[l].reshape(1, d), w_in_p, cos, sin_a, sin_b, gq, gk, mseg)

        t0 = 1 if last else 0
        lamv = jnp.stack([lam_q1[l], lam_k1[l], lam_q2[l], lam_k2[l]])
        mix = _attention(sink[l], lamv, subln_g[l].reshape(1, LANES), q, k, v, t0, lambda_init)
        x1, h2, gates = _outproj(stream, mix, w_out_p, mod, norm2_g[l].reshape(1, d), wrh, wrl, rb, t0)
        stream = (x1, _moe(h2, gates, w_gate, w_up, w_down, l), mod)
    return _residual(*stream, final_g.reshape(1, d), 1, final_norm=True)
```

```python
import functools
import math

import numpy as np
import jax
import jax.numpy as jnp
from jax import lax
from jax.experimental import pallas as pl
from jax.experimental.pallas import tpu as pltpu
from jax.experimental.pallas import tpu_sc as plsc

F32 = jnp.float32
BF16 = jnp.bfloat16

HEAD_DIM = 64
GRID_W = 64
ROPE_BASE = 10000.0
RMS_EPS = 1e-6
WINDOW = 128
N_EXPERTS = 16
EXPERTS_PER_GROUP = 4
LANES = 128
TILE = 256
PAIR = 2
FFN_TILE = 512
SC_WIN = 64
GROUP_LANE = 16
PAIRS = ((0, 1), (0, 2), (0, 3), (1, 3), (1, 2), (2, 3))
NEG_BIG = -1e30
KEY_CHUNK = 256
VMEM_LIMIT = 52 * 1024 * 1024

QA, QB, QC = 512, 256, 256
KA, KB, KC = 512, 128, 128
Q_COLS = QA + QB + QC
K_COLS = KA + KB + KC
V_COLS = 768
ROPE_COLS = Q_COLS + K_COLS


def _dot(a, b):
    return jnp.dot(a, b, preferred_element_type=F32)


def _dot_nt(a, b):
    return lax.dot_general(a, b, (((1,), (1,)), ((), ())), preferred_element_type=F32)


def _split(a):
    hi = a.astype(BF16)
    lo = (a - hi.astype(F32)).astype(BF16)
    return hi, lo


def _dot3(a, b):
    ah, al = _split(a)
    bh, bl = _split(b)
    return _dot(ah, bh) + _dot(ah, bl) + _dot(al, bh)


def _rms(x, g):
    ms = jnp.mean(x * x, axis=-1, keepdims=True)
    return x * lax.rsqrt(ms + RMS_EPS) * g


def _sigmoid(x):
    return 1.0 / (1.0 + jnp.exp(-x))


def _pack_bf16_pairs(y):
    n = y.shape[1] // 2
    yb = y.astype(BF16).astype(F32)
    lo = lax.bitcast_convert_type(yb[:, :n], jnp.uint32) >> 16
    hi = lax.bitcast_convert_type(yb[:, n:], jnp.uint32) & jnp.uint32(0xFFFF0000)
    return hi | lo


def _unpack_bf16_pairs(w):
    lo = lax.bitcast_convert_type(w << 16, F32)
    hi = lax.bitcast_convert_type(w & jnp.uint32(0xFFFF0000), F32)
    return lo, hi


def _stream_dims(stream):
    batch, t_len, d = stream[0].shape
    return batch, (t_len + stream[1].shape[1] if len(stream) == 2 else t_len), d


def _stream_rows(stream_refs, i, tile):
    if len(stream_refs) == 2:
        ctx_ref, x_ref = stream_refs
        return jnp.where(tile == 0, ctx_ref[i], x_ref[i])
    x_ref, y_ref, mod_ref = stream_refs
    lo, hi = _unpack_bf16_pairs(y_ref[i])
    return x_ref[i] + mod_ref[i, 0, 5:6, :] * jnp.concatenate([lo, hi], axis=1)


def _stream_specs(stream, nb, tile_of):
    d = stream[0].shape[-1]
    if len(stream) == 2:
        return [pl.BlockSpec((nb, TILE, d), lambda b, t: (b, 0, 0)),
                pl.BlockSpec((nb, TILE, d), lambda b, t: (b, jnp.maximum(tile_of(t) - 1, 0), 0))]
    tok = lambda b, t: (b, tile_of(t), 0)
    return [pl.BlockSpec((nb, TILE, d), tok),
            pl.BlockSpec((nb, TILE, d // 2), tok),
            pl.BlockSpec((nb, 1, 8, d), lambda b, t: (b, jnp.minimum(tile_of(t), 1), 0, 0))]


def _adaln_kernel(c_ref, w_ref, b_ref, o_ref):
    cv = c_ref[...]
    s = cv * _sigmoid(cv)
    o_ref[0] = _dot3(s, w_ref[0]) + b_ref[0]


def _adaln(cpad, w_ada, b_ada):
    depth, d, n6 = w_ada.shape
    tn = 1536
    return pl.pallas_call(
        _adaln_kernel,
        grid=(depth, n6 // tn),
        in_specs=[pl.BlockSpec((16, d), lambda l, j: (0, 0)),
                  pl.BlockSpec((1, d, tn), lambda l, j: (l, 0, j)),
                  pl.BlockSpec((1, 1, tn), lambda l, j: (l, 0, j))],
        out_specs=pl.BlockSpec((1, 16, tn), lambda l, j: (l, 0, j)),
        out_shape=jax.ShapeDtypeStruct((depth, 16, n6), F32),
        compiler_params=pltpu.CompilerParams(dimension_semantics=("parallel", "parallel"),
                                             vmem_limit_bytes=VMEM_LIMIT),
        name="adaln",
    )(cpad, w_ada, b_ada.reshape(depth, 1, n6))


def _inproj_kernel(*refs, n_stream):
    for base in range(0, refs[0].shape[0], PAIR):
        _inproj_rows(base, pl.program_id(1), refs[:n_stream], *refs[n_stream:])


def _inproj_rows(base, tile, stream_refs, mod_ref, g_ref, w_ref, cos_ref, sa_ref, sb_ref, gq_ref, gk_ref, mseg_ref,
                 q_ref, k_ref, v_ref):
    members = range(base, base + PAIR)
    h = jnp.concatenate(
        [(_rms(_stream_rows(stream_refs, i, tile), g_ref[...]) * (1.0 + mod_ref[i, 0, 1:2, :])
          + mod_ref[i, 0, 0:1, :]).astype(BF16) for i in members], axis=0)
    cos = jnp.concatenate([cos_ref[...]] * PAIR, axis=0)
    sa = jnp.concatenate([sa_ref[...]] * PAIR, axis=0)
    sb = jnp.concatenate([sb_ref[...]] * PAIR, axis=0)
    mseg = mseg_ref[...]

    def put(ref, cols, val):
        for n, i in enumerate(members):
            ref[i, :, cols] = val[n * TILE:(n + 1) * TILE]

    def rope(b):
        return b * cos + pltpu.roll(b, LANES - 16, 1) * sa + pltpu.roll(b, 16, 1) * sb

    def qknorm(b, g):
        hi, lo = _split(b * b)
        ms = (_dot(hi, mseg) + _dot(lo, mseg)) * (1.0 / HEAD_DIM)
        return b * lax.rsqrt(ms + RMS_EPS) * g

    def wcols(*pieces):
        parts = [w_ref[:, a:a + n] for a, n in pieces]
        return parts[0] if len(parts) == 1 else jnp.concatenate(parts, axis=1)

    lane = lax.broadcasted_iota(jnp.int32, (PAIR * TILE, LANES), 1)

    def regroup(p2):
        h0, h1 = p2[:, :LANES], p2[:, LANES:]
        return (jnp.where(lane < HEAD_DIM, h0, pltpu.roll(h1, HEAD_DIM, 1)),
                jnp.where(lane < HEAD_DIM, pltpu.roll(h0, HEAD_DIM, 1), h1))

    q0, kv0 = 0, Q_COLS
    rope_chunks = [wcols((q0, 256)), wcols((q0 + 256, 256)), wcols((q0 + QA, QB)), wcols((q0 + QA + QB, QC)),
                   wcols((kv0, 256)), wcols((kv0 + 256, 256)),
                   wcols((kv0 + 2 * KA, KB), (kv0 + 2 * KA + 2 * KB, KC))]
    v_chunks = [wcols((kv0 + KA, 256)), wcols((kv0 + KA + 256, 256)),
                wcols((kv0 + 2 * KA + KB, KB), (kv0 + 2 * KA + 2 * KB + KC, KC))]
    for c, wc in enumerate(rope_chunks):
        p2 = _dot(h, wc)
        halves = regroup(p2) if c in (2, 3) else (p2[:, :LANES], p2[:, LANES:])
        for half in range(2):
            j = 2 * c + half
            p = halves[half]
            if j in (4, 5):
                p = qknorm(p, gq_ref[...])
            if j == 12:
                p = qknorm(p, gk_ref[...])
            p = rope(p)
            if j < Q_COLS // LANES:
                put(q_ref, slice(j * LANES, (j + 1) * LANES), (p * (HEAD_DIM ** -0.5)).astype(BF16))
            else:
                jk = j - Q_COLS // LANES
                put(k_ref, slice(jk * LANES, (jk + 1) * LANES), p.astype(BF16))
    for c, wc in enumerate(v_chunks):
        put(v_ref, slice(c * 256, (c + 1) * 256), _dot(h, wc).astype(BF16))


def _inproj(stream, mod, g, w, cos, sa, sb, gq, gk, mseg):
    batch, t_len, d = _stream_dims(stream)
    nt = t_len // TILE
    nb = 2 * PAIR
    row = lambda b, t: (t, 0)
    const = lambda b, t: (0, 0)
    tok = lambda b, t: (b, t, 0)
    return pl.pallas_call(
        functools.partial(_inproj_kernel, n_stream=len(stream)),
        grid=(batch // nb, nt),
        in_specs=_stream_specs(stream, nb, lambda t: t) + [
                  pl.BlockSpec((nb, 1, 8, d), lambda b, t: (b, jnp.minimum(t, 1), 0, 0)),
                  pl.BlockSpec((1, d), const),
                  pl.BlockSpec(w.shape, const),
                  pl.BlockSpec((TILE, LANES), row),
                  pl.BlockSpec((TILE, LANES), row),
                  pl.BlockSpec((TILE, LANES), row),
                  pl.BlockSpec((1, LANES), const),
                  pl.BlockSpec((1, LANES), const),
                  pl.BlockSpec((LANES, LANES), const)],
        out_specs=[pl.BlockSpec((nb, TILE, Q_COLS), tok),
                   pl.BlockSpec((nb, TILE, K_COLS), tok),
                   pl.BlockSpec((nb, TILE, V_COLS), tok)],
        out_shape=[jax.ShapeDtypeStruct((batch, t_len, Q_COLS), BF16),
                   jax.ShapeDtypeStruct((batch, t_len, K_COLS), BF16),
                   jax.ShapeDtypeStruct((batch, t_len, V_COLS), BF16)],
        compiler_params=pltpu.CompilerParams(dimension_semantics=("parallel", "parallel"),
                                             vmem_limit_bytes=VMEM_LIMIT),
        name="inproj",
    )(*stream, mod, g, w, cos, sa, sb, gq, gk, mseg)


def _scores_pass(qq, k_ref, bi, col, starts, s_ref, mask_fn=None, extra=None):
    half = KEY_CHUNK // 2
    rows = qq.shape[0]
    mrun = None
    for j, st in enumerate(starts):
        s = _dot_nt(qq, k_ref[bi, pl.ds(st, KEY_CHUNK), col:col + LANES])
        if mask_fn is not None:
            s = mask_fn(j, st, s)
        s_ref[0:rows, j * KEY_CHUNK:(j + 1) * KEY_CHUNK] = s
        mj = jnp.maximum(s[:, :half], s[:, half:])
        mrun = mj if mrun is None else jnp.maximum(mrun, mj)
    m = jnp.max(mrun, axis=-1, keepdims=True)
    return m if extra is None else jnp.maximum(m, extra)


def _values_pass(m, rows, v_ref, bi, col, starts, s_ref, extra=None):
    half = KEY_CHUNK // 2
    lrun = None
    acc = None
    for j, st in enumerate(starts):
        e = jnp.exp(s_ref[0:rows, j * KEY_CHUNK:(j + 1) * KEY_CHUNK] - m)
        lj = e[:, :half] + e[:, half:]
        lrun = lj if lrun is None else lrun + lj
        pv = _dot(e.astype(BF16), v_ref[bi, pl.ds(st, KEY_CHUNK), col:col + LANES])
        acc = pv if acc is None else acc + pv
    l = jnp.sum(lrun, axis=-1, keepdims=True)
    if extra is not None:
        l = l + jnp.exp(extra - m)
    return acc / l


def _softmax_pv(qq, k_ref, v_ref, bi, col, starts, s_ref, mask_fn=None, extra=None):
    m = _scores_pass(qq, k_ref, bi, col, starts, s_ref, mask_fn, extra)
    return _values_pass(m, qq.shape[0], v_ref, bi, col, starts, s_ref, extra)


def _all_chunks(n_keys):
    return [j * KEY_CHUNK for j in range(n_keys // KEY_CHUNK)]


def _half_masks(shape):
    lane = lax.broadcasted_iota(jnp.int32, shape, 1)
    return lane < HEAD_DIM, lane >= HEAD_DIM


def _stack_group_queries(q2, kv):
    lo_m, hi_m = _half_masks((TILE, LANES))
    m = lo_m if kv == 0 else hi_m
    zero = jnp.zeros((TILE, LANES), q2.dtype)
    return jnp.concatenate([jnp.where(m, q2[:, :LANES], zero), jnp.where(m, q2[:, LANES:], zero)], axis=0)


def _merge_kv_outputs(o_kv0, o_kv1):
    lo_m, _ = _half_masks((TILE, LANES))
    kv0 = jnp.where(lo_m, o_kv0[:TILE], pltpu.roll(o_kv0[TILE:], HEAD_DIM, 1))
    kv1 = jnp.where(lo_m, pltpu.roll(o_kv1[:TILE], HEAD_DIM, 1), o_kv1[TILE:])
    return jnp.concatenate([kv0, kv1], axis=1)


def _attn_kernel(sink_ref, lam_ref, subg_ref, qa_ref, qb_ref, qc_ref, ka_ref, kb_ref, kc_ref, va_ref, vb_ref, vc_ref,
                 o_ref, s_ref, *, t0, ctx_len, lambda_init):
    t = pl.program_id(1) + t0
    t_len = ka_ref.shape[1]
    band = 2 * TILE
    lo_m, hi_m = _half_masks((TILE, LANES))
    zero = jnp.zeros((TILE, LANES), BF16)
    row = lax.broadcasted_iota(jnp.int32, (2 * TILE, 1), 0)
    lv = lam_ref[...]
    lam = (jnp.exp(jnp.sum(lv[0:1] * lv[1:2], keepdims=True))
           - jnp.exp(jnp.sum(lv[2:3] * lv[3:4], keepdims=True)) + lambda_init)

    def sink_col(kv):
        return jnp.where(row < TILE, sink_ref[2 * kv], sink_ref[2 * kv + 1])

    def mixers(starts_ab, starts_c, mask_c):
        n_chain = 0
        for bi in range(qa_ref.shape[0]):
            for h in range(QA // LANES):
                q = qa_ref[bi, :, h * LANES:(h + 1) * LANES]
                qq = jnp.concatenate([jnp.where(lo_m, q, zero), jnp.where(hi_m, q, zero)], axis=0)
                o = _softmax_pv(qq, ka_ref, va_ref, bi, h * LANES, starts_ab, s_ref.at[n_chain % 2])
                n_chain += 1
                o = o[:TILE] - lam * o[TILE:]
                o_ref[bi, :, h * LANES:(h + 1) * LANES] = (
                    _rms(o, subg_ref[...]) * (1.0 - lambda_init)).astype(BF16)
            outs = []
            for kv in range(2):
                outs.append(_softmax_pv(_stack_group_queries(qb_ref[bi], kv), kb_ref, vb_ref, bi, 0, starts_ab,
                                        s_ref.at[n_chain % 2]))
                n_chain += 1
            o_ref[bi, :, QA:QA + QB] = _merge_kv_outputs(*outs).astype(BF16)
            outs = []
            for kv in range(2):
                outs.append(_softmax_pv(_stack_group_queries(qc_ref[bi], kv), kc_ref, vc_ref, bi, 0, starts_c,
                                        s_ref.at[n_chain % 2], mask_c, sink_col(kv)))
                n_chain += 1
            o_ref[bi, :, QA + QB:] = _merge_kv_outputs(*outs).astype(BF16)

    def ctx_step():
        mixers([0], [0], None)

    def latent_step():
        q_start = (t - 1) * TILE
        u0 = pl.multiple_of(jnp.minimum(q_start + ctx_len - WINDOW, t_len - band), WINDOW)
        q_pos = q_start + (lax.broadcasted_iota(jnp.int32, (2 * TILE, KEY_CHUNK), 0) & (TILE - 1))
        key_lane = lax.broadcasted_iota(jnp.int32, (2 * TILE, KEY_CHUNK), 1)

        def mask_c(j, st, s):
            if j == 0:
                return s
            k_pos = (st - ctx_len) + key_lane
            valid = (jnp.abs(k_pos - q_pos) <= WINDOW) & (k_pos >= 0)
            return jnp.where(valid, s, NEG_BIG)

        mixers(_all_chunks(t_len), [0] + [u0 + i * KEY_CHUNK for i in range(band // KEY_CHUNK)], mask_c)

    if t0 == 0:
        pl.when(t == 0)(ctx_step)
        pl.when(t > 0)(latent_step)
    else:
        latent_step()


def _attention(sink, lamv, subg, q, k, v, t0, lambda_init):
    batch, t_len, _ = q.shape
    nt = t_len // TILE
    kern = functools.partial(_attn_kernel, t0=t0, ctx_len=TILE, lambda_init=lambda_init)
    const = lambda b, t: (0, 0)
    nb = 1
    qspec = lambda width, blk: pl.BlockSpec((nb, TILE, width), lambda b, t: (b, t + t0, blk))
    kvspec = lambda width, blk: pl.BlockSpec((nb, t_len, width), lambda b, t: (b, 0, blk))
    return pl.pallas_call(
        kern,
        grid=(batch // nb, nt - t0),
        in_specs=[pl.BlockSpec(memory_space=pltpu.SMEM),
                  pl.BlockSpec(lamv.shape, const),
                  pl.BlockSpec((1, LANES), const),
                  qspec(QA, 0), qspec(QB, QA // QB), qspec(QC, (QA + QB) // QC),
                  kvspec(KA, 0), kvspec(KB, KA // KB), kvspec(KC, (KA + KB) // KC),
                  kvspec(KA, 0), kvspec(KB, KA // KB), kvspec(KC, (KA + KB) // KC)],
        out_specs=pl.BlockSpec((nb, TILE, Q_COLS), lambda b, t: (b, t + t0, 0)),
        out_shape=jax.ShapeDtypeStruct((batch, t_len, Q_COLS), BF16),
        scratch_shapes=[pltpu.VMEM((2, 2 * TILE, t_len), F32)],
        compiler_params=pltpu.CompilerParams(dimension_semantics=("parallel", "parallel"),
                                             vmem_limit_bytes=VMEM_LIMIT),
        name="attention",
    )(sink, lamv, subg, q, q, q, k, k, k, v, v, v)


def _route(scores, bias):
    expert = lax.broadcasted_iota(jnp.int32, scores.shape, 0)
    sel = scores + bias
    in_group = expert & (EXPERTS_PER_GROUP - 1)
    group = expert >> 2

    def neighbours(x, idx, step, span):
        for d in (1, 2, 3):
            fwd = (idx + d) < 4
            y = jnp.where(fwd, pltpu.roll(x, N_EXPERTS - d * step, 0), pltpu.roll(x, span - d * step, 0))
            yield y, fwd

    rank = jnp.zeros_like(sel)
    for y, fwd in neighbours(sel, in_group, 1, EXPERTS_PER_GROUP):
        beats = (y > sel) | ((y == sel) & jnp.logical_not(fwd))
        rank = rank + jnp.where(beats, 1.0, 0.0)
    top2 = jnp.where(rank < 2.0, sel, 0.0)
    gsum = top2
    for y, _ in neighbours(top2, in_group, 1, EXPERTS_PER_GROUP):
        gsum = gsum + y
    grank = jnp.zeros_like(sel)
    for y, fwd in neighbours(gsum, group, EXPERTS_PER_GROUP, N_EXPERTS):
        beats = (y > gsum) | ((y == gsum) & jnp.logical_not(fwd))
        grank = grank + jnp.where(beats, 1.0, 0.0)
    best = grank < 0.5
    chosen = best & (rank < 2.0)
    w = jnp.where(chosen, scores, 0.0)
    gates = w / jnp.sum(w, axis=0, keepdims=True)
    gid = jnp.sum(jnp.where(best & (in_group == 0), group.astype(F32), 0.0), axis=0, keepdims=True)
    bits = jnp.sum(jnp.where(chosen, jnp.left_shift(1, in_group).astype(F32), 0.0), axis=0, keepdims=True)
    pair = sum(jnp.where(bits == float((1 << a) + (1 << b)), float(k), 0.0) for k, (a, b) in enumerate(PAIRS))
    bucket = gid * len(PAIRS) + pair
    tokens = scores.shape[1]
    return jnp.concatenate([gates, jnp.broadcast_to(bucket, (8, tokens)),
                            jnp.zeros((LANES - N_EXPERTS - 8, tokens), F32)], axis=0)


def _outproj_kernel(*refs, n_stream, t0):
    stream_refs = refs[:n_stream]
    tile = pl.program_id(1) + t0
    mix_ref, w_ref, mod_ref, g_ref, wrh_ref, wrl_ref, rb_ref, x1_ref, h2_ref, gates_ref = refs[n_stream:]
    for i in range(mix_ref.shape[0]):
        x1 = _stream_rows(stream_refs, i, tile) + mod_ref[i, 0, 2:3, :] * _dot(mix_ref[i], w_ref[...])
        x1_ref[i] = x1
        h2 = _rms(x1, g_ref[...]) * (1.0 + mod_ref[i, 0, 4:5, :]) + mod_ref[i, 0, 3:4, :]
        h2_ref[i] = _pack_bf16_pairs(h2)
        hi, lo = _split(h2)
        logits = _dot_nt(wrh_ref[...], hi) + _dot_nt(wrl_ref[...], hi) + _dot_nt(wrh_ref[...], lo)
        gates_ref[i] = _route(_sigmoid(logits), rb_ref[...]).T


def _outproj(stream, mix, w, mod, g, wrh, wrl, rb, t0):
    batch, t_len, d = _stream_dims(stream)
    nt = t_len // TILE
    nb = 2 * PAIR
    t_out = (nt - t0) * TILE
    const = lambda b, t: (0, 0)
    tok_in = lambda b, t: (b, t + t0, 0)
    tok_out = lambda b, t: (b, t, 0)
    return pl.pallas_call(
        functools.partial(_outproj_kernel, n_stream=len(stream), t0=t0),
        grid=(batch // nb, nt - t0),
        in_specs=_stream_specs(stream, nb, lambda t: t + t0) + [
                  pl.BlockSpec((nb, TILE, Q_COLS), tok_in),
                  pl.BlockSpec(w.shape, const),
                  pl.BlockSpec((nb, 1, 8, d), lambda b, t: (b, jnp.minimum(t + t0, 1), 0, 0)),
                  pl.BlockSpec((1, d), const),
                  pl.BlockSpec((N_EXPERTS, d), const),
                  pl.BlockSpec((N_EXPERTS, d), const),
                  pl.BlockSpec((N_EXPERTS, 1), const)],
        out_specs=[pl.BlockSpec((nb, TILE, d), tok_out),
                   pl.BlockSpec((nb, TILE, d // 2), tok_out),
                   pl.BlockSpec((nb, TILE, LANES), tok_out)],
        out_shape=[jax.ShapeDtypeStruct((batch, t_out, d), F32),
                   jax.ShapeDtypeStruct((batch, t_out, d // 2), jnp.uint32),
                   jax.ShapeDtypeStruct((batch, t_out, LANES), F32)],
        compiler_params=pltpu.CompilerParams(dimension_semantics=("parallel", "parallel"),
                                             vmem_limit_bytes=VMEM_LIMIT),
        name="outproj",
    )(*stream, mix, w, mod, g, wrh, wrl, rb)


def _sc_workers():
    info = plsc.get_sparse_core_info()
    return info.num_cores, info.num_subcores


def _sc_dispatch(hpk, gates, pos2d, n_out):
    n, w = hpk.shape
    gw = gates.shape[1]
    nc, ns = _sc_workers()
    steps = n // (nc * ns * SC_WIN)
    mesh = plsc.VectorSubcoreMesh(core_axis_name="c", subcore_axis_name="s")

    @functools.partial(
        pl.kernel, mesh=mesh,
        out_type=(jax.ShapeDtypeStruct((n_out, w), hpk.dtype), jax.ShapeDtypeStruct((n_out, gw), gates.dtype)),
        scratch_types=[pltpu.VMEM((1, SC_WIN), jnp.int32), pltpu.VMEM((SC_WIN, w), hpk.dtype),
                       pltpu.VMEM((SC_WIN, gw), gates.dtype), pltpu.SemaphoreType.DMA, pltpu.SemaphoreType.DMA],
        name="sc_dispatch")
    def k(h_hbm, g_hbm, pos_hbm, hs_hbm, gs_hbm, idx_v, rows_v, grow_v, sem_in, sem_out):
        wid = lax.axis_index("s") * nc + lax.axis_index("c")

        @pl.loop(0, steps)
        def _(j):
            blk = wid * steps + j
            off = pl.multiple_of(blk * SC_WIN, SC_WIN)
            loads = [pltpu.async_copy(pos_hbm.at[pl.ds(blk, 1)], idx_v, sem_in),
                     pltpu.async_copy(h_hbm.at[pl.ds(off, SC_WIN)], rows_v, sem_in),
                     pltpu.async_copy(g_hbm.at[pl.ds(off, SC_WIN)], grow_v, sem_in)]
            for cp in loads:
                cp.wait()
            stores = [pltpu.async_copy(rows_v, hs_hbm.at[idx_v.at[0]], sem_out),
                      pltpu.async_copy(grow_v, gs_hbm.at[idx_v.at[0]], sem_out)]
            for cp in stores:
                cp.wait()

    return k(hpk, gates, pos2d)


def _sc_combine(ys, pos2d, n):
    w = ys.shape[1]
    nc, ns = _sc_workers()
    steps = n // (nc * ns * SC_WIN)
    mesh = plsc.VectorSubcoreMesh(core_axis_name="c", subcore_axis_name="s")

    @functools.partial(
        pl.kernel, mesh=mesh,
        out_type=jax.ShapeDtypeStruct((n, w), ys.dtype),
        scratch_types=[pltpu.VMEM((1, SC_WIN), jnp.int32), pltpu.VMEM((SC_WIN, w), ys.dtype)],
        name="sc_combine")
    def k(y_hbm, pos_hbm, o_hbm, idx_v, rows_v):
        wid = lax.axis_index("s") * nc + lax.axis_index("c")

        @pl.loop(0, steps)
        def _(j):
            blk = wid * steps + j
            off = pl.multiple_of(blk * SC_WIN, SC_WIN)
            pltpu.sync_copy(pos_hbm.at[pl.ds(blk, 1)], idx_v)
            pltpu.sync_copy(y_hbm.at[idx_v.at[0]], rows_v)
            pltpu.sync_copy(rows_v, o_hbm.at[pl.ds(off, SC_WIN)])

    return k(ys, pos2d)


PLAN_ROWS = 256
PLAN_BLOCK = 2048


def _plan_kernel(gates_ref, pos_ref, tg_ref, tk_ref, slots_ref, g_ref):
    step = pl.program_id(0)

    @pl.when(step == 0)
    def _():
        g_ref[...] = jnp.full(g_ref.shape, -1.0, F32)

    gates = gates_ref[...]
    lane = lax.broadcasted_iota(jnp.int32, gates.shape, 1)
    along_lanes = _dot_nt(jnp.ones((8, LANES), BF16), jnp.where(lane == GROUP_LANE, gates, 0.0).astype(BF16))
    for r in range(PLAN_BLOCK // LANES):
        g_ref[pl.ds(step * (PLAN_BLOCK // LANES) + r, 1), :] = along_lanes[0:1, r * LANES:(r + 1) * LANES]

    pl.when(step == pl.num_programs(0) - 1)(
        functools.partial(_plan_finish, g_ref, pos_ref, tg_ref, tk_ref, slots_ref))


def _plan_finish(g_ref, pos_ref, tg_ref, tk_ref, slots_ref):
    g = g_ref[...]
    rows = g.shape[0]
    n_groups = N_EXPERTS // EXPERTS_PER_GROUP
    before_lane = (lax.broadcasted_iota(jnp.int32, (LANES, LANES), 0)
                   < lax.broadcasted_iota(jnp.int32, (LANES, LANES), 1)).astype(BF16)
    before_row = (lax.broadcasted_iota(jnp.int32, (rows, rows), 1)
                  < lax.broadcasted_iota(jnp.int32, (rows, rows), 0)).astype(BF16)
    tile_row = lax.broadcasted_iota(jnp.int32, (1, LANES), 1).astype(F32) * FFN_TILE
    expert = lax.broadcasted_iota(jnp.int32, (N_EXPERTS, 1), 0)
    pos = jnp.zeros_like(g)
    start = jnp.zeros((1, 1), F32)
    tile_group = jnp.zeros((1, LANES), F32)
    need = jnp.zeros((N_EXPERTS, LANES), F32)
    for grp in range(n_groups):
        for k, (ea, eb) in enumerate(PAIRS):
            m = jnp.where(g == grp * len(PAIRS) + k, 1.0, 0.0)
            in_row = _dot(m.astype(BF16), before_lane)
            row_total = jnp.sum(m, axis=1, keepdims=True)
            rows_before = _dot(before_row, jnp.broadcast_to(row_total, m.shape).astype(BF16))
            pos = pos + m * (start + in_row + rows_before)
            end = start + jnp.sum(row_total, axis=0, keepdims=True)
            holds = jnp.where((tile_row < end) & (tile_row + FFN_TILE > start) & (end > start), 1.0, 0.0)
            uses = jnp.where((expert == grp * EXPERTS_PER_GROUP + ea) | (expert == grp * EXPERTS_PER_GROUP + eb),
                             1.0, 0.0)
            need = need + uses * holds
            start = end
        start = jnp.floor((start + (FFN_TILE - 1)) * (1.0 / FFN_TILE)) * FFN_TILE
        if grp < n_groups - 1:
            tile_group = tile_group + jnp.where(tile_row >= start, 1.0, 0.0)
    pos_ref[...] = pos.astype(jnp.int32)
    tg_ref[...] = tile_group.astype(jnp.int32)
    count = jnp.zeros((1, LANES), F32)
    slots = [tile_group * EXPERTS_PER_GROUP] * EXPERTS_PER_GROUP
    for e in range(N_EXPERTS):
        used = need[e:e + 1] > 0.0
        slots = [jnp.where(used & (count == k), float(e), slot) for k, slot in enumerate(slots)]
        count = count + jnp.where(used, 1.0, 0.0)
    tk_ref[...] = count.astype(jnp.int32)
    slots_ref[...] = jnp.concatenate(slots + [jnp.zeros((8 - EXPERTS_PER_GROUP, LANES), F32)], axis=0).astype(jnp.int32)


def _bucket_plan(gates, n_pad):
    n = gates.shape[0]
    whole = lambda shape: pl.BlockSpec(shape, lambda i: (0, 0))
    pos, tile_group, tile_k, slots = pl.pallas_call(
        _plan_kernel,
        grid=(n // PLAN_BLOCK,),
        in_specs=[pl.BlockSpec((PLAN_BLOCK, LANES), lambda i: (i, 0))],
        out_specs=[whole((PLAN_ROWS, LANES)), whole((1, LANES)), whole((1, LANES)), whole((8, LANES))],
        out_shape=[jax.ShapeDtypeStruct((PLAN_ROWS, LANES), jnp.int32),
                   jax.ShapeDtypeStruct((1, LANES), jnp.int32),
                   jax.ShapeDtypeStruct((1, LANES), jnp.int32),
                   jax.ShapeDtypeStruct((8, LANES), jnp.int32)],
        scratch_shapes=[pltpu.VMEM((PLAN_ROWS, LANES), F32)],
        compiler_params=pltpu.CompilerParams(dimension_semantics=("arbitrary",)),
        name="plan",
    )(gates)
    n_tiles = n_pad // FFN_TILE
    return (pos[:n // LANES].reshape(n // SC_WIN, SC_WIN), tile_group[0, :n_tiles], tile_k[0, :n_tiles],
            [slots[k, :n_tiles] for k in range(EXPERTS_PER_GROUP)])


def _ffn_kernel(tg_ref, tk_ref, s0_ref, s1_ref, s2_ref, s3_ref, hs_ref, gs_ref, wg_ref, wu_ref, wd_ref, ys_ref,
                wgb_ref, wub_ref, wdb_ref):
    i = pl.program_id(0)
    slot_refs = (s0_ref, s1_ref, s2_ref, s3_ref)

    @pl.when((i == 0) | (tg_ref[i] != tg_ref[jnp.maximum(i - 1, 0)]))
    def _():
        wgb_ref[...] = wg_ref[0].astype(BF16)
        wub_ref[...] = wu_ref[0].astype(BF16)
        wdb_ref[...] = wd_ref[0].astype(BF16)

    def run_experts(n_experts):
        def body():
            lo, hi = _unpack_bf16_pairs(hs_ref[...])
            lo = lo.astype(BF16)
            hi = hi.astype(BF16)
            half = lo.shape[1]
            gs = gs_ref[...]
            lane = lax.broadcasted_iota(jnp.int32, gs.shape, 1)
            y = None
            for k in range(n_experts):
                e = slot_refs[k][i]
                local = e - tg_ref[i] * EXPERTS_PER_GROUP
                wg, wu, wd = wgb_ref.at[local], wub_ref.at[local], wdb_ref.at[local]
                a = _dot(lo, wg[:half, :]) + _dot(hi, wg[half:, :])
                u = _dot(lo, wu[:half, :]) + _dot(hi, wu[half:, :])
                gate = jnp.sum(jnp.where(lane == e, gs, 0.0), axis=-1, keepdims=True)
                part = _dot(((a * _sigmoid(a)) * u * gate).astype(BF16), wd[...])
                y = part if y is None else y + part
            ys_ref[...] = _pack_bf16_pairs(y)
        return body

    for n_experts in range(2, EXPERTS_PER_GROUP + 1):
        pl.when(tk_ref[i] == n_experts)(run_experts(n_experts))


def _ffn(tile_group, tile_k, slots, hs, gs, wg, wu, wd, layer):
    n_pad, half = hs.shape
    grp = lambda i, tg, *_: (layer, tg[i], 0, 0)
    row = lambda i, *_: (i, 0)
    gshape = lambda w: (EXPERTS_PER_GROUP,) + w.shape[2:]
    wspec = lambda w: pl.BlockSpec((1,) + gshape(w), grp)
    return pl.pallas_call(
        _ffn_kernel,
        grid_spec=pltpu.PrefetchScalarGridSpec(
            num_scalar_prefetch=2 + EXPERTS_PER_GROUP,
            grid=(n_pad // FFN_TILE,),
            in_specs=[pl.BlockSpec((FFN_TILE, half), row),
                      pl.BlockSpec((FFN_TILE, LANES), row),
                      wspec(wg), wspec(wu), wspec(wd)],
            out_specs=pl.BlockSpec((FFN_TILE, half), row),
            scratch_shapes=[pltpu.VMEM(gshape(wg), BF16), pltpu.VMEM(gshape(wu), BF16),
                            pltpu.VMEM(gshape(wd), BF16)]),
        out_shape=jax.ShapeDtypeStruct((n_pad, half), jnp.uint32),
        compiler_params=pltpu.CompilerParams(dimension_semantics=("arbitrary",), vmem_limit_bytes=VMEM_LIMIT),
        name="ffn",
    )(tile_group, tile_k, *slots, hs, gs, wg, wu, wd)


def _residual_kernel(x_ref, y_ref, mod_ref, fg_ref, o_ref, *, final_norm):
    for i in range(x_ref.shape[0]):
        lo, hi = _unpack_bf16_pairs(y_ref[i])
        y = x_ref[i] + mod_ref[i, 0, 5:6, :] * jnp.concatenate([lo, hi], axis=1)
        if final_norm:
            y = _rms(y, fg_ref[...])
        o_ref[i] = y


def _residual(x1, yg, mod, fg, t0, final_norm):
    batch, t_out, d = x1.shape
    nb = PAIR
    tok = lambda b, t: (b, t, 0)
    return pl.pallas_call(
        functools.partial(_residual_kernel, final_norm=final_norm),
        grid=(batch // nb, t_out // TILE),
        in_specs=[pl.BlockSpec((nb, TILE, d), tok),
                  pl.BlockSpec((nb, TILE, d // 2), tok),
                  pl.BlockSpec((nb, 1, 8, d), lambda b, t: (b, jnp.minimum(t + t0, 1), 0, 0)),
                  pl.BlockSpec((1, d), lambda b, t: (0, 0))],
        out_specs=pl.BlockSpec((nb, TILE, d), tok),
        out_shape=jax.ShapeDtypeStruct((batch, t_out, d), F32),
        compiler_params=pltpu.CompilerParams(dimension_semantics=("parallel", "parallel"),
                                             vmem_limit_bytes=VMEM_LIMIT),
        name="residual",
    )(x1, yg, mod, fg)


def _moe(h2, gates, wg, wu, wd, layer):
    batch, t_out, half = h2.shape
    n = batch * t_out
    n_pad = n + (N_EXPERTS // EXPERTS_PER_GROUP) * FFN_TILE
    assert n % PLAN_BLOCK == 0 and n // LANES <= PLAN_ROWS and n_pad // FFN_TILE <= LANES
    gates = gates.reshape(n, LANES)
    pos2d, tile_group, tile_k, slots = _bucket_plan(gates, n_pad)
    hs, gs = _sc_dispatch(h2.reshape(n, half), gates, pos2d, n_pad)
    ys = _ffn(tile_group, tile_k, slots, hs, gs, wg, wu, wd, layer)
    return _sc_combine(ys, pos2d, n).reshape(batch, t_out, half)


def _rope_tables(seq, ctx_len):
    rows = seq // GRID_W
    row_pos = jnp.repeat(jnp.arange(rows, dtype=F32), GRID_W)
    col_pos = jnp.tile(jnp.arange(GRID_W, dtype=F32), rows)
    axis_dim = HEAD_DIM // 2
    inv_freq = ROPE_BASE ** (-jnp.arange(0, axis_dim, 2, dtype=F32) / axis_dim)
    ang_r = row_pos[:, None] * inv_freq[None, :]
    ang_c = col_pos[:, None] * inv_freq[None, :]
    z = jnp.zeros_like(ang_r)
    cos = jnp.concatenate([jnp.cos(ang_r)] * 2 + [jnp.cos(ang_c)] * 2, axis=-1)
    sin_a = jnp.concatenate([-jnp.sin(ang_r), z, -jnp.sin(ang_c), z], axis=-1)
    sin_b = jnp.concatenate([z, jnp.sin(ang_r), z, jnp.sin(ang_c)], axis=-1)

    def full(tab, fill):
        tab = jnp.tile(tab, (1, LANES // HEAD_DIM))
        return jnp.concatenate([jnp.full((ctx_len, LANES), fill, F32), tab], axis=0)

    return full(cos, 1.0), full(sin_a, 0.0), full(sin_b, 0.0)


def kernel(x, c, ctx, c_ctx, w_ada, b_ada, norm1_g, norm2_g, w_in, w_out, lam_q1, lam_k1, lam_q2, lam_k2,
           subln_g, q_norm_g, k_norm_g, sink, w_router, router_bias, w_gate, w_up, w_down, final_g):
    batch, seq, d = x.shape
    ctx_len = ctx.shape[1]
    depth = w_in.shape[0]
    sc_rows = SC_WIN * math.prod(_sc_workers())
    assert ctx_len == TILE and seq % TILE == 0 and batch % PAIR == 0 and batch <= 15
    assert (batch * seq) % sc_rows == 0 and (batch * (seq + ctx_len)) % sc_rows == 0

    cpad = jnp.zeros((16, d), F32).at[:batch].set(c).at[batch].set(c_ctx)
    mod_all = _adaln(cpad, w_ada, b_ada)
    cos, sin_a, sin_b = _rope_tables(seq, ctx_len)
    mseg = jnp.asarray(np.kron(np.eye(2), np.ones((HEAD_DIM, HEAD_DIM))), BF16)
    wr = w_router.T
    wrh = wr.astype(BF16)
    wrl = (wr - wrh.astype(F32)).astype(BF16)
    rb = router_bias.reshape(N_EXPERTS, 1)

    stream = (ctx, x)

    for l in range(depth):
        last = l == depth - 1
        lambda_init = 0.8 - 0.6 * math.exp(-0.3 * l)
        m6 = mod_all[l].reshape(16, 6, d)
        m8 = jnp.concatenate([m6, jnp.zeros((16, 2, d), F32)], axis=1)
        mod = jnp.stack([jnp.broadcast_to(m8[batch], (batch, 8, d)), m8[:batch]], axis=1)

        w_in_p = w_in[l].astype(BF16)
        w_out_p = w_out[l].astype(BF16)
        gq = jnp.tile(q_norm_g[l], 2).reshape(1, LANES)
        gk = jnp.tile(k_norm_g[l], 2).reshape(1, LANES)
        q, k, v = _inproj(stream, mod, norm1_g[l].reshape(1, d), w_in_p, cos, sin_a, sin_b, gq, gk, mseg)

        t0 = 1 if last else 0
        lamv = jnp.stack([lam_q1[l], lam_k1[l], lam_q2[l], lam_k2[l]])
        mix = _attention(sink[l], lamv, subln_g[l].reshape(1, LANES), q, k, v, t0, lambda_init)
        x1, h2, gates = _outproj(stream, mix, w_out_p, mod, norm2_g[l].reshape(1, d), wrh, wrl, rb, t0)
        stream = (x1, _moe(h2, gates, w_gate, w_up, w_down, l), mod)
    return _residual(*stream, final_g.reshape(1, d), 1, final_norm=True)
```

```python
import functools
import math

import numpy as np
import jax
import jax.numpy as jnp
from jax import lax
from jax.experimental import pallas as pl
from jax.experimental.pallas import tpu as pltpu
from jax.experimental.pallas import tpu_sc as plsc

F32 = jnp.float32
BF16 = jnp.bfloat16

HEAD_DIM = 64
GRID_W = 64
ROPE_BASE = 10000.0
RMS_EPS = 1e-6
WINDOW = 128
N_EXPERTS = 16
EXPERTS_PER_GROUP = 4
LANES = 128
TILE = 256
PAIR = 2
FFN_TILE = 512
SC_WIN = 64
GROUP_LANE = 16
PAIRS = ((0, 1), (0, 2), (0, 3), (1, 3), (1, 2), (2, 3))
NEG_BIG = -1e30
KEY_CHUNK = 256
VMEM_LIMIT = 52 * 1024 * 1024

QA, QB, QC = 512, 256, 256
KA, KB, KC = 512, 128, 128
Q_COLS = QA + QB + QC
K_COLS = KA + KB + KC
V_COLS = 768


def _dot(a, b):
    return jnp.dot(a, b, preferred_element_type=F32)


def _dot_nt(a, b):
    return lax.dot_general(a, b, (((1,), (1,)), ((), ())), preferred_element_type=F32)


def _split(a):
    hi = a.astype(BF16)
    lo = (a - hi.astype(F32)).astype(BF16)
    return hi, lo


def _dot3(a, b):
    ah, al = _split(a)
    bh, bl = _split(b)
    return _dot(ah, bh) + _dot(ah, bl) + _dot(al, bh)


def _rms(x, g):
    ms = jnp.mean(x * x, axis=-1, keepdims=True)
    return x * lax.rsqrt(ms + RMS_EPS) * g


def _sigmoid(x):
    return 1.0 / (1.0 + jnp.exp(-x))


def _pack_bf16_pairs(y):
    n = y.shape[1] // 2
    yb = y.astype(BF16).astype(F32)
    lo = lax.bitcast_convert_type(yb[:, :n], jnp.uint32) >> 16
    hi = lax.bitcast_convert_type(yb[:, n:], jnp.uint32) & jnp.uint32(0xFFFF0000)
    return hi | lo


def _unpack_bf16_pairs(w):
    lo = lax.bitcast_convert_type(w << 16, F32)
    hi = lax.bitcast_convert_type(w & jnp.uint32(0xFFFF0000), F32)
    return lo, hi


def _stream_dims(stream):
    batch, t_len, d = stream[0].shape
    return batch, (t_len + stream[1].shape[1] if len(stream) == 2 else t_len), d


def _stream_rows(stream_refs, i, tile):
    if len(stream_refs) == 2:
        ctx_ref, x_ref = stream_refs
        return jnp.where(tile == 0, ctx_ref[i], x_ref[i])
    x_ref, y_ref, mod_ref = stream_refs
    lo, hi = _unpack_bf16_pairs(y_ref[i])
    return x_ref[i] + mod_ref[i, 0, 5:6, :] * jnp.concatenate([lo, hi], axis=1)


def _stream_specs(stream, nb, tile_of):
    d = stream[0].shape[-1]
    if len(stream) == 2:
        return [pl.BlockSpec((nb, TILE, d), lambda b, t: (b, 0, 0)),
                pl.BlockSpec((nb, TILE, d), lambda b, t: (b, jnp.maximum(tile_of(t) - 1, 0), 0))]
    tok = lambda b, t: (b, tile_of(t), 0)
    return [pl.BlockSpec((nb, TILE, d), tok),
            pl.BlockSpec((nb, TILE, d // 2), tok),
            pl.BlockSpec((nb, 1, 8, d), lambda b, t: (b, jnp.minimum(tile_of(t), 1), 0, 0))]


def _adaln_kernel(c_ref, w_ref, b_ref, o_ref):
    cv = c_ref[...]
    s = cv * _sigmoid(cv)
    o_ref[0] = _dot3(s, w_ref[0]) + b_ref[0]


def _adaln(cpad, w_ada, b_ada):
    depth, d, n6 = w_ada.shape
    tn = 1536
    return pl.pallas_call(
        _adaln_kernel,
        grid=(depth, n6 // tn),
        in_specs=[pl.BlockSpec((16, d), lambda l, j: (0, 0)),
                  pl.BlockSpec((1, d, tn), lambda l, j: (l, 0, j)),
                  pl.BlockSpec((1, 1, tn), lambda l, j: (l, 0, j))],
        out_specs=pl.BlockSpec((1, 16, tn), lambda l, j: (l, 0, j)),
        out_shape=jax.ShapeDtypeStruct((depth, 16, n6), F32),
        compiler_params=pltpu.CompilerParams(dimension_semantics=("parallel", "parallel"),
                                             vmem_limit_bytes=VMEM_LIMIT),
        name="adaln",
    )(cpad, w_ada, b_ada.reshape(depth, 1, n6))


def _inproj_kernel(*refs, n_stream):
    for base in range(0, refs[0].shape[0], PAIR):
        _inproj_rows(base, pl.program_id(1), refs[:n_stream], *refs[n_stream:])


def _inproj_rows(base, tile, stream_refs, mod_ref, g_ref, w_ref, cos_ref, sa_ref, sb_ref, gq_ref, gk_ref, mseg_ref,
                 q_ref, k_ref, v_ref):
    members = range(base, base + PAIR)
    h = jnp.concatenate(
        [(_rms(_stream_rows(stream_refs, i, tile), g_ref[...]) * (1.0 + mod_ref[i, 0, 1:2, :])
          + mod_ref[i, 0, 0:1, :]).astype(BF16) for i in members], axis=0)
    cos = jnp.concatenate([cos_ref[...]] * PAIR, axis=0)
    sa = jnp.concatenate([sa_ref[...]] * PAIR, axis=0)
    sb = jnp.concatenate([sb_ref[...]] * PAIR, axis=0)
    mseg = mseg_ref[...]

    def put(ref, cols, val):
        for n, i in enumerate(members):
            ref[i, :, cols] = val[n * TILE:(n + 1) * TILE]

    def rope(b):
        return b * cos + pltpu.roll(b, LANES - 16, 1) * sa + pltpu.roll(b, 16, 1) * sb

    def qknorm(b, g):
        hi, lo = _split(b * b)
        ms = (_dot(hi, mseg) + _dot(lo, mseg)) * (1.0 / HEAD_DIM)
        return b * lax.rsqrt(ms + RMS_EPS) * g

    def wcols(*pieces):
        parts = [w_ref[:, a:a + n] for a, n in pieces]
        return parts[0] if len(parts) == 1 else jnp.concatenate(parts, axis=1)

    lane = lax.broadcasted_iota(jnp.int32, (PAIR * TILE, LANES), 1)

    def regroup(p2):
        h0, h1 = p2[:, :LANES], p2[:, LANES:]
        return (jnp.where(lane < HEAD_DIM, h0, pltpu.roll(h1, HEAD_DIM, 1)),
                jnp.where(lane < HEAD_DIM, pltpu.roll(h0, HEAD_DIM, 1), h1))

    q0, kv0 = 0, Q_COLS
    rope_chunks = [wcols((q0, 256)), wcols((q0 + 256, 256)), wcols((q0 + QA, QB)), wcols((q0 + QA + QB, QC)),
                   wcols((kv0, 256)), wcols((kv0 + 256, 256)),
                   wcols((kv0 + 2 * KA, KB), (kv0 + 2 * KA + 2 * KB, KC))]
    v_chunks = [wcols((kv0 + KA, 256)), wcols((kv0 + KA + 256, 256)),
                wcols((kv0 + 2 * KA + KB, KB), (kv0 + 2 * KA + 2 * KB + KC, KC))]
    for c, wc in enumerate(rope_chunks):
        p2 = _dot(h, wc)
        halves = regroup(p2) if c in (2, 3) else (p2[:, :LANES], p2[:, LANES:])
        for half in range(2):
            j = 2 * c + half
            p = halves[half]
            if j in (4, 5):
                p = qknorm(p, gq_ref[...])
            if j == 12:
                p = qknorm(p, gk_ref[...])
            p = rope(p)
            if j < Q_COLS // LANES:
                put(q_ref, slice(j * LANES, (j + 1) * LANES), (p * (HEAD_DIM ** -0.5)).astype(BF16))
            else:
                jk = j - Q_COLS // LANES
                put(k_ref, slice(jk * LANES, (jk + 1) * LANES), p.astype(BF16))
    for c, wc in enumerate(v_chunks):
        put(v_ref, slice(c * 256, (c + 1) * 256), _dot(h, wc).astype(BF16))


def _inproj(stream, mod, g, w, cos, sa, sb, gq, gk, mseg):
    batch, t_len, d = _stream_dims(stream)
    nt = t_len // TILE
    nb = 2 * PAIR
    row = lambda b, t: (t, 0)
    const = lambda b, t: (0, 0)
    tok = lambda b, t: (b, t, 0)
    return pl.pallas_call(
        functools.partial(_inproj_kernel, n_stream=len(stream)),
        grid=(batch // nb, nt),
        in_specs=_stream_specs(stream, nb, lambda t: t) + [
                  pl.BlockSpec((nb, 1, 8, d), lambda b, t: (b, jnp.minimum(t, 1), 0, 0)),
                  pl.BlockSpec((1, d), const),
                  pl.BlockSpec(w.shape, const),
                  pl.BlockSpec((TILE, LANES), row),
                  pl.BlockSpec((TILE, LANES), row),
                  pl.BlockSpec((TILE, LANES), row),
                  pl.BlockSpec((1, LANES), const),
                  pl.BlockSpec((1, LANES), const),
                  pl.BlockSpec((LANES, LANES), const)],
        out_specs=[pl.BlockSpec((nb, TILE, Q_COLS), tok),
                   pl.BlockSpec((nb, TILE, K_COLS), tok),
                   pl.BlockSpec((nb, TILE, V_COLS), tok)],
        out_shape=[jax.ShapeDtypeStruct((batch, t_len, Q_COLS), BF16),
                   jax.ShapeDtypeStruct((batch, t_len, K_COLS), BF16),
                   jax.ShapeDtypeStruct((batch, t_len, V_COLS), BF16)],
        compiler_params=pltpu.CompilerParams(dimension_semantics=("parallel", "parallel"),
                                             vmem_limit_bytes=VMEM_LIMIT),
        name="inproj",
    )(*stream, mod, g, w, cos, sa, sb, gq, gk, mseg)


def _scores_pass(qq, k_ref, bi, col, starts, s_ref, mask_fn=None, extra=None):
    half = KEY_CHUNK // 2
    rows = qq.shape[0]
    mrun = None
    for j, st in enumerate(starts):
        s = _dot_nt(qq, k_ref[bi, pl.ds(st, KEY_CHUNK), col:col + LANES])
        if mask_fn is not None:
            s = mask_fn(j, st, s)
        s_ref[0:rows, j * KEY_CHUNK:(j + 1) * KEY_CHUNK] = s
        mj = jnp.maximum(s[:, :half], s[:, half:])
        mrun = mj if mrun is None else jnp.maximum(mrun, mj)
    m = jnp.max(mrun, axis=-1, keepdims=True)
    return m if extra is None else jnp.maximum(m, extra)


def _values_pass(m, rows, v_ref, bi, col, starts, s_ref, extra=None):
    half = KEY_CHUNK // 2
    lrun = None
    acc = None
    for j, st in enumerate(starts):
        e = jnp.exp(s_ref[0:rows, j * KEY_CHUNK:(j + 1) * KEY_CHUNK] - m)
        lj = e[:, :half] + e[:, half:]
        lrun = lj if lrun is None else lrun + lj
        pv = _dot(e.astype(BF16), v_ref[bi, pl.ds(st, KEY_CHUNK), col:col + LANES])
        acc = pv if acc is None else acc + pv
    l = jnp.sum(lrun, axis=-1, keepdims=True)
    if extra is not None:
        l = l + jnp.exp(extra - m)
    return acc / l


def _softmax_pv(qq, k_ref, v_ref, bi, col, starts, s_ref, mask_fn=None, extra=None):
    m = _scores_pass(qq, k_ref, bi, col, starts, s_ref, mask_fn, extra)
    return _values_pass(m, qq.shape[0], v_ref, bi, col, starts, s_ref, extra)


def _all_chunks(n_keys):
    return [j * KEY_CHUNK for j in range(n_keys // KEY_CHUNK)]


def _half_masks(shape):
    lane = lax.broadcasted_iota(jnp.int32, shape, 1)
    return lane < HEAD_DIM, lane >= HEAD_DIM


def _stack_group_queries(q2, kv):
    lo_m, hi_m = _half_masks((TILE, LANES))
    m = lo_m if kv == 0 else hi_m
    zero = jnp.zeros((TILE, LANES), q2.dtype)
    return jnp.concatenate([jnp.where(m, q2[:, :LANES], zero), jnp.where(m, q2[:, LANES:], zero)], axis=0)


def _merge_kv_outputs(o_kv0, o_kv1):
    lo_m, _ = _half_masks((TILE, LANES))
    kv0 = jnp.where(lo_m, o_kv0[:TILE], pltpu.roll(o_kv0[TILE:], HEAD_DIM, 1))
    kv1 = jnp.where(lo_m, pltpu.roll(o_kv1[:TILE], HEAD_DIM, 1), o_kv1[TILE:])
    return jnp.concatenate([kv0, kv1], axis=1)


def _attn_kernel(sink_ref, lam_ref, subg_ref, qa_ref, qb_ref, qc_ref, ka_ref, kb_ref, kc_ref, va_ref, vb_ref, vc_ref,
                 o_ref, s_ref, *, t0, ctx_len, lambda_init):
    t = pl.program_id(1) + t0
    t_len = ka_ref.shape[1]
    band = 2 * TILE
    lo_m, hi_m = _half_masks((TILE, LANES))
    zero = jnp.zeros((TILE, LANES), BF16)
    row = lax.broadcasted_iota(jnp.int32, (2 * TILE, 1), 0)
    lv = lam_ref[...]
    lam = (jnp.exp(jnp.sum(lv[0:1] * lv[1:2], keepdims=True))
           - jnp.exp(jnp.sum(lv[2:3] * lv[3:4], keepdims=True)) + lambda_init)

    def sink_col(kv):
        return jnp.where(row < TILE, sink_ref[2 * kv], sink_ref[2 * kv + 1])

    def mixers(starts_ab, starts_c, mask_c):
        n_chain = 0
        for bi in range(qa_ref.shape[0]):
            for h in range(QA // LANES):
                q = qa_ref[bi, :, h * LANES:(h + 1) * LANES]
                qq = jnp.concatenate([jnp.where(lo_m, q, zero), jnp.where(hi_m, q, zero)], axis=0)
                o = _softmax_pv(qq, ka_ref, va_ref, bi, h * LANES, starts_ab, s_ref.at[n_chain % 2])
                n_chain += 1
                o = o[:TILE] - lam * o[TILE:]
                o_ref[bi, :, h * LANES:(h + 1) * LANES] = (
                    _rms(o, subg_ref[...]) * (1.0 - lambda_init)).astype(BF16)
            outs = []
            for kv in range(2):
                outs.append(_softmax_pv(_stack_group_queries(qb_ref[bi], kv), kb_ref, vb_ref, bi, 0, starts_ab,
                                        s_ref.at[n_chain % 2]))
                n_chain += 1
            o_ref[bi, :, QA:QA + QB] = _merge_kv_outputs(*outs).astype(BF16)
            outs = []
            for kv in range(2):
                outs.append(_softmax_pv(_stack_group_queries(qc_ref[bi], kv), kc_ref, vc_ref, bi, 0, starts_c,
                                        s_ref.at[n_chain % 2], mask_c, sink_col(kv)))
                n_chain += 1
            o_ref[bi, :, QA + QB:] = _merge_kv_outputs(*outs).astype(BF16)

    def ctx_step():
        mixers([0], [0], None)

    def latent_step():
        q_start = (t - 1) * TILE
        u0 = pl.multiple_of(jnp.minimum(q_start + ctx_len - WINDOW, t_len - band), WINDOW)
        q_pos = q_start + (lax.broadcasted_iota(jnp.int32, (2 * TILE, KEY_CHUNK), 0) & (TILE - 1))
        key_lane = lax.broadcasted_iota(jnp.int32, (2 * TILE, KEY_CHUNK), 1)

        def mask_c(j, st, s):
            if j == 0:
                return s
            k_pos = (st - ctx_len) + key_lane
            valid = (jnp.abs(k_pos - q_pos) <= WINDOW) & (k_pos >= 0)
            return jnp.where(valid, s, NEG_BIG)

        mixers(_all_chunks(t_len), [0] + [u0 + i * KEY_CHUNK for i in range(band // KEY_CHUNK)], mask_c)

    if t0 == 0:
        pl.when(t == 0)(ctx_step)
        pl.when(t > 0)(latent_step)
    else:
        latent_step()


def _attention(sink, lamv, subg, q, k, v, t0, lambda_init):
    batch, t_len, _ = q.shape
    nt = t_len // TILE
    kern = functools.partial(_attn_kernel, t0=t0, ctx_len=TILE, lambda_init=lambda_init)
    const = lambda b, t: (0, 0)
    nb = 1
    qspec = lambda width, blk: pl.BlockSpec((nb, TILE, width), lambda b, t: (b, t + t0, blk))
    kvspec = lambda width, blk: pl.BlockSpec((nb, t_len, width), lambda b, t: (b, 0, blk))
    return pl.pallas_call(
        kern,
        grid=(batch // nb, nt - t0),
        in_specs=[pl.BlockSpec(memory_space=pltpu.SMEM),
                  pl.BlockSpec(lamv.shape, const),
                  pl.BlockSpec((1, LANES), const),
                  qspec(QA, 0), qspec(QB, QA // QB), qspec(QC, (QA + QB) // QC),
                  kvspec(KA, 0), kvspec(KB, KA // KB), kvspec(KC, (KA + KB) // KC),
                  kvspec(KA, 0), kvspec(KB, KA // KB), kvspec(KC, (KA + KB) // KC)],
        out_specs=pl.BlockSpec((nb, TILE, Q_COLS), lambda b, t: (b, t + t0, 0)),
        out_shape=jax.ShapeDtypeStruct((batch, t_len, Q_COLS), BF16),
        scratch_shapes=[pltpu.VMEM((2, 2 * TILE, t_len), F32)],
        compiler_params=pltpu.CompilerParams(dimension_semantics=("parallel", "parallel"),
                                             vmem_limit_bytes=VMEM_LIMIT),
        name="attention",
    )(sink, lamv, subg, q, q, q, k, k, k, v, v, v)


def _route(scores, bias):
    expert = lax.broadcasted_iota(jnp.int32, scores.shape, 0)
    sel = scores + bias
    in_group = expert & (EXPERTS_PER_GROUP - 1)
    group = expert >> 2

    def neighbours(x, idx, step, span):
        for d in (1, 2, 3):
            fwd = (idx + d) < 4
            y = jnp.where(fwd, pltpu.roll(x, N_EXPERTS - d * step, 0), pltpu.roll(x, span - d * step, 0))
            yield y, fwd

    rank = jnp.zeros_like(sel)
    for y, fwd in neighbours(sel, in_group, 1, EXPERTS_PER_GROUP):
        beats = (y > sel) | ((y == sel) & jnp.logical_not(fwd))
        rank = rank + jnp.where(beats, 1.0, 0.0)
    top2 = jnp.where(rank < 2.0, sel, 0.0)
    gsum = top2
    for y, _ in neighbours(top2, in_group, 1, EXPERTS_PER_GROUP):
        gsum = gsum + y
    grank = jnp.zeros_like(sel)
    for y, fwd in neighbours(gsum, group, EXPERTS_PER_GROUP, N_EXPERTS):
        beats = (y > gsum) | ((y == gsum) & jnp.logical_not(fwd))
        grank = grank + jnp.where(beats, 1.0, 0.0)
    best = grank < 0.5
    chosen = best & (rank < 2.0)
    w = jnp.where(chosen, scores, 0.0)
    gates = w / jnp.sum(w, axis=0, keepdims=True)
    gid = jnp.sum(jnp.where(best & (in_group == 0), group.astype(F32), 0.0), axis=0, keepdims=True)
    bits = jnp.sum(jnp.where(chosen, jnp.left_shift(1, in_group).astype(F32), 0.0), axis=0, keepdims=True)
    pair = sum(jnp.where(bits == float((1 << a) + (1 << b)), float(k), 0.0) for k, (a, b) in enumerate(PAIRS))
    bucket = gid * len(PAIRS) + pair
    tokens = scores.shape[1]
    return jnp.concatenate([gates, jnp.broadcast_to(bucket, (8, tokens)),
                            jnp.zeros((LANES - N_EXPERTS - 8, tokens), F32)], axis=0)


def _outproj_kernel(*refs, n_stream, t0):
    stream_refs = refs[:n_stream]
    tile = pl.program_id(1) + t0
    mix_ref, w_ref, mod_ref, g_ref, wrh_ref, wrl_ref, rb_ref, x1_ref, h2_ref, gates_ref = refs[n_stream:]
    for i in range(mix_ref.shape[0]):
        x1 = _stream_rows(stream_refs, i, tile) + mod_ref[i, 0, 2:3, :] * _dot(mix_ref[i], w_ref[...])
        x1_ref[i] = x1
        h2 = _rms(x1, g_ref[...]) * (1.0 + mod_ref[i, 0, 4:5, :]) + mod_ref[i, 0, 3:4, :]
        h2_ref[i] = _pack_bf16_pairs(h2)
        hi, lo = _split(h2)
        logits = _dot_nt(wrh_ref[...], hi) + _dot_nt(wrl_ref[...], hi) + _dot_nt(wrh_ref[...], lo)
        gates_ref[i] = _route(_sigmoid(logits), rb_ref[...]).T


def _outproj(stream, mix, w, mod, g, wrh, wrl, rb, t0):
    batch, t_len, d = _stream_dims(stream)
    nt = t_len // TILE
    nb = 2 * PAIR
    t_out = (nt - t0) * TILE
    const = lambda b, t: (0, 0)
    tok_in = lambda b, t: (b, t + t0, 0)
    tok_out = lambda b, t: (b, t, 0)
    return pl.pallas_call(
        functools.partial(_outproj_kernel, n_stream=len(stream), t0=t0),
        grid=(batch // nb, nt - t0),
        in_specs=_stream_specs(stream, nb, lambda t: t + t0) + [
                  pl.BlockSpec((nb, TILE, Q_COLS), tok_in),
                  pl.BlockSpec(w.shape, const),
                  pl.BlockSpec((nb, 1, 8, d), lambda b, t: (b, jnp.minimum(t + t0, 1), 0, 0)),
                  pl.BlockSpec((1, d), const),
                  pl.BlockSpec((N_EXPERTS, d), const),
                  pl.BlockSpec((N_EXPERTS, d), const),
                  pl.BlockSpec((N_EXPERTS, 1), const)],
        out_specs=[pl.BlockSpec((nb, TILE, d), tok_out),
                   pl.BlockSpec((nb, TILE, d // 2), tok_out),
                   pl.BlockSpec((nb, TILE, LANES), tok_out)],
        out_shape=[jax.ShapeDtypeStruct((batch, t_out, d), F32),
                   jax.ShapeDtypeStruct((batch, t_out, d // 2), jnp.uint32),
                   jax.ShapeDtypeStruct((batch, t_out, LANES), F32)],
        compiler_params=pltpu.CompilerParams(dimension_semantics=("parallel", "parallel"),
                                             vmem_limit_bytes=VMEM_LIMIT),
        name="outproj",
    )(*stream, mix, w, mod, g, wrh, wrl, rb)


def _sc_workers():
    info = plsc.get_sparse_core_info()
    return info.num_cores, info.num_subcores


def _sc_dispatch(hpk, gates, pos2d, n_out):
    n, w = hpk.shape
    gw = gates.shape[1]
    nc, ns = _sc_workers()
    steps = n // (nc * ns * SC_WIN)
    mesh = plsc.VectorSubcoreMesh(core_axis_name="c", subcore_axis_name="s")

    @functools.partial(
        pl.kernel, mesh=mesh,
        out_type=(jax.ShapeDtypeStruct((n_out, w), hpk.dtype), jax.ShapeDtypeStruct((n_out, gw), gates.dtype)),
        scratch_types=[pltpu.VMEM((1, SC_WIN), jnp.int32), pltpu.VMEM((SC_WIN, w), hpk.dtype),
                       pltpu.VMEM((SC_WIN, gw), gates.dtype), pltpu.SemaphoreType.DMA, pltpu.SemaphoreType.DMA],
        name="sc_dispatch")
    def k(h_hbm, g_hbm, pos_hbm, hs_hbm, gs_hbm, idx_v, rows_v, grow_v, sem_in, sem_out):
        wid = lax.axis_index("s") * nc + lax.axis_index("c")

        @pl.loop(0, steps)
        def _(j):
            blk = wid * steps + j
            off = pl.multiple_of(blk * SC_WIN, SC_WIN)
            loads = [pltpu.async_copy(pos_hbm.at[pl.ds(blk, 1)], idx_v, sem_in),
                     pltpu.async_copy(h_hbm.at[pl.ds(off, SC_WIN)], rows_v, sem_in),
                     pltpu.async_copy(g_hbm.at[pl.ds(off, SC_WIN)], grow_v, sem_in)]
            for cp in loads:
                cp.wait()
            stores = [pltpu.async_copy(rows_v, hs_hbm.at[idx_v.at[0]], sem_out),
                      pltpu.async_copy(grow_v, gs_hbm.at[idx_v.at[0]], sem_out)]
            for cp in stores:
                cp.wait()

    return k(hpk, gates, pos2d)


def _sc_combine(ys, pos2d, n):
    w = ys.shape[1]
    nc, ns = _sc_workers()
    steps = n // (nc * ns * SC_WIN)
    mesh = plsc.VectorSubcoreMesh(core_axis_name="c", subcore_axis_name="s")

    @functools.partial(
        pl.kernel, mesh=mesh,
        out_type=jax.ShapeDtypeStruct((n, w), ys.dtype),
        scratch_types=[pltpu.VMEM((1, SC_WIN), jnp.int32), pltpu.VMEM((SC_WIN, w), ys.dtype)],
        name="sc_combine")
    def k(y_hbm, pos_hbm, o_hbm, idx_v, rows_v):
        wid = lax.axis_index("s") * nc + lax.axis_index("c")

        @pl.loop(0, steps)
        def _(j):
            blk = wid * steps + j
            off = pl.multiple_of(blk * SC_WIN, SC_WIN)
            pltpu.sync_copy(pos_hbm.at[pl.ds(blk, 1)], idx_v)
            pltpu.sync_copy(y_hbm.at[idx_v.at[0]], rows_v)
            pltpu.sync_copy(rows_v, o_hbm.at[pl.ds(off, SC_WIN)])

    return k(ys, pos2d)


PLAN_ROWS = 256
PLAN_BLOCK = 2048


def _plan_kernel(gates_ref, pos_ref, tg_ref, tk_ref, slots_ref, g_ref):
    step = pl.program_id(0)

    @pl.when(step == 0)
    def _():
        g_ref[...] = jnp.full(g_ref.shape, -1.0, F32)

    gates = gates_ref[...]
    lane = lax.broadcasted_iota(jnp.int32, gates.shape, 1)
    along_lanes = _dot_nt(jnp.ones((8, LANES), BF16), jnp.where(lane == GROUP_LANE, gates, 0.0).astype(BF16))
    for r in range(PLAN_BLOCK // LANES):
        g_ref[pl.ds(step * (PLAN_BLOCK // LANES) + r, 1), :] = along_lanes[0:1, r * LANES:(r + 1) * LANES]

    pl.when(step == pl.num_programs(0) - 1)(
        functools.partial(_plan_finish, g_ref, pos_ref, tg_ref, tk_ref, slots_ref))


def _plan_finish(g_ref, pos_ref, tg_ref, tk_ref, slots_ref):
    g = g_ref[...]
    rows = g.shape[0]
    n_groups = N_EXPERTS // EXPERTS_PER_GROUP
    before_lane = (lax.broadcasted_iota(jnp.int32, (LANES, LANES), 0)
                   < lax.broadcasted_iota(jnp.int32, (LANES, LANES), 1)).astype(BF16)
    before_row = (lax.broadcasted_iota(jnp.int32, (rows, rows), 1)
                  < lax.broadcasted_iota(jnp.int32, (rows, rows), 0)).astype(BF16)
    tile_row = lax.broadcasted_iota(jnp.int32, (1, LANES), 1).astype(F32) * FFN_TILE
    expert = lax.broadcasted_iota(jnp.int32, (N_EXPERTS, 1), 0)
    pos = jnp.zeros_like(g)
    start = jnp.zeros((1, 1), F32)
    tile_group = jnp.zeros((1, LANES), F32)
    need = jnp.zeros((N_EXPERTS, LANES), F32)
    for grp in range(n_groups):
        for k, (ea, eb) in enumerate(PAIRS):
            m = jnp.where(g == grp * len(PAIRS) + k, 1.0, 0.0)
            in_row = _dot(m.astype(BF16), before_lane)
            row_total = jnp.sum(m, axis=1, keepdims=True)
            rows_before = _dot(before_row, jnp.broadcast_to(row_total, m.shape).astype(BF16))
            pos = pos + m * (start + in_row + rows_before)
            end = start + jnp.sum(row_total, axis=0, keepdims=True)
            holds = jnp.where((tile_row < end) & (tile_row + FFN_TILE > start) & (end > start), 1.0, 0.0)
            uses = jnp.where((expert == grp * EXPERTS_PER_GROUP + ea) | (expert == grp * EXPERTS_PER_GROUP + eb),
                             1.0, 0.0)
            need = need + uses * holds
            start = end
        start = jnp.floor((start + (FFN_TILE - 1)) * (1.0 / FFN_TILE)) * FFN_TILE
        if grp < n_groups - 1:
            tile_group = tile_group + jnp.where(tile_row >= start, 1.0, 0.0)
    pos_ref[...] = pos.astype(jnp.int32)
    tg_ref[...] = tile_group.astype(jnp.int32)
    count = jnp.zeros((1, LANES), F32)
    slots = [tile_group * EXPERTS_PER_GROUP] * EXPERTS_PER_GROUP
    for e in range(N_EXPERTS):
        used = need[e:e + 1] > 0.0
        slots = [jnp.where(used & (count == k), float(e), slot) for k, slot in enumerate(slots)]
        count = count + jnp.where(used, 1.0, 0.0)
    tk_ref[...] = count.astype(jnp.int32)
    slots_ref[...] = jnp.concatenate(slots + [jnp.zeros((8 - EXPERTS_PER_GROUP, LANES), F32)], axis=0).astype(jnp.int32)


def _bucket_plan(gates, n_pad):
    n = gates.shape[0]
    whole = lambda shape: pl.BlockSpec(shape, lambda i: (0, 0))
    pos, tile_group, tile_k, slots = pl.pallas_call(
        _plan_kernel,
        grid=(n // PLAN_BLOCK,),
        in_specs=[pl.BlockSpec((PLAN_BLOCK, LANES), lambda i: (i, 0))],
        out_specs=[whole((PLAN_ROWS, LANES)), whole((1, LANES)), whole((1, LANES)), whole((8, LANES))],
        out_shape=[jax.ShapeDtypeStruct((PLAN_ROWS, LANES), jnp.int32),
                   jax.ShapeDtypeStruct((1, LANES), jnp.int32),
                   jax.ShapeDtypeStruct((1, LANES), jnp.int32),
                   jax.ShapeDtypeStruct((8, LANES), jnp.int32)],
        scratch_shapes=[pltpu.VMEM((PLAN_ROWS, LANES), F32)],
        compiler_params=pltpu.CompilerParams(dimension_semantics=("arbitrary",)),
        name="plan",
    )(gates)
    n_tiles = n_pad // FFN_TILE
    return (pos[:n // LANES].reshape(n // SC_WIN, SC_WIN), tile_group[0, :n_tiles], tile_k[0, :n_tiles],
            [slots[k, :n_tiles] for k in range(EXPERTS_PER_GROUP)])


def _ffn_kernel(tg_ref, tk_ref, s0_ref, s1_ref, s2_ref, s3_ref, hs_ref, gs_ref, wg_ref, wu_ref, wd_ref, ys_ref,
                wgb_ref, wub_ref, wdb_ref):
    i = pl.program_id(0)
    slot_refs = (s0_ref, s1_ref, s2_ref, s3_ref)

    @pl.when((i == 0) | (tg_ref[i] != tg_ref[jnp.maximum(i - 1, 0)]))
    def _():
        wgb_ref[...] = wg_ref[0].astype(BF16)
        wub_ref[...] = wu_ref[0].astype(BF16)
        wdb_ref[...] = wd_ref[0].astype(BF16)

    def run_experts(n_experts):
        def body():
            lo, hi = _unpack_bf16_pairs(hs_ref[...])
            lo = lo.astype(BF16)
            hi = hi.astype(BF16)
            half = lo.shape[1]
            gs = gs_ref[...]
            lane = lax.broadcasted_iota(jnp.int32, gs.shape, 1)
            y = None
            for k in range(n_experts):
                e = slot_refs[k][i]
                local = e - tg_ref[i] * EXPERTS_PER_GROUP
                wg, wu, wd = wgb_ref.at[local], wub_ref.at[local], wdb_ref.at[local]
                a = _dot(lo, wg[:half, :]) + _dot(hi, wg[half:, :])
                u = _dot(lo, wu[:half, :]) + _dot(hi, wu[half:, :])
                gate = jnp.sum(jnp.where(lane == e, gs, 0.0), axis=-1, keepdims=True)
                part = _dot(((a * _sigmoid(a)) * u * gate).astype(BF16), wd[...])
                y = part if y is None else y + part
            ys_ref[...] = _pack_bf16_pairs(y)
        return body

    for n_experts in range(2, EXPERTS_PER_GROUP + 1):
        pl.when(tk_ref[i] == n_experts)(run_experts(n_experts))


def _ffn(tile_group, tile_k, slots, hs, gs, wg, wu, wd, layer):
    n_pad, half = hs.shape
    grp = lambda i, tg, *_: (layer, tg[i], 0, 0)
    row = lambda i, *_: (i, 0)
    gshape = lambda w: (EXPERTS_PER_GROUP,) + w.shape[2:]
    wspec = lambda w: pl.BlockSpec((1,) + gshape(w), grp)
    return pl.pallas_call(
        _ffn_kernel,
        grid_spec=pltpu.PrefetchScalarGridSpec(
            num_scalar_prefetch=2 + EXPERTS_PER_GROUP,
            grid=(n_pad // FFN_TILE,),
            in_specs=[pl.BlockSpec((FFN_TILE, half), row),
                      pl.BlockSpec((FFN_TILE, LANES), row),
                      wspec(wg), wspec(wu), wspec(wd)],
            out_specs=pl.BlockSpec((FFN_TILE, half), row),
            scratch_shapes=[pltpu.VMEM(gshape(wg), BF16), pltpu.VMEM(gshape(wu), BF16),
                            pltpu.VMEM(gshape(wd), BF16)]),
        out_shape=jax.ShapeDtypeStruct((n_pad, half), jnp.uint32),
        compiler_params=pltpu.CompilerParams(dimension_semantics=("arbitrary",), vmem_limit_bytes=VMEM_LIMIT),
        name="ffn",
    )(tile_group, tile_k, *slots, hs, gs, wg, wu, wd)


def _residual_kernel(x_ref, y_ref, mod_ref, fg_ref, o_ref, *, final_norm):
    for i in range(x_ref.shape[0]):
        lo, hi = _unpack_bf16_pairs(y_ref[i])
        y = x_ref[i] + mod_ref[i, 0, 5:6, :] * jnp.concatenate([lo, hi], axis=1)
        if final_norm:
            y = _rms(y, fg_ref[...])
        o_ref[i] = y


def _residual(x1, yg, mod, fg, t0, final_norm):
    batch, t_out, d = x1.shape
    nb = PAIR
    tok = lambda b, t: (b, t, 0)
    return pl.pallas_call(
        functools.partial(_residual_kernel, final_norm=final_norm),
        grid=(batch // nb, t_out // TILE),
        in_specs=[pl.BlockSpec((nb, TILE, d), tok),
                  pl.BlockSpec((nb, TILE, d // 2), tok),
                  pl.BlockSpec((nb, 1, 8, d), lambda b, t: (b, jnp.minimum(t + t0, 1), 0, 0)),
                  pl.BlockSpec((1, d), lambda b, t: (0, 0))],
        out_specs=pl.BlockSpec((nb, TILE, d), tok),
        out_shape=jax.ShapeDtypeStruct((batch, t_out, d), F32),
        compiler_params=pltpu.CompilerParams(dimension_semantics=("parallel", "parallel"),
                                             vmem_limit_bytes=VMEM_LIMIT),
        name="residual",
    )(x1, yg, mod, fg)


def _moe(h2, gates, wg, wu, wd, layer):
    batch, t_out, half = h2.shape
    n = batch * t_out
    n_pad = n + (N_EXPERTS // EXPERTS_PER_GROUP) * FFN_TILE
    assert n % PLAN_BLOCK == 0 and n // LANES <= PLAN_ROWS and n_pad // FFN_TILE <= LANES
    gates = gates.reshape(n, LANES)
    pos2d, tile_group, tile_k, slots = _bucket_plan(gates, n_pad)
    hs, gs = _sc_dispatch(h2.reshape(n, half), gates, pos2d, n_pad)
    ys = _ffn(tile_group, tile_k, slots, hs, gs, wg, wu, wd, layer)
    return _sc_combine(ys, pos2d, n).reshape(batch, t_out, half)


def _rope_tables(seq, ctx_len):
    rows = seq // GRID_W
    row_pos = jnp.repeat(jnp.arange(rows, dtype=F32), GRID_W)
    col_pos = jnp.tile(jnp.arange(GRID_W, dtype=F32), rows)
    axis_dim = HEAD_DIM // 2
    inv_freq = ROPE_BASE ** (-jnp.arange(0, axis_dim, 2, dtype=F32) / axis_dim)
    ang_r = row_pos[:, None] * inv_freq[None, :]
    ang_c = col_pos[:, None] * inv_freq[None, :]
    z = jnp.zeros_like(ang_r)
    cos = jnp.concatenate([jnp.cos(ang_r)] * 2 + [jnp.cos(ang_c)] * 2, axis=-1)
    sin_a = jnp.concatenate([-jnp.sin(ang_r), z, -jnp.sin(ang_c), z], axis=-1)
    sin_b = jnp.concatenate([z, jnp.sin(ang_r), z, jnp.sin(ang_c)], axis=-1)

    def full(tab, fill):
        tab = jnp.tile(tab, (1, LANES // HEAD_DIM))
        return jnp.concatenate([jnp.full((ctx_len, LANES), fill, F32), tab], axis=0)

    return full(cos, 1.0), full(sin_a, 0.0), full(sin_b, 0.0)


def kernel(x, c, ctx, c_ctx, w_ada, b_ada, norm1_g, norm2_g, w_in, w_out, lam_q1, lam_k1, lam_q2, lam_k2,
           subln_g, q_norm_g, k_norm_g, sink, w_router, router_bias, w_gate, w_up, w_down, final_g):
    batch, seq, d = x.shape
    ctx_len = ctx.shape[1]
    depth = w_in.shape[0]
    sc_rows = SC_WIN * math.prod(_sc_workers())
    assert ctx_len == TILE and seq % TILE == 0 and batch % PAIR == 0 and batch <= 15
    assert (batch * seq) % sc_rows == 0 and (batch * (seq + ctx_len)) % sc_rows == 0

    cpad = jnp.zeros((16, d), F32).at[:batch].set(c).at[batch].set(c_ctx)
    mod_all = _adaln(cpad, w_ada, b_ada)
    cos, sin_a, sin_b = _rope_tables(seq, ctx_len)
    mseg = jnp.asarray(np.kron(np.eye(2), np.ones((HEAD_DIM, HEAD_DIM))), BF16)
    wr = w_router.T
    wrh = wr.astype(BF16)
    wrl = (wr - wrh.astype(F32)).astype(BF16)
    rb = router_bias.reshape(N_EXPERTS, 1)

    stream = (ctx, x)

    for l in range(depth):
        last = l == depth - 1
        lambda_init = 0.8 - 0.6 * math.exp(-0.3 * l)
        m6 = mod_all[l].reshape(16, 6, d)
        m8 = jnp.concatenate([m6, jnp.zeros((16, 2, d), F32)], axis=1)
        mod = jnp.stack([jnp.broadcast_to(m8[batch], (batch, 8, d)), m8[:batch]], axis=1)

        w_in_p = w_in[l].astype(BF16)
        w_out_p = w_out[l].astype(BF16)
        gq = jnp.tile(q_norm_g[l], 2).reshape(1, LANES)
        gk = jnp.tile(k_norm_g[l], 2).reshape(1, LANES)
        q, k, v = _inproj(stream, mod, norm1_g[l].reshape(1, d), w_in_p, cos, sin_a, sin_b, gq, gk, mseg)

        t0 = 1 if last else 0
        lamv = jnp.stack([lam_q1[l], lam_k1[l], lam_q2[l], lam_k2[l]])
        mix = _attention(sink[l], lamv, subln_g[l].reshape(1, LANES), q, k, v, t0, lambda_init)
        x1, h2, gates = _outproj(stream, mix, w_out_p, mod, norm2_g[l].reshape(1, d), wrh, wrl, rb, t0)
        stream = (x1, _moe(h2, gates, w_gate, w_up, w_down, l), mod)
    return _residual(*stream, final_g.reshape(1, d), 1, final_norm=True)
```

```python
import functools
import math

import numpy as np
import jax
import jax.numpy as jnp
from jax import lax
from jax.experimental import pallas as pl
from jax.experimental.pallas import tpu as pltpu
from jax.experimental.pallas import tpu_sc as plsc

F32 = jnp.float32
BF16 = jnp.bfloat16

HEAD_DIM = 64
GRID_W = 64
ROPE_BASE = 10000.0
RMS_EPS = 1e-6
WINDOW = 128
N_EXPERTS = 16
EXPERTS_PER_GROUP = 4
LANES = 128
TILE = 256
PAIR = 2
FFN_TILE = 512
SC_WIN = 64
GROUP_LANE = 16
PAIRS = ((0, 1), (0, 2), (0, 3), (1, 3), (1, 2), (2, 3))
NEG_BIG = -1e30
KEY_CHUNK = 256
VMEM_LIMIT = 52 * 1024 * 1024

QA, QB, QC = 512, 256, 256
KA, KB, KC = 512, 128, 128
Q_COLS = QA + QB + QC
K_COLS = KA + KB + KC
V_COLS = 768


def _dot(a, b):
    return jnp.dot(a, b, preferred_element_type=F32)


def _dot_nt(a, b):
    return lax.dot_general(a, b, (((1,), (1,)), ((), ())), preferred_element_type=F32)


def _split(a):
    hi = a.astype(BF16)
    lo = (a - hi.astype(F32)).astype(BF16)
    return hi, lo


def _dot3(a, b):
    ah, al = _split(a)
    bh, bl = _split(b)
    return _dot(ah, bh) + _dot(ah, bl) + _dot(al, bh)


def _rms(x, g):
    ms = jnp.mean(x * x, axis=-1, keepdims=True)
    return x * lax.rsqrt(ms + RMS_EPS) * g


def _sigmoid(x):
    return 1.0 / (1.0 + jnp.exp(-x))


def _pack_bf16_pairs(y):
    n = y.shape[1] // 2
    yb = y.astype(BF16).astype(F32)
    lo = lax.bitcast_convert_type(yb[:, :n], jnp.uint32) >> 16
    hi = lax.bitcast_convert_type(yb[:, n:], jnp.uint32) & jnp.uint32(0xFFFF0000)
    return hi | lo


def _unpack_bf16_pairs(w):
    lo = lax.bitcast_convert_type(w << 16, F32)
    hi = lax.bitcast_convert_type(w & jnp.uint32(0xFFFF0000), F32)
    return lo, hi


def _stream_dims(stream):
    batch, t_len, d = stream[0].shape
    return batch, (t_len + stream[1].shape[1] if len(stream) == 2 else t_len), d


def _stream_rows(stream_refs, i, tile):
    if len(stream_refs) == 2:
        ctx_ref, x_ref = stream_refs
        return jnp.where(tile == 0, ctx_ref[i], x_ref[i])
    x_ref, y_ref, mod_ref = stream_refs
    lo, hi = _unpack_bf16_pairs(y_ref[i])
    return x_ref[i] + mod_ref[i, 0, 5:6, :] * jnp.concatenate([lo, hi], axis=1)


def _stream_specs(stream, nb, tile_of):
    d = stream[0].shape[-1]
    if len(stream) == 2:
        return [pl.BlockSpec((nb, TILE, d), lambda b, t: (b, 0, 0)),
                pl.BlockSpec((nb, TILE, d), lambda b, t: (b, jnp.maximum(tile_of(t) - 1, 0), 0))]
    tok = lambda b, t: (b, tile_of(t), 0)
    return [pl.BlockSpec((nb, TILE, d), tok),
            pl.BlockSpec((nb, TILE, d // 2), tok),
            pl.BlockSpec((nb, 1, 8, d), lambda b, t: (b, jnp.minimum(tile_of(t), 1), 0, 0))]


def _adaln_kernel(c_ref, w_ref, b_ref, o_ref):
    cv = c_ref[...]
    s = cv * _sigmoid(cv)
    o_ref[0] = _dot3(s, w_ref[0]) + b_ref[0]


def _adaln(cpad, w_ada, b_ada):
    depth, d, n6 = w_ada.shape
    tn = 1536
    return pl.pallas_call(
        _adaln_kernel,
        grid=(depth, n6 // tn),
        in_specs=[pl.BlockSpec((16, d), lambda l, j: (0, 0)),
                  pl.BlockSpec((1, d, tn), lambda l, j: (l, 0, j)),
                  pl.BlockSpec((1, 1, tn), lambda l, j: (l, 0, j))],
        out_specs=pl.BlockSpec((1, 16, tn), lambda l, j: (l, 0, j)),
        out_shape=jax.ShapeDtypeStruct((depth, 16, n6), F32),
        compiler_params=pltpu.CompilerParams(dimension_semantics=("parallel", "parallel"),
                                             vmem_limit_bytes=VMEM_LIMIT),
        name="adaln",
    )(cpad, w_ada, b_ada.reshape(depth, 1, n6))


def _inproj_kernel(*refs, n_stream):
    for base in range(0, refs[0].shape[0], PAIR):
        _inproj_rows(base, pl.program_id(1), refs[:n_stream], *refs[n_stream:])


def _inproj_rows(base, tile, stream_refs, mod_ref, g_ref, w_ref, cos_ref, sa_ref, sb_ref, gq_ref, gk_ref, mseg_ref,
                 q_ref, k_ref, v_ref):
    members = range(base, base + PAIR)
    h = jnp.concatenate(
        [(_rms(_stream_rows(stream_refs, i, tile), g_ref[...]) * (1.0 + mod_ref[i, 0, 1:2, :])
          + mod_ref[i, 0, 0:1, :]).astype(BF16) for i in members], axis=0)
    cos = jnp.concatenate([cos_ref[...]] * PAIR, axis=0)
    sa = jnp.concatenate([sa_ref[...]] * PAIR, axis=0)
    sb = jnp.concatenate([sb_ref[...]] * PAIR, axis=0)
    mseg = mseg_ref[...]

    def put(ref, cols, val):
        for n, i in enumerate(members):
            ref[i, :, cols] = val[n * TILE:(n + 1) * TILE]

    def rope(b):
        return b * cos + pltpu.roll(b, LANES - 16, 1) * sa + pltpu.roll(b, 16, 1) * sb

    def qknorm(b, g):
        hi, lo = _split(b * b)
        ms = (_dot(hi, mseg) + _dot(lo, mseg)) * (1.0 / HEAD_DIM)
        return b * lax.rsqrt(ms + RMS_EPS) * g

    def wcols(*pieces):
        parts = [w_ref[:, a:a + n] for a, n in pieces]
        return parts[0] if len(parts) == 1 else jnp.concatenate(parts, axis=1)

    lane = lax.broadcasted_iota(jnp.int32, (PAIR * TILE, LANES), 1)

    def regroup(p2):
        h0, h1 = p2[:, :LANES], p2[:, LANES:]
        return (jnp.where(lane < HEAD_DIM, h0, pltpu.roll(h1, HEAD_DIM, 1)),
                jnp.where(lane < HEAD_DIM, pltpu.roll(h0, HEAD_DIM, 1), h1))

    q0, kv0 = 0, Q_COLS
    rope_chunks = [wcols((q0, 256)), wcols((q0 + 256, 256)), wcols((q0 + QA, QB)), wcols((q0 + QA + QB, QC)),
                   wcols((kv0, 256)), wcols((kv0 + 256, 256)),
                   wcols((kv0 + 2 * KA, KB), (kv0 + 2 * KA + 2 * KB, KC))]
    v_chunks = [wcols((kv0 + KA, 256)), wcols((kv0 + KA + 256, 256)),
                wcols((kv0 + 2 * KA + KB, KB), (kv0 + 2 * KA + 2 * KB + KC, KC))]
    for c, wc in enumerate(rope_chunks):
        p2 = _dot(h, wc)
        halves = regroup(p2) if c in (2, 3) else (p2[:, :LANES], p2[:, LANES:])
        for half in range(2):
            j = 2 * c + half
            p = halves[half]
            if j in (4, 5):
                p = qknorm(p, gq_ref[...])
            if j == 12:
                p = qknorm(p, gk_ref[...])
            p = rope(p)
            if j < Q_COLS // LANES:
                put(q_ref, slice(j * LANES, (j + 1) * LANES), (p * (HEAD_DIM ** -0.5)).astype(BF16))
            else:
                jk = j - Q_COLS // LANES
                put(k_ref, slice(jk * LANES, (jk + 1) * LANES), p.astype(BF16))
    for c, wc in enumerate(v_chunks):
        put(v_ref, slice(c * 256, (c + 1) * 256), _dot(h, wc).astype(BF16))


def _inproj(stream, mod, g, w, cos, sa, sb, gq, gk, mseg):
    batch, t_len, d = _stream_dims(stream)
    nt = t_len // TILE
    nb = 2 * PAIR
    row = lambda b, t: (t, 0)
    const = lambda b, t: (0, 0)
    tok = lambda b, t: (b, t, 0)
    return pl.pallas_call(
        functools.partial(_inproj_kernel, n_stream=len(stream)),
        grid=(batch // nb, nt),
        in_specs=_stream_specs(stream, nb, lambda t: t) + [
                  pl.BlockSpec((nb, 1, 8, d), lambda b, t: (b, jnp.minimum(t, 1), 0, 0)),
                  pl.BlockSpec((1, d), const),
                  pl.BlockSpec(w.shape, const),
                  pl.BlockSpec((TILE, LANES), row),
                  pl.BlockSpec((TILE, LANES), row),
                  pl.BlockSpec((TILE, LANES), row),
                  pl.BlockSpec((1, LANES), const),
                  pl.BlockSpec((1, LANES), const),
                  pl.BlockSpec((LANES, LANES), const)],
        out_specs=[pl.BlockSpec((nb, TILE, Q_COLS), tok),
                   pl.BlockSpec((nb, TILE, K_COLS), tok),
                   pl.BlockSpec((nb, TILE, V_COLS), tok)],
        out_shape=[jax.ShapeDtypeStruct((batch, t_len, Q_COLS), BF16),
                   jax.ShapeDtypeStruct((batch, t_len, K_COLS), BF16),
                   jax.ShapeDtypeStruct((batch, t_len, V_COLS), BF16)],
        compiler_params=pltpu.CompilerParams(dimension_semantics=("parallel", "parallel"),
                                             vmem_limit_bytes=VMEM_LIMIT),
        name="inproj",
    )(*stream, mod, g, w, cos, sa, sb, gq, gk, mseg)


def _scores_pass(qq, k_ref, bi, col, starts, s_ref, mask_fn=None, extra=None):
    half = KEY_CHUNK // 2
    rows = qq.shape[0]
    mrun = None
    for j, st in enumerate(starts):
        s = _dot_nt(qq, k_ref[bi, pl.ds(st, KEY_CHUNK), col:col + LANES])
        if mask_fn is not None:
            s = mask_fn(j, st, s)
        s_ref[0:rows, j * KEY_CHUNK:(j + 1) * KEY_CHUNK] = s
        mj = jnp.maximum(s[:, :half], s[:, half:])
        mrun = mj if mrun is None else jnp.maximum(mrun, mj)
    m = jnp.max(mrun, axis=-1, keepdims=True)
    return m if extra is None else jnp.maximum(m, extra)


def _values_pass(m, rows, v_ref, bi, col, starts, s_ref, extra=None):
    half = KEY_CHUNK // 2
    lrun = None
    acc = None
    for j, st in enumerate(starts):
        e = jnp.exp(s_ref[0:rows, j * KEY_CHUNK:(j + 1) * KEY_CHUNK] - m)
        lj = e[:, :half] + e[:, half:]
        lrun = lj if lrun is None else lrun + lj
        pv = _dot(e.astype(BF16), v_ref[bi, pl.ds(st, KEY_CHUNK), col:col + LANES])
        acc = pv if acc is None else acc + pv
    l = jnp.sum(lrun, axis=-1, keepdims=True)
    if extra is not None:
        l = l + jnp.exp(extra - m)
    return acc / l


def _softmax_pv(qq, k_ref, v_ref, bi, col, starts, s_ref, mask_fn=None, extra=None):
    m = _scores_pass(qq, k_ref, bi, col, starts, s_ref, mask_fn, extra)
    return _values_pass(m, qq.shape[0], v_ref, bi, col, starts, s_ref, extra)


def _all_chunks(n_keys):
    return [j * KEY_CHUNK for j in range(n_keys // KEY_CHUNK)]


def _half_masks(shape):
    lane = lax.broadcasted_iota(jnp.int32, shape, 1)
    return lane < HEAD_DIM, lane >= HEAD_DIM


def _stack_group_queries(q2, kv):
    lo_m, hi_m = _half_masks((TILE, LANES))
    m = lo_m if kv == 0 else hi_m
    zero = jnp.zeros((TILE, LANES), q2.dtype)
    return jnp.concatenate([jnp.where(m, q2[:, :LANES], zero), jnp.where(m, q2[:, LANES:], zero)], axis=0)


def _merge_kv_outputs(o_kv0, o_kv1):
    lo_m, _ = _half_masks((TILE, LANES))
    kv0 = jnp.where(lo_m, o_kv0[:TILE], pltpu.roll(o_kv0[TILE:], HEAD_DIM, 1))
    kv1 = jnp.where(lo_m, pltpu.roll(o_kv1[:TILE], HEAD_DIM, 1), o_kv1[TILE:])
    return jnp.concatenate([kv0, kv1], axis=1)


def _attn_kernel(sink_ref, lam_ref, subg_ref, qa_ref, qb_ref, qc_ref, ka_ref, kb_ref, kc_ref, va_ref, vb_ref, vc_ref,
                 o_ref, s_ref, *, t0, ctx_len, lambda_init):
    t = pl.program_id(1) + t0
    t_len = ka_ref.shape[1]
    band = 2 * TILE
    lo_m, hi_m = _half_masks((TILE, LANES))
    zero = jnp.zeros((TILE, LANES), BF16)
    row = lax.broadcasted_iota(jnp.int32, (2 * TILE, 1), 0)
    lv = lam_ref[...]
    lam = (jnp.exp(jnp.sum(lv[0:1] * lv[1:2], keepdims=True))
           - jnp.exp(jnp.sum(lv[2:3] * lv[3:4], keepdims=True)) + lambda_init)

    def sink_col(kv):
        return jnp.where(row < TILE, sink_ref[2 * kv], sink_ref[2 * kv + 1])

    def mixers(starts_ab, starts_c, mask_c):
        n_chain = 0
        for bi in range(qa_ref.shape[0]):
            for h in range(QA // LANES):
                q = qa_ref[bi, :, h * LANES:(h + 1) * LANES]
                qq = jnp.concatenate([jnp.where(lo_m, q, zero), jnp.where(hi_m, q, zero)], axis=0)
                o = _softmax_pv(qq, ka_ref, va_ref, bi, h * LANES, starts_ab, s_ref.at[n_chain % 2])
                n_chain += 1
                o = o[:TILE] - lam * o[TILE:]
                o_ref[bi, :, h * LANES:(h + 1) * LANES] = (
                    _rms(o, subg_ref[...]) * (1.0 - lambda_init)).astype(BF16)
            outs = []
            for kv in range(2):
                outs.append(_softmax_pv(_stack_group_queries(qb_ref[bi], kv), kb_ref, vb_ref, bi, 0, starts_ab,
                                        s_ref.at[n_chain % 2]))
                n_chain += 1
            o_ref[bi, :, QA:QA + QB] = _merge_kv_outputs(*outs).astype(BF16)
            outs = []
            for kv in range(2):
                outs.append(_softmax_pv(_stack_group_queries(qc_ref[bi], kv), kc_ref, vc_ref, bi, 0, starts_c,
                                        s_ref.at[n_chain % 2], mask_c, sink_col(kv)))
                n_chain += 1
            o_ref[bi, :, QA + QB:] = _merge_kv_outputs(*outs).astype(BF16)

    def ctx_step():
        mixers([0], [0], None)

    def latent_step():
        q_start = (t - 1) * TILE
        u0 = pl.multiple_of(jnp.minimum(q_start + ctx_len - WINDOW, t_len - band), WINDOW)
        q_pos = q_start + (lax.broadcasted_iota(jnp.int32, (2 * TILE, KEY_CHUNK), 0) & (TILE - 1))
        key_lane = lax.broadcasted_iota(jnp.int32, (2 * TILE, KEY_CHUNK), 1)

        def mask_c(j, st, s):
            if j == 0:
                return s
            k_pos = (st - ctx_len) + key_lane
            valid = (jnp.abs(k_pos - q_pos) <= WINDOW) & (k_pos >= 0)
            return jnp.where(valid, s, NEG_BIG)

        mixers(_all_chunks(t_len), [0] + [u0 + i * KEY_CHUNK for i in range(band // KEY_CHUNK)], mask_c)

    if t0 == 0:
        pl.when(t == 0)(ctx_step)
        pl.when(t > 0)(latent_step)
    else:
        latent_step()


def _attention(sink, lamv, subg, q, k, v, t0, lambda_init):
    batch, t_len, _ = q.shape
    nt = t_len // TILE
    kern = functools.partial(_attn_kernel, t0=t0, ctx_len=TILE, lambda_init=lambda_init)
    const = lambda b, t: (0, 0)
    nb = 1
    qspec = lambda width, blk: pl.BlockSpec((nb, TILE, width), lambda b, t: (b, t + t0, blk))
    kvspec = lambda width, blk: pl.BlockSpec((nb, t_len, width), lambda b, t: (b, 0, blk))
    return pl.pallas_call(
        kern,
        grid=(batch // nb, nt - t0),
        in_specs=[pl.BlockSpec(memory_space=pltpu.SMEM),
                  pl.BlockSpec(lamv.shape, const),
                  pl.BlockSpec((1, LANES), const),
                  qspec(QA, 0), qspec(QB, QA // QB), qspec(QC, (QA + QB) // QC),
                  kvspec(KA, 0), kvspec(KB, KA // KB), kvspec(KC, (KA + KB) // KC),
                  kvspec(KA, 0), kvspec(KB, KA // KB), kvspec(KC, (KA + KB) // KC)],
        out_specs=pl.BlockSpec((nb, TILE, Q_COLS), lambda b, t: (b, t + t0, 0)),
        out_shape=jax.ShapeDtypeStruct((batch, t_len, Q_COLS), BF16),
        scratch_shapes=[pltpu.VMEM((2, 2 * TILE, t_len), F32)],
        compiler_params=pltpu.CompilerParams(dimension_semantics=("parallel", "parallel"),
                                             vmem_limit_bytes=VMEM_LIMIT),
        name="attention",
    )(sink, lamv, subg, q, q, q, k, k, k, v, v, v)


def _route(scores, bias):
    expert = lax.broadcasted_iota(jnp.int32, scores.shape, 0)
    sel = scores + bias
    in_group = expert & (EXPERTS_PER_GROUP - 1)
    group = expert >> 2

    def neighbours(x, idx, step, span):
        for d in (1, 2, 3):
            fwd = (idx + d) < 4
            y = jnp.where(fwd, pltpu.roll(x, N_EXPERTS - d * step, 0), pltpu.roll(x, span - d * step, 0))
            yield y, fwd

    rank = jnp.zeros_like(sel)
    for y, fwd in neighbours(sel, in_group, 1, EXPERTS_PER_GROUP):
        beats = (y > sel) | ((y == sel) & jnp.logical_not(fwd))
        rank = rank + jnp.where(beats, 1.0, 0.0)
    top2 = jnp.where(rank < 2.0, sel, 0.0)
    gsum = top2
    for y, _ in neighbours(top2, in_group, 1, EXPERTS_PER_GROUP):
        gsum = gsum + y
    grank = jnp.zeros_like(sel)
    for y, fwd in neighbours(gsum, group, EXPERTS_PER_GROUP, N_EXPERTS):
        beats = (y > gsum) | ((y == gsum) & jnp.logical_not(fwd))
        grank = grank + jnp.where(beats, 1.0, 0.0)
    best = grank < 0.5
    chosen = best & (rank < 2.0)
    w = jnp.where(chosen, scores, 0.0)
    gates = w / jnp.sum(w, axis=0, keepdims=True)
    gid = jnp.sum(jnp.where(best & (in_group == 0), group.astype(F32), 0.0), axis=0, keepdims=True)
    bits = jnp.sum(jnp.where(chosen, jnp.left_shift(1, in_group).astype(F32), 0.0), axis=0, keepdims=True)
    pair = sum(jnp.where(bits == float((1 << a) + (1 << b)), float(k), 0.0) for k, (a, b) in enumerate(PAIRS))
    bucket = gid * len(PAIRS) + pair
    tokens = scores.shape[1]
    return jnp.concatenate([gates, jnp.broadcast_to(bucket, (8, tokens)),
                            jnp.zeros((LANES - N_EXPERTS - 8, tokens), F32)], axis=0)


def _outproj_kernel(*refs, n_stream, t0):
    stream_refs = refs[:n_stream]
    tile = pl.program_id(1) + t0
    mix_ref, w_ref, mod_ref, g_ref, wrh_ref, wrl_ref, rb_ref, x1_ref, h2_ref, gates_ref, bucket_ref = refs[n_stream:]
    for i in range(mix_ref.shape[0]):
        x1 = _stream_rows(stream_refs, i, tile) + mod_ref[i, 0, 2:3, :] * _dot(mix_ref[i], w_ref[...])
        x1_ref[i] = x1
        h2 = _rms(x1, g_ref[...]) * (1.0 + mod_ref[i, 0, 4:5, :]) + mod_ref[i, 0, 3:4, :]
        h2_ref[i] = _pack_bf16_pairs(h2)
        hi, lo = _split(h2)
        logits = _dot_nt(wrh_ref[...], hi) + _dot_nt(wrl_ref[...], hi) + _dot_nt(wrh_ref[...], lo)
        routed = _route(_sigmoid(logits), rb_ref[...])
        gates_ref[i] = routed.T
        bucket_ref[i] = routed[GROUP_LANE:GROUP_LANE + 8]


def _outproj(stream, mix, w, mod, g, wrh, wrl, rb, t0):
    batch, t_len, d = _stream_dims(stream)
    nt = t_len // TILE
    nb = 2 * PAIR
    t_out = (nt - t0) * TILE
    const = lambda b, t: (0, 0)
    tok_in = lambda b, t: (b, t + t0, 0)
    tok_out = lambda b, t: (b, t, 0)
    return pl.pallas_call(
        functools.partial(_outproj_kernel, n_stream=len(stream), t0=t0),
        grid=(batch // nb, nt - t0),
        in_specs=_stream_specs(stream, nb, lambda t: t + t0) + [
                  pl.BlockSpec((nb, TILE, Q_COLS), tok_in),
                  pl.BlockSpec(w.shape, const),
                  pl.BlockSpec((nb, 1, 8, d), lambda b, t: (b, jnp.minimum(t + t0, 1), 0, 0)),
                  pl.BlockSpec((1, d), const),
                  pl.BlockSpec((N_EXPERTS, d), const),
                  pl.BlockSpec((N_EXPERTS, d), const),
                  pl.BlockSpec((N_EXPERTS, 1), const)],
        out_specs=[pl.BlockSpec((nb, TILE, d), tok_out),
                   pl.BlockSpec((nb, TILE, d // 2), tok_out),
                   pl.BlockSpec((nb, TILE, LANES), tok_out),
                   pl.BlockSpec((nb, 8, TILE), tok_out)],
        out_shape=[jax.ShapeDtypeStruct((batch, t_out, d), F32),
                   jax.ShapeDtypeStruct((batch, t_out, d // 2), jnp.uint32),
                   jax.ShapeDtypeStruct((batch, t_out, LANES), F32),
                   jax.ShapeDtypeStruct((batch, (nt - t0) * 8, TILE), F32)],
        compiler_params=pltpu.CompilerParams(dimension_semantics=("parallel", "parallel"),
                                             vmem_limit_bytes=VMEM_LIMIT),
        name="outproj",
    )(*stream, mix, w, mod, g, wrh, wrl, rb)


def _sc_workers():
    info = plsc.get_sparse_core_info()
    return info.num_cores, info.num_subcores


def _sc_dispatch(hpk, gates, pos2d, n_out):
    n, w = hpk.shape
    gw = gates.shape[1]
    nc, ns = _sc_workers()
    steps = n // (nc * ns * SC_WIN)
    mesh = plsc.VectorSubcoreMesh(core_axis_name="c", subcore_axis_name="s")

    @functools.partial(
        pl.kernel, mesh=mesh,
        out_type=(jax.ShapeDtypeStruct((n_out, w), hpk.dtype), jax.ShapeDtypeStruct((n_out, gw), gates.dtype)),
        scratch_types=[pltpu.VMEM((1, SC_WIN), jnp.int32), pltpu.VMEM((SC_WIN, w), hpk.dtype),
                       pltpu.VMEM((SC_WIN, gw), gates.dtype), pltpu.SemaphoreType.DMA, pltpu.SemaphoreType.DMA],
        name="sc_dispatch")
    def k(h_hbm, g_hbm, pos_hbm, hs_hbm, gs_hbm, idx_v, rows_v, grow_v, sem_in, sem_out):
        wid = lax.axis_index("s") * nc + lax.axis_index("c")

        @pl.loop(0, steps)
        def _(j):
            blk = wid * steps + j
            off = pl.multiple_of(blk * SC_WIN, SC_WIN)
            loads = [pltpu.async_copy(pos_hbm.at[pl.ds(blk, 1)], idx_v, sem_in),
                     pltpu.async_copy(h_hbm.at[pl.ds(off, SC_WIN)], rows_v, sem_in),
                     pltpu.async_copy(g_hbm.at[pl.ds(off, SC_WIN)], grow_v, sem_in)]
            for cp in loads:
                cp.wait()
            stores = [pltpu.async_copy(rows_v, hs_hbm.at[idx_v.at[0]], sem_out),
                      pltpu.async_copy(grow_v, gs_hbm.at[idx_v.at[0]], sem_out)]
            for cp in stores:
                cp.wait()

    return k(hpk, gates, pos2d)


def _sc_combine(ys, pos2d, n):
    w = ys.shape[1]
    nc, ns = _sc_workers()
    steps = n // (nc * ns * SC_WIN)
    mesh = plsc.VectorSubcoreMesh(core_axis_name="c", subcore_axis_name="s")

    @functools.partial(
        pl.kernel, mesh=mesh,
        out_type=jax.ShapeDtypeStruct((n, w), ys.dtype),
        scratch_types=[pltpu.VMEM((1, SC_WIN), jnp.int32), pltpu.VMEM((SC_WIN, w), ys.dtype)],
        name="sc_combine")
    def k(y_hbm, pos_hbm, o_hbm, idx_v, rows_v):
        wid = lax.axis_index("s") * nc + lax.axis_index("c")

        @pl.loop(0, steps)
        def _(j):
            blk = wid * steps + j
            off = pl.multiple_of(blk * SC_WIN, SC_WIN)
            pltpu.sync_copy(pos_hbm.at[pl.ds(blk, 1)], idx_v)
            pltpu.sync_copy(y_hbm.at[idx_v.at[0]], rows_v)
            pltpu.sync_copy(rows_v, o_hbm.at[pl.ds(off, SC_WIN)])

    return k(ys, pos2d)


PLAN_ROWS = 256


def _plan_kernel(g_ref, pos_ref, tg_ref, tk_ref, slots_ref):
    _plan_finish(g_ref, pos_ref, tg_ref, tk_ref, slots_ref)


def _plan_finish(g_ref, pos_ref, tg_ref, tk_ref, slots_ref):
    g = g_ref[...]
    rows = g.shape[0]
    n_groups = N_EXPERTS // EXPERTS_PER_GROUP
    before_lane = (lax.broadcasted_iota(jnp.int32, (LANES, LANES), 0)
                   < lax.broadcasted_iota(jnp.int32, (LANES, LANES), 1)).astype(BF16)
    before_row = (lax.broadcasted_iota(jnp.int32, (rows, rows), 1)
                  < lax.broadcasted_iota(jnp.int32, (rows, rows), 0)).astype(BF16)
    tile_row = lax.broadcasted_iota(jnp.int32, (1, LANES), 1).astype(F32) * FFN_TILE
    expert = lax.broadcasted_iota(jnp.int32, (N_EXPERTS, 1), 0)
    pos = jnp.zeros_like(g)
    start = jnp.zeros((1, 1), F32)
    tile_group = jnp.zeros((1, LANES), F32)
    need = jnp.zeros((N_EXPERTS, LANES), F32)
    for grp in range(n_groups):
        for k, (ea, eb) in enumerate(PAIRS):
            m = jnp.where(g == grp * len(PAIRS) + k, 1.0, 0.0)
            in_row = _dot(m.astype(BF16), before_lane)
            row_total = jnp.sum(m, axis=1, keepdims=True)
            rows_before = _dot(before_row, jnp.broadcast_to(row_total, m.shape).astype(BF16))
            pos = pos + m * (start + in_row + rows_before)
            end = start + jnp.sum(row_total, axis=0, keepdims=True)
            holds = jnp.where((tile_row < end) & (tile_row + FFN_TILE > start) & (end > start), 1.0, 0.0)
            uses = jnp.where((expert == grp * EXPERTS_PER_GROUP + ea) | (expert == grp * EXPERTS_PER_GROUP + eb),
                             1.0, 0.0)
            need = need + uses * holds
            start = end
        start = jnp.floor((start + (FFN_TILE - 1)) * (1.0 / FFN_TILE)) * FFN_TILE
        if grp < n_groups - 1:
            tile_group = tile_group + jnp.where(tile_row >= start, 1.0, 0.0)
    pos_ref[...] = pos.astype(jnp.int32)
    tg_ref[...] = tile_group.astype(jnp.int32)
    count = jnp.zeros((1, LANES), F32)
    slots = [tile_group * EXPERTS_PER_GROUP] * EXPERTS_PER_GROUP
    for e in range(N_EXPERTS):
        used = need[e:e + 1] > 0.0
        slots = [jnp.where(used & (count == k), float(e), slot) for k, slot in enumerate(slots)]
        count = count + jnp.where(used, 1.0, 0.0)
    tk_ref[...] = count.astype(jnp.int32)
    slots_ref[...] = jnp.concatenate(slots + [jnp.zeros((8 - EXPERTS_PER_GROUP, LANES), F32)], axis=0).astype(jnp.int32)


def _bucket_plan(bucket, n, n_pad):
    g = bucket.reshape(bucket.shape[0], -1, 8, TILE)[:, :, 0, :].reshape(n // LANES, LANES)
    g = jnp.pad(g, ((0, PLAN_ROWS - n // LANES), (0, 0)), constant_values=-1.0)
    pos, tile_group, tile_k, slots = pl.pallas_call(
        _plan_kernel,
        out_shape=[jax.ShapeDtypeStruct((PLAN_ROWS, LANES), jnp.int32),
                   jax.ShapeDtypeStruct((1, LANES), jnp.int32),
                   jax.ShapeDtypeStruct((1, LANES), jnp.int32),
                   jax.ShapeDtypeStruct((8, LANES), jnp.int32)],
        name="plan",
    )(g)
    n_tiles = n_pad // FFN_TILE
    return (pos[:n // LANES].reshape(n // SC_WIN, SC_WIN), tile_group[0, :n_tiles], tile_k[0, :n_tiles],
            [slots[k, :n_tiles] for k in range(EXPERTS_PER_GROUP)])


def _ffn_kernel(tg_ref, tk_ref, s0_ref, s1_ref, s2_ref, s3_ref, hs_ref, gs_ref, wg_ref, wu_ref, wd_ref, ys_ref,
                wgb_ref, wub_ref, wdb_ref):
    i = pl.program_id(0)
    slot_refs = (s0_ref, s1_ref, s2_ref, s3_ref)

    @pl.when((i == 0) | (tg_ref[i] != tg_ref[jnp.maximum(i - 1, 0)]))
    def _():
        wgb_ref[...] = wg_ref[0].astype(BF16)
        wub_ref[...] = wu_ref[0].astype(BF16)
        wdb_ref[...] = wd_ref[0].astype(BF16)

    def run_experts(n_experts):
        def body():
            lo, hi = _unpack_bf16_pairs(hs_ref[...])
            lo = lo.astype(BF16)
            hi = hi.astype(BF16)
            half = lo.shape[1]
            gs = gs_ref[...]
            lane = lax.broadcasted_iota(jnp.int32, gs.shape, 1)
            y = None
            for k in range(n_experts):
                e = slot_refs[k][i]
                local = e - tg_ref[i] * EXPERTS_PER_GROUP
                wg, wu, wd = wgb_ref.at[local], wub_ref.at[local], wdb_ref.at[local]
                a = _dot(lo, wg[:half, :]) + _dot(hi, wg[half:, :])
                u = _dot(lo, wu[:half, :]) + _dot(hi, wu[half:, :])
                gate = jnp.sum(jnp.where(lane == e, gs, 0.0), axis=-1, keepdims=True)
                part = _dot(((a * _sigmoid(a)) * u * gate).astype(BF16), wd[...])
                y = part if y is None else y + part
            ys_ref[...] = _pack_bf16_pairs(y)
        return body

    for n_experts in range(2, EXPERTS_PER_GROUP + 1):
        pl.when(tk_ref[i] == n_experts)(run_experts(n_experts))


def _ffn(tile_group, tile_k, slots, hs, gs, wg, wu, wd, layer):
    n_pad, half = hs.shape
    grp = lambda i, tg, *_: (layer, tg[i], 0, 0)
    row = lambda i, *_: (i, 0)
    gshape = lambda w: (EXPERTS_PER_GROUP,) + w.shape[2:]
    wspec = lambda w: pl.BlockSpec((1,) + gshape(w), grp)
    return pl.pallas_call(
        _ffn_kernel,
        grid_spec=pltpu.PrefetchScalarGridSpec(
            num_scalar_prefetch=2 + EXPERTS_PER_GROUP,
            grid=(n_pad // FFN_TILE,),
            in_specs=[pl.BlockSpec((FFN_TILE, half), row),
                      pl.BlockSpec((FFN_TILE, LANES), row),
                      wspec(wg), wspec(wu), wspec(wd)],
            out_specs=pl.BlockSpec((FFN_TILE, half), row),
            scratch_shapes=[pltpu.VMEM(gshape(wg), BF16), pltpu.VMEM(gshape(wu), BF16),
                            pltpu.VMEM(gshape(wd), BF16)]),
        out_shape=jax.ShapeDtypeStruct((n_pad, half), jnp.uint32),
        compiler_params=pltpu.CompilerParams(dimension_semantics=("arbitrary",), vmem_limit_bytes=VMEM_LIMIT),
        name="ffn",
    )(tile_group, tile_k, *slots, hs, gs, wg, wu, wd)


def _residual_kernel(x_ref, y_ref, mod_ref, fg_ref, o_ref, *, final_norm):
    for i in range(x_ref.shape[0]):
        lo, hi = _unpack_bf16_pairs(y_ref[i])
        y = x_ref[i] + mod_ref[i, 0, 5:6, :] * jnp.concatenate([lo, hi], axis=1)
        if final_norm:
            y = _rms(y, fg_ref[...])
        o_ref[i] = y


def _residual(x1, yg, mod, fg, t0, final_norm):
    batch, t_out, d = x1.shape
    nb = PAIR
    tok = lambda b, t: (b, t, 0)
    return pl.pallas_call(
        functools.partial(_residual_kernel, final_norm=final_norm),
        grid=(batch // nb, t_out // TILE),
        in_specs=[pl.BlockSpec((nb, TILE, d), tok),
                  pl.BlockSpec((nb, TILE, d // 2), tok),
                  pl.BlockSpec((nb, 1, 8, d), lambda b, t: (b, jnp.minimum(t + t0, 1), 0, 0)),
                  pl.BlockSpec((1, d), lambda b, t: (0, 0))],
        out_specs=pl.BlockSpec((nb, TILE, d), tok),
        out_shape=jax.ShapeDtypeStruct((batch, t_out, d), F32),
        compiler_params=pltpu.CompilerParams(dimension_semantics=("parallel", "parallel"),
                                             vmem_limit_bytes=VMEM_LIMIT),
        name="residual",
    )(x1, yg, mod, fg)


def _moe(h2, gates, bucket, wg, wu, wd, layer):
    batch, t_out, half = h2.shape
    n = batch * t_out
    n_pad = n + (N_EXPERTS // EXPERTS_PER_GROUP) * FFN_TILE
    assert n % LANES == 0 and n // LANES <= PLAN_ROWS and n_pad // FFN_TILE <= LANES
    pos2d, tile_group, tile_k, slots = _bucket_plan(bucket, n, n_pad)
    hs, gs = _sc_dispatch(h2.reshape(n, half), gates.reshape(n, LANES), pos2d, n_pad)
    ys = _ffn(tile_group, tile_k, slots, hs, gs, wg, wu, wd, layer)
    return _sc_combine(ys, pos2d, n).reshape(batch, t_out, half)


def _rope_tables(seq, ctx_len):
    rows = seq // GRID_W
    row_pos = jnp.repeat(jnp.arange(rows, dtype=F32), GRID_W)
    col_pos = jnp.tile(jnp.arange(GRID_W, dtype=F32), rows)
    axis_dim = HEAD_DIM // 2
    inv_freq = ROPE_BASE ** (-jnp.arange(0, axis_dim, 2, dtype=F32) / axis_dim)
    ang_r = row_pos[:, None] * inv_freq[None, :]
    ang_c = col_pos[:, None] * inv_freq[None, :]
    z = jnp.zeros_like(ang_r)
    cos = jnp.concatenate([jnp.cos(ang_r)] * 2 + [jnp.cos(ang_c)] * 2, axis=-1)
    sin_a = jnp.concatenate([-jnp.sin(ang_r), z, -jnp.sin(ang_c), z], axis=-1)
    sin_b = jnp.concatenate([z, jnp.sin(ang_r), z, jnp.sin(ang_c)], axis=-1)

    def full(tab, fill):
        tab = jnp.tile(tab, (1, LANES // HEAD_DIM))
        return jnp.concatenate([jnp.full((ctx_len, LANES), fill, F32), tab], axis=0)

    return full(cos, 1.0), full(sin_a, 0.0), full(sin_b, 0.0)


def kernel(x, c, ctx, c_ctx, w_ada, b_ada, norm1_g, norm2_g, w_in, w_out, lam_q1, lam_k1, lam_q2, lam_k2,
           subln_g, q_norm_g, k_norm_g, sink, w_router, router_bias, w_gate, w_up, w_down, final_g):
    batch, seq, d = x.shape
    ctx_len = ctx.shape[1]
    depth = w_in.shape[0]
    sc_rows = SC_WIN * math.prod(_sc_workers())
    assert ctx_len == TILE and seq % TILE == 0 and batch % PAIR == 0 and batch <= 15
    assert (batch * seq) % sc_rows == 0 and (batch * (seq + ctx_len)) % sc_rows == 0

    cpad = jnp.zeros((16, d), F32).at[:batch].set(c).at[batch].set(c_ctx)
    mod_all = _adaln(cpad, w_ada, b_ada)
    cos, sin_a, sin_b = _rope_tables(seq, ctx_len)
    mseg = jnp.asarray(np.kron(np.eye(2), np.ones((HEAD_DIM, HEAD_DIM))), BF16)
    wr = w_router.T
    wrh = wr.astype(BF16)
    wrl = (wr - wrh.astype(F32)).astype(BF16)
    rb = router_bias.reshape(N_EXPERTS, 1)

    stream = (ctx, x)

    for l in range(depth):
        last = l == depth - 1
        lambda_init = 0.8 - 0.6 * math.exp(-0.3 * l)
        m6 = mod_all[l].reshape(16, 6, d)
        m8 = jnp.concatenate([m6, jnp.zeros((16, 2, d), F32)], axis=1)
        mod = jnp.stack([jnp.broadcast_to(m8[batch], (batch, 8, d)), m8[:batch]], axis=1)

        w_in_p = w_in[l].astype(BF16)
        w_out_p = w_out[l].astype(BF16)
        gq = jnp.tile(q_norm_g[l], 2).reshape(1, LANES)
        gk = jnp.tile(k_norm_g[l], 2).reshape(1, LANES)
        q, k, v = _inproj(stream, mod, norm1_g[l].reshape(1, d), w_in_p, cos, sin_a, sin_b, gq, gk, mseg)

        t0 = 1 if last else 0
        lamv = jnp.stack([lam_q1[l], lam_k1[l], lam_q2[l], lam_k2[l]])
        mix = _attention(sink[l], lamv, subln_g[l].reshape(1, LANES), q, k, v, t0, lambda_init)
        x1, h2, gates, bucket = _outproj(stream, mix, w_out_p, mod, norm2_g[l].reshape(1, d), wrh, wrl, rb, t0)
        stream = (x1, _moe(h2, gates, bucket, w_gate, w_up, w_down, l), mod)
    return _residual(*stream, final_g.reshape(1, d), 1, final_norm=True)
```

```python
import functools
import math

import numpy as np
import jax
import jax.numpy as jnp
from jax import lax
from jax.experimental import pallas as pl
from jax.experimental.pallas import tpu as pltpu
from jax.experimental.pallas import tpu_sc as plsc

F32 = jnp.float32
BF16 = jnp.bfloat16

HEAD_DIM = 64
GRID_W = 64
ROPE_BASE = 10000.0
RMS_EPS = 1e-6
WINDOW = 128
N_EXPERTS = 16
EXPERTS_PER_GROUP = 4
LANES = 128
TILE = 256
PAIR = 2
FFN_TILE = 512
SC_WIN = 32
STREAM_BATCH = 4
GROUP_LANE = 16
PAIRS = ((0, 1), (0, 2), (0, 3), (1, 3), (1, 2), (2, 3))
NEG_BIG = -1e30
KEY_CHUNK = 256
VMEM_LIMIT = 52 * 1024 * 1024

QA, QB, QC = 512, 256, 256
KA, KB, KC = 512, 128, 128
Q_COLS = QA + QB + QC
K_COLS = KA + KB + KC
V_COLS = 768


def _dot(a, b):
    return jnp.dot(a, b, preferred_element_type=F32)


def _dot_nt(a, b):
    return lax.dot_general(a, b, (((1,), (1,)), ((), ())), preferred_element_type=F32)


def _split(a):
    hi = a.astype(BF16)
    lo = (a - hi.astype(F32)).astype(BF16)
    return hi, lo


def _dot3(a, b):
    ah, al = _split(a)
    bh, bl = _split(b)
    return _dot(ah, bh) + _dot(ah, bl) + _dot(al, bh)


def _rms(x, g):
    ms = jnp.mean(x * x, axis=-1, keepdims=True)
    return x * lax.rsqrt(ms + RMS_EPS) * g


def _sigmoid(x):
    return 1.0 / (1.0 + jnp.exp(-x))


def _pack_bf16_pairs(y):
    n = y.shape[1] // 2
    yb = y.astype(BF16).astype(F32)
    lo = lax.bitcast_convert_type(yb[:, :n], jnp.uint32) >> 16
    hi = lax.bitcast_convert_type(yb[:, n:], jnp.uint32) & jnp.uint32(0xFFFF0000)
    return hi | lo


def _unpack_bf16_pairs(w):
    lo = lax.bitcast_convert_type(w << 16, F32)
    hi = lax.bitcast_convert_type(w & jnp.uint32(0xFFFF0000), F32)
    return lo, hi


def _stream_dims(stream):
    batch, t_len, d = stream[0].shape
    return batch, (t_len + stream[1].shape[1] if len(stream) == 2 else t_len), d


def _stream_rows(stream_refs, i, tile):
    if len(stream_refs) == 2:
        ctx_ref, x_ref = stream_refs
        return jnp.where(tile == 0, ctx_ref[i], x_ref[i])
    x_ref, y_ref, mod_ref = stream_refs
    lo, hi = _unpack_bf16_pairs(y_ref[i])
    return x_ref[i] + mod_ref[i, 0, 5:6, :] * jnp.concatenate([lo, hi], axis=1)


def _stream_specs(stream, nb, tile_of):
    d = stream[0].shape[-1]
    if len(stream) == 2:
        return [pl.BlockSpec((nb, TILE, d), lambda b, t: (b, 0, 0)),
                pl.BlockSpec((nb, TILE, d), lambda b, t: (b, jnp.maximum(tile_of(t) - 1, 0), 0))]
    tok = lambda b, t: (b, tile_of(t), 0)
    return [pl.BlockSpec((nb, TILE, d), tok),
            pl.BlockSpec((nb, TILE, d // 2), tok),
            pl.BlockSpec((nb, 1, 8, d), lambda b, t: (b, jnp.minimum(tile_of(t), 1), 0, 0))]


def _adaln_kernel(c_ref, w_ref, b_ref, o_ref):
    cv = c_ref[...]
    s = cv * _sigmoid(cv)
    o_ref[0] = _dot3(s, w_ref[0]) + b_ref[0]


def _adaln(cpad, w_ada, b_ada):
    depth, d, n6 = w_ada.shape
    tn = 1536
    return pl.pallas_call(
        _adaln_kernel,
        grid=(depth, n6 // tn),
        in_specs=[pl.BlockSpec((16, d), lambda l, j: (0, 0)),
                  pl.BlockSpec((1, d, tn), lambda l, j: (l, 0, j)),
                  pl.BlockSpec((1, 1, tn), lambda l, j: (l, 0, j))],
        out_specs=pl.BlockSpec((1, 16, tn), lambda l, j: (l, 0, j)),
        out_shape=jax.ShapeDtypeStruct((depth, 16, n6), F32),
        compiler_params=pltpu.CompilerParams(dimension_semantics=("parallel", "parallel"),
                                             vmem_limit_bytes=VMEM_LIMIT),
        name="adaln",
    )(cpad, w_ada, b_ada.reshape(depth, 1, n6))


def _inproj_kernel(*refs, n_stream):
    for base in range(0, refs[0].shape[0], PAIR):
        _inproj_rows(base, pl.program_id(1), refs[:n_stream], *refs[n_stream:])


def _inproj_rows(base, tile, stream_refs, mod_ref, g_ref, w_ref, cos_ref, sa_ref, sb_ref, gq_ref, gk_ref, mseg_ref,
                 q_ref, k_ref, v_ref):
    members = range(base, base + PAIR)
    h = jnp.concatenate(
        [(_rms(_stream_rows(stream_refs, i, tile), g_ref[...]) * (1.0 + mod_ref[i, 0, 1:2, :])
          + mod_ref[i, 0, 0:1, :]).astype(BF16) for i in members], axis=0)
    cos = jnp.concatenate([cos_ref[...]] * PAIR, axis=0)
    sa = jnp.concatenate([sa_ref[...]] * PAIR, axis=0)
    sb = jnp.concatenate([sb_ref[...]] * PAIR, axis=0)
    mseg = mseg_ref[...]

    def put(ref, cols, val):
        for n, i in enumerate(members):
            ref[i, :, cols] = val[n * TILE:(n + 1) * TILE]

    def rope(b):
        return b * cos + pltpu.roll(b, LANES - 16, 1) * sa + pltpu.roll(b, 16, 1) * sb

    def qknorm(b, g):
        hi, lo = _split(b * b)
        ms = (_dot(hi, mseg) + _dot(lo, mseg)) * (1.0 / HEAD_DIM)
        return b * lax.rsqrt(ms + RMS_EPS) * g

    def wcols(*pieces):
        parts = [w_ref[:, a:a + n] for a, n in pieces]
        return parts[0] if len(parts) == 1 else jnp.concatenate(parts, axis=1)

    lane = lax.broadcasted_iota(jnp.int32, (PAIR * TILE, LANES), 1)

    def regroup(p2):
        h0, h1 = p2[:, :LANES], p2[:, LANES:]
        return (jnp.where(lane < HEAD_DIM, h0, pltpu.roll(h1, HEAD_DIM, 1)),
                jnp.where(lane < HEAD_DIM, pltpu.roll(h0, HEAD_DIM, 1), h1))

    q0, kv0 = 0, Q_COLS
    rope_chunks = [wcols((q0, 256)), wcols((q0 + 256, 256)), wcols((q0 + QA, QB)), wcols((q0 + QA + QB, QC)),
                   wcols((kv0, 256)), wcols((kv0 + 256, 256)),
                   wcols((kv0 + 2 * KA, KB), (kv0 + 2 * KA + 2 * KB, KC))]
    v_chunks = [wcols((kv0 + KA, 256)), wcols((kv0 + KA + 256, 256)),
                wcols((kv0 + 2 * KA + KB, KB), (kv0 + 2 * KA + 2 * KB + KC, KC))]
    for c, wc in enumerate(rope_chunks):
        p2 = _dot(h, wc)
        halves = regroup(p2) if c in (2, 3) else (p2[:, :LANES], p2[:, LANES:])
        for half in range(2):
            j = 2 * c + half
            p = halves[half]
            if j in (4, 5):
                p = qknorm(p, gq_ref[...])
            if j == 12:
                p = qknorm(p, gk_ref[...])
            p = rope(p)
            if j < Q_COLS // LANES:
                put(q_ref, slice(j * LANES, (j + 1) * LANES), (p * (HEAD_DIM ** -0.5)).astype(BF16))
            else:
                jk = j - Q_COLS // LANES
                put(k_ref, slice(jk * LANES, (jk + 1) * LANES), p.astype(BF16))
    for c, wc in enumerate(v_chunks):
        put(v_ref, slice(c * 256, (c + 1) * 256), _dot(h, wc).astype(BF16))


def _inproj(stream, mod, g, w, cos, sa, sb, gq, gk, mseg):
    batch, t_len, d = _stream_dims(stream)
    nt = t_len // TILE
    nb = 2 * PAIR
    row = lambda b, t: (t, 0)
    const = lambda b, t: (0, 0)
    tok = lambda b, t: (b, t, 0)
    return pl.pallas_call(
        functools.partial(_inproj_kernel, n_stream=len(stream)),
        grid=(batch // nb, nt),
        in_specs=_stream_specs(stream, nb, lambda t: t) + [
                  pl.BlockSpec((nb, 1, 8, d), lambda b, t: (b, jnp.minimum(t, 1), 0, 0)),
                  pl.BlockSpec((1, d), const),
                  pl.BlockSpec(w.shape, const),
                  pl.BlockSpec((TILE, LANES), row),
                  pl.BlockSpec((TILE, LANES), row),
                  pl.BlockSpec((TILE, LANES), row),
                  pl.BlockSpec((1, LANES), const),
                  pl.BlockSpec((1, LANES), const),
                  pl.BlockSpec((LANES, LANES), const)],
        out_specs=[pl.BlockSpec((nb, TILE, Q_COLS), tok),
                   pl.BlockSpec((nb, TILE, K_COLS), tok),
                   pl.BlockSpec((nb, TILE, V_COLS), tok)],
        out_shape=[jax.ShapeDtypeStruct((batch, t_len, Q_COLS), BF16),
                   jax.ShapeDtypeStruct((batch, t_len, K_COLS), BF16),
                   jax.ShapeDtypeStruct((batch, t_len, V_COLS), BF16)],
        compiler_params=pltpu.CompilerParams(dimension_semantics=("parallel", "parallel"),
                                             vmem_limit_bytes=VMEM_LIMIT),
        name="inproj",
    )(*stream, mod, g, w, cos, sa, sb, gq, gk, mseg)


def _scores_pass(qq, k_ref, bi, col, starts, s_ref, mask_fn=None, extra=None):
    half = KEY_CHUNK // 2
    rows = qq.shape[0]
    mrun = None
    for j, st in enumerate(starts):
        s = _dot_nt(qq, k_ref[bi, pl.ds(st, KEY_CHUNK), col:col + LANES])
        if mask_fn is not None:
            s = mask_fn(j, st, s)
        s_ref[0:rows, j * KEY_CHUNK:(j + 1) * KEY_CHUNK] = s
        mj = jnp.maximum(s[:, :half], s[:, half:])
        mrun = mj if mrun is None else jnp.maximum(mrun, mj)
    m = jnp.max(mrun, axis=-1, keepdims=True)
    return m if extra is None else jnp.maximum(m, extra)


def _values_pass(m, rows, v_ref, bi, col, starts, s_ref, extra=None):
    half = KEY_CHUNK // 2
    lrun = None
    acc = None
    for j, st in enumerate(starts):
        e = jnp.exp(s_ref[0:rows, j * KEY_CHUNK:(j + 1) * KEY_CHUNK] - m)
        lj = e[:, :half] + e[:, half:]
        lrun = lj if lrun is None else lrun + lj
        pv = _dot(e.astype(BF16), v_ref[bi, pl.ds(st, KEY_CHUNK), col:col + LANES])
        acc = pv if acc is None else acc + pv
    l = jnp.sum(lrun, axis=-1, keepdims=True)
    if extra is not None:
        l = l + jnp.exp(extra - m)
    return acc / l


def _softmax_pv(qq, k_ref, v_ref, bi, col, starts, s_ref, mask_fn=None, extra=None):
    m = _scores_pass(qq, k_ref, bi, col, starts, s_ref, mask_fn, extra)
    return _values_pass(m, qq.shape[0], v_ref, bi, col, starts, s_ref, extra)


def _all_chunks(n_keys):
    return [j * KEY_CHUNK for j in range(n_keys // KEY_CHUNK)]


def _half_masks(shape):
    lane = lax.broadcasted_iota(jnp.int32, shape, 1)
    return lane < HEAD_DIM, lane >= HEAD_DIM


def _stack_group_queries(q2, kv):
    lo_m, hi_m = _half_masks((TILE, LANES))
    m = lo_m if kv == 0 else hi_m
    zero = jnp.zeros((TILE, LANES), q2.dtype)
    return jnp.concatenate([jnp.where(m, q2[:, :LANES], zero), jnp.where(m, q2[:, LANES:], zero)], axis=0)


def _merge_kv_outputs(o_kv0, o_kv1):
    lo_m, _ = _half_masks((TILE, LANES))
    kv0 = jnp.where(lo_m, o_kv0[:TILE], pltpu.roll(o_kv0[TILE:], HEAD_DIM, 1))
    kv1 = jnp.where(lo_m, pltpu.roll(o_kv1[:TILE], HEAD_DIM, 1), o_kv1[TILE:])
    return jnp.concatenate([kv0, kv1], axis=1)


def _attn_kernel(sink_ref, lam_ref, subg_ref, qa_ref, qb_ref, qc_ref, ka_ref, kb_ref, kc_ref, va_ref, vb_ref, vc_ref,
                 o_ref, s_ref, *, t0, ctx_len, lambda_init):
    t = pl.program_id(1) + t0
    t_len = ka_ref.shape[1]
    band = 2 * TILE
    lo_m, hi_m = _half_masks((TILE, LANES))
    zero = jnp.zeros((TILE, LANES), BF16)
    row = lax.broadcasted_iota(jnp.int32, (2 * TILE, 1), 0)
    lv = lam_ref[...]
    lam = (jnp.exp(jnp.sum(lv[0:1] * lv[1:2], keepdims=True))
           - jnp.exp(jnp.sum(lv[2:3] * lv[3:4], keepdims=True)) + lambda_init)

    def sink_col(kv):
        return jnp.where(row < TILE, sink_ref[2 * kv], sink_ref[2 * kv + 1])

    def mixers(starts_ab, starts_c, mask_c):
        n_chain = 0
        for bi in range(qa_ref.shape[0]):
            for h in range(QA // LANES):
                q = qa_ref[bi, :, h * LANES:(h + 1) * LANES]
                qq = jnp.concatenate([jnp.where(lo_m, q, zero), jnp.where(hi_m, q, zero)], axis=0)
                o = _softmax_pv(qq, ka_ref, va_ref, bi, h * LANES, starts_ab, s_ref.at[n_chain % 2])
                n_chain += 1
                o = o[:TILE] - lam * o[TILE:]
                o_ref[bi, :, h * LANES:(h + 1) * LANES] = (
                    _rms(o, subg_ref[...]) * (1.0 - lambda_init)).astype(BF16)
            outs = []
            for kv in range(2):
                outs.append(_softmax_pv(_stack_group_queries(qb_ref[bi], kv), kb_ref, vb_ref, bi, 0, starts_ab,
                                        s_ref.at[n_chain % 2]))
                n_chain += 1
            o_ref[bi, :, QA:QA + QB] = _merge_kv_outputs(*outs).astype(BF16)
            outs = []
            for kv in range(2):
                outs.append(_softmax_pv(_stack_group_queries(qc_ref[bi], kv), kc_ref, vc_ref, bi, 0, starts_c,
                                        s_ref.at[n_chain % 2], mask_c, sink_col(kv)))
                n_chain += 1
            o_ref[bi, :, QA + QB:] = _merge_kv_outputs(*outs).astype(BF16)

    def ctx_step():
        mixers([0], [0], None)

    def latent_step():
        q_start = (t - 1) * TILE
        u0 = pl.multiple_of(jnp.minimum(q_start + ctx_len - WINDOW, t_len - band), WINDOW)
        q_pos = q_start + (lax.broadcasted_iota(jnp.int32, (2 * TILE, KEY_CHUNK), 0) & (TILE - 1))
        key_lane = lax.broadcasted_iota(jnp.int32, (2 * TILE, KEY_CHUNK), 1)

        def mask_c(j, st, s):
            if j == 0:
                return s
            k_pos = (st - ctx_len) + key_lane
            valid = (jnp.abs(k_pos - q_pos) <= WINDOW) & (k_pos >= 0)
            return jnp.where(valid, s, NEG_BIG)

        mixers(_all_chunks(t_len), [0] + [u0 + i * KEY_CHUNK for i in range(band // KEY_CHUNK)], mask_c)

    if t0 == 0:
        pl.when(t == 0)(ctx_step)
        pl.when(t > 0)(latent_step)
    else:
        latent_step()


def _attention(sink, lamv, subg, q, k, v, t0, lambda_init):
    batch, t_len, _ = q.shape
    nt = t_len // TILE
    kern = functools.partial(_attn_kernel, t0=t0, ctx_len=TILE, lambda_init=lambda_init)
    const = lambda b, t: (0, 0)
    nb = 1
    qspec = lambda width, blk: pl.BlockSpec((nb, TILE, width), lambda b, t: (b, t + t0, blk))
    kvspec = lambda width, blk: pl.BlockSpec((nb, t_len, width), lambda b, t: (b, 0, blk))
    return pl.pallas_call(
        kern,
        grid=(batch // nb, nt - t0),
        in_specs=[pl.BlockSpec(memory_space=pltpu.SMEM),
                  pl.BlockSpec(lamv.shape, const),
                  pl.BlockSpec((1, LANES), const),
                  qspec(QA, 0), qspec(QB, QA // QB), qspec(QC, (QA + QB) // QC),
                  kvspec(KA, 0), kvspec(KB, KA // KB), kvspec(KC, (KA + KB) // KC),
                  kvspec(KA, 0), kvspec(KB, KA // KB), kvspec(KC, (KA + KB) // KC)],
        out_specs=pl.BlockSpec((nb, TILE, Q_COLS), lambda b, t: (b, t + t0, 0)),
        out_shape=jax.ShapeDtypeStruct((batch, t_len, Q_COLS), BF16),
        scratch_shapes=[pltpu.VMEM((2, 2 * TILE, t_len), F32)],
        compiler_params=pltpu.CompilerParams(dimension_semantics=("parallel", "parallel"),
                                             vmem_limit_bytes=VMEM_LIMIT),
        name="attention",
    )(sink, lamv, subg, q, q, q, k, k, k, v, v, v)


def _route(scores, bias):
    expert = lax.broadcasted_iota(jnp.int32, scores.shape, 0)
    sel = scores + bias
    in_group = expert & (EXPERTS_PER_GROUP - 1)
    group = expert >> 2

    def neighbours(x, idx, step, span):
        for d in (1, 2, 3):
            fwd = (idx + d) < 4
            y = jnp.where(fwd, pltpu.roll(x, N_EXPERTS - d * step, 0), pltpu.roll(x, span - d * step, 0))
            yield y, fwd

    rank = jnp.zeros_like(sel)
    for y, fwd in neighbours(sel, in_group, 1, EXPERTS_PER_GROUP):
        beats = (y > sel) | ((y == sel) & jnp.logical_not(fwd))
        rank = rank + jnp.where(beats, 1.0, 0.0)
    top2 = jnp.where(rank < 2.0, sel, 0.0)
    gsum = top2
    for y, _ in neighbours(top2, in_group, 1, EXPERTS_PER_GROUP):
        gsum = gsum + y
    grank = jnp.zeros_like(sel)
    for y, fwd in neighbours(gsum, group, EXPERTS_PER_GROUP, N_EXPERTS):
        beats = (y > gsum) | ((y == gsum) & jnp.logical_not(fwd))
        grank = grank + jnp.where(beats, 1.0, 0.0)
    best = grank < 0.5
    chosen = best & (rank < 2.0)
    w = jnp.where(chosen, scores, 0.0)
    gates = w / jnp.sum(w, axis=0, keepdims=True)
    gid = jnp.sum(jnp.where(best & (in_group == 0), group.astype(F32), 0.0), axis=0, keepdims=True)
    bits = jnp.sum(jnp.where(chosen, jnp.left_shift(1, in_group).astype(F32), 0.0), axis=0, keepdims=True)
    pair = sum(jnp.where(bits == float((1 << a) + (1 << b)), float(k), 0.0) for k, (a, b) in enumerate(PAIRS))
    bucket = gid * len(PAIRS) + pair
    tokens = scores.shape[1]
    return jnp.concatenate([gates, jnp.broadcast_to(bucket, (8, tokens)),
                            jnp.zeros((LANES - N_EXPERTS - 8, tokens), F32)], axis=0)


def _outproj_kernel(*refs, n_stream, t0):
    stream_refs = refs[:n_stream]
    tile = pl.program_id(1) + t0
    mix_ref, w_ref, mod_ref, g_ref, wrh_ref, wrl_ref, rb_ref, x1_ref, h2_ref, gates_ref, bucket_ref = refs[n_stream:]
    for i in range(mix_ref.shape[0]):
        x1 = _stream_rows(stream_refs, i, tile) + mod_ref[i, 0, 2:3, :] * _dot(mix_ref[i], w_ref[...])
        x1_ref[i] = x1
        h2 = _rms(x1, g_ref[...]) * (1.0 + mod_ref[i, 0, 4:5, :]) + mod_ref[i, 0, 3:4, :]
        h2_ref[i] = _pack_bf16_pairs(h2)
        hi, lo = _split(h2)
        logits = _dot_nt(wrh_ref[...], hi) + _dot_nt(wrl_ref[...], hi) + _dot_nt(wrh_ref[...], lo)
        routed = _route(_sigmoid(logits), rb_ref[...])
        gates_ref[i] = routed.T
        bucket_ref[i] = routed[GROUP_LANE:GROUP_LANE + 8]


def _outproj(stream, mix, w, mod, g, wrh, wrl, rb, t0):
    batch, t_len, d = _stream_dims(stream)
    nt = t_len // TILE
    nb = 2 * PAIR
    t_out = (nt - t0) * TILE
    const = lambda b, t: (0, 0)
    tok_in = lambda b, t: (b, t + t0, 0)
    tok_out = lambda b, t: (b, t, 0)
    return pl.pallas_call(
        functools.partial(_outproj_kernel, n_stream=len(stream), t0=t0),
        grid=(batch // nb, nt - t0),
        in_specs=_stream_specs(stream, nb, lambda t: t + t0) + [
                  pl.BlockSpec((nb, TILE, Q_COLS), tok_in),
                  pl.BlockSpec(w.shape, const),
                  pl.BlockSpec((nb, 1, 8, d), lambda b, t: (b, jnp.minimum(t + t0, 1), 0, 0)),
                  pl.BlockSpec((1, d), const),
                  pl.BlockSpec((N_EXPERTS, d), const),
                  pl.BlockSpec((N_EXPERTS, d), const),
                  pl.BlockSpec((N_EXPERTS, 1), const)],
        out_specs=[pl.BlockSpec((nb, TILE, d), tok_out),
                   pl.BlockSpec((nb, TILE, d // 2), tok_out),
                   pl.BlockSpec((nb, TILE, LANES), tok_out),
                   pl.BlockSpec((nb, 8, TILE), tok_out)],
        out_shape=[jax.ShapeDtypeStruct((batch, t_out, d), F32),
                   jax.ShapeDtypeStruct((batch, t_out, d // 2), jnp.uint32),
                   jax.ShapeDtypeStruct((batch, t_out, LANES), F32),
                   jax.ShapeDtypeStruct((batch, (nt - t0) * 8, TILE), F32)],
        compiler_params=pltpu.CompilerParams(dimension_semantics=("parallel", "parallel"),
                                             vmem_limit_bytes=VMEM_LIMIT),
        name="outproj",
    )(*stream, mix, w, mod, g, wrh, wrl, rb)


def _sc_workers():
    info = plsc.get_sparse_core_info()
    return info.num_cores, info.num_subcores


def _sc_dispatch(hpk, gates, pos2d, n_out):
    n, w = hpk.shape
    gw = gates.shape[1]
    nc, ns = _sc_workers()
    steps = n // (nc * ns * SC_WIN)
    mesh = plsc.VectorSubcoreMesh(core_axis_name="c", subcore_axis_name="s")

    @functools.partial(
        pl.kernel, mesh=mesh,
        out_type=(jax.ShapeDtypeStruct((n_out, w), hpk.dtype), jax.ShapeDtypeStruct((n_out, gw), gates.dtype)),
        scratch_types=[pltpu.VMEM((1, SC_WIN), jnp.int32), pltpu.VMEM((SC_WIN, w), hpk.dtype),
                       pltpu.VMEM((SC_WIN, gw), gates.dtype), pltpu.SemaphoreType.DMA, pltpu.SemaphoreType.DMA],
        name="sc_dispatch")
    def k(h_hbm, g_hbm, pos_hbm, hs_hbm, gs_hbm, idx_v, rows_v, grow_v, sem_in, sem_out):
        wid = lax.axis_index("s") * nc + lax.axis_index("c")

        @pl.loop(0, steps)
        def _(j):
            blk = wid * steps + j
            off = pl.multiple_of(blk * SC_WIN, SC_WIN)
            loads = [pltpu.async_copy(pos_hbm.at[pl.ds(blk, 1)], idx_v, sem_in),
                     pltpu.async_copy(h_hbm.at[pl.ds(off, SC_WIN)], rows_v, sem_in),
                     pltpu.async_copy(g_hbm.at[pl.ds(off, SC_WIN)], grow_v, sem_in)]
            for cp in loads:
                cp.wait()
            stores = [pltpu.async_copy(rows_v, hs_hbm.at[idx_v.at[0]], sem_out),
                      pltpu.async_copy(grow_v, gs_hbm.at[idx_v.at[0]], sem_out)]
            for cp in stores:
                cp.wait()

    return k(hpk, gates, pos2d)


def _sc_combine(ys, pos2d, n):
    w = ys.shape[1]
    nc, ns = _sc_workers()
    steps = n // (nc * ns * SC_WIN)
    mesh = plsc.VectorSubcoreMesh(core_axis_name="c", subcore_axis_name="s")

    @functools.partial(
        pl.kernel, mesh=mesh,
        out_type=jax.ShapeDtypeStruct((n, w), ys.dtype),
        scratch_types=[pltpu.VMEM((1, SC_WIN), jnp.int32), pltpu.VMEM((SC_WIN, w), ys.dtype)],
        name="sc_combine")
    def k(y_hbm, pos_hbm, o_hbm, idx_v, rows_v):
        wid = lax.axis_index("s") * nc + lax.axis_index("c")

        @pl.loop(0, steps)
        def _(j):
            blk = wid * steps + j
            off = pl.multiple_of(blk * SC_WIN, SC_WIN)
            pltpu.sync_copy(pos_hbm.at[pl.ds(blk, 1)], idx_v)
            pltpu.sync_copy(y_hbm.at[idx_v.at[0]], rows_v)
            pltpu.sync_copy(rows_v, o_hbm.at[pl.ds(off, SC_WIN)])

    return k(ys, pos2d)


PLAN_ROWS = 256


def _plan_kernel(g_ref, pos_ref, tg_ref, tk_ref, slots_ref):
    _plan_finish(g_ref, pos_ref, tg_ref, tk_ref, slots_ref)


def _plan_finish(g_ref, pos_ref, tg_ref, tk_ref, slots_ref):
    g = g_ref[...]
    rows = g.shape[0]
    n_groups = N_EXPERTS // EXPERTS_PER_GROUP
    before_lane = (lax.broadcasted_iota(jnp.int32, (LANES, LANES), 0)
                   < lax.broadcasted_iota(jnp.int32, (LANES, LANES), 1)).astype(BF16)
    before_row = (lax.broadcasted_iota(jnp.int32, (rows, rows), 1)
                  < lax.broadcasted_iota(jnp.int32, (rows, rows), 0)).astype(BF16)
    tile_row = lax.broadcasted_iota(jnp.int32, (1, LANES), 1).astype(F32) * FFN_TILE
    expert = lax.broadcasted_iota(jnp.int32, (N_EXPERTS, 1), 0)
    pos = jnp.zeros_like(g)
    start = jnp.zeros((1, 1), F32)
    tile_group = jnp.zeros((1, LANES), F32)
    need = jnp.zeros((N_EXPERTS, LANES), F32)
    for grp in range(n_groups):
        for k, (ea, eb) in enumerate(PAIRS):
            m = jnp.where(g == grp * len(PAIRS) + k, 1.0, 0.0)
            in_row = _dot(m.astype(BF16), before_lane)
            row_total = jnp.sum(m, axis=1, keepdims=True)
            rows_before = _dot(before_row, jnp.broadcast_to(row_total, m.shape).astype(BF16))
            pos = pos + m * (start + in_row + rows_before)
            end = start + jnp.sum(row_total, axis=0, keepdims=True)
            holds = jnp.where((tile_row < end) & (tile_row + FFN_TILE > start) & (end > start), 1.0, 0.0)
            uses = jnp.where((expert == grp * EXPERTS_PER_GROUP + ea) | (expert == grp * EXPERTS_PER_GROUP + eb),
                             1.0, 0.0)
            need = need + uses * holds
            start = end
        start = jnp.floor((start + (FFN_TILE - 1)) * (1.0 / FFN_TILE)) * FFN_TILE
        if grp < n_groups - 1:
            tile_group = tile_group + jnp.where(tile_row >= start, 1.0, 0.0)
    pos_ref[...] = pos.astype(jnp.int32)
    tg_ref[...] = tile_group.astype(jnp.int32)
    count = jnp.zeros((1, LANES), F32)
    slots = [tile_group * EXPERTS_PER_GROUP] * EXPERTS_PER_GROUP
    for e in range(N_EXPERTS):
        used = need[e:e + 1] > 0.0
        slots = [jnp.where(used & (count == k), float(e), slot) for k, slot in enumerate(slots)]
        count = count + jnp.where(used, 1.0, 0.0)
    tk_ref[...] = count.astype(jnp.int32)
    slots_ref[...] = jnp.concatenate(slots + [jnp.zeros((8 - EXPERTS_PER_GROUP, LANES), F32)], axis=0).astype(jnp.int32)


def _bucket_plan(bucket, n, n_pad):
    g = bucket.reshape(bucket.shape[0], -1, 8, TILE)[:, :, 0, :].reshape(n // LANES, LANES)
    g = jnp.pad(g, ((0, PLAN_ROWS - n // LANES), (0, 0)), constant_values=-1.0)
    pos, tile_group, tile_k, slots = pl.pallas_call(
        _plan_kernel,
        out_shape=[jax.ShapeDtypeStruct((PLAN_ROWS, LANES), jnp.int32),
                   jax.ShapeDtypeStruct((1, LANES), jnp.int32),
                   jax.ShapeDtypeStruct((1, LANES), jnp.int32),
                   jax.ShapeDtypeStruct((8, LANES), jnp.int32)],
        name="plan",
    )(g)
    n_tiles = n_pad // FFN_TILE
    return (pos[:n // LANES].reshape(n // SC_WIN, SC_WIN), tile_group[0, :n_tiles], tile_k[0, :n_tiles],
            [slots[k, :n_tiles] for k in range(EXPERTS_PER_GROUP)])


def _ffn_kernel(tg_ref, tk_ref, s0_ref, s1_ref, s2_ref, s3_ref, hs_ref, gs_ref, wg_ref, wu_ref, wd_ref, ys_ref,
                wgb_ref, wub_ref, wdb_ref):
    i = pl.program_id(0)
    slot_refs = (s0_ref, s1_ref, s2_ref, s3_ref)

    @pl.when((i == 0) | (tg_ref[i] != tg_ref[jnp.maximum(i - 1, 0)]))
    def _():
        wgb_ref[...] = wg_ref[0].astype(BF16)
        wub_ref[...] = wu_ref[0].astype(BF16)
        wdb_ref[...] = wd_ref[0].astype(BF16)

    def run_experts(n_experts):
        def body():
            lo, hi = _unpack_bf16_pairs(hs_ref[...])
            lo = lo.astype(BF16)
            hi = hi.astype(BF16)
            half = lo.shape[1]
            gs = gs_ref[...]
            lane = lax.broadcasted_iota(jnp.int32, gs.shape, 1)
            y = None
            for k in range(n_experts):
                e = slot_refs[k][i]
                local = e - tg_ref[i] * EXPERTS_PER_GROUP
                wg, wu, wd = wgb_ref.at[local], wub_ref.at[local], wdb_ref.at[local]
                a = _dot(lo, wg[:half, :]) + _dot(hi, wg[half:, :])
                u = _dot(lo, wu[:half, :]) + _dot(hi, wu[half:, :])
                gate = jnp.sum(jnp.where(lane == e, gs, 0.0), axis=-1, keepdims=True)
                part = _dot(((a * _sigmoid(a)) * u * gate).astype(BF16), wd[...])
                y = part if y is None else y + part
            ys_ref[...] = _pack_bf16_pairs(y)
        return body

    for n_experts in range(2, EXPERTS_PER_GROUP + 1):
        pl.when(tk_ref[i] == n_experts)(run_experts(n_experts))


def _ffn(tile_group, tile_k, slots, hs, gs, wg, wu, wd, layer):
    n_pad, half = hs.shape
    grp = lambda i, tg, *_: (layer, tg[i], 0, 0)
    row = lambda i, *_: (i, 0)
    gshape = lambda w: (EXPERTS_PER_GROUP,) + w.shape[2:]
    wspec = lambda w: pl.BlockSpec((1,) + gshape(w), grp)
    return pl.pallas_call(
        _ffn_kernel,
        grid_spec=pltpu.PrefetchScalarGridSpec(
            num_scalar_prefetch=2 + EXPERTS_PER_GROUP,
            grid=(n_pad // FFN_TILE,),
            in_specs=[pl.BlockSpec((FFN_TILE, half), row),
                      pl.BlockSpec((FFN_TILE, LANES), row),
                      wspec(wg), wspec(wu), wspec(wd)],
            out_specs=pl.BlockSpec((FFN_TILE, half), row),
            scratch_shapes=[pltpu.VMEM(gshape(wg), BF16), pltpu.VMEM(gshape(wu), BF16),
                            pltpu.VMEM(gshape(wd), BF16)]),
        out_shape=jax.ShapeDtypeStruct((n_pad, half), jnp.uint32),
        compiler_params=pltpu.CompilerParams(dimension_semantics=("arbitrary",), vmem_limit_bytes=VMEM_LIMIT),
        name="ffn",
    )(tile_group, tile_k, *slots, hs, gs, wg, wu, wd)


def _residual_kernel(x_ref, y_ref, mod_ref, fg_ref, o_ref, *, final_norm):
    for i in range(x_ref.shape[0]):
        lo, hi = _unpack_bf16_pairs(y_ref[i])
        y = x_ref[i] + mod_ref[i, 0, 5:6, :] * jnp.concatenate([lo, hi], axis=1)
        if final_norm:
            y = _rms(y, fg_ref[...])
        o_ref[i] = y


def _residual(x1, yg, mod, fg, t0, final_norm):
    batch, t_out, d = x1.shape
    nb = PAIR
    tok = lambda b, t: (b, t, 0)
    return pl.pallas_call(
        functools.partial(_residual_kernel, final_norm=final_norm),
        grid=(batch // nb, t_out // TILE),
        in_specs=[pl.BlockSpec((nb, TILE, d), tok),
                  pl.BlockSpec((nb, TILE, d // 2), tok),
                  pl.BlockSpec((nb, 1, 8, d), lambda b, t: (b, jnp.minimum(t + t0, 1), 0, 0)),
                  pl.BlockSpec((1, d), lambda b, t: (0, 0))],
        out_specs=pl.BlockSpec((nb, TILE, d), tok),
        out_shape=jax.ShapeDtypeStruct((batch, t_out, d), F32),
        compiler_params=pltpu.CompilerParams(dimension_semantics=("parallel", "parallel"),
                                             vmem_limit_bytes=VMEM_LIMIT),
        name="residual",
    )(x1, yg, mod, fg)


def _moe(h2, gates, bucket, wg, wu, wd, layer):
    batch, t_out, half = h2.shape
    n = batch * t_out
    n_pad = n + (N_EXPERTS // EXPERTS_PER_GROUP) * FFN_TILE
    assert n % LANES == 0 and n // LANES <= PLAN_ROWS and n_pad // FFN_TILE <= LANES
    pos2d, tile_group, tile_k, slots = _bucket_plan(bucket, n, n_pad)
    hs, gs = _sc_dispatch(h2.reshape(n, half), gates.reshape(n, LANES), pos2d, n_pad)
    ys = _ffn(tile_group, tile_k, slots, hs, gs, wg, wu, wd, layer)
    return _sc_combine(ys, pos2d, n).reshape(batch, t_out, half)


def _rope_tables(seq, ctx_len):
    rows = seq // GRID_W
    row_pos = jnp.repeat(jnp.arange(rows, dtype=F32), GRID_W)
    col_pos = jnp.tile(jnp.arange(GRID_W, dtype=F32), rows)
    axis_dim = HEAD_DIM // 2
    inv_freq = ROPE_BASE ** (-jnp.arange(0, axis_dim, 2, dtype=F32) / axis_dim)
    ang_r = row_pos[:, None] * inv_freq[None, :]
    ang_c = col_pos[:, None] * inv_freq[None, :]
    z = jnp.zeros_like(ang_r)
    cos = jnp.concatenate([jnp.cos(ang_r)] * 2 + [jnp.cos(ang_c)] * 2, axis=-1)
    sin_a = jnp.concatenate([-jnp.sin(ang_r), z, -jnp.sin(ang_c), z], axis=-1)
    sin_b = jnp.concatenate([z, jnp.sin(ang_r), z, jnp.sin(ang_c)], axis=-1)

    def full(tab, fill):
        tab = jnp.tile(tab, (1, LANES // HEAD_DIM))
        return jnp.concatenate([jnp.full((ctx_len, LANES), fill, F32), tab], axis=0)

    return full(cos, 1.0), full(sin_a, 0.0), full(sin_b, 0.0)


def kernel(x, c, ctx, c_ctx, w_ada, b_ada, norm1_g, norm2_g, w_in, w_out, lam_q1, lam_k1, lam_q2, lam_k2,
           subln_g, q_norm_g, k_norm_g, sink, w_router, router_bias, w_gate, w_up, w_down, final_g):
    batch, seq, d = x.shape
    ctx_len = ctx.shape[1]
    depth = w_in.shape[0]
    sc_rows = SC_WIN * math.prod(_sc_workers())
    assert ctx_len == TILE and seq % TILE == 0 and batch % STREAM_BATCH == 0 and batch <= 15
    assert STREAM_BATCH == 2 * PAIR
    assert (STREAM_BATCH * seq) % sc_rows == 0 and (STREAM_BATCH * (seq + ctx_len)) % sc_rows == 0

    cpad = jnp.zeros((16, d), F32).at[:batch].set(c).at[batch].set(c_ctx)
    mod_all = _adaln(cpad, w_ada, b_ada)
    cos, sin_a, sin_b = _rope_tables(seq, ctx_len)
    mseg = jnp.asarray(np.kron(np.eye(2), np.ones((HEAD_DIM, HEAD_DIM))), BF16)
    wr = w_router.T
    wrh = wr.astype(BF16)
    wrl = (wr - wrh.astype(F32)).astype(BF16)
    rb = router_bias.reshape(N_EXPERTS, 1)

    w_in_p = [w_in[l].astype(BF16) for l in range(depth)]
    w_out_p = [w_out[l].astype(BF16) for l in range(depth)]
    m8 = [jnp.concatenate([mod_all[l].reshape(16, 6, d), jnp.zeros((16, 2, d), F32)], axis=1) for l in range(depth)]

    outs = []
    for b0 in range(0, batch, STREAM_BATCH):
        nbs = STREAM_BATCH
        stream = (ctx[b0:b0 + nbs], x[b0:b0 + nbs])
        for l in range(depth):
            last = l == depth - 1
            lambda_init = 0.8 - 0.6 * math.exp(-0.3 * l)
            mod = jnp.stack([jnp.broadcast_to(m8[l][batch], (nbs, 8, d)), m8[l][b0:b0 + nbs]], axis=1)
            gq = jnp.tile(q_norm_g[l], 2).reshape(1, LANES)
            gk = jnp.tile(k_norm_g[l], 2).reshape(1, LANES)
            q, k, v = _inproj(stream, mod, norm1_g[l].reshape(1, d), w_in_p[l], cos, sin_a, sin_b, gq, gk, mseg)

            t0 = 1 if last else 0
            lamv = jnp.stack([lam_q1[l], lam_k1[l], lam_q2[l], lam_k2[l]])
            mix = _attention(sink[l], lamv, subln_g[l].reshape(1, LANES), q, k, v, t0, lambda_init)
            x1, h2, gates, bucket = _outproj(stream, mix, w_out_p[l], mod, norm2_g[l].reshape(1, d), wrh, wrl, rb, t0)
            stream = (x1, _moe(h2, gates, bucket, w_gate, w_up, w_down, l), mod)
        outs.append(_residual(*stream, final_g.reshape(1, d), 1, final_norm=True))
    return jnp.concatenate(outs, axis=0)
```

```python
import functools
import math
from typing import NamedTuple

import numpy as np
import jax
import jax.numpy as jnp
from jax import lax
from jax.experimental import pallas as pl
from jax.experimental.pallas import tpu as pltpu
from jax.experimental.pallas import tpu_sc as plsc

F32 = jnp.float32
BF16 = jnp.bfloat16

HEAD_DIM = 64
GRID_W = 64
ROPE_BASE = 10000.0
RMS_EPS = 1e-6
WINDOW = 128
N_EXPERTS = 16
EXPERTS_PER_GROUP = 4
LANES = 128
TILE = 256
PAIR = 2
FFN_TILE = 256
SC_WIN = 32
STREAM_BATCH = 4
GROUP_LANE = 16
PAIRS = ((0, 1), (0, 2), (0, 3), (1, 3), (1, 2), (2, 3))
NEG_BIG = -1e30
KEY_CHUNK = 256
VMEM_LIMIT = 52 * 1024 * 1024

QA, QB, QC = 512, 256, 256
KA, KB, KC = 512, 128, 128
Q_COLS = QA + QB + QC
K_COLS = KA + KB + KC
V_COLS = 768


def _dot(a, b):
    return jnp.dot(a, b, preferred_element_type=F32)


def _dot_nt(a, b):
    return lax.dot_general(a, b, (((1,), (1,)), ((), ())), preferred_element_type=F32)


def _split(a):
    hi = a.astype(BF16)
    lo = (a - hi.astype(F32)).astype(BF16)
    return hi, lo


def _dot3(a, b):
    ah, al = _split(a)
    bh, bl = _split(b)
    return _dot(ah, bh) + _dot(ah, bl) + _dot(al, bh)


def _rms(x, g):
    ms = jnp.mean(x * x, axis=-1, keepdims=True)
    return x * lax.rsqrt(ms + RMS_EPS) * g


def _sigmoid(x):
    return 1.0 / (1.0 + jnp.exp(-x))


def _pack_bf16_pairs(y):
    n = y.shape[1] // 2
    yb = y.astype(BF16).astype(F32)
    lo = lax.bitcast_convert_type(yb[:, :n], jnp.uint32) >> 16
    hi = lax.bitcast_convert_type(yb[:, n:], jnp.uint32) & jnp.uint32(0xFFFF0000)
    return hi | lo


def _unpack_bf16_pairs(w):
    lo = lax.bitcast_convert_type(w << 16, F32)
    hi = lax.bitcast_convert_type(w & jnp.uint32(0xFFFF0000), F32)
    return lo, hi


class _Inputs(NamedTuple):
    ctx: jax.Array
    x: jax.Array
    first: int


def _stream_arrays(stream):
    return (stream.ctx, stream.x) if isinstance(stream, _Inputs) else stream


def _stream_dims(stream):
    if isinstance(stream, _Inputs):
        return STREAM_BATCH, stream.ctx.shape[1] + stream.x.shape[1], stream.x.shape[2]
    return stream[0].shape


def _stream_rows(stream_refs, i, tile):
    if len(stream_refs) == 2:
        ctx_ref, x_ref = stream_refs
        return jnp.where(tile == 0, ctx_ref[i], x_ref[i])
    x_ref, y_ref, mod_ref = stream_refs
    lo, hi = _unpack_bf16_pairs(y_ref[i])
    return x_ref[i] + mod_ref[i, 0, 5:6, :] * jnp.concatenate([lo, hi], axis=1)


def _stream_specs(stream, nb, tile_of):
    d = stream[0].shape[-1]
    if isinstance(stream, _Inputs):
        b0 = stream.first // nb
        return [pl.BlockSpec((nb, TILE, d), lambda b, t: (b + b0, 0, 0)),
                pl.BlockSpec((nb, TILE, d), lambda b, t: (b + b0, jnp.maximum(tile_of(t) - 1, 0), 0))]
    tok = lambda b, t: (b, tile_of(t), 0)
    return [pl.BlockSpec((nb, TILE, d), tok),
            pl.BlockSpec((nb, TILE, d // 2), tok),
            pl.BlockSpec((nb, 1, 8, d), lambda b, t: (b, jnp.minimum(tile_of(t), 1), 0, 0))]


def _adaln_kernel(c_ref, w_ref, b_ref, o_ref):
    cv = c_ref[...]
    s = cv * _sigmoid(cv)
    o_ref[0] = _dot3(s, w_ref[0]) + b_ref[0]


def _adaln(cpad, w_ada, b_ada):
    depth, d, n6 = w_ada.shape
    tn = 1536
    return pl.pallas_call(
        _adaln_kernel,
        grid=(depth, n6 // tn),
        in_specs=[pl.BlockSpec((16, d), lambda l, j: (0, 0)),
                  pl.BlockSpec((1, d, tn), lambda l, j: (l, 0, j)),
                  pl.BlockSpec((1, 1, tn), lambda l, j: (l, 0, j))],
        out_specs=pl.BlockSpec((1, 16, tn), lambda l, j: (l, 0, j)),
        out_shape=jax.ShapeDtypeStruct((depth, 16, n6), F32),
        compiler_params=pltpu.CompilerParams(dimension_semantics=("parallel", "parallel"),
                                             vmem_limit_bytes=VMEM_LIMIT),
        name="adaln",
    )(cpad, w_ada, b_ada.reshape(depth, 1, n6))


def _inproj_kernel(*refs, n_stream):
    for base in range(0, refs[0].shape[0], PAIR):
        _inproj_rows(base, pl.program_id(1), refs[:n_stream], *refs[n_stream:])


def _inproj_rows(base, tile, stream_refs, mod_ref, g_ref, w_ref, cos_ref, sa_ref, sb_ref, gq_ref, gk_ref, mseg_ref,
                 q_ref, k_ref, v_ref):
    members = range(base, base + PAIR)
    h = jnp.concatenate(
        [(_rms(_stream_rows(stream_refs, i, tile), g_ref[...]) * (1.0 + mod_ref[i, 0, 1:2, :])
          + mod_ref[i, 0, 0:1, :]).astype(BF16) for i in members], axis=0)
    cos = jnp.concatenate([cos_ref[...]] * PAIR, axis=0)
    sa = jnp.concatenate([sa_ref[...]] * PAIR, axis=0)
    sb = jnp.concatenate([sb_ref[...]] * PAIR, axis=0)
    mseg = mseg_ref[...]

    def put(ref, cols, val):
        for n, i in enumerate(members):
            ref[i, :, cols] = val[n * TILE:(n + 1) * TILE]

    def rope(b):
        return b * cos + pltpu.roll(b, LANES - 16, 1) * sa + pltpu.roll(b, 16, 1) * sb

    def qknorm(b, g):
        hi, lo = _split(b * b)
        ms = (_dot(hi, mseg) + _dot(lo, mseg)) * (1.0 / HEAD_DIM)
        return b * lax.rsqrt(ms + RMS_EPS) * g

    def wcols(*pieces):
        parts = [w_ref[:, a:a + n] for a, n in pieces]
        return parts[0] if len(parts) == 1 else jnp.concatenate(parts, axis=1)

    lane = lax.broadcasted_iota(jnp.int32, (PAIR * TILE, LANES), 1)

    def regroup(p2):
        h0, h1 = p2[:, :LANES], p2[:, LANES:]
        return (jnp.where(lane < HEAD_DIM, h0, pltpu.roll(h1, HEAD_DIM, 1)),
                jnp.where(lane < HEAD_DIM, pltpu.roll(h0, HEAD_DIM, 1), h1))

    q0, kv0 = 0, Q_COLS
    rope_chunks = [wcols((q0, 256)), wcols((q0 + 256, 256)), wcols((q0 + QA, QB)), wcols((q0 + QA + QB, QC)),
                   wcols((kv0, 256)), wcols((kv0 + 256, 256)),
                   wcols((kv0 + 2 * KA, KB), (kv0 + 2 * KA + 2 * KB, KC))]
    v_chunks = [wcols((kv0 + KA, 256)), wcols((kv0 + KA + 256, 256)),
                wcols((kv0 + 2 * KA + KB, KB), (kv0 + 2 * KA + 2 * KB + KC, KC))]
    for c, wc in enumerate(rope_chunks):
        p2 = _dot(h, wc)
        halves = regroup(p2) if c in (2, 3) else (p2[:, :LANES], p2[:, LANES:])
        for half in range(2):
            j = 2 * c + half
            p = halves[half]
            if j in (4, 5):
                p = qknorm(p, gq_ref[...])
            if j == 12:
                p = qknorm(p, gk_ref[...])
            p = rope(p)
            if j < Q_COLS // LANES:
                put(q_ref, slice(j * LANES, (j + 1) * LANES), (p * (HEAD_DIM ** -0.5)).astype(BF16))
            else:
                jk = j - Q_COLS // LANES
                put(k_ref, slice(jk * LANES, (jk + 1) * LANES), p.astype(BF16))
    for c, wc in enumerate(v_chunks):
        put(v_ref, slice(c * 256, (c + 1) * 256), _dot(h, wc).astype(BF16))


def _inproj(stream, mod, g, w, cos, sa, sb, gq, gk, mseg):
    batch, t_len, d = _stream_dims(stream)
    nt = t_len // TILE
    nb = 2 * PAIR
    row = lambda b, t: (t, 0)
    const = lambda b, t: (0, 0)
    tok = lambda b, t: (b, t, 0)
    return pl.pallas_call(
        functools.partial(_inproj_kernel, n_stream=len(_stream_arrays(stream))),
        grid=(batch // nb, nt),
        in_specs=_stream_specs(stream, nb, lambda t: t) + [
                  pl.BlockSpec((nb, 1, 8, d), lambda b, t: (b, jnp.minimum(t, 1), 0, 0)),
                  pl.BlockSpec((1, d), const),
                  pl.BlockSpec(w.shape, const),
                  pl.BlockSpec((TILE, LANES), row),
                  pl.BlockSpec((TILE, LANES), row),
                  pl.BlockSpec((TILE, LANES), row),
                  pl.BlockSpec((1, LANES), const),
                  pl.BlockSpec((1, LANES), const),
                  pl.BlockSpec((LANES, LANES), const)],
        out_specs=[pl.BlockSpec((nb, TILE, Q_COLS), tok),
                   pl.BlockSpec((nb, TILE, K_COLS), tok),
                   pl.BlockSpec((nb, TILE, V_COLS), tok)],
        out_shape=[jax.ShapeDtypeStruct((batch, t_len, Q_COLS), BF16),
                   jax.ShapeDtypeStruct((batch, t_len, K_COLS), BF16),
                   jax.ShapeDtypeStruct((batch, t_len, V_COLS), BF16)],
        compiler_params=pltpu.CompilerParams(dimension_semantics=("parallel", "parallel"),
                                             vmem_limit_bytes=VMEM_LIMIT),
        name="inproj",
    )(*_stream_arrays(stream), mod, g, w, cos, sa, sb, gq, gk, mseg)


def _scores_pass(qq, k_ref, bi, col, starts, s_ref, mask_fn=None, extra=None):
    half = KEY_CHUNK // 2
    rows = qq.shape[0]
    mrun = None
    for j, st in enumerate(starts):
        s = _dot_nt(qq, k_ref[bi, pl.ds(st, KEY_CHUNK), col:col + LANES])
        if mask_fn is not None:
            s = mask_fn(j, st, s)
        s_ref[0:rows, j * KEY_CHUNK:(j + 1) * KEY_CHUNK] = s
        mj = jnp.maximum(s[:, :half], s[:, half:])
        mrun = mj if mrun is None else jnp.maximum(mrun, mj)
    m = jnp.max(mrun, axis=-1, keepdims=True)
    return m if extra is None else jnp.maximum(m, extra)


def _values_pass(m, rows, v_ref, bi, col, starts, s_ref, extra=None):
    half = KEY_CHUNK // 2
    lrun = None
    acc = None
    for j, st in enumerate(starts):
        e = jnp.exp(s_ref[0:rows, j * KEY_CHUNK:(j + 1) * KEY_CHUNK] - m)
        lj = e[:, :half] + e[:, half:]
        lrun = lj if lrun is None else lrun + lj
        pv = _dot(e.astype(BF16), v_ref[bi, pl.ds(st, KEY_CHUNK), col:col + LANES])
        acc = pv if acc is None else acc + pv
    l = jnp.sum(lrun, axis=-1, keepdims=True)
    if extra is not None:
        l = l + jnp.exp(extra - m)
    return acc / l


def _softmax_pv(qq, k_ref, v_ref, bi, col, starts, s_ref, mask_fn=None, extra=None):
    m = _scores_pass(qq, k_ref, bi, col, starts, s_ref, mask_fn, extra)
    return _values_pass(m, qq.shape[0], v_ref, bi, col, starts, s_ref, extra)


def _all_chunks(n_keys):
    return [j * KEY_CHUNK for j in range(n_keys // KEY_CHUNK)]


def _half_masks(shape):
    lane = lax.broadcasted_iota(jnp.int32, shape, 1)
    return lane < HEAD_DIM, lane >= HEAD_DIM


def _stack_group_queries(q2, kv):
    lo_m, hi_m = _half_masks((TILE, LANES))
    m = lo_m if kv == 0 else hi_m
    zero = jnp.zeros((TILE, LANES), q2.dtype)
    return jnp.concatenate([jnp.where(m, q2[:, :LANES], zero), jnp.where(m, q2[:, LANES:], zero)], axis=0)


def _merge_kv_outputs(o_kv0, o_kv1):
    lo_m, _ = _half_masks((TILE, LANES))
    kv0 = jnp.where(lo_m, o_kv0[:TILE], pltpu.roll(o_kv0[TILE:], HEAD_DIM, 1))
    kv1 = jnp.where(lo_m, pltpu.roll(o_kv1[:TILE], HEAD_DIM, 1), o_kv1[TILE:])
    return jnp.concatenate([kv0, kv1], axis=1)


def _attn_kernel(sink_ref, lam_ref, subg_ref, qa_ref, qb_ref, qc_ref, ka_ref, kb_ref, kc_ref, va_ref, vb_ref, vc_ref,
                 o_ref, s_ref, *, t0, ctx_len, lambda_init):
    t = pl.program_id(1) + t0
    t_len = ka_ref.shape[1]
    band = 2 * TILE
    lo_m, hi_m = _half_masks((TILE, LANES))
    zero = jnp.zeros((TILE, LANES), BF16)
    row = lax.broadcasted_iota(jnp.int32, (2 * TILE, 1), 0)
    lv = lam_ref[...]
    lam = (jnp.exp(jnp.sum(lv[0:1] * lv[1:2], keepdims=True))
           - jnp.exp(jnp.sum(lv[2:3] * lv[3:4], keepdims=True)) + lambda_init)

    def sink_col(kv):
        return jnp.where(row < TILE, sink_ref[2 * kv], sink_ref[2 * kv + 1])

    def mixers(starts_ab, starts_c, mask_c):
        n_chain = 0
        for bi in range(qa_ref.shape[0]):
            for h in range(QA // LANES):
                q = qa_ref[bi, :, h * LANES:(h + 1) * LANES]
                qq = jnp.concatenate([jnp.where(lo_m, q, zero), jnp.where(hi_m, q, zero)], axis=0)
                o = _softmax_pv(qq, ka_ref, va_ref, bi, h * LANES, starts_ab, s_ref.at[n_chain % 2])
                n_chain += 1
                o = o[:TILE] - lam * o[TILE:]
                o_ref[bi, :, h * LANES:(h + 1) * LANES] = (
                    _rms(o, subg_ref[...]) * (1.0 - lambda_init)).astype(BF16)
            outs = []
            for kv in range(2):
                outs.append(_softmax_pv(_stack_group_queries(qb_ref[bi], kv), kb_ref, vb_ref, bi, 0, starts_ab,
                                        s_ref.at[n_chain % 2]))
                n_chain += 1
            o_ref[bi, :, QA:QA + QB] = _merge_kv_outputs(*outs).astype(BF16)
            outs = []
            for kv in range(2):
                outs.append(_softmax_pv(_stack_group_queries(qc_ref[bi], kv), kc_ref, vc_ref, bi, 0, starts_c,
                                        s_ref.at[n_chain % 2], mask_c, sink_col(kv)))
                n_chain += 1
            o_ref[bi, :, QA + QB:] = _merge_kv_outputs(*outs).astype(BF16)

    def ctx_step():
        mixers([0], [0], None)

    def latent_step():
        q_start = (t - 1) * TILE
        u0 = pl.multiple_of(jnp.minimum(q_start + ctx_len - WINDOW, t_len - band), WINDOW)
        q_pos = q_start + (lax.broadcasted_iota(jnp.int32, (2 * TILE, KEY_CHUNK), 0) & (TILE - 1))
        key_lane = lax.broadcasted_iota(jnp.int32, (2 * TILE, KEY_CHUNK), 1)

        def mask_c(j, st, s):
            if j == 0:
                return s
            k_pos = (st - ctx_len) + key_lane
            valid = (jnp.abs(k_pos - q_pos) <= WINDOW) & (k_pos >= 0)
            return jnp.where(valid, s, NEG_BIG)

        mixers(_all_chunks(t_len), [0] + [u0 + i * KEY_CHUNK for i in range(band // KEY_CHUNK)], mask_c)

    if t0 == 0:
        pl.when(t == 0)(ctx_step)
        pl.when(t > 0)(latent_step)
    else:
        latent_step()


def _attention(sink, lamv, subg, q, k, v, t0, lambda_init):
    batch, t_len, _ = q.shape
    nt = t_len // TILE
    kern = functools.partial(_attn_kernel, t0=t0, ctx_len=TILE, lambda_init=lambda_init)
    const = lambda b, t: (0, 0)
    nb = 1
    qspec = lambda width, blk: pl.BlockSpec((nb, TILE, width), lambda b, t: (b, t + t0, blk))
    kvspec = lambda width, blk: pl.BlockSpec((nb, t_len, width), lambda b, t: (b, 0, blk))
    return pl.pallas_call(
        kern,
        grid=(batch // nb, nt - t0),
        in_specs=[pl.BlockSpec(memory_space=pltpu.SMEM),
                  pl.BlockSpec(lamv.shape, const),
                  pl.BlockSpec((1, LANES), const),
                  qspec(QA, 0), qspec(QB, QA // QB), qspec(QC, (QA + QB) // QC),
                  kvspec(KA, 0), kvspec(KB, KA // KB), kvspec(KC, (KA + KB) // KC),
                  kvspec(KA, 0), kvspec(KB, KA // KB), kvspec(KC, (KA + KB) // KC)],
        out_specs=pl.BlockSpec((nb, TILE, Q_COLS), lambda b, t: (b, t + t0, 0)),
        out_shape=jax.ShapeDtypeStruct((batch, t_len, Q_COLS), BF16),
        scratch_shapes=[pltpu.VMEM((2, 2 * TILE, t_len), F32)],
        compiler_params=pltpu.CompilerParams(dimension_semantics=("parallel", "parallel"),
                                             vmem_limit_bytes=VMEM_LIMIT),
        name="attention",
    )(sink, lamv, subg, q, q, q, k, k, k, v, v, v)


def _route(scores, bias):
    expert = lax.broadcasted_iota(jnp.int32, scores.shape, 0)
    sel = scores + bias
    in_group = expert & (EXPERTS_PER_GROUP - 1)
    group = expert >> 2

    def neighbours(x, idx, step, span):
        for d in (1, 2, 3):
            fwd = (idx + d) < 4
            y = jnp.where(fwd, pltpu.roll(x, N_EXPERTS - d * step, 0), pltpu.roll(x, span - d * step, 0))
            yield y, fwd

    rank = jnp.zeros_like(sel)
    for y, fwd in neighbours(sel, in_group, 1, EXPERTS_PER_GROUP):
        beats = (y > sel) | ((y == sel) & jnp.logical_not(fwd))
        rank = rank + jnp.where(beats, 1.0, 0.0)
    top2 = jnp.where(rank < 2.0, sel, 0.0)
    gsum = top2
    for y, _ in neighbours(top2, in_group, 1, EXPERTS_PER_GROUP):
        gsum = gsum + y
    grank = jnp.zeros_like(sel)
    for y, fwd in neighbours(gsum, group, EXPERTS_PER_GROUP, N_EXPERTS):
        beats = (y > gsum) | ((y == gsum) & jnp.logical_not(fwd))
        grank = grank + jnp.where(beats, 1.0, 0.0)
    best = grank < 0.5
    chosen = best & (rank < 2.0)
    w = jnp.where(chosen, scores, 0.0)
    gates = w / jnp.sum(w, axis=0, keepdims=True)
    gid = jnp.sum(jnp.where(best & (in_group == 0), group.astype(F32), 0.0), axis=0, keepdims=True)
    bits = jnp.sum(jnp.where(chosen, jnp.left_shift(1, in_group).astype(F32), 0.0), axis=0, keepdims=True)
    pair = sum(jnp.where(bits == float((1 << a) + (1 << b)), float(k), 0.0) for k, (a, b) in enumerate(PAIRS))
    bucket = gid * len(PAIRS) + pair
    tokens = scores.shape[1]
    return jnp.concatenate([gates, jnp.broadcast_to(bucket, (8, tokens)),
                            jnp.zeros((LANES - N_EXPERTS - 8, tokens), F32)], axis=0)


def _outproj_kernel(*refs, n_stream, t0):
    stream_refs = refs[:n_stream]
    tile = pl.program_id(1) + t0
    mix_ref, w_ref, mod_ref, g_ref, wrh_ref, wrl_ref, rb_ref, x1_ref, h2_ref, gates_ref, bucket_ref = refs[n_stream:]
    for i in range(mix_ref.shape[0]):
        x1 = _stream_rows(stream_refs, i, tile) + mod_ref[i, 0, 2:3, :] * _dot(mix_ref[i], w_ref[...])
        x1_ref[i] = x1
        h2 = _rms(x1, g_ref[...]) * (1.0 + mod_ref[i, 0, 4:5, :]) + mod_ref[i, 0, 3:4, :]
        h2_ref[i] = _pack_bf16_pairs(h2)
        hi, lo = _split(h2)
        logits = _dot_nt(wrh_ref[...], hi) + _dot_nt(wrl_ref[...], hi) + _dot_nt(wrh_ref[...], lo)
        routed = _route(_sigmoid(logits), rb_ref[...])
        gates_ref[i] = routed.T
        bucket_ref[i] = routed[GROUP_LANE:GROUP_LANE + 8]


def _outproj(stream, mix, w, mod, g, wrh, wrl, rb, t0):
    batch, t_len, d = _stream_dims(stream)
    nt = t_len // TILE
    nb = 2 * PAIR
    t_out = (nt - t0) * TILE
    const = lambda b, t: (0, 0)
    tok_in = lambda b, t: (b, t + t0, 0)
    tok_out = lambda b, t: (b, t, 0)
    return pl.pallas_call(
        functools.partial(_outproj_kernel, n_stream=len(_stream_arrays(stream)), t0=t0),
        grid=(batch // nb, nt - t0),
        in_specs=_stream_specs(stream, nb, lambda t: t + t0) + [
                  pl.BlockSpec((nb, TILE, Q_COLS), tok_in),
                  pl.BlockSpec(w.shape, const),
                  pl.BlockSpec((nb, 1, 8, d), lambda b, t: (b, jnp.minimum(t + t0, 1), 0, 0)),
                  pl.BlockSpec((1, d), const),
                  pl.BlockSpec((N_EXPERTS, d), const),
                  pl.BlockSpec((N_EXPERTS, d), const),
                  pl.BlockSpec((N_EXPERTS, 1), const)],
        out_specs=[pl.BlockSpec((nb, TILE, d), tok_out),
                   pl.BlockSpec((nb, TILE, d // 2), tok_out),
                   pl.BlockSpec((nb, TILE, LANES), tok_out),
                   pl.BlockSpec((nb, 8, TILE), tok_out)],
        out_shape=[jax.ShapeDtypeStruct((batch, t_out, d), F32),
                   jax.ShapeDtypeStruct((batch, t_out, d // 2), jnp.uint32),
                   jax.ShapeDtypeStruct((batch, t_out, LANES), F32),
                   jax.ShapeDtypeStruct((batch, (nt - t0) * 8, TILE), F32)],
        compiler_params=pltpu.CompilerParams(dimension_semantics=("parallel", "parallel"),
                                             vmem_limit_bytes=VMEM_LIMIT),
        name="outproj",
    )(*_stream_arrays(stream), mix, w, mod, g, wrh, wrl, rb)


def _sc_workers():
    info = plsc.get_sparse_core_info()
    return info.num_cores, info.num_subcores


def _sc_dispatch(hpk, gates, pos2d, n_out):
    n, w = hpk.shape
    gw = gates.shape[1]
    nc, ns = _sc_workers()
    steps = n // (nc * ns * SC_WIN)
    mesh = plsc.VectorSubcoreMesh(core_axis_name="c", subcore_axis_name="s")

    @functools.partial(
        pl.kernel, mesh=mesh,
        out_type=(jax.ShapeDtypeStruct((n_out, w), hpk.dtype), jax.ShapeDtypeStruct((n_out, gw), gates.dtype)),
        scratch_types=[pltpu.VMEM((1, SC_WIN), jnp.int32), pltpu.VMEM((SC_WIN, w), hpk.dtype),
                       pltpu.VMEM((SC_WIN, gw), gates.dtype), pltpu.SemaphoreType.DMA, pltpu.SemaphoreType.DMA],
        name="sc_dispatch")
    def k(h_hbm, g_hbm, pos_hbm, hs_hbm, gs_hbm, idx_v, rows_v, grow_v, sem_in, sem_out):
        wid = lax.axis_index("s") * nc + lax.axis_index("c")

        @pl.loop(0, steps)
        def _(j):
            blk = wid * steps + j
            off = pl.multiple_of(blk * SC_WIN, SC_WIN)
            loads = [pltpu.async_copy(pos_hbm.at[pl.ds(blk, 1)], idx_v, sem_in),
                     pltpu.async_copy(h_hbm.at[pl.ds(off, SC_WIN)], rows_v, sem_in),
                     pltpu.async_copy(g_hbm.at[pl.ds(off, SC_WIN)], grow_v, sem_in)]
            for cp in loads:
                cp.wait()
            stores = [pltpu.async_copy(rows_v, hs_hbm.at[idx_v.at[0]], sem_out),
                      pltpu.async_copy(grow_v, gs_hbm.at[idx_v.at[0]], sem_out)]
            for cp in stores:
                cp.wait()

    return k(hpk, gates, pos2d)


def _sc_combine(ys, pos2d, n):
    w = ys.shape[1]
    nc, ns = _sc_workers()
    steps = n // (nc * ns * SC_WIN)
    mesh = plsc.VectorSubcoreMesh(core_axis_name="c", subcore_axis_name="s")

    @functools.partial(
        pl.kernel, mesh=mesh,
        out_type=jax.ShapeDtypeStruct((n, w), ys.dtype),
        scratch_types=[pltpu.VMEM((1, SC_WIN), jnp.int32), pltpu.VMEM((SC_WIN, w), ys.dtype)],
        name="sc_combine")
    def k(y_hbm, pos_hbm, o_hbm, idx_v, rows_v):
        wid = lax.axis_index("s") * nc + lax.axis_index("c")

        @pl.loop(0, steps)
        def _(j):
            blk = wid * steps + j
            off = pl.multiple_of(blk * SC_WIN, SC_WIN)
            pltpu.sync_copy(pos_hbm.at[pl.ds(blk, 1)], idx_v)
            pltpu.sync_copy(y_hbm.at[idx_v.at[0]], rows_v)
            pltpu.sync_copy(rows_v, o_hbm.at[pl.ds(off, SC_WIN)])

    return k(ys, pos2d)


PLAN_ROWS = 256


def _plan_kernel(g_ref, pos_ref, tg_ref, tk_ref, slots_ref):
    _plan_finish(g_ref, pos_ref, tg_ref, tk_ref, slots_ref)


def _plan_finish(g_ref, pos_ref, tg_ref, tk_ref, slots_ref):
    g = g_ref[...]
    rows = g.shape[0]
    n_groups = N_EXPERTS // EXPERTS_PER_GROUP
    before_lane = (lax.broadcasted_iota(jnp.int32, (LANES, LANES), 0)
                   < lax.broadcasted_iota(jnp.int32, (LANES, LANES), 1)).astype(BF16)
    before_row = (lax.broadcasted_iota(jnp.int32, (rows, rows), 1)
                  < lax.broadcasted_iota(jnp.int32, (rows, rows), 0)).astype(BF16)
    tile_row = lax.broadcasted_iota(jnp.int32, (1, LANES), 1).astype(F32) * FFN_TILE
    expert = lax.broadcasted_iota(jnp.int32, (N_EXPERTS, 1), 0)
    pos = jnp.zeros_like(g)
    start = jnp.zeros((1, 1), F32)
    tile_group = jnp.zeros((1, LANES), F32)
    need = jnp.zeros((N_EXPERTS, LANES), F32)
    for grp in range(n_groups):
        for k, (ea, eb) in enumerate(PAIRS):
            m = jnp.where(g == grp * len(PAIRS) + k, 1.0, 0.0)
            in_row = _dot(m.astype(BF16), before_lane)
            row_total = jnp.sum(m, axis=1, keepdims=True)
            rows_before = _dot(before_row, jnp.broadcast_to(row_total, m.shape).astype(BF16))
            pos = pos + m * (start + in_row + rows_before)
            end = start + jnp.sum(row_total, axis=0, keepdims=True)
            holds = jnp.where((tile_row < end) & (tile_row + FFN_TILE > start) & (end > start), 1.0, 0.0)
            uses = jnp.where((expert == grp * EXPERTS_PER_GROUP + ea) | (expert == grp * EXPERTS_PER_GROUP + eb),
                             1.0, 0.0)
            need = need + uses * holds
            start = end
        start = jnp.floor((start + (FFN_TILE - 1)) * (1.0 / FFN_TILE)) * FFN_TILE
        if grp < n_groups - 1:
            tile_group = tile_group + jnp.where(tile_row >= start, 1.0, 0.0)
    pos_ref[...] = pos.astype(jnp.int32)
    tg_ref[...] = tile_group.astype(jnp.int32)
    count = jnp.zeros((1, LANES), F32)
    slots = [tile_group * EXPERTS_PER_GROUP] * EXPERTS_PER_GROUP
    for e in range(N_EXPERTS):
        used = need[e:e + 1] > 0.0
        slots = [jnp.where(used & (count == k), float(e), slot) for k, slot in enumerate(slots)]
        count = count + jnp.where(used, 1.0, 0.0)
    tk_ref[...] = count.astype(jnp.int32)
    slots_ref[...] = jnp.concatenate(slots + [jnp.zeros((8 - EXPERTS_PER_GROUP, LANES), F32)], axis=0).astype(jnp.int32)


def _bucket_plan(bucket, n, n_pad):
    g = bucket.reshape(bucket.shape[0], -1, 8, TILE)[:, :, 0, :].reshape(n // LANES, LANES)
    g = jnp.pad(g, ((0, PLAN_ROWS - n // LANES), (0, 0)), constant_values=-1.0)
    pos, tile_group, tile_k, slots = pl.pallas_call(
        _plan_kernel,
        out_shape=[jax.ShapeDtypeStruct((PLAN_ROWS, LANES), jnp.int32),
                   jax.ShapeDtypeStruct((1, LANES), jnp.int32),
                   jax.ShapeDtypeStruct((1, LANES), jnp.int32),
                   jax.ShapeDtypeStruct((8, LANES), jnp.int32)],
        name="plan",
    )(g)
    n_tiles = n_pad // FFN_TILE
    return (pos[:n // LANES].reshape(n // SC_WIN, SC_WIN), tile_group[0, :n_tiles], tile_k[0, :n_tiles],
            [slots[k, :n_tiles] for k in range(EXPERTS_PER_GROUP)])


def _ffn_kernel(tg_ref, tk_ref, s0_ref, s1_ref, s2_ref, s3_ref, hs_ref, gs_ref, wg_ref, wu_ref, wd_ref, ys_ref,
                wgb_ref, wub_ref, wdb_ref):
    i = pl.program_id(0)
    slot_refs = (s0_ref, s1_ref, s2_ref, s3_ref)

    @pl.when((i == 0) | (tg_ref[i] != tg_ref[jnp.maximum(i - 1, 0)]))
    def _():
        wgb_ref[...] = wg_ref[0].astype(BF16)
        wub_ref[...] = wu_ref[0].astype(BF16)
        wdb_ref[...] = wd_ref[0].astype(BF16)

    def run_experts(n_experts):
        def body():
            lo, hi = _unpack_bf16_pairs(hs_ref[...])
            lo = lo.astype(BF16)
            hi = hi.astype(BF16)
            half = lo.shape[1]
            gs = gs_ref[...]
            lane = lax.broadcasted_iota(jnp.int32, gs.shape, 1)
            y = None
            for k in range(n_experts):
                e = slot_refs[k][i]
                local = e - tg_ref[i] * EXPERTS_PER_GROUP
                wg, wu, wd = wgb_ref.at[local], wub_ref.at[local], wdb_ref.at[local]
                a = _dot(lo, wg[:half, :]) + _dot(hi, wg[half:, :])
                u = _dot(lo, wu[:half, :]) + _dot(hi, wu[half:, :])
                gate = jnp.sum(jnp.where(lane == e, gs, 0.0), axis=-1, keepdims=True)
                part = _dot(((a * _sigmoid(a)) * u * gate).astype(BF16), wd[...])
                y = part if y is None else y + part
            ys_ref[...] = _pack_bf16_pairs(y)
        return body

    for n_experts in range(2, EXPERTS_PER_GROUP + 1):
        pl.when(tk_ref[i] == n_experts)(run_experts(n_experts))


def _ffn(tile_group, tile_k, slots, hs, gs, wg, wu, wd, layer):
    n_pad, half = hs.shape
    grp = lambda i, tg, *_: (layer, tg[i], 0, 0)
    row = lambda i, *_: (i, 0)
    gshape = lambda w: (EXPERTS_PER_GROUP,) + w.shape[2:]
    wspec = lambda w: pl.BlockSpec((1,) + gshape(w), grp)
    return pl.pallas_call(
        _ffn_kernel,
        grid_spec=pltpu.PrefetchScalarGridSpec(
            num_scalar_prefetch=2 + EXPERTS_PER_GROUP,
            grid=(n_pad // FFN_TILE,),
            in_specs=[pl.BlockSpec((FFN_TILE, half), row),
                      pl.BlockSpec((FFN_TILE, LANES), row),
                      wspec(wg), wspec(wu), wspec(wd)],
            out_specs=pl.BlockSpec((FFN_TILE, half), row),
            scratch_shapes=[pltpu.VMEM(gshape(wg), BF16), pltpu.VMEM(gshape(wu), BF16),
                            pltpu.VMEM(gshape(wd), BF16)]),
        out_shape=jax.ShapeDtypeStruct((n_pad, half), jnp.uint32),
        compiler_params=pltpu.CompilerParams(dimension_semantics=("arbitrary",), vmem_limit_bytes=VMEM_LIMIT),
        name="ffn",
    )(tile_group, tile_k, *slots, hs, gs, wg, wu, wd)


def _residual_kernel(x_ref, y_ref, mod_ref, fg_ref, *rest, final_norm):
    o_ref = rest[-1]
    for i in range(x_ref.shape[0]):
        lo, hi = _unpack_bf16_pairs(y_ref[i])
        y = x_ref[i] + mod_ref[i, 0, 5:6, :] * jnp.concatenate([lo, hi], axis=1)
        if final_norm:
            y = _rms(y, fg_ref[...])
        o_ref[i] = y


def _residual(x1, yg, mod, fg, t0, final_norm, out_batch, first, prev):
    batch, t_out, d = x1.shape
    nb = PAIR
    tok = lambda b, t: (b, t, 0)
    args = (x1, yg, mod, fg) + (() if prev is None else (prev,))
    return pl.pallas_call(
        functools.partial(_residual_kernel, final_norm=final_norm),
        grid=(batch // nb, t_out // TILE),
        in_specs=[pl.BlockSpec((nb, TILE, d), tok),
                  pl.BlockSpec((nb, TILE, d // 2), tok),
                  pl.BlockSpec((nb, 1, 8, d), lambda b, t: (b, jnp.minimum(t + t0, 1), 0, 0)),
                  pl.BlockSpec((1, d), lambda b, t: (0, 0))] + (
                      [] if prev is None else [pl.BlockSpec(memory_space=pl.ANY)]),
        out_specs=pl.BlockSpec((nb, TILE, d), lambda b, t: (b + first // nb, t, 0)),
        out_shape=jax.ShapeDtypeStruct((out_batch, t_out, d), F32),
        input_output_aliases={} if prev is None else {4: 0},
        compiler_params=pltpu.CompilerParams(dimension_semantics=("parallel", "parallel"),
                                             vmem_limit_bytes=VMEM_LIMIT),
        name="residual",
    )(*args)


def _moe(h2, gates, bucket, wg, wu, wd, layer):
    batch, t_out, half = h2.shape
    n = batch * t_out
    n_pad = n + (N_EXPERTS // EXPERTS_PER_GROUP) * FFN_TILE
    assert n % LANES == 0 and n // LANES <= PLAN_ROWS and n_pad // FFN_TILE <= LANES
    pos2d, tile_group, tile_k, slots = _bucket_plan(bucket, n, n_pad)
    hs, gs = _sc_dispatch(h2.reshape(n, half), gates.reshape(n, LANES), pos2d, n_pad)
    ys = _ffn(tile_group, tile_k, slots, hs, gs, wg, wu, wd, layer)
    return _sc_combine(ys, pos2d, n).reshape(batch, t_out, half)


def _rope_tables(seq, ctx_len):
    rows = seq // GRID_W
    row_pos = jnp.repeat(jnp.arange(rows, dtype=F32), GRID_W)
    col_pos = jnp.tile(jnp.arange(GRID_W, dtype=F32), rows)
    axis_dim = HEAD_DIM // 2
    inv_freq = ROPE_BASE ** (-jnp.arange(0, axis_dim, 2, dtype=F32) / axis_dim)
    ang_r = row_pos[:, None] * inv_freq[None, :]
    ang_c = col_pos[:, None] * inv_freq[None, :]
    z = jnp.zeros_like(ang_r)
    cos = jnp.concatenate([jnp.cos(ang_r)] * 2 + [jnp.cos(ang_c)] * 2, axis=-1)
    sin_a = jnp.concatenate([-jnp.sin(ang_r), z, -jnp.sin(ang_c), z], axis=-1)
    sin_b = jnp.concatenate([z, jnp.sin(ang_r), z, jnp.sin(ang_c)], axis=-1)

    def full(tab, fill):
        tab = jnp.tile(tab, (1, LANES // HEAD_DIM))
        return jnp.concatenate([jnp.full((ctx_len, LANES), fill, F32), tab], axis=0)

    return full(cos, 1.0), full(sin_a, 0.0), full(sin_b, 0.0)


def kernel(x, c, ctx, c_ctx, w_ada, b_ada, norm1_g, norm2_g, w_in, w_out, lam_q1, lam_k1, lam_q2, lam_k2,
           subln_g, q_norm_g, k_norm_g, sink, w_router, router_bias, w_gate, w_up, w_down, final_g):
    batch, seq, d = x.shape
    ctx_len = ctx.shape[1]
    depth = w_in.shape[0]
    sc_rows = SC_WIN * math.prod(_sc_workers())
    assert ctx_len == TILE and seq % TILE == 0 and batch % STREAM_BATCH == 0 and batch <= 15
    assert STREAM_BATCH == 2 * PAIR
    assert (STREAM_BATCH * seq) % sc_rows == 0 and (STREAM_BATCH * (seq + ctx_len)) % sc_rows == 0

    cpad = jnp.zeros((16, d), F32).at[:batch].set(c).at[batch].set(c_ctx)
    mod_all = _adaln(cpad, w_ada, b_ada)
    cos, sin_a, sin_b = _rope_tables(seq, ctx_len)
    mseg = jnp.asarray(np.kron(np.eye(2), np.ones((HEAD_DIM, HEAD_DIM))), BF16)
    wr = w_router.T
    wrh = wr.astype(BF16)
    wrl = (wr - wrh.astype(F32)).astype(BF16)
    rb = router_bias.reshape(N_EXPERTS, 1)

    w_in_p = [w_in[l].astype(BF16) for l in range(depth)]
    w_out_p = [w_out[l].astype(BF16) for l in range(depth)]
    m8 = [jnp.concatenate([mod_all[l].reshape(16, 6, d), jnp.zeros((16, 2, d), F32)], axis=1) for l in range(depth)]

    out = None
    for b0 in range(0, batch, STREAM_BATCH):
        nbs = STREAM_BATCH
        stream = _Inputs(ctx, x, b0)
        for l in range(depth):
            last = l == depth - 1
            lambda_init = 0.8 - 0.6 * math.exp(-0.3 * l)
            mod = jnp.stack([jnp.broadcast_to(m8[l][batch], (nbs, 8, d)), m8[l][b0:b0 + nbs]], axis=1)
            gq = jnp.tile(q_norm_g[l], 2).reshape(1, LANES)
            gk = jnp.tile(k_norm_g[l], 2).reshape(1, LANES)
            q, k, v = _inproj(stream, mod, norm1_g[l].reshape(1, d), w_in_p[l], cos, sin_a, sin_b, gq, gk, mseg)

            t0 = 1 if last else 0
            lamv = jnp.stack([lam_q1[l], lam_k1[l], lam_q2[l], lam_k2[l]])
            mix = _attention(sink[l], lamv, subln_g[l].reshape(1, LANES), q, k, v, t0, lambda_init)
            x1, h2, gates, bucket = _outproj(stream, mix, w_out_p[l], mod, norm2_g[l].reshape(1, d), wrh, wrl, rb, t0)
            stream = (x1, _moe(h2, gates, bucket, w_gate, w_up, w_down, l), mod)
        out = _residual(*stream, final_g.reshape(1, d), 1, True, batch, b0, out)
    return out
```

```python
import functools
import math

import numpy as np
import jax
import jax.numpy as jnp
from jax import lax
from jax.experimental import pallas as pl
from jax.experimental.pallas import tpu as pltpu
from jax.experimental.pallas import tpu_sc as plsc

F32 = jnp.float32
BF16 = jnp.bfloat16

HEAD_DIM = 64
GRID_W = 64
ROPE_BASE = 10000.0
RMS_EPS = 1e-6
WINDOW = 128
N_EXPERTS = 16
EXPERTS_PER_GROUP = 4
LANES = 128
TILE = 256
PAIR = 2
FFN_TILE = 512
SC_WIN = 64
GROUP_LANE = 16
PAIRS = ((0, 1), (0, 2), (0, 3), (1, 3), (1, 2), (2, 3))
NEG_BIG = -1e30
KEY_CHUNK = 256
VMEM_LIMIT = 52 * 1024 * 1024

QA, QB, QC = 512, 256, 256
KA, KB, KC = 512, 128, 128
Q_COLS = QA + QB + QC
K_COLS = KA + KB + KC
V_COLS = 768


def _dot(a, b):
    return jnp.dot(a, b, preferred_element_type=F32)


def _dot_nt(a, b):
    return lax.dot_general(a, b, (((1,), (1,)), ((), ())), preferred_element_type=F32)


def _split(a):
    hi = a.astype(BF16)
    lo = (a - hi.astype(F32)).astype(BF16)
    return hi, lo


def _dot3(a, b):
    ah, al = _split(a)
    bh, bl = _split(b)
    return _dot(ah, bh) + _dot(ah, bl) + _dot(al, bh)


def _rms(x, g):
    ms = jnp.mean(x * x, axis=-1, keepdims=True)
    return x * lax.rsqrt(ms + RMS_EPS) * g


def _sigmoid(x):
    return 1.0 / (1.0 + jnp.exp(-x))


def _pack_bf16_pairs(y):
    n = y.shape[1] // 2
    yb = y.astype(BF16).astype(F32)
    lo = lax.bitcast_convert_type(yb[:, :n], jnp.uint32) >> 16
    hi = lax.bitcast_convert_type(yb[:, n:], jnp.uint32) & jnp.uint32(0xFFFF0000)
    return hi | lo


def _unpack_bf16_pairs(w):
    lo = lax.bitcast_convert_type(w << 16, F32)
    hi = lax.bitcast_convert_type(w & jnp.uint32(0xFFFF0000), F32)
    return lo, hi


def _stream_dims(stream):
    batch, t_len, d = stream[0].shape
    return batch, (t_len + stream[1].shape[1] if len(stream) == 2 else t_len), d


def _stream_rows(stream_refs, i, tile):
    if len(stream_refs) == 2:
        ctx_ref, x_ref = stream_refs
        return jnp.where(tile == 0, ctx_ref[i], x_ref[i])
    x_ref, y_ref, mod_ref = stream_refs
    lo, hi = _unpack_bf16_pairs(y_ref[i])
    return x_ref[i] + mod_ref[i, 0, 5:6, :] * jnp.concatenate([lo, hi], axis=1)


def _stream_specs(stream, nb, tile_of):
    d = stream[0].shape[-1]
    if len(stream) == 2:
        return [pl.BlockSpec((nb, TILE, d), lambda b, t: (b, 0, 0)),
                pl.BlockSpec((nb, TILE, d), lambda b, t: (b, jnp.maximum(tile_of(t) - 1, 0), 0))]
    tok = lambda b, t: (b, tile_of(t), 0)
    return [pl.BlockSpec((nb, TILE, d), tok),
            pl.BlockSpec((nb, TILE, d // 2), tok),
            pl.BlockSpec((nb, 1, 8, d), lambda b, t: (b, jnp.minimum(tile_of(t), 1), 0, 0))]


def _adaln_kernel(c_ref, w_ref, b_ref, o_ref):
    cv = c_ref[...]
    s = cv * _sigmoid(cv)
    o_ref[0] = _dot3(s, w_ref[0]) + b_ref[0]


def _adaln(cpad, w_ada, b_ada):
    depth, d, n6 = w_ada.shape
    tn = 1536
    return pl.pallas_call(
        _adaln_kernel,
        grid=(depth, n6 // tn),
        in_specs=[pl.BlockSpec((16, d), lambda l, j: (0, 0)),
                  pl.BlockSpec((1, d, tn), lambda l, j: (l, 0, j)),
                  pl.BlockSpec((1, 1, tn), lambda l, j: (l, 0, j))],
        out_specs=pl.BlockSpec((1, 16, tn), lambda l, j: (l, 0, j)),
        out_shape=jax.ShapeDtypeStruct((depth, 16, n6), F32),
        compiler_params=pltpu.CompilerParams(dimension_semantics=("parallel", "parallel"),
                                             vmem_limit_bytes=VMEM_LIMIT),
        name="adaln",
    )(cpad, w_ada, b_ada.reshape(depth, 1, n6))


def _inproj_kernel(*refs, n_stream):
    for base in range(0, refs[0].shape[0], PAIR):
        _inproj_rows(base, pl.program_id(1), refs[:n_stream], *refs[n_stream:])


def _inproj_rows(base, tile, stream_refs, mod_ref, g_ref, w_ref, cos_ref, sa_ref, sb_ref, gq_ref, gk_ref, mseg_ref,
                 q_ref, k_ref, v_ref):
    members = range(base, base + PAIR)
    h = jnp.concatenate(
        [(_rms(_stream_rows(stream_refs, i, tile), g_ref[...]) * (1.0 + mod_ref[i, 0, 1:2, :])
          + mod_ref[i, 0, 0:1, :]).astype(BF16) for i in members], axis=0)
    cos = jnp.concatenate([cos_ref[...]] * PAIR, axis=0)
    sa = jnp.concatenate([sa_ref[...]] * PAIR, axis=0)
    sb = jnp.concatenate([sb_ref[...]] * PAIR, axis=0)
    mseg = mseg_ref[...]

    def put(ref, cols, val):
        for n, i in enumerate(members):
            ref[i, :, cols] = val[n * TILE:(n + 1) * TILE]

    def rope(b):
        return b * cos + pltpu.roll(b, LANES - 16, 1) * sa + pltpu.roll(b, 16, 1) * sb

    def qknorm(b, g):
        hi, lo = _split(b * b)
        ms = (_dot(hi, mseg) + _dot(lo, mseg)) * (1.0 / HEAD_DIM)
        return b * lax.rsqrt(ms + RMS_EPS) * g

    def wcols(*pieces):
        parts = [w_ref[:, a:a + n] for a, n in pieces]
        return parts[0] if len(parts) == 1 else jnp.concatenate(parts, axis=1)

    lane = lax.broadcasted_iota(jnp.int32, (PAIR * TILE, LANES), 1)

    def regroup(p2):
        h0, h1 = p2[:, :LANES], p2[:, LANES:]
        return (jnp.where(lane < HEAD_DIM, h0, pltpu.roll(h1, HEAD_DIM, 1)),
                jnp.where(lane < HEAD_DIM, pltpu.roll(h0, HEAD_DIM, 1), h1))

    q0, kv0 = 0, Q_COLS
    rope_chunks = [wcols((q0, 256)), wcols((q0 + 256, 256)), wcols((q0 + QA, QB)), wcols((q0 + QA + QB, QC)),
                   wcols((kv0, 256)), wcols((kv0 + 256, 256)),
                   wcols((kv0 + 2 * KA, KB), (kv0 + 2 * KA + 2 * KB, KC))]
    v_chunks = [wcols((kv0 + KA, 256)), wcols((kv0 + KA + 256, 256)),
                wcols((kv0 + 2 * KA + KB, KB), (kv0 + 2 * KA + 2 * KB + KC, KC))]
    for c, wc in enumerate(rope_chunks):
        p2 = _dot(h, wc)
        halves = regroup(p2) if c in (2, 3) else (p2[:, :LANES], p2[:, LANES:])
        for half in range(2):
            j = 2 * c + half
            p = halves[half]
            if j in (4, 5):
                p = qknorm(p, gq_ref[...])
            if j == 12:
                p = qknorm(p, gk_ref[...])
            p = rope(p)
            if j < Q_COLS // LANES:
                put(q_ref, slice(j * LANES, (j + 1) * LANES), (p * (HEAD_DIM ** -0.5)).astype(BF16))
            else:
                jk = j - Q_COLS // LANES
                put(k_ref, slice(jk * LANES, (jk + 1) * LANES), p.astype(BF16))
    for c, wc in enumerate(v_chunks):
        put(v_ref, slice(c * 256, (c + 1) * 256), _dot(h, wc).astype(BF16))


def _inproj(stream, mod, g, w, cos, sa, sb, gq, gk, mseg):
    batch, t_len, d = _stream_dims(stream)
    nt = t_len // TILE
    nb = 2 * PAIR
    row = lambda b, t: (t, 0)
    const = lambda b, t: (0, 0)
    tok = lambda b, t: (b, t, 0)
    return pl.pallas_call(
        functools.partial(_inproj_kernel, n_stream=len(stream)),
        grid=(batch // nb, nt),
        in_specs=_stream_specs(stream, nb, lambda t: t) + [
                  pl.BlockSpec((nb, 1, 8, d), lambda b, t: (b, jnp.minimum(t, 1), 0, 0)),
                  pl.BlockSpec((1, d), const),
                  pl.BlockSpec(w.shape, const),
                  pl.BlockSpec((TILE, LANES), row),
                  pl.BlockSpec((TILE, LANES), row),
                  pl.BlockSpec((TILE, LANES), row),
                  pl.BlockSpec((1, LANES), const),
                  pl.BlockSpec((1, LANES), const),
                  pl.BlockSpec((LANES, LANES), const)],
        out_specs=[pl.BlockSpec((nb, TILE, Q_COLS), tok),
                   pl.BlockSpec((nb, TILE, K_COLS), tok),
                   pl.BlockSpec((nb, TILE, V_COLS), tok)],
        out_shape=[jax.ShapeDtypeStruct((batch, t_len, Q_COLS), BF16),
                   jax.ShapeDtypeStruct((batch, t_len, K_COLS), BF16),
                   jax.ShapeDtypeStruct((batch, t_len, V_COLS), BF16)],
        compiler_params=pltpu.CompilerParams(dimension_semantics=("parallel", "parallel"),
                                             vmem_limit_bytes=VMEM_LIMIT),
        name="inproj",
    )(*stream, mod, g, w, cos, sa, sb, gq, gk, mseg)


def _scores_pass(qq, k_ref, bi, col, starts, s_ref, mask_fn=None, extra=None):
    half = KEY_CHUNK // 2
    rows = qq.shape[0]
    mrun = None
    for j, st in enumerate(starts):
        s = _dot_nt(qq, k_ref[bi, pl.ds(st, KEY_CHUNK), col:col + LANES])
        if mask_fn is not None:
            s = mask_fn(j, st, s)
        s_ref[0:rows, j * KEY_CHUNK:(j + 1) * KEY_CHUNK] = s
        mj = jnp.maximum(s[:, :half], s[:, half:])
        mrun = mj if mrun is None else jnp.maximum(mrun, mj)
    m = jnp.max(mrun, axis=-1, keepdims=True)
    return m if extra is None else jnp.maximum(m, extra)


def _values_pass(m, rows, v_ref, bi, col, starts, s_ref, extra=None):
    half = KEY_CHUNK // 2
    lrun = None
    acc = None
    for j, st in enumerate(starts):
        e = jnp.exp(s_ref[0:rows, j * KEY_CHUNK:(j + 1) * KEY_CHUNK] - m)
        lj = e[:, :half] + e[:, half:]
        lrun = lj if lrun is None else lrun + lj
        pv = _dot(e.astype(BF16), v_ref[bi, pl.ds(st, KEY_CHUNK), col:col + LANES])
        acc = pv if acc is None else acc + pv
    l = jnp.sum(lrun, axis=-1, keepdims=True)
    if extra is not None:
        l = l + jnp.exp(extra - m)
    return acc / l


def _softmax_pv(qq, k_ref, v_ref, bi, col, starts, s_ref, mask_fn=None, extra=None):
    m = _scores_pass(qq, k_ref, bi, col, starts, s_ref, mask_fn, extra)
    return _values_pass(m, qq.shape[0], v_ref, bi, col, starts, s_ref, extra)


def _all_chunks(n_keys):
    return [j * KEY_CHUNK for j in range(n_keys // KEY_CHUNK)]


def _half_masks(shape):
    lane = lax.broadcasted_iota(jnp.int32, shape, 1)
    return lane < HEAD_DIM, lane >= HEAD_DIM


def _stack_group_queries(q2, kv):
    lo_m, hi_m = _half_masks((TILE, LANES))
    m = lo_m if kv == 0 else hi_m
    zero = jnp.zeros((TILE, LANES), q2.dtype)
    return jnp.concatenate([jnp.where(m, q2[:, :LANES], zero), jnp.where(m, q2[:, LANES:], zero)], axis=0)


def _merge_kv_outputs(o_kv0, o_kv1):
    lo_m, _ = _half_masks((TILE, LANES))
    kv0 = jnp.where(lo_m, o_kv0[:TILE], pltpu.roll(o_kv0[TILE:], HEAD_DIM, 1))
    kv1 = jnp.where(lo_m, pltpu.roll(o_kv1[:TILE], HEAD_DIM, 1), o_kv1[TILE:])
    return jnp.concatenate([kv0, kv1], axis=1)


def _attn_kernel(sink_ref, lam_ref, subg_ref, qa_ref, qb_ref, qc_ref, ka_ref, kb_ref, kc_ref, va_ref, vb_ref, vc_ref,
                 o_ref, s_ref, *, t0, ctx_len, lambda_init):
    t = pl.program_id(1) + t0
    t_len = ka_ref.shape[1]
    band = 2 * TILE
    lo_m, hi_m = _half_masks((TILE, LANES))
    zero = jnp.zeros((TILE, LANES), BF16)
    row = lax.broadcasted_iota(jnp.int32, (2 * TILE, 1), 0)
    lv = lam_ref[...]
    lam = (jnp.exp(jnp.sum(lv[0:1] * lv[1:2], keepdims=True))
           - jnp.exp(jnp.sum(lv[2:3] * lv[3:4], keepdims=True)) + lambda_init)

    def sink_col(kv):
        return jnp.where(row < TILE, sink_ref[2 * kv], sink_ref[2 * kv + 1])

    def mixers(starts_ab, starts_c, mask_c):
        n_chain = 0
        for bi in range(qa_ref.shape[0]):
            for h in range(QA // LANES):
                q = qa_ref[bi, :, h * LANES:(h + 1) * LANES]
                qq = jnp.concatenate([jnp.where(lo_m, q, zero), jnp.where(hi_m, q, zero)], axis=0)
                o = _softmax_pv(qq, ka_ref, va_ref, bi, h * LANES, starts_ab, s_ref.at[n_chain % 2])
                n_chain += 1
                o = o[:TILE] - lam * o[TILE:]
                o_ref[bi, :, h * LANES:(h + 1) * LANES] = (
                    _rms(o, subg_ref[...]) * (1.0 - lambda_init)).astype(BF16)
            outs = []
            for kv in range(2):
                outs.append(_softmax_pv(_stack_group_queries(qb_ref[bi], kv), kb_ref, vb_ref, bi, 0, starts_ab,
                                        s_ref.at[n_chain % 2]))
                n_chain += 1
            o_ref[bi, :, QA:QA + QB] = _merge_kv_outputs(*outs).astype(BF16)
            outs = []
            for kv in range(2):
                outs.append(_softmax_pv(_stack_group_queries(qc_ref[bi], kv), kc_ref, vc_ref, bi, 0, starts_c,
                                        s_ref.at[n_chain % 2], mask_c, sink_col(kv)))
                n_chain += 1
            o_ref[bi, :, QA + QB:] = _merge_kv_outputs(*outs).astype(BF16)

    def ctx_step():
        mixers([0], [0], None)

    def latent_step():
        q_start = (t - 1) * TILE
        u0 = pl.multiple_of(jnp.minimum(q_start + ctx_len - WINDOW, t_len - band), WINDOW)
        q_pos = q_start + (lax.broadcasted_iota(jnp.int32, (2 * TILE, KEY_CHUNK), 0) & (TILE - 1))
        key_lane = lax.broadcasted_iota(jnp.int32, (2 * TILE, KEY_CHUNK), 1)

        def mask_c(j, st, s):
            if j == 0:
                return s
            k_pos = (st - ctx_len) + key_lane
            valid = (jnp.abs(k_pos - q_pos) <= WINDOW) & (k_pos >= 0)
            return jnp.where(valid, s, NEG_BIG)

        mixers(_all_chunks(t_len), [0] + [u0 + i * KEY_CHUNK for i in range(band // KEY_CHUNK)], mask_c)

    if t0 == 0:
        pl.when(t == 0)(ctx_step)
        pl.when(t > 0)(latent_step)
    else:
        latent_step()


def _attention(sink, lamv, subg, q, k, v, t0, lambda_init):
    batch, t_len, _ = q.shape
    nt = t_len // TILE
    kern = functools.partial(_attn_kernel, t0=t0, ctx_len=TILE, lambda_init=lambda_init)
    const = lambda b, t: (0, 0)
    nb = 1
    qspec = lambda width, blk: pl.BlockSpec((nb, TILE, width), lambda b, t: (b, t + t0, blk))
    kvspec = lambda width, blk: pl.BlockSpec((nb, t_len, width), lambda b, t: (b, 0, blk))
    return pl.pallas_call(
        kern,
        grid=(batch // nb, nt - t0),
        in_specs=[pl.BlockSpec(memory_space=pltpu.SMEM),
                  pl.BlockSpec(lamv.shape, const),
                  pl.BlockSpec((1, LANES), const),
                  qspec(QA, 0), qspec(QB, QA // QB), qspec(QC, (QA + QB) // QC),
                  kvspec(KA, 0), kvspec(KB, KA // KB), kvspec(KC, (KA + KB) // KC),
                  kvspec(KA, 0), kvspec(KB, KA // KB), kvspec(KC, (KA + KB) // KC)],
        out_specs=pl.BlockSpec((nb, TILE, Q_COLS), lambda b, t: (b, t + t0, 0)),
        out_shape=jax.ShapeDtypeStruct((batch, t_len, Q_COLS), BF16),
        scratch_shapes=[pltpu.VMEM((2, 2 * TILE, t_len), F32)],
        compiler_params=pltpu.CompilerParams(dimension_semantics=("parallel", "parallel"),
                                             vmem_limit_bytes=VMEM_LIMIT),
        name="attention",
    )(sink, lamv, subg, q, q, q, k, k, k, v, v, v)


def _route(scores, bias):
    expert = lax.broadcasted_iota(jnp.int32, scores.shape, 0)
    sel = scores + bias
    in_group = expert & (EXPERTS_PER_GROUP - 1)
    group = expert >> 2

    def neighbours(x, idx, step, span):
        for d in (1, 2, 3):
            fwd = (idx + d) < 4
            y = jnp.where(fwd, pltpu.roll(x, N_EXPERTS - d * step, 0), pltpu.roll(x, span - d * step, 0))
            yield y, fwd

    rank = jnp.zeros_like(sel)
    for y, fwd in neighbours(sel, in_group, 1, EXPERTS_PER_GROUP):
        beats = (y > sel) | ((y == sel) & jnp.logical_not(fwd))
        rank = rank + jnp.where(beats, 1.0, 0.0)
    top2 = jnp.where(rank < 2.0, sel, 0.0)
    gsum = top2
    for y, _ in neighbours(top2, in_group, 1, EXPERTS_PER_GROUP):
        gsum = gsum + y
    grank = jnp.zeros_like(sel)
    for y, fwd in neighbours(gsum, group, EXPERTS_PER_GROUP, N_EXPERTS):
        beats = (y > gsum) | ((y == gsum) & jnp.logical_not(fwd))
        grank = grank + jnp.where(beats, 1.0, 0.0)
    best = grank < 0.5
    chosen = best & (rank < 2.0)
    w = jnp.where(chosen, scores, 0.0)
    gates = w / jnp.sum(w, axis=0, keepdims=True)
    gid = jnp.sum(jnp.where(best & (in_group == 0), group.astype(F32), 0.0), axis=0, keepdims=True)
    bits = jnp.sum(jnp.where(chosen, jnp.left_shift(1, in_group).astype(F32), 0.0), axis=0, keepdims=True)
    pair = sum(jnp.where(bits == float((1 << a) + (1 << b)), float(k), 0.0) for k, (a, b) in enumerate(PAIRS))
    bucket = gid * len(PAIRS) + pair
    tokens = scores.shape[1]
    return jnp.concatenate([gates, jnp.broadcast_to(bucket, (8, tokens)),
                            jnp.zeros((LANES - N_EXPERTS - 8, tokens), F32)], axis=0)


def _outproj_kernel(*refs, n_stream, t0):
    stream_refs = refs[:n_stream]
    tile = pl.program_id(1) + t0
    mix_ref, w_ref, mod_ref, g_ref, wrh_ref, wrl_ref, rb_ref, x1_ref, h2_ref, gates_ref, bucket_ref = refs[n_stream:]
    for i in range(mix_ref.shape[0]):
        x1 = _stream_rows(stream_refs, i, tile) + mod_ref[i, 0, 2:3, :] * _dot(mix_ref[i], w_ref[...])
        x1_ref[i] = x1
        h2 = _rms(x1, g_ref[...]) * (1.0 + mod_ref[i, 0, 4:5, :]) + mod_ref[i, 0, 3:4, :]
        h2_ref[i] = _pack_bf16_pairs(h2)
        hi, lo = _split(h2)
        logits = _dot_nt(wrh_ref[...], hi) + _dot_nt(wrl_ref[...], hi) + _dot_nt(wrh_ref[...], lo)
        routed = _route(_sigmoid(logits), rb_ref[...])
        gates_ref[i] = routed.T
        bucket_ref[i] = routed[GROUP_LANE:GROUP_LANE + 8]


def _outproj(stream, mix, w, mod, g, wrh, wrl, rb, t0):
    batch, t_len, d = _stream_dims(stream)
    nt = t_len // TILE
    nb = 2 * PAIR
    t_out = (nt - t0) * TILE
    const = lambda b, t: (0, 0)
    tok_in = lambda b, t: (b, t + t0, 0)
    tok_out = lambda b, t: (b, t, 0)
    return pl.pallas_call(
        functools.partial(_outproj_kernel, n_stream=len(stream), t0=t0),
        grid=(batch // nb, nt - t0),
        in_specs=_stream_specs(stream, nb, lambda t: t + t0) + [
                  pl.BlockSpec((nb, TILE, Q_COLS), tok_in),
                  pl.BlockSpec(w.shape, const),
                  pl.BlockSpec((nb, 1, 8, d), lambda b, t: (b, jnp.minimum(t + t0, 1), 0, 0)),
                  pl.BlockSpec((1, d), const),
                  pl.BlockSpec((N_EXPERTS, d), const),
                  pl.BlockSpec((N_EXPERTS, d), const),
                  pl.BlockSpec((N_EXPERTS, 1), const)],
        out_specs=[pl.BlockSpec((nb, TILE, d), tok_out),
                   pl.BlockSpec((nb, TILE, d // 2), tok_out),
                   pl.BlockSpec((nb, TILE, LANES), tok_out),
                   pl.BlockSpec((nb, 8, TILE), tok_out)],
        out_shape=[jax.ShapeDtypeStruct((batch, t_out, d), F32),
                   jax.ShapeDtypeStruct((batch, t_out, d // 2), jnp.uint32),
                   jax.ShapeDtypeStruct((batch, t_out, LANES), F32),
                   jax.ShapeDtypeStruct((batch, (nt - t0) * 8, TILE), F32)],
        compiler_params=pltpu.CompilerParams(dimension_semantics=("parallel", "parallel"),
                                             vmem_limit_bytes=VMEM_LIMIT),
        name="outproj",
    )(*stream, mix, w, mod, g, wrh, wrl, rb)


def _sc_workers():
    info = plsc.get_sparse_core_info()
    return info.num_cores, info.num_subcores


def _sc_dispatch(hpk, gates, pos2d, n_out):
    n, w = hpk.shape
    gw = gates.shape[1]
    nc, ns = _sc_workers()
    steps = n // (nc * ns * SC_WIN)
    mesh = plsc.VectorSubcoreMesh(core_axis_name="c", subcore_axis_name="s")

    @functools.partial(
        pl.kernel, mesh=mesh,
        out_type=(jax.ShapeDtypeStruct((n_out, w), hpk.dtype), jax.ShapeDtypeStruct((n_out, gw), gates.dtype)),
        scratch_types=[pltpu.VMEM((1, SC_WIN), jnp.int32), pltpu.VMEM((SC_WIN, w), hpk.dtype),
                       pltpu.VMEM((SC_WIN, gw), gates.dtype), pltpu.SemaphoreType.DMA, pltpu.SemaphoreType.DMA],
        name="sc_dispatch")
    def k(h_hbm, g_hbm, pos_hbm, hs_hbm, gs_hbm, idx_v, rows_v, grow_v, sem_in, sem_out):
        wid = lax.axis_index("s") * nc + lax.axis_index("c")

        @pl.loop(0, steps)
        def _(j):
            blk = wid * steps + j
            off = pl.multiple_of(blk * SC_WIN, SC_WIN)
            loads = [pltpu.async_copy(pos_hbm.at[pl.ds(blk, 1)], idx_v, sem_in),
                     pltpu.async_copy(h_hbm.at[pl.ds(off, SC_WIN)], rows_v, sem_in),
                     pltpu.async_copy(g_hbm.at[pl.ds(off, SC_WIN)], grow_v, sem_in)]
            for cp in loads:
                cp.wait()
            stores = [pltpu.async_copy(rows_v, hs_hbm.at[idx_v.at[0]], sem_out),
                      pltpu.async_copy(grow_v, gs_hbm.at[idx_v.at[0]], sem_out)]
            for cp in stores:
                cp.wait()

    return k(hpk, gates, pos2d)


def _sc_combine(ys, pos2d, n):
    w = ys.shape[1]
    nc, ns = _sc_workers()
    steps = n // (nc * ns * SC_WIN)
    mesh = plsc.VectorSubcoreMesh(core_axis_name="c", subcore_axis_name="s")

    @functools.partial(
        pl.kernel, mesh=mesh,
        out_type=jax.ShapeDtypeStruct((n, w), ys.dtype),
        scratch_types=[pltpu.VMEM((1, SC_WIN), jnp.int32), pltpu.VMEM((SC_WIN, w), ys.dtype)],
        name="sc_combine")
    def k(y_hbm, pos_hbm, o_hbm, idx_v, rows_v):
        wid = lax.axis_index("s") * nc + lax.axis_index("c")

        @pl.loop(0, steps)
        def _(j):
            blk = wid * steps + j
            off = pl.multiple_of(blk * SC_WIN, SC_WIN)
            pltpu.sync_copy(pos_hbm.at[pl.ds(blk, 1)], idx_v)
            pltpu.sync_copy(y_hbm.at[idx_v.at[0]], rows_v)
            pltpu.sync_copy(rows_v, o_hbm.at[pl.ds(off, SC_WIN)])

    return k(ys, pos2d)


PLAN_ROWS = 256


def _plan_kernel(g_ref, pos_ref, tg_ref, tk_ref, slots_ref):
    _plan_finish(g_ref, pos_ref, tg_ref, tk_ref, slots_ref)


def _plan_finish(g_ref, pos_ref, tg_ref, tk_ref, slots_ref):
    g = g_ref[...]
    rows = g.shape[0]
    n_groups = N_EXPERTS // EXPERTS_PER_GROUP
    before_lane = (lax.broadcasted_iota(jnp.int32, (LANES, LANES), 0)
                   < lax.broadcasted_iota(jnp.int32, (LANES, LANES), 1)).astype(BF16)
    before_row = (lax.broadcasted_iota(jnp.int32, (rows, rows), 1)
                  < lax.broadcasted_iota(jnp.int32, (rows, rows), 0)).astype(BF16)
    tile_row = lax.broadcasted_iota(jnp.int32, (1, LANES), 1).astype(F32) * FFN_TILE
    expert = lax.broadcasted_iota(jnp.int32, (N_EXPERTS, 1), 0)
    pos = jnp.zeros_like(g)
    start = jnp.zeros((1, 1), F32)
    tile_group = jnp.zeros((1, LANES), F32)
    need = jnp.zeros((N_EXPERTS, LANES), F32)
    for grp in range(n_groups):
        for k, (ea, eb) in enumerate(PAIRS):
            m = jnp.where(g == grp * len(PAIRS) + k, 1.0, 0.0)
            in_row = _dot(m.astype(BF16), before_lane)
            row_total = jnp.sum(m, axis=1, keepdims=True)
            rows_before = _dot(before_row, jnp.broadcast_to(row_total, m.shape).astype(BF16))
            pos = pos + m * (start + in_row + rows_before)
            end = start + jnp.sum(row_total, axis=0, keepdims=True)
            holds = jnp.where((tile_row < end) & (tile_row + FFN_TILE > start) & (end > start), 1.0, 0.0)
            uses = jnp.where((expert == grp * EXPERTS_PER_GROUP + ea) | (expert == grp * EXPERTS_PER_GROUP + eb),
                             1.0, 0.0)
            need = need + uses * holds
            start = end
        start = jnp.floor((start + (FFN_TILE - 1)) * (1.0 / FFN_TILE)) * FFN_TILE
        if grp < n_groups - 1:
            tile_group = tile_group + jnp.where(tile_row >= start, 1.0, 0.0)
    pos_ref[...] = pos.astype(jnp.int32)
    tg_ref[...] = tile_group.astype(jnp.int32)
    count = jnp.zeros((1, LANES), F32)
    slots = [tile_group * EXPERTS_PER_GROUP] * EXPERTS_PER_GROUP
    for e in range(N_EXPERTS):
        used = need[e:e + 1] > 0.0
        slots = [jnp.where(used & (count == k), float(e), slot) for k, slot in enumerate(slots)]
        count = count + jnp.where(used, 1.0, 0.0)
    tk_ref[...] = count.astype(jnp.int32)
    slots_ref[...] = jnp.concatenate(slots + [jnp.zeros((8 - EXPERTS_PER_GROUP, LANES), F32)], axis=0).astype(jnp.int32)


def _bucket_plan(bucket, n, n_pad):
    g = bucket.reshape(bucket.shape[0], -1, 8, TILE)[:, :, 0, :].reshape(n // LANES, LANES)
    g = jnp.pad(g, ((0, PLAN_ROWS - n // LANES), (0, 0)), constant_values=-1.0)
    pos, tile_group, tile_k, slots = pl.pallas_call(
        _plan_kernel,
        out_shape=[jax.ShapeDtypeStruct((PLAN_ROWS, LANES), jnp.int32),
                   jax.ShapeDtypeStruct((1, LANES), jnp.int32),
                   jax.ShapeDtypeStruct((1, LANES), jnp.int32),
                   jax.ShapeDtypeStruct((8, LANES), jnp.int32)],
        name="plan",
    )(g)
    n_tiles = n_pad // FFN_TILE
    return (pos[:n // LANES].reshape(n // SC_WIN, SC_WIN), tile_group[0, :n_tiles], tile_k[0, :n_tiles],
            [slots[k, :n_tiles] for k in range(EXPERTS_PER_GROUP)])


def _ffn_kernel(tg_ref, tk_ref, s0_ref, s1_ref, s2_ref, s3_ref, hs_ref, gs_ref, wg_ref, wu_ref, wd_ref, ys_ref):
    i = pl.program_id(0)
    slot_refs = (s0_ref, s1_ref, s2_ref, s3_ref)

    def run_experts(n_experts):
        def body():
            lo, hi = _unpack_bf16_pairs(hs_ref[...])
            lo = lo.astype(BF16)
            hi = hi.astype(BF16)
            half = lo.shape[1]
            gs = gs_ref[...]
            lane = lax.broadcasted_iota(jnp.int32, gs.shape, 1)
            y = None
            for k in range(n_experts):
                e = slot_refs[k][i]
                local = e - tg_ref[i] * EXPERTS_PER_GROUP
                wg, wu, wd = wg_ref.at[local], wu_ref.at[local], wd_ref.at[local]
                a = _dot(lo, wg[:half, :]) + _dot(hi, wg[half:, :])
                u = _dot(lo, wu[:half, :]) + _dot(hi, wu[half:, :])
                gate = jnp.sum(jnp.where(lane == e, gs, 0.0), axis=-1, keepdims=True)
                part = _dot(((a * _sigmoid(a)) * u * gate).astype(BF16), wd[...])
                y = part if y is None else y + part
            ys_ref[...] = _pack_bf16_pairs(y)
        return body

    for n_experts in range(2, EXPERTS_PER_GROUP + 1):
        pl.when(tk_ref[i] == n_experts)(run_experts(n_experts))


def _ffn(tile_group, tile_k, slots, hs, gs, wg, wu, wd):
    n_pad, half = hs.shape
    grp = lambda i, tg, *_: (tg[i], 0, 0)
    row = lambda i, *_: (i, 0)
    wspec = lambda w: pl.BlockSpec((EXPERTS_PER_GROUP,) + w.shape[1:], grp)
    return pl.pallas_call(
        _ffn_kernel,
        grid_spec=pltpu.PrefetchScalarGridSpec(
            num_scalar_prefetch=2 + EXPERTS_PER_GROUP,
            grid=(n_pad // FFN_TILE,),
            in_specs=[pl.BlockSpec((FFN_TILE, half), row),
                      pl.BlockSpec((FFN_TILE, LANES), row),
                      wspec(wg), wspec(wu), wspec(wd)],
            out_specs=pl.BlockSpec((FFN_TILE, half), row)),
        out_shape=jax.ShapeDtypeStruct((n_pad, half), jnp.uint32),
        compiler_params=pltpu.CompilerParams(dimension_semantics=("parallel",), vmem_limit_bytes=VMEM_LIMIT),
        name="ffn",
    )(tile_group, tile_k, *slots, hs, gs, wg, wu, wd)


def _residual_kernel(x_ref, y_ref, mod_ref, fg_ref, o_ref, *, final_norm):
    for i in range(x_ref.shape[0]):
        lo, hi = _unpack_bf16_pairs(y_ref[i])
        y = x_ref[i] + mod_ref[i, 0, 5:6, :] * jnp.concatenate([lo, hi], axis=1)
        if final_norm:
            y = _rms(y, fg_ref[...])
        o_ref[i] = y


def _residual(x1, yg, mod, fg, t0, final_norm):
    batch, t_out, d = x1.shape
    nb = PAIR
    tok = lambda b, t: (b, t, 0)
    return pl.pallas_call(
        functools.partial(_residual_kernel, final_norm=final_norm),
        grid=(batch // nb, t_out // TILE),
        in_specs=[pl.BlockSpec((nb, TILE, d), tok),
                  pl.BlockSpec((nb, TILE, d // 2), tok),
                  pl.BlockSpec((nb, 1, 8, d), lambda b, t: (b, jnp.minimum(t + t0, 1), 0, 0)),
                  pl.BlockSpec((1, d), lambda b, t: (0, 0))],
        out_specs=pl.BlockSpec((nb, TILE, d), tok),
        out_shape=jax.ShapeDtypeStruct((batch, t_out, d), F32),
        compiler_params=pltpu.CompilerParams(dimension_semantics=("parallel", "parallel"),
                                             vmem_limit_bytes=VMEM_LIMIT),
        name="residual",
    )(x1, yg, mod, fg)


def _moe(h2, gates, bucket, wg, wu, wd):
    batch, t_out, half = h2.shape
    n = batch * t_out
    n_pad = n + (N_EXPERTS // EXPERTS_PER_GROUP) * FFN_TILE
    assert n % LANES == 0 and n // LANES <= PLAN_ROWS and n_pad // FFN_TILE <= LANES
    pos2d, tile_group, tile_k, slots = _bucket_plan(bucket, n, n_pad)
    hs, gs = _sc_dispatch(h2.reshape(n, half), gates.reshape(n, LANES), pos2d, n_pad)
    ys = _ffn(tile_group, tile_k, slots, hs, gs, wg, wu, wd)
    return _sc_combine(ys, pos2d, n).reshape(batch, t_out, half)


def _rope_tables(seq, ctx_len):
    rows = seq // GRID_W
    row_pos = jnp.repeat(jnp.arange(rows, dtype=F32), GRID_W)
    col_pos = jnp.tile(jnp.arange(GRID_W, dtype=F32), rows)
    axis_dim = HEAD_DIM // 2
    inv_freq = ROPE_BASE ** (-jnp.arange(0, axis_dim, 2, dtype=F32) / axis_dim)
    ang_r = row_pos[:, None] * inv_freq[None, :]
    ang_c = col_pos[:, None] * inv_freq[None, :]
    z = jnp.zeros_like(ang_r)
    cos = jnp.concatenate([jnp.cos(ang_r)] * 2 + [jnp.cos(ang_c)] * 2, axis=-1)
    sin_a = jnp.concatenate([-jnp.sin(ang_r), z, -jnp.sin(ang_c), z], axis=-1)
    sin_b = jnp.concatenate([z, jnp.sin(ang_r), z, jnp.sin(ang_c)], axis=-1)

    def full(tab, fill):
        tab = jnp.tile(tab, (1, LANES // HEAD_DIM))
        return jnp.concatenate([jnp.full((ctx_len, LANES), fill, F32), tab], axis=0)

    return full(cos, 1.0), full(sin_a, 0.0), full(sin_b, 0.0)


def kernel(x, c, ctx, c_ctx, w_ada, b_ada, norm1_g, norm2_g, w_in, w_out, lam_q1, lam_k1, lam_q2, lam_k2,
           subln_g, q_norm_g, k_norm_g, sink, w_router, router_bias, w_gate, w_up, w_down, final_g):
    batch, seq, d = x.shape
    ctx_len = ctx.shape[1]
    depth = w_in.shape[0]
    sc_rows = SC_WIN * math.prod(_sc_workers())
    assert ctx_len == TILE and seq % TILE == 0 and batch % PAIR == 0 and batch <= 15
    assert (batch * seq) % sc_rows == 0 and (batch * (seq + ctx_len)) % sc_rows == 0

    cpad = jnp.zeros((16, d), F32).at[:batch].set(c).at[batch].set(c_ctx)
    mod_all = _adaln(cpad, w_ada, b_ada)
    cos, sin_a, sin_b = _rope_tables(seq, ctx_len)
    mseg = jnp.asarray(np.kron(np.eye(2), np.ones((HEAD_DIM, HEAD_DIM))), BF16)
    wr = w_router.T
    wrh = wr.astype(BF16)
    wrl = (wr - wrh.astype(F32)).astype(BF16)
    rb = router_bias.reshape(N_EXPERTS, 1)

    stream = (ctx, x)

    for l in range(depth):
        last = l == depth - 1
        lambda_init = 0.8 - 0.6 * math.exp(-0.3 * l)
        m6 = mod_all[l].reshape(16, 6, d)
        m8 = jnp.concatenate([m6, jnp.zeros((16, 2, d), F32)], axis=1)
        mod = jnp.stack([jnp.broadcast_to(m8[batch], (batch, 8, d)), m8[:batch]], axis=1)

        w_in_p = w_in[l].astype(BF16)
        w_out_p = w_out[l].astype(BF16)
        gq = jnp.tile(q_norm_g[l], 2).reshape(1, LANES)
        gk = jnp.tile(k_norm_g[l], 2).reshape(1, LANES)
        q, k, v = _inproj(stream, mod, norm1_g[l].reshape(1, d), w_in_p, cos, sin_a, sin_b, gq, gk, mseg)

        t0 = 1 if last else 0
        lamv = jnp.stack([lam_q1[l], lam_k1[l], lam_q2[l], lam_k2[l]])
        mix = _attention(sink[l], lamv, subln_g[l].reshape(1, LANES), q, k, v, t0, lambda_init)
        x1, h2, gates, bucket = _outproj(stream, mix, w_out_p, mod, norm2_g[l].reshape(1, d), wrh, wrl, rb, t0)
        experts = [w[l].astype(BF16) for w in (w_gate, w_up, w_down)]
        stream = (x1, _moe(h2, gates, bucket, *experts), mod)
    return _residual(*stream, final_g.reshape(1, d), 1, final_norm=True)
```

```python
import functools
import math

import numpy as np
import jax
import jax.numpy as jnp
from jax import lax
from jax.experimental import pallas as pl
from jax.experimental.pallas import tpu as pltpu
from jax.experimental.pallas import tpu_sc as plsc

F32 = jnp.float32
BF16 = jnp.bfloat16

HEAD_DIM = 64
GRID_W = 64
ROPE_BASE = 10000.0
RMS_EPS = 1e-6
WINDOW = 128
N_EXPERTS = 16
EXPERTS_PER_GROUP = 4
LANES = 128
TILE = 256
PAIR = 2
FFN_TILE = 512
SC_WIN = 64
GROUP_LANE = 16
PAIRS = ((0, 1), (0, 2), (0, 3), (1, 3), (1, 2), (2, 3))
NEG_BIG = -1e30
KEY_CHUNK = 256
VMEM_LIMIT = 52 * 1024 * 1024

QA, QB, QC = 512, 256, 256
KA, KB, KC = 512, 128, 128
Q_COLS = QA + QB + QC
K_COLS = KA + KB + KC
V_COLS = 768


def _dot(a, b):
    return jnp.dot(a, b, preferred_element_type=F32)


def _dot_nt(a, b):
    return lax.dot_general(a, b, (((1,), (1,)), ((), ())), preferred_element_type=F32)


def _split(a):
    hi = a.astype(BF16)
    lo = (a - hi.astype(F32)).astype(BF16)
    return hi, lo


def _dot3(a, b):
    ah, al = _split(a)
    bh, bl = _split(b)
    return _dot(ah, bh) + _dot(ah, bl) + _dot(al, bh)


def _rms(x, g):
    ms = jnp.mean(x * x, axis=-1, keepdims=True)
    return x * lax.rsqrt(ms + RMS_EPS) * g


def _sigmoid(x):
    return 1.0 / (1.0 + jnp.exp(-x))


def _pack_bf16_pairs(y):
    n = y.shape[1] // 2
    yb = y.astype(BF16).astype(F32)
    lo = lax.bitcast_convert_type(yb[:, :n], jnp.uint32) >> 16
    hi = lax.bitcast_convert_type(yb[:, n:], jnp.uint32) & jnp.uint32(0xFFFF0000)
    return hi | lo


def _unpack_bf16_pairs(w):
    lo = lax.bitcast_convert_type(w << 16, F32)
    hi = lax.bitcast_convert_type(w & jnp.uint32(0xFFFF0000), F32)
    return lo, hi


def _stream_dims(stream):
    batch, t_len, d = stream[0].shape
    return batch, (t_len + stream[1].shape[1] if len(stream) == 2 else t_len), d


def _stream_rows(stream_refs, i, tile):
    if len(stream_refs) == 2:
        ctx_ref, x_ref = stream_refs
        return jnp.where(tile == 0, ctx_ref[i], x_ref[i])
    x_ref, y_ref, mod_ref = stream_refs
    lo, hi = _unpack_bf16_pairs(y_ref[i])
    return x_ref[i] + mod_ref[i, 0, 5:6, :] * jnp.concatenate([lo, hi], axis=1)


def _stream_specs(stream, nb, tile_of):
    d = stream[0].shape[-1]
    if len(stream) == 2:
        return [pl.BlockSpec((nb, TILE, d), lambda b, t: (b, 0, 0)),
                pl.BlockSpec((nb, TILE, d), lambda b, t: (b, jnp.maximum(tile_of(t) - 1, 0), 0))]
    tok = lambda b, t: (b, tile_of(t), 0)
    return [pl.BlockSpec((nb, TILE, d), tok),
            pl.BlockSpec((nb, TILE, d // 2), tok),
            pl.BlockSpec((nb, 1, 8, d), lambda b, t: (b, jnp.minimum(tile_of(t), 1), 0, 0))]


def _adaln_kernel(c_ref, w_ref, b_ref, o_ref):
    cv = c_ref[...]
    s = cv * _sigmoid(cv)
    o_ref[0] = _dot3(s, w_ref[0]) + b_ref[0]


def _adaln(cpad, w_ada, b_ada):
    depth, d, n6 = w_ada.shape
    tn = 1536
    return pl.pallas_call(
        _adaln_kernel,
        grid=(depth, n6 // tn),
        in_specs=[pl.BlockSpec((16, d), lambda l, j: (0, 0)),
                  pl.BlockSpec((1, d, tn), lambda l, j: (l, 0, j)),
                  pl.BlockSpec((1, 1, tn), lambda l, j: (l, 0, j))],
        out_specs=pl.BlockSpec((1, 16, tn), lambda l, j: (l, 0, j)),
        out_shape=jax.ShapeDtypeStruct((depth, 16, n6), F32),
        compiler_params=pltpu.CompilerParams(dimension_semantics=("parallel", "parallel"),
                                             vmem_limit_bytes=VMEM_LIMIT),
        name="adaln",
    )(cpad, w_ada, b_ada.reshape(depth, 1, n6))


def _inproj_kernel(*refs, n_stream):
    rest = refs[n_stream:]
    consts, experts_f32, qkv, experts_bf16 = rest[:9], rest[9:12], rest[12:15], rest[15:]
    for src, dst in zip(experts_f32, experts_bf16):
        dst[0] = src[0, 0].astype(BF16)
    for base in range(0, refs[0].shape[0], PAIR):
        _inproj_rows(base, pl.program_id(1), refs[:n_stream], *consts, *qkv)


def _inproj_rows(base, tile, stream_refs, mod_ref, g_ref, w_ref, cos_ref, sa_ref, sb_ref, gq_ref, gk_ref, mseg_ref,
                 q_ref, k_ref, v_ref):
    members = range(base, base + PAIR)
    h = jnp.concatenate(
        [(_rms(_stream_rows(stream_refs, i, tile), g_ref[...]) * (1.0 + mod_ref[i, 0, 1:2, :])
          + mod_ref[i, 0, 0:1, :]).astype(BF16) for i in members], axis=0)
    cos = jnp.concatenate([cos_ref[...]] * PAIR, axis=0)
    sa = jnp.concatenate([sa_ref[...]] * PAIR, axis=0)
    sb = jnp.concatenate([sb_ref[...]] * PAIR, axis=0)
    mseg = mseg_ref[...]

    def put(ref, cols, val):
        for n, i in enumerate(members):
            ref[i, :, cols] = val[n * TILE:(n + 1) * TILE]

    def rope(b):
        return b * cos + pltpu.roll(b, LANES - 16, 1) * sa + pltpu.roll(b, 16, 1) * sb

    def qknorm(b, g):
        hi, lo = _split(b * b)
        ms = (_dot(hi, mseg) + _dot(lo, mseg)) * (1.0 / HEAD_DIM)
        return b * lax.rsqrt(ms + RMS_EPS) * g

    def wcols(*pieces):
        parts = [w_ref[:, a:a + n] for a, n in pieces]
        return parts[0] if len(parts) == 1 else jnp.concatenate(parts, axis=1)

    lane = lax.broadcasted_iota(jnp.int32, (PAIR * TILE, LANES), 1)

    def regroup(p2):
        h0, h1 = p2[:, :LANES], p2[:, LANES:]
        return (jnp.where(lane < HEAD_DIM, h0, pltpu.roll(h1, HEAD_DIM, 1)),
                jnp.where(lane < HEAD_DIM, pltpu.roll(h0, HEAD_DIM, 1), h1))

    q0, kv0 = 0, Q_COLS
    rope_chunks = [wcols((q0, 256)), wcols((q0 + 256, 256)), wcols((q0 + QA, QB)), wcols((q0 + QA + QB, QC)),
                   wcols((kv0, 256)), wcols((kv0 + 256, 256)),
                   wcols((kv0 + 2 * KA, KB), (kv0 + 2 * KA + 2 * KB, KC))]
    v_chunks = [wcols((kv0 + KA, 256)), wcols((kv0 + KA + 256, 256)),
                wcols((kv0 + 2 * KA + KB, KB), (kv0 + 2 * KA + 2 * KB + KC, KC))]
    for c, wc in enumerate(rope_chunks):
        p2 = _dot(h, wc)
        halves = regroup(p2) if c in (2, 3) else (p2[:, :LANES], p2[:, LANES:])
        for half in range(2):
            j = 2 * c + half
            p = halves[half]
            if j in (4, 5):
                p = qknorm(p, gq_ref[...])
            if j == 12:
                p = qknorm(p, gk_ref[...])
            p = rope(p)
            if j < Q_COLS // LANES:
                put(q_ref, slice(j * LANES, (j + 1) * LANES), (p * (HEAD_DIM ** -0.5)).astype(BF16))
            else:
                jk = j - Q_COLS // LANES
                put(k_ref, slice(jk * LANES, (jk + 1) * LANES), p.astype(BF16))
    for c, wc in enumerate(v_chunks):
        put(v_ref, slice(c * 256, (c + 1) * 256), _dot(h, wc).astype(BF16))


def _inproj(stream, mod, g, w, cos, sa, sb, gq, gk, mseg, experts, layer):
    batch, t_len, d = _stream_dims(stream)
    nt = t_len // TILE
    nb = 2 * PAIR
    assert (batch // nb) * nt >= N_EXPERTS
    expert_of = lambda b, t: jnp.minimum(b * nt + t, N_EXPERTS - 1)
    row = lambda b, t: (t, 0)
    const = lambda b, t: (0, 0)
    tok = lambda b, t: (b, t, 0)
    return pl.pallas_call(
        functools.partial(_inproj_kernel, n_stream=len(stream)),
        grid=(batch // nb, nt),
        in_specs=_stream_specs(stream, nb, lambda t: t) + [
                  pl.BlockSpec((nb, 1, 8, d), lambda b, t: (b, jnp.minimum(t, 1), 0, 0)),
                  pl.BlockSpec((1, d), const),
                  pl.BlockSpec(w.shape, const),
                  pl.BlockSpec((TILE, LANES), row),
                  pl.BlockSpec((TILE, LANES), row),
                  pl.BlockSpec((TILE, LANES), row),
                  pl.BlockSpec((1, LANES), const),
                  pl.BlockSpec((1, LANES), const),
                  pl.BlockSpec((LANES, LANES), const)] + [
                  pl.BlockSpec((1, 1) + e.shape[2:], lambda b, t: (layer, expert_of(b, t), 0, 0)) for e in experts],
        out_specs=[pl.BlockSpec((nb, TILE, Q_COLS), tok),
                   pl.BlockSpec((nb, TILE, K_COLS), tok),
                   pl.BlockSpec((nb, TILE, V_COLS), tok)] + [
                   pl.BlockSpec((1,) + e.shape[2:], lambda b, t: (expert_of(b, t), 0, 0)) for e in experts],
        out_shape=[jax.ShapeDtypeStruct((batch, t_len, Q_COLS), BF16),
                   jax.ShapeDtypeStruct((batch, t_len, K_COLS), BF16),
                   jax.ShapeDtypeStruct((batch, t_len, V_COLS), BF16)] + [
                   jax.ShapeDtypeStruct(e.shape[1:], BF16) for e in experts],
        compiler_params=pltpu.CompilerParams(dimension_semantics=("arbitrary", "arbitrary"),
                                             vmem_limit_bytes=VMEM_LIMIT),
        name="inproj",
    )(*stream, mod, g, w, cos, sa, sb, gq, gk, mseg, *experts)


def _scores_pass(qq, k_ref, bi, col, starts, s_ref, mask_fn=None, extra=None):
    half = KEY_CHUNK // 2
    rows = qq.shape[0]
    mrun = None
    for j, st in enumerate(starts):
        s = _dot_nt(qq, k_ref[bi, pl.ds(st, KEY_CHUNK), col:col + LANES])
        if mask_fn is not None:
            s = mask_fn(j, st, s)
        s_ref[0:rows, j * KEY_CHUNK:(j + 1) * KEY_CHUNK] = s
        mj = jnp.maximum(s[:, :half], s[:, half:])
        mrun = mj if mrun is None else jnp.maximum(mrun, mj)
    m = jnp.max(mrun, axis=-1, keepdims=True)
    return m if extra is None else jnp.maximum(m, extra)


def _values_pass(m, rows, v_ref, bi, col, starts, s_ref, extra=None):
    half = KEY_CHUNK // 2
    lrun = None
    acc = None
    for j, st in enumerate(starts):
        e = jnp.exp(s_ref[0:rows, j * KEY_CHUNK:(j + 1) * KEY_CHUNK] - m)
        lj = e[:, :half] + e[:, half:]
        lrun = lj if lrun is None else lrun + lj
        pv = _dot(e.astype(BF16), v_ref[bi, pl.ds(st, KEY_CHUNK), col:col + LANES])
        acc = pv if acc is None else acc + pv
    l = jnp.sum(lrun, axis=-1, keepdims=True)
    if extra is not None:
        l = l + jnp.exp(extra - m)
    return acc / l


def _softmax_pv(qq, k_ref, v_ref, bi, col, starts, s_ref, mask_fn=None, extra=None):
    m = _scores_pass(qq, k_ref, bi, col, starts, s_ref, mask_fn, extra)
    return _values_pass(m, qq.shape[0], v_ref, bi, col, starts, s_ref, extra)


def _all_chunks(n_keys):
    return [j * KEY_CHUNK for j in range(n_keys // KEY_CHUNK)]


def _half_masks(shape):
    lane = lax.broadcasted_iota(jnp.int32, shape, 1)
    return lane < HEAD_DIM, lane >= HEAD_DIM


def _stack_group_queries(q2, kv):
    lo_m, hi_m = _half_masks((TILE, LANES))
    m = lo_m if kv == 0 else hi_m
    zero = jnp.zeros((TILE, LANES), q2.dtype)
    return jnp.concatenate([jnp.where(m, q2[:, :LANES], zero), jnp.where(m, q2[:, LANES:], zero)], axis=0)


def _merge_kv_outputs(o_kv0, o_kv1):
    lo_m, _ = _half_masks((TILE, LANES))
    kv0 = jnp.where(lo_m, o_kv0[:TILE], pltpu.roll(o_kv0[TILE:], HEAD_DIM, 1))
    kv1 = jnp.where(lo_m, pltpu.roll(o_kv1[:TILE], HEAD_DIM, 1), o_kv1[TILE:])
    return jnp.concatenate([kv0, kv1], axis=1)


def _attn_kernel(sink_ref, lam_ref, subg_ref, qa_ref, qb_ref, qc_ref, ka_ref, kb_ref, kc_ref, va_ref, vb_ref, vc_ref,
                 o_ref, s_ref, *, t0, ctx_len, lambda_init):
    t = pl.program_id(1) + t0
    t_len = ka_ref.shape[1]
    band = 2 * TILE
    lo_m, hi_m = _half_masks((TILE, LANES))
    zero = jnp.zeros((TILE, LANES), BF16)
    row = lax.broadcasted_iota(jnp.int32, (2 * TILE, 1), 0)
    lv = lam_ref[...]
    lam = (jnp.exp(jnp.sum(lv[0:1] * lv[1:2], keepdims=True))
           - jnp.exp(jnp.sum(lv[2:3] * lv[3:4], keepdims=True)) + lambda_init)

    def sink_col(kv):
        return jnp.where(row < TILE, sink_ref[2 * kv], sink_ref[2 * kv + 1])

    def mixers(starts_ab, starts_c, mask_c):
        n_chain = 0
        for bi in range(qa_ref.shape[0]):
            for h in range(QA // LANES):
                q = qa_ref[bi, :, h * LANES:(h + 1) * LANES]
                qq = jnp.concatenate([jnp.where(lo_m, q, zero), jnp.where(hi_m, q, zero)], axis=0)
                o = _softmax_pv(qq, ka_ref, va_ref, bi, h * LANES, starts_ab, s_ref.at[n_chain % 2])
                n_chain += 1
                o = o[:TILE] - lam * o[TILE:]
                o_ref[bi, :, h * LANES:(h + 1) * LANES] = (
                    _rms(o, subg_ref[...]) * (1.0 - lambda_init)).astype(BF16)
            outs = []
            for kv in range(2):
                outs.append(_softmax_pv(_stack_group_queries(qb_ref[bi], kv), kb_ref, vb_ref, bi, 0, starts_ab,
                                        s_ref.at[n_chain % 2]))
                n_chain += 1
            o_ref[bi, :, QA:QA + QB] = _merge_kv_outputs(*outs).astype(BF16)
            outs = []
            for kv in range(2):
                outs.append(_softmax_pv(_stack_group_queries(qc_ref[bi], kv), kc_ref, vc_ref, bi, 0, starts_c,
                                        s_ref.at[n_chain % 2], mask_c, sink_col(kv)))
                n_chain += 1
            o_ref[bi, :, QA + QB:] = _merge_kv_outputs(*outs).astype(BF16)

    def ctx_step():
        mixers([0], [0], None)

    def latent_step():
        q_start = (t - 1) * TILE
        u0 = pl.multiple_of(jnp.minimum(q_start + ctx_len - WINDOW, t_len - band), WINDOW)
        q_pos = q_start + (lax.broadcasted_iota(jnp.int32, (2 * TILE, KEY_CHUNK), 0) & (TILE - 1))
        key_lane = lax.broadcasted_iota(jnp.int32, (2 * TILE, KEY_CHUNK), 1)

        def mask_c(j, st, s):
            if j == 0:
                return s
            k_pos = (st - ctx_len) + key_lane
            valid = (jnp.abs(k_pos - q_pos) <= WINDOW) & (k_pos >= 0)
            return jnp.where(valid, s, NEG_BIG)

        mixers(_all_chunks(t_len), [0] + [u0 + i * KEY_CHUNK for i in range(band // KEY_CHUNK)], mask_c)

    if t0 == 0:
        pl.when(t == 0)(ctx_step)
        pl.when(t > 0)(latent_step)
    else:
        latent_step()


def _attention(sink, lamv, subg, q, k, v, t0, lambda_init):
    batch, t_len, _ = q.shape
    nt = t_len // TILE
    kern = functools.partial(_attn_kernel, t0=t0, ctx_len=TILE, lambda_init=lambda_init)
    const = lambda b, t: (0, 0)
    nb = 1
    qspec = lambda width, blk: pl.BlockSpec((nb, TILE, width), lambda b, t: (b, t + t0, blk))
    kvspec = lambda width, blk: pl.BlockSpec((nb, t_len, width), lambda b, t: (b, 0, blk))
    return pl.pallas_call(
        kern,
        grid=(batch // nb, nt - t0),
        in_specs=[pl.BlockSpec(memory_space=pltpu.SMEM),
                  pl.BlockSpec(lamv.shape, const),
                  pl.BlockSpec((1, LANES), const),
                  qspec(QA, 0), qspec(QB, QA // QB), qspec(QC, (QA + QB) // QC),
                  kvspec(KA, 0), kvspec(KB, KA // KB), kvspec(KC, (KA + KB) // KC),
                  kvspec(KA, 0), kvspec(KB, KA // KB), kvspec(KC, (KA + KB) // KC)],
        out_specs=pl.BlockSpec((nb, TILE, Q_COLS), lambda b, t: (b, t + t0, 0)),
        out_shape=jax.ShapeDtypeStruct((batch, t_len, Q_COLS), BF16),
        scratch_shapes=[pltpu.VMEM((2, 2 * TILE, t_len), F32)],
        compiler_params=pltpu.CompilerParams(dimension_semantics=("parallel", "parallel"),
                                             vmem_limit_bytes=VMEM_LIMIT),
        name="attention",
    )(sink, lamv, subg, q, q, q, k, k, k, v, v, v)


def _route(scores, bias):
    expert = lax.broadcasted_iota(jnp.int32, scores.shape, 0)
    sel = scores + bias
    in_group = expert & (EXPERTS_PER_GROUP - 1)
    group = expert >> 2

    def neighbours(x, idx, step, span):
        for d in (1, 2, 3):
            fwd = (idx + d) < 4
            y = jnp.where(fwd, pltpu.roll(x, N_EXPERTS - d * step, 0), pltpu.roll(x, span - d * step, 0))
            yield y, fwd

    rank = jnp.zeros_like(sel)
    for y, fwd in neighbours(sel, in_group, 1, EXPERTS_PER_GROUP):
        beats = (y > sel) | ((y == sel) & jnp.logical_not(fwd))
        rank = rank + jnp.where(beats, 1.0, 0.0)
    top2 = jnp.where(rank < 2.0, sel, 0.0)
    gsum = top2
    for y, _ in neighbours(top2, in_group, 1, EXPERTS_PER_GROUP):
        gsum = gsum + y
    grank = jnp.zeros_like(sel)
    for y, fwd in neighbours(gsum, group, EXPERTS_PER_GROUP, N_EXPERTS):
        beats = (y > gsum) | ((y == gsum) & jnp.logical_not(fwd))
        grank = grank + jnp.where(beats, 1.0, 0.0)
    best = grank < 0.5
    chosen = best & (rank < 2.0)
    w = jnp.where(chosen, scores, 0.0)
    gates = w / jnp.sum(w, axis=0, keepdims=True)
    gid = jnp.sum(jnp.where(best & (in_group == 0), group.astype(F32), 0.0), axis=0, keepdims=True)
    bits = jnp.sum(jnp.where(chosen, jnp.left_shift(1, in_group).astype(F32), 0.0), axis=0, keepdims=True)
    pair = sum(jnp.where(bits == float((1 << a) + (1 << b)), float(k), 0.0) for k, (a, b) in enumerate(PAIRS))
    bucket = gid * len(PAIRS) + pair
    tokens = scores.shape[1]
    return jnp.concatenate([gates, jnp.broadcast_to(bucket, (8, tokens)),
                            jnp.zeros((LANES - N_EXPERTS - 8, tokens), F32)], axis=0)


def _outproj_kernel(*refs, n_stream, t0):
    stream_refs = refs[:n_stream]
    tile = pl.program_id(1) + t0
    mix_ref, w_ref, mod_ref, g_ref, wrh_ref, wrl_ref, rb_ref, x1_ref, h2_ref, gates_ref, bucket_ref = refs[n_stream:]
    for i in range(mix_ref.shape[0]):
        x1 = _stream_rows(stream_refs, i, tile) + mod_ref[i, 0, 2:3, :] * _dot(mix_ref[i], w_ref[...])
        x1_ref[i] = x1
        h2 = _rms(x1, g_ref[...]) * (1.0 + mod_ref[i, 0, 4:5, :]) + mod_ref[i, 0, 3:4, :]
        h2_ref[i] = _pack_bf16_pairs(h2)
        hi, lo = _split(h2)
        logits = _dot_nt(wrh_ref[...], hi) + _dot_nt(wrl_ref[...], hi) + _dot_nt(wrh_ref[...], lo)
        routed = _route(_sigmoid(logits), rb_ref[...])
        gates_ref[i] = routed.T
        bucket_ref[i] = routed[GROUP_LANE:GROUP_LANE + 8]


def _outproj(stream, mix, w, mod, g, wrh, wrl, rb, t0):
    batch, t_len, d = _stream_dims(stream)
    nt = t_len // TILE
    nb = 2 * PAIR
    t_out = (nt - t0) * TILE
    const = lambda b, t: (0, 0)
    tok_in = lambda b, t: (b, t + t0, 0)
    tok_out = lambda b, t: (b, t, 0)
    return pl.pallas_call(
        functools.partial(_outproj_kernel, n_stream=len(stream), t0=t0),
        grid=(batch // nb, nt - t0),
        in_specs=_stream_specs(stream, nb, lambda t: t + t0) + [
                  pl.BlockSpec((nb, TILE, Q_COLS), tok_in),
                  pl.BlockSpec(w.shape, const),
                  pl.BlockSpec((nb, 1, 8, d), lambda b, t: (b, jnp.minimum(t + t0, 1), 0, 0)),
                  pl.BlockSpec((1, d), const),
                  pl.BlockSpec((N_EXPERTS, d), const),
                  pl.BlockSpec((N_EXPERTS, d), const),
                  pl.BlockSpec((N_EXPERTS, 1), const)],
        out_specs=[pl.BlockSpec((nb, TILE, d), tok_out),
                   pl.BlockSpec((nb, TILE, d // 2), tok_out),
                   pl.BlockSpec((nb, TILE, LANES), tok_out),
                   pl.BlockSpec((nb, 8, TILE), tok_out)],
        out_shape=[jax.ShapeDtypeStruct((batch, t_out, d), F32),
                   jax.ShapeDtypeStruct((batch, t_out, d // 2), jnp.uint32),
                   jax.ShapeDtypeStruct((batch, t_out, LANES), F32),
                   jax.ShapeDtypeStruct((batch, (nt - t0) * 8, TILE), F32)],
        compiler_params=pltpu.CompilerParams(dimension_semantics=("parallel", "parallel"),
                                             vmem_limit_bytes=VMEM_LIMIT),
        name="outproj",
    )(*stream, mix, w, mod, g, wrh, wrl, rb)


def _sc_workers():
    info = plsc.get_sparse_core_info()
    return info.num_cores, info.num_subcores


def _sc_dispatch(hpk, gates, pos2d, n_out):
    n, w = hpk.shape
    gw = gates.shape[1]
    nc, ns = _sc_workers()
    steps = n // (nc * ns * SC_WIN)
    mesh = plsc.VectorSubcoreMesh(core_axis_name="c", subcore_axis_name="s")

    @functools.partial(
        pl.kernel, mesh=mesh,
        out_type=(jax.ShapeDtypeStruct((n_out, w), hpk.dtype), jax.ShapeDtypeStruct((n_out, gw), gates.dtype)),
        scratch_types=[pltpu.VMEM((1, SC_WIN), jnp.int32), pltpu.VMEM((SC_WIN, w), hpk.dtype),
                       pltpu.VMEM((SC_WIN, gw), gates.dtype), pltpu.SemaphoreType.DMA, pltpu.SemaphoreType.DMA],
        name="sc_dispatch")
    def k(h_hbm, g_hbm, pos_hbm, hs_hbm, gs_hbm, idx_v, rows_v, grow_v, sem_in, sem_out):
        wid = lax.axis_index("s") * nc + lax.axis_index("c")

        @pl.loop(0, steps)
        def _(j):
            blk = wid * steps + j
            off = pl.multiple_of(blk * SC_WIN, SC_WIN)
            loads = [pltpu.async_copy(pos_hbm.at[pl.ds(blk, 1)], idx_v, sem_in),
                     pltpu.async_copy(h_hbm.at[pl.ds(off, SC_WIN)], rows_v, sem_in),
                     pltpu.async_copy(g_hbm.at[pl.ds(off, SC_WIN)], grow_v, sem_in)]
            for cp in loads:
                cp.wait()
            stores = [pltpu.async_copy(rows_v, hs_hbm.at[idx_v.at[0]], sem_out),
                      pltpu.async_copy(grow_v, gs_hbm.at[idx_v.at[0]], sem_out)]
            for cp in stores:
                cp.wait()

    return k(hpk, gates, pos2d)


def _sc_combine(ys, pos2d, n):
    w = ys.shape[1]
    nc, ns = _sc_workers()
    steps = n // (nc * ns * SC_WIN)
    mesh = plsc.VectorSubcoreMesh(core_axis_name="c", subcore_axis_name="s")

    @functools.partial(
        pl.kernel, mesh=mesh,
        out_type=jax.ShapeDtypeStruct((n, w), ys.dtype),
        scratch_types=[pltpu.VMEM((1, SC_WIN), jnp.int32), pltpu.VMEM((SC_WIN, w), ys.dtype)],
        name="sc_combine")
    def k(y_hbm, pos_hbm, o_hbm, idx_v, rows_v):
        wid = lax.axis_index("s") * nc + lax.axis_index("c")

        @pl.loop(0, steps)
        def _(j):
            blk = wid * steps + j
            off = pl.multiple_of(blk * SC_WIN, SC_WIN)
            pltpu.sync_copy(pos_hbm.at[pl.ds(blk, 1)], idx_v)
            pltpu.sync_copy(y_hbm.at[idx_v.at[0]], rows_v)
            pltpu.sync_copy(rows_v, o_hbm.at[pl.ds(off, SC_WIN)])

    return k(ys, pos2d)


PLAN_ROWS = 256


def _plan_kernel(g_ref, pos_ref, tg_ref, tk_ref, slots_ref):
    _plan_finish(g_ref, pos_ref, tg_ref, tk_ref, slots_ref)


def _plan_finish(g_ref, pos_ref, tg_ref, tk_ref, slots_ref):
    g = g_ref[...]
    rows = g.shape[0]
    n_groups = N_EXPERTS // EXPERTS_PER_GROUP
    before_lane = (lax.broadcasted_iota(jnp.int32, (LANES, LANES), 0)
                   < lax.broadcasted_iota(jnp.int32, (LANES, LANES), 1)).astype(BF16)
    before_row = (lax.broadcasted_iota(jnp.int32, (rows, rows), 1)
                  < lax.broadcasted_iota(jnp.int32, (rows, rows), 0)).astype(BF16)
    tile_row = lax.broadcasted_iota(jnp.int32, (1, LANES), 1).astype(F32) * FFN_TILE
    expert = lax.broadcasted_iota(jnp.int32, (N_EXPERTS, 1), 0)
    pos = jnp.zeros_like(g)
    start = jnp.zeros((1, 1), F32)
    tile_group = jnp.zeros((1, LANES), F32)
    need = jnp.zeros((N_EXPERTS, LANES), F32)
    for grp in range(n_groups):
        for k, (ea, eb) in enumerate(PAIRS):
            m = jnp.where(g == grp * len(PAIRS) + k, 1.0, 0.0)
            in_row = _dot(m.astype(BF16), before_lane)
            row_total = jnp.sum(m, axis=1, keepdims=True)
            rows_before = _dot(before_row, jnp.broadcast_to(row_total, m.shape).astype(BF16))
            pos = pos + m * (start + in_row + rows_before)
            end = start + jnp.sum(row_total, axis=0, keepdims=True)
            holds = jnp.where((tile_row < end) & (tile_row + FFN_TILE > start) & (end > start), 1.0, 0.0)
            uses = jnp.where((expert == grp * EXPERTS_PER_GROUP + ea) | (expert == grp * EXPERTS_PER_GROUP + eb),
                             1.0, 0.0)
            need = need + uses * holds
            start = end
        start = jnp.floor((start + (FFN_TILE - 1)) * (1.0 / FFN_TILE)) * FFN_TILE
        if grp < n_groups - 1:
            tile_group = tile_group + jnp.where(tile_row >= start, 1.0, 0.0)
    pos_ref[...] = pos.astype(jnp.int32)
    tg_ref[...] = tile_group.astype(jnp.int32)
    count = jnp.zeros((1, LANES), F32)
    slots = [tile_group * EXPERTS_PER_GROUP] * EXPERTS_PER_GROUP
    for e in range(N_EXPERTS):
        used = need[e:e + 1] > 0.0
        slots = [jnp.where(used & (count == k), float(e), slot) for k, slot in enumerate(slots)]
        count = count + jnp.where(used, 1.0, 0.0)
    tk_ref[...] = count.astype(jnp.int32)
    slots_ref[...] = jnp.concatenate(slots + [jnp.zeros((8 - EXPERTS_PER_GROUP, LANES), F32)], axis=0).astype(jnp.int32)


def _bucket_plan(bucket, n, n_pad):
    g = bucket.reshape(bucket.shape[0], -1, 8, TILE)[:, :, 0, :].reshape(n // LANES, LANES)
    g = jnp.pad(g, ((0, PLAN_ROWS - n // LANES), (0, 0)), constant_values=-1.0)
    pos, tile_group, tile_k, slots = pl.pallas_call(
        _plan_kernel,
        out_shape=[jax.ShapeDtypeStruct((PLAN_ROWS, LANES), jnp.int32),
                   jax.ShapeDtypeStruct((1, LANES), jnp.int32),
                   jax.ShapeDtypeStruct((1, LANES), jnp.int32),
                   jax.ShapeDtypeStruct((8, LANES), jnp.int32)],
        name="plan",
    )(g)
    n_tiles = n_pad // FFN_TILE
    return (pos[:n // LANES].reshape(n // SC_WIN, SC_WIN), tile_group[0, :n_tiles], tile_k[0, :n_tiles],
            [slots[k, :n_tiles] for k in range(EXPERTS_PER_GROUP)])


def _ffn_kernel(tg_ref, tk_ref, s0_ref, s1_ref, s2_ref, s3_ref, hs_ref, gs_ref, wg_ref, wu_ref, wd_ref, ys_ref):
    i = pl.program_id(0)
    slot_refs = (s0_ref, s1_ref, s2_ref, s3_ref)

    def run_experts(n_experts):
        def body():
            lo, hi = _unpack_bf16_pairs(hs_ref[...])
            lo = lo.astype(BF16)
            hi = hi.astype(BF16)
            half = lo.shape[1]
            gs = gs_ref[...]
            lane = lax.broadcasted_iota(jnp.int32, gs.shape, 1)
            y = None
            for k in range(n_experts):
                e = slot_refs[k][i]
                local = e - tg_ref[i] * EXPERTS_PER_GROUP
                wg, wu, wd = wg_ref.at[local], wu_ref.at[local], wd_ref.at[local]
                a = _dot(lo, wg[:half, :]) + _dot(hi, wg[half:, :])
                u = _dot(lo, wu[:half, :]) + _dot(hi, wu[half:, :])
                gate = jnp.sum(jnp.where(lane == e, gs, 0.0), axis=-1, keepdims=True)
                part = _dot(((a * _sigmoid(a)) * u * gate).astype(BF16), wd[...])
                y = part if y is None else y + part
            ys_ref[...] = _pack_bf16_pairs(y)
        return body

    for n_experts in range(2, EXPERTS_PER_GROUP + 1):
        pl.when(tk_ref[i] == n_experts)(run_experts(n_experts))


def _ffn(tile_group, tile_k, slots, hs, gs, wg, wu, wd):
    n_pad, half = hs.shape
    grp = lambda i, tg, *_: (tg[i], 0, 0)
    row = lambda i, *_: (i, 0)
    wspec = lambda w: pl.BlockSpec((EXPERTS_PER_GROUP,) + w.shape[1:], grp)
    return pl.pallas_call(
        _ffn_kernel,
        grid_spec=pltpu.PrefetchScalarGridSpec(
            num_scalar_prefetch=2 + EXPERTS_PER_GROUP,
            grid=(n_pad // FFN_TILE,),
            in_specs=[pl.BlockSpec((FFN_TILE, half), row),
                      pl.BlockSpec((FFN_TILE, LANES), row),
                      wspec(wg), wspec(wu), wspec(wd)],
            out_specs=pl.BlockSpec((FFN_TILE, half), row)),
        out_shape=jax.ShapeDtypeStruct((n_pad, half), jnp.uint32),
        compiler_params=pltpu.CompilerParams(dimension_semantics=("parallel",), vmem_limit_bytes=VMEM_LIMIT),
        name="ffn",
    )(tile_group, tile_k, *slots, hs, gs, wg, wu, wd)


def _residual_kernel(x_ref, y_ref, mod_ref, fg_ref, o_ref, *, final_norm):
    for i in range(x_ref.shape[0]):
        lo, hi = _unpack_bf16_pairs(y_ref[i])
        y = x_ref[i] + mod_ref[i, 0, 5:6, :] * jnp.concatenate([lo, hi], axis=1)
        if final_norm:
            y = _rms(y, fg_ref[...])
        o_ref[i] = y


def _residual(x1, yg, mod, fg, t0, final_norm):
    batch, t_out, d = x1.shape
    nb = PAIR
    tok = lambda b, t: (b, t, 0)
    return pl.pallas_call(
        functools.partial(_residual_kernel, final_norm=final_norm),
        grid=(batch // nb, t_out // TILE),
        in_specs=[pl.BlockSpec((nb, TILE, d), tok),
                  pl.BlockSpec((nb, TILE, d // 2), tok),
                  pl.BlockSpec((nb, 1, 8, d), lambda b, t: (b, jnp.minimum(t + t0, 1), 0, 0)),
                  pl.BlockSpec((1, d), lambda b, t: (0, 0))],
        out_specs=pl.BlockSpec((nb, TILE, d), tok),
        out_shape=jax.ShapeDtypeStruct((batch, t_out, d), F32),
        compiler_params=pltpu.CompilerParams(dimension_semantics=("parallel", "parallel"),
                                             vmem_limit_bytes=VMEM_LIMIT),
        name="residual",
    )(x1, yg, mod, fg)


def _moe(h2, gates, bucket, wg, wu, wd):
    batch, t_out, half = h2.shape
    n = batch * t_out
    n_pad = n + (N_EXPERTS // EXPERTS_PER_GROUP) * FFN_TILE
    assert n % LANES == 0 and n // LANES <= PLAN_ROWS and n_pad // FFN_TILE <= LANES
    pos2d, tile_group, tile_k, slots = _bucket_plan(bucket, n, n_pad)
    hs, gs = _sc_dispatch(h2.reshape(n, half), gates.reshape(n, LANES), pos2d, n_pad)
    ys = _ffn(tile_group, tile_k, slots, hs, gs, wg, wu, wd)
    return _sc_combine(ys, pos2d, n).reshape(batch, t_out, half)


def _rope_tables(seq, ctx_len):
    rows = seq // GRID_W
    row_pos = jnp.repeat(jnp.arange(rows, dtype=F32), GRID_W)
    col_pos = jnp.tile(jnp.arange(GRID_W, dtype=F32), rows)
    axis_dim = HEAD_DIM // 2
    inv_freq = ROPE_BASE ** (-jnp.arange(0, axis_dim, 2, dtype=F32) / axis_dim)
    ang_r = row_pos[:, None] * inv_freq[None, :]
    ang_c = col_pos[:, None] * inv_freq[None, :]
    z = jnp.zeros_like(ang_r)
    cos = jnp.concatenate([jnp.cos(ang_r)] * 2 + [jnp.cos(ang_c)] * 2, axis=-1)
    sin_a = jnp.concatenate([-jnp.sin(ang_r), z, -jnp.sin(ang_c), z], axis=-1)
    sin_b = jnp.concatenate([z, jnp.sin(ang_r), z, jnp.sin(ang_c)], axis=-1)

    def full(tab, fill):
        tab = jnp.tile(tab, (1, LANES // HEAD_DIM))
        return jnp.concatenate([jnp.full((ctx_len, LANES), fill, F32), tab], axis=0)

    return full(cos, 1.0), full(sin_a, 0.0), full(sin_b, 0.0)


def kernel(x, c, ctx, c_ctx, w_ada, b_ada, norm1_g, norm2_g, w_in, w_out, lam_q1, lam_k1, lam_q2, lam_k2,
           subln_g, q_norm_g, k_norm_g, sink, w_router, router_bias, w_gate, w_up, w_down, final_g):
    batch, seq, d = x.shape
    ctx_len = ctx.shape[1]
    depth = w_in.shape[0]
    sc_rows = SC_WIN * math.prod(_sc_workers())
    assert ctx_len == TILE and seq % TILE == 0 and batch % PAIR == 0 and batch <= 15
    assert (batch * seq) % sc_rows == 0 and (batch * (seq + ctx_len)) % sc_rows == 0

    cpad = jnp.zeros((16, d), F32).at[:batch].set(c).at[batch].set(c_ctx)
    mod_all = _adaln(cpad, w_ada, b_ada)
    cos, sin_a, sin_b = _rope_tables(seq, ctx_len)
    mseg = jnp.asarray(np.kron(np.eye(2), np.ones((HEAD_DIM, HEAD_DIM))), BF16)
    wr = w_router.T
    wrh = wr.astype(BF16)
    wrl = (wr - wrh.astype(F32)).astype(BF16)
    rb = router_bias.reshape(N_EXPERTS, 1)

    stream = (ctx, x)

    for l in range(depth):
        last = l == depth - 1
        lambda_init = 0.8 - 0.6 * math.exp(-0.3 * l)
        m6 = mod_all[l].reshape(16, 6, d)
        m8 = jnp.concatenate([m6, jnp.zeros((16, 2, d), F32)], axis=1)
        mod = jnp.stack([jnp.broadcast_to(m8[batch], (batch, 8, d)), m8[:batch]], axis=1)

        w_in_p = w_in[l].astype(BF16)
        w_out_p = w_out[l].astype(BF16)
        gq = jnp.tile(q_norm_g[l], 2).reshape(1, LANES)
        gk = jnp.tile(k_norm_g[l], 2).reshape(1, LANES)
        q, k, v, *experts = _inproj(stream, mod, norm1_g[l].reshape(1, d), w_in_p, cos, sin_a, sin_b, gq, gk, mseg,
                                    (w_gate, w_up, w_down), l)

        t0 = 1 if last else 0
        lamv = jnp.stack([lam_q1[l], lam_k1[l], lam_q2[l], lam_k2[l]])
        mix = _attention(sink[l], lamv, subln_g[l].reshape(1, LANES), q, k, v, t0, lambda_init)
        x1, h2, gates, bucket = _outproj(stream, mix, w_out_p, mod, norm2_g[l].reshape(1, d), wrh, wrl, rb, t0)
        stream = (x1, _moe(h2, gates, bucket, *experts), mod)
    return _residual(*stream, final_g.reshape(1, d), 1, final_norm=True)
```

```python
import functools
import math

import numpy as np
import jax
import jax.numpy as jnp
from jax import lax
from jax.experimental import pallas as pl
from jax.experimental.pallas import tpu as pltpu
from jax.experimental.pallas import tpu_sc as plsc

F32 = jnp.float32
BF16 = jnp.bfloat16

HEAD_DIM = 64
GRID_W = 64
ROPE_BASE = 10000.0
RMS_EPS = 1e-6
WINDOW = 128
N_EXPERTS = 16
EXPERTS_PER_GROUP = 4
LANES = 128
TILE = 256
PAIR = 2
FFN_TILE = 512
SC_WIN = 64
GROUP_LANE = 16
PAIRS = ((0, 1), (0, 2), (0, 3), (1, 3), (1, 2), (2, 3))
NEG_BIG = -1e30
KEY_CHUNK = 256
VMEM_LIMIT = 52 * 1024 * 1024

QA, QB, QC = 512, 256, 256
KA, KB, KC = 512, 128, 128
Q_COLS = QA + QB + QC
K_COLS = KA + KB + KC
V_COLS = 768


def _dot(a, b):
    return jnp.dot(a, b, preferred_element_type=F32)


def _dot_nt(a, b):
    return lax.dot_general(a, b, (((1,), (1,)), ((), ())), preferred_element_type=F32)


def _split(a):
    hi = a.astype(BF16)
    lo = (a - hi.astype(F32)).astype(BF16)
    return hi, lo


def _dot3(a, b):
    ah, al = _split(a)
    bh, bl = _split(b)
    return _dot(ah, bh) + _dot(ah, bl) + _dot(al, bh)


def _rms(x, g):
    ms = jnp.mean(x * x, axis=-1, keepdims=True)
    return x * lax.rsqrt(ms + RMS_EPS) * g


def _sigmoid(x):
    return 1.0 / (1.0 + jnp.exp(-x))


def _pack_bf16_pairs(y):
    n = y.shape[1] // 2
    yb = y.astype(BF16).astype(F32)
    lo = lax.bitcast_convert_type(yb[:, :n], jnp.uint32) >> 16
    hi = lax.bitcast_convert_type(yb[:, n:], jnp.uint32) & jnp.uint32(0xFFFF0000)
    return hi | lo


def _unpack_bf16_pairs(w):
    lo = lax.bitcast_convert_type(w << 16, F32)
    hi = lax.bitcast_convert_type(w & jnp.uint32(0xFFFF0000), F32)
    return lo, hi


def _stream_dims(stream):
    batch, t_len, d = stream[0].shape
    return batch, (t_len + stream[1].shape[1] if len(stream) == 2 else t_len), d


def _stream_rows(stream_refs, i, tile):
    if len(stream_refs) == 2:
        ctx_ref, x_ref = stream_refs
        return jnp.where(tile == 0, ctx_ref[i], x_ref[i])
    x_ref, y_ref, mod_ref = stream_refs
    lo, hi = _unpack_bf16_pairs(y_ref[i])
    return x_ref[i] + mod_ref[i, 0, 5:6, :] * jnp.concatenate([lo, hi], axis=1)


def _stream_specs(stream, nb, tile_of):
    d = stream[0].shape[-1]
    if len(stream) == 2:
        return [pl.BlockSpec((nb, TILE, d), lambda b, t: (b, 0, 0)),
                pl.BlockSpec((nb, TILE, d), lambda b, t: (b, jnp.maximum(tile_of(t) - 1, 0), 0))]
    tok = lambda b, t: (b, tile_of(t), 0)
    return [pl.BlockSpec((nb, TILE, d), tok),
            pl.BlockSpec((nb, TILE, d // 2), tok),
            pl.BlockSpec((nb, 1, 8, d), lambda b, t: (b, jnp.minimum(tile_of(t), 1), 0, 0))]


def _adaln_kernel(c_ref, w_ref, b_ref, o_ref):
    cv = c_ref[...]
    s = cv * _sigmoid(cv)
    o_ref[0] = _dot3(s, w_ref[0]) + b_ref[0]


def _adaln(cpad, w_ada, b_ada):
    depth, d, n6 = w_ada.shape
    tn = 1536
    return pl.pallas_call(
        _adaln_kernel,
        grid=(depth, n6 // tn),
        in_specs=[pl.BlockSpec((16, d), lambda l, j: (0, 0)),
                  pl.BlockSpec((1, d, tn), lambda l, j: (l, 0, j)),
                  pl.BlockSpec((1, 1, tn), lambda l, j: (l, 0, j))],
        out_specs=pl.BlockSpec((1, 16, tn), lambda l, j: (l, 0, j)),
        out_shape=jax.ShapeDtypeStruct((depth, 16, n6), F32),
        compiler_params=pltpu.CompilerParams(dimension_semantics=("parallel", "parallel"),
                                             vmem_limit_bytes=VMEM_LIMIT),
        name="adaln",
    )(cpad, w_ada, b_ada.reshape(depth, 1, n6))


def _inproj_kernel(*refs, n_stream):
    rest = refs[n_stream:]
    n_jobs = (len(rest) - 12) // 2
    consts, jobs_f32, qkv, jobs_bf16 = rest[:9], rest[9:9 + n_jobs], rest[9 + n_jobs:12 + n_jobs], rest[12 + n_jobs:]
    for src, dst in zip(jobs_f32, jobs_bf16):
        dst[...] = src[0].astype(BF16)
    for base in range(0, refs[0].shape[0], PAIR):
        _inproj_rows(base, pl.program_id(1), refs[:n_stream], *consts, *qkv)


def _inproj_rows(base, tile, stream_refs, mod_ref, g_ref, w_ref, cos_ref, sa_ref, sb_ref, gq_ref, gk_ref, mseg_ref,
                 q_ref, k_ref, v_ref):
    members = range(base, base + PAIR)
    h = jnp.concatenate(
        [(_rms(_stream_rows(stream_refs, i, tile), g_ref[...]) * (1.0 + mod_ref[i, 0, 1:2, :])
          + mod_ref[i, 0, 0:1, :]).astype(BF16) for i in members], axis=0)
    cos = jnp.concatenate([cos_ref[...]] * PAIR, axis=0)
    sa = jnp.concatenate([sa_ref[...]] * PAIR, axis=0)
    sb = jnp.concatenate([sb_ref[...]] * PAIR, axis=0)
    mseg = mseg_ref[...]

    def put(ref, cols, val):
        for n, i in enumerate(members):
            ref[i, :, cols] = val[n * TILE:(n + 1) * TILE]

    def rope(b):
        return b * cos + pltpu.roll(b, LANES - 16, 1) * sa + pltpu.roll(b, 16, 1) * sb

    def qknorm(b, g):
        hi, lo = _split(b * b)
        ms = (_dot(hi, mseg) + _dot(lo, mseg)) * (1.0 / HEAD_DIM)
        return b * lax.rsqrt(ms + RMS_EPS) * g

    def wcols(*pieces):
        parts = [w_ref[:, a:a + n] for a, n in pieces]
        return parts[0] if len(parts) == 1 else jnp.concatenate(parts, axis=1)

    lane = lax.broadcasted_iota(jnp.int32, (PAIR * TILE, LANES), 1)

    def regroup(p2):
        h0, h1 = p2[:, :LANES], p2[:, LANES:]
        return (jnp.where(lane < HEAD_DIM, h0, pltpu.roll(h1, HEAD_DIM, 1)),
                jnp.where(lane < HEAD_DIM, pltpu.roll(h0, HEAD_DIM, 1), h1))

    q0, kv0 = 0, Q_COLS
    rope_chunks = [wcols((q0, 256)), wcols((q0 + 256, 256)), wcols((q0 + QA, QB)), wcols((q0 + QA + QB, QC)),
                   wcols((kv0, 256)), wcols((kv0 + 256, 256)),
                   wcols((kv0 + 2 * KA, KB), (kv0 + 2 * KA + 2 * KB, KC))]
    v_chunks = [wcols((kv0 + KA, 256)), wcols((kv0 + KA + 256, 256)),
                wcols((kv0 + 2 * KA + KB, KB), (kv0 + 2 * KA + 2 * KB + KC, KC))]
    for c, wc in enumerate(rope_chunks):
        p2 = _dot(h, wc)
        halves = regroup(p2) if c in (2, 3) else (p2[:, :LANES], p2[:, LANES:])
        for half in range(2):
            j = 2 * c + half
            p = halves[half]
            if j in (4, 5):
                p = qknorm(p, gq_ref[...])
            if j == 12:
                p = qknorm(p, gk_ref[...])
            p = rope(p)
            if j < Q_COLS // LANES:
                put(q_ref, slice(j * LANES, (j + 1) * LANES), (p * (HEAD_DIM ** -0.5)).astype(BF16))
            else:
                jk = j - Q_COLS // LANES
                put(k_ref, slice(jk * LANES, (jk + 1) * LANES), p.astype(BF16))
    for c, wc in enumerate(v_chunks):
        put(v_ref, slice(c * 256, (c + 1) * 256), _dot(h, wc).astype(BF16))


def _inproj(stream, mod, g, w, cos, sa, sb, gq, gk, mseg, experts, layer, dense):
    batch, t_len, d = _stream_dims(stream)
    nt = t_len // TILE
    nb = 2 * PAIR
    assert (batch // nb) * nt >= N_EXPERTS
    expert_of = lambda b, t: jnp.minimum(b * nt + t, N_EXPERTS - 1)
    slab = lambda a: (a.shape[1] // N_EXPERTS, a.shape[2])
    row = lambda b, t: (t, 0)
    const = lambda b, t: (0, 0)
    tok = lambda b, t: (b, t, 0)
    return pl.pallas_call(
        functools.partial(_inproj_kernel, n_stream=len(stream)),
        grid=(batch // nb, nt),
        in_specs=_stream_specs(stream, nb, lambda t: t) + [
                  pl.BlockSpec((nb, 1, 8, d), lambda b, t: (b, jnp.minimum(t, 1), 0, 0)),
                  pl.BlockSpec((1, d), const),
                  pl.BlockSpec(w.shape, const),
                  pl.BlockSpec((TILE, LANES), row),
                  pl.BlockSpec((TILE, LANES), row),
                  pl.BlockSpec((TILE, LANES), row),
                  pl.BlockSpec((1, LANES), const),
                  pl.BlockSpec((1, LANES), const),
                  pl.BlockSpec((LANES, LANES), const)] + [
                  pl.BlockSpec((1, 1) + e.shape[2:], lambda b, t: (layer, expert_of(b, t), 0, 0)) for e in experts] + [
                  pl.BlockSpec((1,) + slab(a), lambda b, t, li=li: (li, expert_of(b, t), 0)) for a, li in dense],
        out_specs=[pl.BlockSpec((nb, TILE, Q_COLS), tok),
                   pl.BlockSpec((nb, TILE, K_COLS), tok),
                   pl.BlockSpec((nb, TILE, V_COLS), tok)] + [
                   pl.BlockSpec((1,) + e.shape[2:], lambda b, t: (expert_of(b, t), 0, 0)) for e in experts] + [
                   pl.BlockSpec(slab(a), lambda b, t: (expert_of(b, t), 0)) for a, _ in dense],
        out_shape=[jax.ShapeDtypeStruct((batch, t_len, Q_COLS), BF16),
                   jax.ShapeDtypeStruct((batch, t_len, K_COLS), BF16),
                   jax.ShapeDtypeStruct((batch, t_len, V_COLS), BF16)] + [
                   jax.ShapeDtypeStruct(e.shape[1:], BF16) for e in experts] + [
                   jax.ShapeDtypeStruct(a.shape[1:], BF16) for a, _ in dense],
        compiler_params=pltpu.CompilerParams(dimension_semantics=("arbitrary", "arbitrary"),
                                             vmem_limit_bytes=VMEM_LIMIT),
        name="inproj",
    )(*stream, mod, g, w, cos, sa, sb, gq, gk, mseg, *experts, *[a for a, _ in dense])


def _scores_pass(qq, k_ref, bi, col, starts, s_ref, mask_fn=None, extra=None):
    half = KEY_CHUNK // 2
    rows = qq.shape[0]
    mrun = None
    for j, st in enumerate(starts):
        s = _dot_nt(qq, k_ref[bi, pl.ds(st, KEY_CHUNK), col:col + LANES])
        if mask_fn is not None:
            s = mask_fn(j, st, s)
        s_ref[0:rows, j * KEY_CHUNK:(j + 1) * KEY_CHUNK] = s
        mj = jnp.maximum(s[:, :half], s[:, half:])
        mrun = mj if mrun is None else jnp.maximum(mrun, mj)
    m = jnp.max(mrun, axis=-1, keepdims=True)
    return m if extra is None else jnp.maximum(m, extra)


def _values_pass(m, rows, v_ref, bi, col, starts, s_ref, extra=None):
    half = KEY_CHUNK // 2
    lrun = None
    acc = None
    for j, st in enumerate(starts):
        e = jnp.exp(s_ref[0:rows, j * KEY_CHUNK:(j + 1) * KEY_CHUNK] - m)
        lj = e[:, :half] + e[:, half:]
        lrun = lj if lrun is None else lrun + lj
        pv = _dot(e.astype(BF16), v_ref[bi, pl.ds(st, KEY_CHUNK), col:col + LANES])
        acc = pv if acc is None else acc + pv
    l = jnp.sum(lrun, axis=-1, keepdims=True)
    if extra is not None:
        l = l + jnp.exp(extra - m)
    return acc / l


def _softmax_pv(qq, k_ref, v_ref, bi, col, starts, s_ref, mask_fn=None, extra=None):
    m = _scores_pass(qq, k_ref, bi, col, starts, s_ref, mask_fn, extra)
    return _values_pass(m, qq.shape[0], v_ref, bi, col, starts, s_ref, extra)


def _all_chunks(n_keys):
    return [j * KEY_CHUNK for j in range(n_keys // KEY_CHUNK)]


def _half_masks(shape):
    lane = lax.broadcasted_iota(jnp.int32, shape, 1)
    return lane < HEAD_DIM, lane >= HEAD_DIM


def _stack_group_queries(q2, kv):
    lo_m, hi_m = _half_masks((TILE, LANES))
    m = lo_m if kv == 0 else hi_m
    zero = jnp.zeros((TILE, LANES), q2.dtype)
    return jnp.concatenate([jnp.where(m, q2[:, :LANES], zero), jnp.where(m, q2[:, LANES:], zero)], axis=0)


def _merge_kv_outputs(o_kv0, o_kv1):
    lo_m, _ = _half_masks((TILE, LANES))
    kv0 = jnp.where(lo_m, o_kv0[:TILE], pltpu.roll(o_kv0[TILE:], HEAD_DIM, 1))
    kv1 = jnp.where(lo_m, pltpu.roll(o_kv1[:TILE], HEAD_DIM, 1), o_kv1[TILE:])
    return jnp.concatenate([kv0, kv1], axis=1)


def _attn_kernel(sink_ref, lam_ref, subg_ref, qa_ref, qb_ref, qc_ref, ka_ref, kb_ref, kc_ref, va_ref, vb_ref, vc_ref,
                 o_ref, s_ref, *, t0, ctx_len, lambda_init):
    t = pl.program_id(1) + t0
    t_len = ka_ref.shape[1]
    band = 2 * TILE
    lo_m, hi_m = _half_masks((TILE, LANES))
    zero = jnp.zeros((TILE, LANES), BF16)
    row = lax.broadcasted_iota(jnp.int32, (2 * TILE, 1), 0)
    lv = lam_ref[...]
    lam = (jnp.exp(jnp.sum(lv[0:1] * lv[1:2], keepdims=True))
           - jnp.exp(jnp.sum(lv[2:3] * lv[3:4], keepdims=True)) + lambda_init)

    def sink_col(kv):
        return jnp.where(row < TILE, sink_ref[2 * kv], sink_ref[2 * kv + 1])

    def mixers(starts_ab, starts_c, mask_c):
        n_chain = 0
        for bi in range(qa_ref.shape[0]):
            for h in range(QA // LANES):
                q = qa_ref[bi, :, h * LANES:(h + 1) * LANES]
                qq = jnp.concatenate([jnp.where(lo_m, q, zero), jnp.where(hi_m, q, zero)], axis=0)
                o = _softmax_pv(qq, ka_ref, va_ref, bi, h * LANES, starts_ab, s_ref.at[n_chain % 2])
                n_chain += 1
                o = o[:TILE] - lam * o[TILE:]
                o_ref[bi, :, h * LANES:(h + 1) * LANES] = (
                    _rms(o, subg_ref[...]) * (1.0 - lambda_init)).astype(BF16)
            outs = []
            for kv in range(2):
                outs.append(_softmax_pv(_stack_group_queries(qb_ref[bi], kv), kb_ref, vb_ref, bi, 0, starts_ab,
                                        s_ref.at[n_chain % 2]))
                n_chain += 1
            o_ref[bi, :, QA:QA + QB] = _merge_kv_outputs(*outs).astype(BF16)
            outs = []
            for kv in range(2):
                outs.append(_softmax_pv(_stack_group_queries(qc_ref[bi], kv), kc_ref, vc_ref, bi, 0, starts_c,
                                        s_ref.at[n_chain % 2], mask_c, sink_col(kv)))
                n_chain += 1
            o_ref[bi, :, QA + QB:] = _merge_kv_outputs(*outs).astype(BF16)

    def ctx_step():
        mixers([0], [0], None)

    def latent_step():
        q_start = (t - 1) * TILE
        u0 = pl.multiple_of(jnp.minimum(q_start + ctx_len - WINDOW, t_len - band), WINDOW)
        q_pos = q_start + (lax.broadcasted_iota(jnp.int32, (2 * TILE, KEY_CHUNK), 0) & (TILE - 1))
        key_lane = lax.broadcasted_iota(jnp.int32, (2 * TILE, KEY_CHUNK), 1)

        def mask_c(j, st, s):
            if j == 0:
                return s
            k_pos = (st - ctx_len) + key_lane
            valid = (jnp.abs(k_pos - q_pos) <= WINDOW) & (k_pos >= 0)
            return jnp.where(valid, s, NEG_BIG)

        mixers(_all_chunks(t_len), [0] + [u0 + i * KEY_CHUNK for i in range(band // KEY_CHUNK)], mask_c)

    if t0 == 0:
        pl.when(t == 0)(ctx_step)
        pl.when(t > 0)(latent_step)
    else:
        latent_step()


def _attention(sink, lamv, subg, q, k, v, t0, lambda_init):
    batch, t_len, _ = q.shape
    nt = t_len // TILE
    kern = functools.partial(_attn_kernel, t0=t0, ctx_len=TILE, lambda_init=lambda_init)
    const = lambda b, t: (0, 0)
    nb = 1
    qspec = lambda width, blk: pl.BlockSpec((nb, TILE, width), lambda b, t: (b, t + t0, blk))
    kvspec = lambda width, blk: pl.BlockSpec((nb, t_len, width), lambda b, t: (b, 0, blk))
    return pl.pallas_call(
        kern,
        grid=(batch // nb, nt - t0),
        in_specs=[pl.BlockSpec(memory_space=pltpu.SMEM),
                  pl.BlockSpec(lamv.shape, const),
                  pl.BlockSpec((1, LANES), const),
                  qspec(QA, 0), qspec(QB, QA // QB), qspec(QC, (QA + QB) // QC),
                  kvspec(KA, 0), kvspec(KB, KA // KB), kvspec(KC, (KA + KB) // KC),
                  kvspec(KA, 0), kvspec(KB, KA // KB), kvspec(KC, (KA + KB) // KC)],
        out_specs=pl.BlockSpec((nb, TILE, Q_COLS), lambda b, t: (b, t + t0, 0)),
        out_shape=jax.ShapeDtypeStruct((batch, t_len, Q_COLS), BF16),
        scratch_shapes=[pltpu.VMEM((2, 2 * TILE, t_len), F32)],
        compiler_params=pltpu.CompilerParams(dimension_semantics=("parallel", "parallel"),
                                             vmem_limit_bytes=VMEM_LIMIT),
        name="attention",
    )(sink, lamv, subg, q, q, q, k, k, k, v, v, v)


def _route(scores, bias):
    expert = lax.broadcasted_iota(jnp.int32, scores.shape, 0)
    sel = scores + bias
    in_group = expert & (EXPERTS_PER_GROUP - 1)
    group = expert >> 2

    def neighbours(x, idx, step, span):
        for d in (1, 2, 3):
            fwd = (idx + d) < 4
            y = jnp.where(fwd, pltpu.roll(x, N_EXPERTS - d * step, 0), pltpu.roll(x, span - d * step, 0))
            yield y, fwd

    rank = jnp.zeros_like(sel)
    for y, fwd in neighbours(sel, in_group, 1, EXPERTS_PER_GROUP):
        beats = (y > sel) | ((y == sel) & jnp.logical_not(fwd))
        rank = rank + jnp.where(beats, 1.0, 0.0)
    top2 = jnp.where(rank < 2.0, sel, 0.0)
    gsum = top2
    for y, _ in neighbours(top2, in_group, 1, EXPERTS_PER_GROUP):
        gsum = gsum + y
    grank = jnp.zeros_like(sel)
    for y, fwd in neighbours(gsum, group, EXPERTS_PER_GROUP, N_EXPERTS):
        beats = (y > gsum) | ((y == gsum) & jnp.logical_not(fwd))
        grank = grank + jnp.where(beats, 1.0, 0.0)
    best = grank < 0.5
    chosen = best & (rank < 2.0)
    w = jnp.where(chosen, scores, 0.0)
    gates = w / jnp.sum(w, axis=0, keepdims=True)
    gid = jnp.sum(jnp.where(best & (in_group == 0), group.astype(F32), 0.0), axis=0, keepdims=True)
    bits = jnp.sum(jnp.where(chosen, jnp.left_shift(1, in_group).astype(F32), 0.0), axis=0, keepdims=True)
    pair = sum(jnp.where(bits == float((1 << a) + (1 << b)), float(k), 0.0) for k, (a, b) in enumerate(PAIRS))
    bucket = gid * len(PAIRS) + pair
    tokens = scores.shape[1]
    return jnp.concatenate([gates, jnp.broadcast_to(bucket, (8, tokens)),
                            jnp.zeros((LANES - N_EXPERTS - 8, tokens), F32)], axis=0)


def _outproj_kernel(*refs, n_stream, t0):
    stream_refs = refs[:n_stream]
    tile = pl.program_id(1) + t0
    mix_ref, w_ref, mod_ref, g_ref, wrh_ref, wrl_ref, rb_ref, x1_ref, h2_ref, gates_ref, bucket_ref = refs[n_stream:]
    for i in range(mix_ref.shape[0]):
        x1 = _stream_rows(stream_refs, i, tile) + mod_ref[i, 0, 2:3, :] * _dot(mix_ref[i], w_ref[...])
        x1_ref[i] = x1
        h2 = _rms(x1, g_ref[...]) * (1.0 + mod_ref[i, 0, 4:5, :]) + mod_ref[i, 0, 3:4, :]
        h2_ref[i] = _pack_bf16_pairs(h2)
        hi, lo = _split(h2)
        logits = _dot_nt(wrh_ref[...], hi) + _dot_nt(wrl_ref[...], hi) + _dot_nt(wrh_ref[...], lo)
        routed = _route(_sigmoid(logits), rb_ref[...])
        gates_ref[i] = routed.T
        bucket_ref[i] = routed[GROUP_LANE:GROUP_LANE + 8]


def _outproj(stream, mix, w, mod, g, wrh, wrl, rb, t0):
    batch, t_len, d = _stream_dims(stream)
    nt = t_len // TILE
    nb = 2 * PAIR
    t_out = (nt - t0) * TILE
    const = lambda b, t: (0, 0)
    tok_in = lambda b, t: (b, t + t0, 0)
    tok_out = lambda b, t: (b, t, 0)
    return pl.pallas_call(
        functools.partial(_outproj_kernel, n_stream=len(stream), t0=t0),
        grid=(batch // nb, nt - t0),
        in_specs=_stream_specs(stream, nb, lambda t: t + t0) + [
                  pl.BlockSpec((nb, TILE, Q_COLS), tok_in),
                  pl.BlockSpec(w.shape, const),
                  pl.BlockSpec((nb, 1, 8, d), lambda b, t: (b, jnp.minimum(t + t0, 1), 0, 0)),
                  pl.BlockSpec((1, d), const),
                  pl.BlockSpec((N_EXPERTS, d), const),
                  pl.BlockSpec((N_EXPERTS, d), const),
                  pl.BlockSpec((N_EXPERTS, 1), const)],
        out_specs=[pl.BlockSpec((nb, TILE, d), tok_out),
                   pl.BlockSpec((nb, TILE, d // 2), tok_out),
                   pl.BlockSpec((nb, TILE, LANES), tok_out),
                   pl.BlockSpec((nb, 8, TILE), tok_out)],
        out_shape=[jax.ShapeDtypeStruct((batch, t_out, d), F32),
                   jax.ShapeDtypeStruct((batch, t_out, d // 2), jnp.uint32),
                   jax.ShapeDtypeStruct((batch, t_out, LANES), F32),
                   jax.ShapeDtypeStruct((batch, (nt - t0) * 8, TILE), F32)],
        compiler_params=pltpu.CompilerParams(dimension_semantics=("parallel", "parallel"),
                                             vmem_limit_bytes=VMEM_LIMIT),
        name="outproj",
    )(*stream, mix, w, mod, g, wrh, wrl, rb)


def _sc_workers():
    info = plsc.get_sparse_core_info()
    return info.num_cores, info.num_subcores


def _sc_dispatch(hpk, gates, pos2d, n_out):
    n, w = hpk.shape
    gw = gates.shape[1]
    nc, ns = _sc_workers()
    steps = n // (nc * ns * SC_WIN)
    mesh = plsc.VectorSubcoreMesh(core_axis_name="c", subcore_axis_name="s")

    @functools.partial(
        pl.kernel, mesh=mesh,
        out_type=(jax.ShapeDtypeStruct((n_out, w), hpk.dtype), jax.ShapeDtypeStruct((n_out, gw), gates.dtype)),
        scratch_types=[pltpu.VMEM((1, SC_WIN), jnp.int32), pltpu.VMEM((SC_WIN, w), hpk.dtype),
                       pltpu.VMEM((SC_WIN, gw), gates.dtype), pltpu.SemaphoreType.DMA, pltpu.SemaphoreType.DMA],
        name="sc_dispatch")
    def k(h_hbm, g_hbm, pos_hbm, hs_hbm, gs_hbm, idx_v, rows_v, grow_v, sem_in, sem_out):
        wid = lax.axis_index("s") * nc + lax.axis_index("c")

        @pl.loop(0, steps)
        def _(j):
            blk = wid * steps + j
            off = pl.multiple_of(blk * SC_WIN, SC_WIN)
            loads = [pltpu.async_copy(pos_hbm.at[pl.ds(blk, 1)], idx_v, sem_in),
                     pltpu.async_copy(h_hbm.at[pl.ds(off, SC_WIN)], rows_v, sem_in),
                     pltpu.async_copy(g_hbm.at[pl.ds(off, SC_WIN)], grow_v, sem_in)]
            for cp in loads:
                cp.wait()
            stores = [pltpu.async_copy(rows_v, hs_hbm.at[idx_v.at[0]], sem_out),
                      pltpu.async_copy(grow_v, gs_hbm.at[idx_v.at[0]], sem_out)]
            for cp in stores:
                cp.wait()

    return k(hpk, gates, pos2d)


def _sc_combine(ys, pos2d, n):
    w = ys.shape[1]
    nc, ns = _sc_workers()
    steps = n // (nc * ns * SC_WIN)
    mesh = plsc.VectorSubcoreMesh(core_axis_name="c", subcore_axis_name="s")

    @functools.partial(
        pl.kernel, mesh=mesh,
        out_type=jax.ShapeDtypeStruct((n, w), ys.dtype),
        scratch_types=[pltpu.VMEM((1, SC_WIN), jnp.int32), pltpu.VMEM((SC_WIN, w), ys.dtype)],
        name="sc_combine")
    def k(y_hbm, pos_hbm, o_hbm, idx_v, rows_v):
        wid = lax.axis_index("s") * nc + lax.axis_index("c")

        @pl.loop(0, steps)
        def _(j):
            blk = wid * steps + j
            off = pl.multiple_of(blk * SC_WIN, SC_WIN)
            pltpu.sync_copy(pos_hbm.at[pl.ds(blk, 1)], idx_v)
            pltpu.sync_copy(y_hbm.at[idx_v.at[0]], rows_v)
            pltpu.sync_copy(rows_v, o_hbm.at[pl.ds(off, SC_WIN)])

    return k(ys, pos2d)


PLAN_ROWS = 256


def _plan_kernel(g_ref, pos_ref, tg_ref, tk_ref, slots_ref):
    _plan_finish(g_ref, pos_ref, tg_ref, tk_ref, slots_ref)


def _plan_finish(g_ref, pos_ref, tg_ref, tk_ref, slots_ref):
    g = g_ref[...]
    rows = g.shape[0]
    n_groups = N_EXPERTS // EXPERTS_PER_GROUP
    before_lane = (lax.broadcasted_iota(jnp.int32, (LANES, LANES), 0)
                   < lax.broadcasted_iota(jnp.int32, (LANES, LANES), 1)).astype(BF16)
    before_row = (lax.broadcasted_iota(jnp.int32, (rows, rows), 1)
                  < lax.broadcasted_iota(jnp.int32, (rows, rows), 0)).astype(BF16)
    tile_row = lax.broadcasted_iota(jnp.int32, (1, LANES), 1).astype(F32) * FFN_TILE
    expert = lax.broadcasted_iota(jnp.int32, (N_EXPERTS, 1), 0)
    pos = jnp.zeros_like(g)
    start = jnp.zeros((1, 1), F32)
    tile_group = jnp.zeros((1, LANES), F32)
    need = jnp.zeros((N_EXPERTS, LANES), F32)
    for grp in range(n_groups):
        for k, (ea, eb) in enumerate(PAIRS):
            m = jnp.where(g == grp * len(PAIRS) + k, 1.0, 0.0)
            in_row = _dot(m.astype(BF16), before_lane)
            row_total = jnp.sum(m, axis=1, keepdims=True)
            rows_before = _dot(before_row, jnp.broadcast_to(row_total, m.shape).astype(BF16))
            pos = pos + m * (start + in_row + rows_before)
            end = start + jnp.sum(row_total, axis=0, keepdims=True)
            holds = jnp.where((tile_row < end) & (tile_row + FFN_TILE > start) & (end > start), 1.0, 0.0)
            uses = jnp.where((expert == grp * EXPERTS_PER_GROUP + ea) | (expert == grp * EXPERTS_PER_GROUP + eb),
                             1.0, 0.0)
            need = need + uses * holds
            start = end
        start = jnp.floor((start + (FFN_TILE - 1)) * (1.0 / FFN_TILE)) * FFN_TILE
        if grp < n_groups - 1:
            tile_group = tile_group + jnp.where(tile_row >= start, 1.0, 0.0)
    pos_ref[...] = pos.astype(jnp.int32)
    tg_ref[...] = tile_group.astype(jnp.int32)
    count = jnp.zeros((1, LANES), F32)
    slots = [tile_group * EXPERTS_PER_GROUP] * EXPERTS_PER_GROUP
    for e in range(N_EXPERTS):
        used = need[e:e + 1] > 0.0
        slots = [jnp.where(used & (count == k), float(e), slot) for k, slot in enumerate(slots)]
        count = count + jnp.where(used, 1.0, 0.0)
    tk_ref[...] = count.astype(jnp.int32)
    slots_ref[...] = jnp.concatenate(slots + [jnp.zeros((8 - EXPERTS_PER_GROUP, LANES), F32)], axis=0).astype(jnp.int32)


def _bucket_plan(bucket, n, n_pad):
    g = bucket.reshape(bucket.shape[0], -1, 8, TILE)[:, :, 0, :].reshape(n // LANES, LANES)
    g = jnp.pad(g, ((0, PLAN_ROWS - n // LANES), (0, 0)), constant_values=-1.0)
    pos, tile_group, tile_k, slots = pl.pallas_call(
        _plan_kernel,
        out_shape=[jax.ShapeDtypeStruct((PLAN_ROWS, LANES), jnp.int32),
                   jax.ShapeDtypeStruct((1, LANES), jnp.int32),
                   jax.ShapeDtypeStruct((1, LANES), jnp.int32),
                   jax.ShapeDtypeStruct((8, LANES), jnp.int32)],
        name="plan",
    )(g)
    n_tiles = n_pad // FFN_TILE
    return (pos[:n // LANES].reshape(n // SC_WIN, SC_WIN), tile_group[0, :n_tiles], tile_k[0, :n_tiles],
            [slots[k, :n_tiles] for k in range(EXPERTS_PER_GROUP)])


def _ffn_kernel(tg_ref, tk_ref, s0_ref, s1_ref, s2_ref, s3_ref, hs_ref, gs_ref, wg_ref, wu_ref, wd_ref, ys_ref):
    i = pl.program_id(0)
    slot_refs = (s0_ref, s1_ref, s2_ref, s3_ref)

    def run_experts(n_experts):
        def body():
            lo, hi = _unpack_bf16_pairs(hs_ref[...])
            lo = lo.astype(BF16)
            hi = hi.astype(BF16)
            half = lo.shape[1]
            gs = gs_ref[...]
            lane = lax.broadcasted_iota(jnp.int32, gs.shape, 1)
            y = None
            for k in range(n_experts):
                e = slot_refs[k][i]
                local = e - tg_ref[i] * EXPERTS_PER_GROUP
                wg, wu, wd = wg_ref.at[local], wu_ref.at[local], wd_ref.at[local]
                a = _dot(lo, wg[:half, :]) + _dot(hi, wg[half:, :])
                u = _dot(lo, wu[:half, :]) + _dot(hi, wu[half:, :])
                gate = jnp.sum(jnp.where(lane == e, gs, 0.0), axis=-1, keepdims=True)
                part = _dot(((a * _sigmoid(a)) * u * gate).astype(BF16), wd[...])
                y = part if y is None else y + part
            ys_ref[...] = _pack_bf16_pairs(y)
        return body

    for n_experts in range(2, EXPERTS_PER_GROUP + 1):
        pl.when(tk_ref[i] == n_experts)(run_experts(n_experts))


def _ffn(tile_group, tile_k, slots, hs, gs, wg, wu, wd):
    n_pad, half = hs.shape
    grp = lambda i, tg, *_: (tg[i], 0, 0)
    row = lambda i, *_: (i, 0)
    wspec = lambda w: pl.BlockSpec((EXPERTS_PER_GROUP,) + w.shape[1:], grp)
    return pl.pallas_call(
        _ffn_kernel,
        grid_spec=pltpu.PrefetchScalarGridSpec(
            num_scalar_prefetch=2 + EXPERTS_PER_GROUP,
            grid=(n_pad // FFN_TILE,),
            in_specs=[pl.BlockSpec((FFN_TILE, half), row),
                      pl.BlockSpec((FFN_TILE, LANES), row),
                      wspec(wg), wspec(wu), wspec(wd)],
            out_specs=pl.BlockSpec((FFN_TILE, half), row)),
        out_shape=jax.ShapeDtypeStruct((n_pad, half), jnp.uint32),
        compiler_params=pltpu.CompilerParams(dimension_semantics=("parallel",), vmem_limit_bytes=VMEM_LIMIT),
        name="ffn",
    )(tile_group, tile_k, *slots, hs, gs, wg, wu, wd)


def _residual_kernel(x_ref, y_ref, mod_ref, fg_ref, o_ref, *, final_norm):
    for i in range(x_ref.shape[0]):
        lo, hi = _unpack_bf16_pairs(y_ref[i])
        y = x_ref[i] + mod_ref[i, 0, 5:6, :] * jnp.concatenate([lo, hi], axis=1)
        if final_norm:
            y = _rms(y, fg_ref[...])
        o_ref[i] = y


def _residual(x1, yg, mod, fg, t0, final_norm):
    batch, t_out, d = x1.shape
    nb = PAIR
    tok = lambda b, t: (b, t, 0)
    return pl.pallas_call(
        functools.partial(_residual_kernel, final_norm=final_norm),
        grid=(batch // nb, t_out // TILE),
        in_specs=[pl.BlockSpec((nb, TILE, d), tok),
                  pl.BlockSpec((nb, TILE, d // 2), tok),
                  pl.BlockSpec((nb, 1, 8, d), lambda b, t: (b, jnp.minimum(t + t0, 1), 0, 0)),
                  pl.BlockSpec((1, d), lambda b, t: (0, 0))],
        out_specs=pl.BlockSpec((nb, TILE, d), tok),
        out_shape=jax.ShapeDtypeStruct((batch, t_out, d), F32),
        compiler_params=pltpu.CompilerParams(dimension_semantics=("parallel", "parallel"),
                                             vmem_limit_bytes=VMEM_LIMIT),
        name="residual",
    )(x1, yg, mod, fg)


def _moe(h2, gates, bucket, wg, wu, wd):
    batch, t_out, half = h2.shape
    n = batch * t_out
    n_pad = n + (N_EXPERTS // EXPERTS_PER_GROUP) * FFN_TILE
    assert n % LANES == 0 and n // LANES <= PLAN_ROWS and n_pad // FFN_TILE <= LANES
    pos2d, tile_group, tile_k, slots = _bucket_plan(bucket, n, n_pad)
    hs, gs = _sc_dispatch(h2.reshape(n, half), gates.reshape(n, LANES), pos2d, n_pad)
    ys = _ffn(tile_group, tile_k, slots, hs, gs, wg, wu, wd)
    return _sc_combine(ys, pos2d, n).reshape(batch, t_out, half)


def _rope_tables(seq, ctx_len):
    rows = seq // GRID_W
    row_pos = jnp.repeat(jnp.arange(rows, dtype=F32), GRID_W)
    col_pos = jnp.tile(jnp.arange(GRID_W, dtype=F32), rows)
    axis_dim = HEAD_DIM // 2
    inv_freq = ROPE_BASE ** (-jnp.arange(0, axis_dim, 2, dtype=F32) / axis_dim)
    ang_r = row_pos[:, None] * inv_freq[None, :]
    ang_c = col_pos[:, None] * inv_freq[None, :]
    z = jnp.zeros_like(ang_r)
    cos = jnp.concatenate([jnp.cos(ang_r)] * 2 + [jnp.cos(ang_c)] * 2, axis=-1)
    sin_a = jnp.concatenate([-jnp.sin(ang_r), z, -jnp.sin(ang_c), z], axis=-1)
    sin_b = jnp.concatenate([z, jnp.sin(ang_r), z, jnp.sin(ang_c)], axis=-1)

    def full(tab, fill):
        tab = jnp.tile(tab, (1, LANES // HEAD_DIM))
        return jnp.concatenate([jnp.full((ctx_len, LANES), fill, F32), tab], axis=0)

    return full(cos, 1.0), full(sin_a, 0.0), full(sin_b, 0.0)


def kernel(x, c, ctx, c_ctx, w_ada, b_ada, norm1_g, norm2_g, w_in, w_out, lam_q1, lam_k1, lam_q2, lam_k2,
           subln_g, q_norm_g, k_norm_g, sink, w_router, router_bias, w_gate, w_up, w_down, final_g):
    batch, seq, d = x.shape
    ctx_len = ctx.shape[1]
    depth = w_in.shape[0]
    sc_rows = SC_WIN * math.prod(_sc_workers())
    assert ctx_len == TILE and seq % TILE == 0 and batch % PAIR == 0 and batch <= 15
    assert (batch * seq) % sc_rows == 0 and (batch * (seq + ctx_len)) % sc_rows == 0

    cpad = jnp.zeros((16, d), F32).at[:batch].set(c).at[batch].set(c_ctx)
    mod_all = _adaln(cpad, w_ada, b_ada)
    cos, sin_a, sin_b = _rope_tables(seq, ctx_len)
    mseg = jnp.asarray(np.kron(np.eye(2), np.ones((HEAD_DIM, HEAD_DIM))), BF16)
    wr = w_router.T
    wrh = wr.astype(BF16)
    wrl = (wr - wrh.astype(F32)).astype(BF16)
    rb = router_bias.reshape(N_EXPERTS, 1)

    stream = (ctx, x)

    for l in range(depth):
        last = l == depth - 1
        lambda_init = 0.8 - 0.6 * math.exp(-0.3 * l)
        m6 = mod_all[l].reshape(16, 6, d)
        m8 = jnp.concatenate([m6, jnp.zeros((16, 2, d), F32)], axis=1)
        mod = jnp.stack([jnp.broadcast_to(m8[batch], (batch, 8, d)), m8[:batch]], axis=1)

        w_in_p = w_in[0].astype(BF16) if l == 0 else w_in_next
        dense = [(w_out, l)] + ([] if last else [(w_in, l + 1)])
        gq = jnp.tile(q_norm_g[l], 2).reshape(1, LANES)
        gk = jnp.tile(k_norm_g[l], 2).reshape(1, LANES)
        q, k, v, *cast = _inproj(stream, mod, norm1_g[l].reshape(1, d), w_in_p, cos, sin_a, sin_b, gq, gk, mseg,
                                 (w_gate, w_up, w_down), l, dense)
        experts, w_out_p = cast[:3], cast[3]
        w_in_next = None if last else cast[4]

        t0 = 1 if last else 0
        lamv = jnp.stack([lam_q1[l], lam_k1[l], lam_q2[l], lam_k2[l]])
        mix = _attention(sink[l], lamv, subln_g[l].reshape(1, LANES), q, k, v, t0, lambda_init)
        x1, h2, gates, bucket = _outproj(stream, mix, w_out_p, mod, norm2_g[l].reshape(1, d), wrh, wrl, rb, t0)
        stream = (x1, _moe(h2, gates, bucket, *experts), mod)
    return _residual(*stream, final_g.reshape(1, d), 1, final_norm=True)
```

```python
import functools
import math

import numpy as np
import jax
import jax.numpy as jnp
from jax import lax
from jax.experimental import pallas as pl
from jax.experimental.pallas import tpu as pltpu
from jax.experimental.pallas import tpu_sc as plsc

F32 = jnp.float32
BF16 = jnp.bfloat16

HEAD_DIM = 64
GRID_W = 64
ROPE_BASE = 10000.0
RMS_EPS = 1e-6
WINDOW = 128
N_EXPERTS = 16
EXPERTS_PER_GROUP = 4
LANES = 128
TILE = 256
PAIR = 2
FFN_TILE = 512
SC_WIN = 64
GROUP_LANE = 16
PAIRS = ((0, 1), (0, 2), (0, 3), (1, 3), (1, 2), (2, 3))
NEG_BIG = -1e30
KEY_CHUNK = 256
VMEM_LIMIT = 52 * 1024 * 1024

QA, QB, QC = 512, 256, 256
KA, KB, KC = 512, 128, 128
Q_COLS = QA + QB + QC
K_COLS = KA + KB + KC
V_COLS = 768


def _dot(a, b):
    return jnp.dot(a, b, preferred_element_type=F32)


def _dot_nt(a, b):
    return lax.dot_general(a, b, (((1,), (1,)), ((), ())), preferred_element_type=F32)


def _split(a):
    hi = a.astype(BF16)
    lo = (a - hi.astype(F32)).astype(BF16)
    return hi, lo


def _dot3(a, b):
    ah, al = _split(a)
    bh, bl = _split(b)
    return _dot(ah, bh) + _dot(ah, bl) + _dot(al, bh)


def _rms(x, g):
    ms = jnp.mean(x * x, axis=-1, keepdims=True)
    return x * lax.rsqrt(ms + RMS_EPS) * g


def _sigmoid(x):
    return 1.0 / (1.0 + jnp.exp(-x))


def _pack_bf16_pairs(y):
    n = y.shape[1] // 2
    yb = y.astype(BF16).astype(F32)
    lo = lax.bitcast_convert_type(yb[:, :n], jnp.uint32) >> 16
    hi = lax.bitcast_convert_type(yb[:, n:], jnp.uint32) & jnp.uint32(0xFFFF0000)
    return hi | lo


def _unpack_bf16_pairs(w):
    lo = lax.bitcast_convert_type(w << 16, F32)
    hi = lax.bitcast_convert_type(w & jnp.uint32(0xFFFF0000), F32)
    return lo, hi


def _stream_dims(stream):
    batch, t_len, d = stream[0].shape
    return batch, (t_len + stream[1].shape[1] if len(stream) == 2 else t_len), d


def _stream_rows(stream_refs, i, tile):
    if len(stream_refs) == 2:
        ctx_ref, x_ref = stream_refs
        return jnp.where(tile == 0, ctx_ref[i], x_ref[i])
    x_ref, y_ref, mod_ref = stream_refs
    lo, hi = _unpack_bf16_pairs(y_ref[i])
    return x_ref[i] + mod_ref[i, 0, 5:6, :] * jnp.concatenate([lo, hi], axis=1)


def _stream_specs(stream, nb, tile_of):
    d = stream[0].shape[-1]
    if len(stream) == 2:
        return [pl.BlockSpec((nb, TILE, d), lambda b, t: (b, 0, 0)),
                pl.BlockSpec((nb, TILE, d), lambda b, t: (b, jnp.maximum(tile_of(t) - 1, 0), 0))]
    tok = lambda b, t: (b, tile_of(t), 0)
    return [pl.BlockSpec((nb, TILE, d), tok),
            pl.BlockSpec((nb, TILE, d // 2), tok),
            pl.BlockSpec((nb, 1, 8, d), lambda b, t: (b, jnp.minimum(tile_of(t), 1), 0, 0))]


def _adaln_kernel(c_ref, w_ref, b_ref, win_ref, o_ref, winb_ref):
    cv = c_ref[...]
    s = cv * _sigmoid(cv)
    o_ref[0] = _dot3(s, w_ref[0]) + b_ref[0]
    winb_ref[...] = win_ref[0].astype(BF16)


def _adaln(cpad, w_ada, b_ada, w_in):
    depth, d, n6 = w_ada.shape
    tn = 1536
    nj = n6 // tn
    rows, cols = w_in.shape[1] // (depth * nj), w_in.shape[2]
    assert rows * depth * nj == w_in.shape[1] and rows % 16 == 0
    return pl.pallas_call(
        _adaln_kernel,
        grid=(depth, nj),
        in_specs=[pl.BlockSpec((16, d), lambda l, j: (0, 0)),
                  pl.BlockSpec((1, d, tn), lambda l, j: (l, 0, j)),
                  pl.BlockSpec((1, 1, tn), lambda l, j: (l, 0, j)),
                  pl.BlockSpec((1, rows, cols), lambda l, j: (0, l * nj + j, 0))],
        out_specs=[pl.BlockSpec((1, 16, tn), lambda l, j: (l, 0, j)),
                   pl.BlockSpec((rows, cols), lambda l, j: (l * nj + j, 0))],
        out_shape=[jax.ShapeDtypeStruct((depth, 16, n6), F32),
                   jax.ShapeDtypeStruct(w_in.shape[1:], BF16)],
        compiler_params=pltpu.CompilerParams(dimension_semantics=("parallel", "parallel"),
                                             vmem_limit_bytes=VMEM_LIMIT),
        name="adaln",
    )(cpad, w_ada, b_ada.reshape(depth, 1, n6), w_in)


def _inproj_kernel(*refs, n_stream):
    rest = refs[n_stream:]
    n_jobs = (len(rest) - 12) // 2
    consts, jobs_f32, qkv, jobs_bf16 = rest[:9], rest[9:9 + n_jobs], rest[9 + n_jobs:12 + n_jobs], rest[12 + n_jobs:]
    for src, dst in zip(jobs_f32, jobs_bf16):
        dst[...] = src[0].astype(BF16)
    for base in range(0, refs[0].shape[0], PAIR):
        _inproj_rows(base, pl.program_id(1), refs[:n_stream], *consts, *qkv)


def _inproj_rows(base, tile, stream_refs, mod_ref, g_ref, w_ref, cos_ref, sa_ref, sb_ref, gq_ref, gk_ref, mseg_ref,
                 q_ref, k_ref, v_ref):
    members = range(base, base + PAIR)
    h = jnp.concatenate(
        [(_rms(_stream_rows(stream_refs, i, tile), g_ref[...]) * (1.0 + mod_ref[i, 0, 1:2, :])
          + mod_ref[i, 0, 0:1, :]).astype(BF16) for i in members], axis=0)
    cos = jnp.concatenate([cos_ref[...]] * PAIR, axis=0)
    sa = jnp.concatenate([sa_ref[...]] * PAIR, axis=0)
    sb = jnp.concatenate([sb_ref[...]] * PAIR, axis=0)
    mseg = mseg_ref[...]

    def put(ref, cols, val):
        for n, i in enumerate(members):
            ref[i, :, cols] = val[n * TILE:(n + 1) * TILE]

    def rope(b):
        return b * cos + pltpu.roll(b, LANES - 16, 1) * sa + pltpu.roll(b, 16, 1) * sb

    def qknorm(b, g):
        hi, lo = _split(b * b)
        ms = (_dot(hi, mseg) + _dot(lo, mseg)) * (1.0 / HEAD_DIM)
        return b * lax.rsqrt(ms + RMS_EPS) * g

    def wcols(*pieces):
        parts = [w_ref[:, a:a + n] for a, n in pieces]
        return parts[0] if len(parts) == 1 else jnp.concatenate(parts, axis=1)

    lane = lax.broadcasted_iota(jnp.int32, (PAIR * TILE, LANES), 1)

    def regroup(p2):
        h0, h1 = p2[:, :LANES], p2[:, LANES:]
        return (jnp.where(lane < HEAD_DIM, h0, pltpu.roll(h1, HEAD_DIM, 1)),
                jnp.where(lane < HEAD_DIM, pltpu.roll(h0, HEAD_DIM, 1), h1))

    q0, kv0 = 0, Q_COLS
    rope_chunks = [wcols((q0, 256)), wcols((q0 + 256, 256)), wcols((q0 + QA, QB)), wcols((q0 + QA + QB, QC)),
                   wcols((kv0, 256)), wcols((kv0 + 256, 256)),
                   wcols((kv0 + 2 * KA, KB), (kv0 + 2 * KA + 2 * KB, KC))]
    v_chunks = [wcols((kv0 + KA, 256)), wcols((kv0 + KA + 256, 256)),
                wcols((kv0 + 2 * KA + KB, KB), (kv0 + 2 * KA + 2 * KB + KC, KC))]
    for c, wc in enumerate(rope_chunks):
        p2 = _dot(h, wc)
        halves = regroup(p2) if c in (2, 3) else (p2[:, :LANES], p2[:, LANES:])
        for half in range(2):
            j = 2 * c + half
            p = halves[half]
            if j in (4, 5):
                p = qknorm(p, gq_ref[...])
            if j == 12:
                p = qknorm(p, gk_ref[...])
            p = rope(p)
            if j < Q_COLS // LANES:
                put(q_ref, slice(j * LANES, (j + 1) * LANES), (p * (HEAD_DIM ** -0.5)).astype(BF16))
            else:
                jk = j - Q_COLS // LANES
                put(k_ref, slice(jk * LANES, (jk + 1) * LANES), p.astype(BF16))
    for c, wc in enumerate(v_chunks):
        put(v_ref, slice(c * 256, (c + 1) * 256), _dot(h, wc).astype(BF16))


def _inproj(stream, mod, g, w, cos, sa, sb, gq, gk, mseg, experts, layer, dense):
    batch, t_len, d = _stream_dims(stream)
    nt = t_len // TILE
    nb = 2 * PAIR
    assert (batch // nb) * nt >= N_EXPERTS
    expert_of = lambda b, t: jnp.minimum(b * nt + t, N_EXPERTS - 1)
    slab = lambda a: (a.shape[1] // N_EXPERTS, a.shape[2])
    row = lambda b, t: (t, 0)
    const = lambda b, t: (0, 0)
    tok = lambda b, t: (b, t, 0)
    return pl.pallas_call(
        functools.partial(_inproj_kernel, n_stream=len(stream)),
        grid=(batch // nb, nt),
        in_specs=_stream_specs(stream, nb, lambda t: t) + [
                  pl.BlockSpec((nb, 1, 8, d), lambda b, t: (b, jnp.minimum(t, 1), 0, 0)),
                  pl.BlockSpec((1, d), const),
                  pl.BlockSpec(w.shape, const),
                  pl.BlockSpec((TILE, LANES), row),
                  pl.BlockSpec((TILE, LANES), row),
                  pl.BlockSpec((TILE, LANES), row),
                  pl.BlockSpec((1, LANES), const),
                  pl.BlockSpec((1, LANES), const),
                  pl.BlockSpec((LANES, LANES), const)] + [
                  pl.BlockSpec((1, 1) + e.shape[2:], lambda b, t: (layer, expert_of(b, t), 0, 0)) for e in experts] + [
                  pl.BlockSpec((1,) + slab(a), lambda b, t, li=li: (li, expert_of(b, t), 0)) for a, li in dense],
        out_specs=[pl.BlockSpec((nb, TILE, Q_COLS), tok),
                   pl.BlockSpec((nb, TILE, K_COLS), tok),
                   pl.BlockSpec((nb, TILE, V_COLS), tok)] + [
                   pl.BlockSpec((1,) + e.shape[2:], lambda b, t: (expert_of(b, t), 0, 0)) for e in experts] + [
                   pl.BlockSpec(slab(a), lambda b, t: (expert_of(b, t), 0)) for a, _ in dense],
        out_shape=[jax.ShapeDtypeStruct((batch, t_len, Q_COLS), BF16),
                   jax.ShapeDtypeStruct((batch, t_len, K_COLS), BF16),
                   jax.ShapeDtypeStruct((batch, t_len, V_COLS), BF16)] + [
                   jax.ShapeDtypeStruct(e.shape[1:], BF16) for e in experts] + [
                   jax.ShapeDtypeStruct(a.shape[1:], BF16) for a, _ in dense],
        compiler_params=pltpu.CompilerParams(dimension_semantics=("arbitrary", "arbitrary"),
                                             vmem_limit_bytes=VMEM_LIMIT),
        name="inproj",
    )(*stream, mod, g, w, cos, sa, sb, gq, gk, mseg, *experts, *[a for a, _ in dense])


def _scores_pass(qq, k_ref, bi, col, starts, s_ref, mask_fn=None, extra=None):
    half = KEY_CHUNK // 2
    rows = qq.shape[0]
    mrun = None
    for j, st in enumerate(starts):
        s = _dot_nt(qq, k_ref[bi, pl.ds(st, KEY_CHUNK), col:col + LANES])
        if mask_fn is not None:
            s = mask_fn(j, st, s)
        s_ref[0:rows, j * KEY_CHUNK:(j + 1) * KEY_CHUNK] = s
        mj = jnp.maximum(s[:, :half], s[:, half:])
        mrun = mj if mrun is None else jnp.maximum(mrun, mj)
    m = jnp.max(mrun, axis=-1, keepdims=True)
    return m if extra is None else jnp.maximum(m, extra)


def _values_pass(m, rows, v_ref, bi, col, starts, s_ref, extra=None):
    half = KEY_CHUNK // 2
    lrun = None
    acc = None
    for j, st in enumerate(starts):
        e = jnp.exp(s_ref[0:rows, j * KEY_CHUNK:(j + 1) * KEY_CHUNK] - m)
        lj = e[:, :half] + e[:, half:]
        lrun = lj if lrun is None else lrun + lj
        pv = _dot(e.astype(BF16), v_ref[bi, pl.ds(st, KEY_CHUNK), col:col + LANES])
        acc = pv if acc is None else acc + pv
    l = jnp.sum(lrun, axis=-1, keepdims=True)
    if extra is not None:
        l = l + jnp.exp(extra - m)
    return acc / l


def _softmax_pv(qq, k_ref, v_ref, bi, col, starts, s_ref, mask_fn=None, extra=None):
    m = _scores_pass(qq, k_ref, bi, col, starts, s_ref, mask_fn, extra)
    return _values_pass(m, qq.shape[0], v_ref, bi, col, starts, s_ref, extra)


def _all_chunks(n_keys):
    return [j * KEY_CHUNK for j in range(n_keys // KEY_CHUNK)]


def _half_masks(shape):
    lane = lax.broadcasted_iota(jnp.int32, shape, 1)
    return lane < HEAD_DIM, lane >= HEAD_DIM


def _stack_group_queries(q2, kv):
    lo_m, hi_m = _half_masks((TILE, LANES))
    m = lo_m if kv == 0 else hi_m
    zero = jnp.zeros((TILE, LANES), q2.dtype)
    return jnp.concatenate([jnp.where(m, q2[:, :LANES], zero), jnp.where(m, q2[:, LANES:], zero)], axis=0)


def _merge_kv_outputs(o_kv0, o_kv1):
    lo_m, _ = _half_masks((TILE, LANES))
    kv0 = jnp.where(lo_m, o_kv0[:TILE], pltpu.roll(o_kv0[TILE:], HEAD_DIM, 1))
    kv1 = jnp.where(lo_m, pltpu.roll(o_kv1[:TILE], HEAD_DIM, 1), o_kv1[TILE:])
    return jnp.concatenate([kv0, kv1], axis=1)


def _attn_kernel(sink_ref, lam_ref, subg_ref, qa_ref, qb_ref, qc_ref, ka_ref, kb_ref, kc_ref, va_ref, vb_ref, vc_ref,
                 o_ref, s_ref, *, t0, ctx_len, lambda_init):
    t = pl.program_id(1) + t0
    t_len = ka_ref.shape[1]
    band = 2 * TILE
    lo_m, hi_m = _half_masks((TILE, LANES))
    zero = jnp.zeros((TILE, LANES), BF16)
    row = lax.broadcasted_iota(jnp.int32, (2 * TILE, 1), 0)
    lv = lam_ref[...]
    lam = (jnp.exp(jnp.sum(lv[0:1] * lv[1:2], keepdims=True))
           - jnp.exp(jnp.sum(lv[2:3] * lv[3:4], keepdims=True)) + lambda_init)

    def sink_col(kv):
        return jnp.where(row < TILE, sink_ref[2 * kv], sink_ref[2 * kv + 1])

    def mixers(starts_ab, starts_c, mask_c):
        n_chain = 0
        for bi in range(qa_ref.shape[0]):
            for h in range(QA // LANES):
                q = qa_ref[bi, :, h * LANES:(h + 1) * LANES]
                qq = jnp.concatenate([jnp.where(lo_m, q, zero), jnp.where(hi_m, q, zero)], axis=0)
                o = _softmax_pv(qq, ka_ref, va_ref, bi, h * LANES, starts_ab, s_ref.at[n_chain % 2])
                n_chain += 1
                o = o[:TILE] - lam * o[TILE:]
                o_ref[bi, :, h * LANES:(h + 1) * LANES] = (
                    _rms(o, subg_ref[...]) * (1.0 - lambda_init)).astype(BF16)
            outs = []
            for kv in range(2):
                outs.append(_softmax_pv(_stack_group_queries(qb_ref[bi], kv), kb_ref, vb_ref, bi, 0, starts_ab,
                                        s_ref.at[n_chain % 2]))
                n_chain += 1
            o_ref[bi, :, QA:QA + QB] = _merge_kv_outputs(*outs).astype(BF16)
            outs = []
            for kv in range(2):
                outs.append(_softmax_pv(_stack_group_queries(qc_ref[bi], kv), kc_ref, vc_ref, bi, 0, starts_c,
                                        s_ref.at[n_chain % 2], mask_c, sink_col(kv)))
                n_chain += 1
            o_ref[bi, :, QA + QB:] = _merge_kv_outputs(*outs).astype(BF16)

    def ctx_step():
        mixers([0], [0], None)

    def latent_step():
        q_start = (t - 1) * TILE
        u0 = pl.multiple_of(jnp.minimum(q_start + ctx_len - WINDOW, t_len - band), WINDOW)
        q_pos = q_start + (lax.broadcasted_iota(jnp.int32, (2 * TILE, KEY_CHUNK), 0) & (TILE - 1))
        key_lane = lax.broadcasted_iota(jnp.int32, (2 * TILE, KEY_CHUNK), 1)

        def mask_c(j, st, s):
            if j == 0:
                return s
            k_pos = (st - ctx_len) + key_lane
            valid = (jnp.abs(k_pos - q_pos) <= WINDOW) & (k_pos >= 0)
            return jnp.where(valid, s, NEG_BIG)

        mixers(_all_chunks(t_len), [0] + [u0 + i * KEY_CHUNK for i in range(band // KEY_CHUNK)], mask_c)

    if t0 == 0:
        pl.when(t == 0)(ctx_step)
        pl.when(t > 0)(latent_step)
    else:
        latent_step()


def _attention(sink, lamv, subg, q, k, v, t0, lambda_init):
    batch, t_len, _ = q.shape
    nt = t_len // TILE
    kern = functools.partial(_attn_kernel, t0=t0, ctx_len=TILE, lambda_init=lambda_init)
    const = lambda b, t: (0, 0)
    nb = 1
    qspec = lambda width, blk: pl.BlockSpec((nb, TILE, width), lambda b, t: (b, t + t0, blk))
    kvspec = lambda width, blk: pl.BlockSpec((nb, t_len, width), lambda b, t: (b, 0, blk))
    return pl.pallas_call(
        kern,
        grid=(batch // nb, nt - t0),
        in_specs=[pl.BlockSpec(memory_space=pltpu.SMEM),
                  pl.BlockSpec(lamv.shape, const),
                  pl.BlockSpec((1, LANES), const),
                  qspec(QA, 0), qspec(QB, QA // QB), qspec(QC, (QA + QB) // QC),
                  kvspec(KA, 0), kvspec(KB, KA // KB), kvspec(KC, (KA + KB) // KC),
                  kvspec(KA, 0), kvspec(KB, KA // KB), kvspec(KC, (KA + KB) // KC)],
        out_specs=pl.BlockSpec((nb, TILE, Q_COLS), lambda b, t: (b, t + t0, 0)),
        out_shape=jax.ShapeDtypeStruct((batch, t_len, Q_COLS), BF16),
        scratch_shapes=[pltpu.VMEM((2, 2 * TILE, t_len), F32)],
        compiler_params=pltpu.CompilerParams(dimension_semantics=("parallel", "parallel"),
                                             vmem_limit_bytes=VMEM_LIMIT),
        name="attention",
    )(sink, lamv, subg, q, q, q, k, k, k, v, v, v)


def _route(scores, bias):
    expert = lax.broadcasted_iota(jnp.int32, scores.shape, 0)
    sel = scores + bias
    in_group = expert & (EXPERTS_PER_GROUP - 1)
    group = expert >> 2

    def neighbours(x, idx, step, span):
        for d in (1, 2, 3):
            fwd = (idx + d) < 4
            y = jnp.where(fwd, pltpu.roll(x, N_EXPERTS - d * step, 0), pltpu.roll(x, span - d * step, 0))
            yield y, fwd

    rank = jnp.zeros_like(sel)
    for y, fwd in neighbours(sel, in_group, 1, EXPERTS_PER_GROUP):
        beats = (y > sel) | ((y == sel) & jnp.logical_not(fwd))
        rank = rank + jnp.where(beats, 1.0, 0.0)
    top2 = jnp.where(rank < 2.0, sel, 0.0)
    gsum = top2
    for y, _ in neighbours(top2, in_group, 1, EXPERTS_PER_GROUP):
        gsum = gsum + y
    grank = jnp.zeros_like(sel)
    for y, fwd in neighbours(gsum, group, EXPERTS_PER_GROUP, N_EXPERTS):
        beats = (y > gsum) | ((y == gsum) & jnp.logical_not(fwd))
        grank = grank + jnp.where(beats, 1.0, 0.0)
    best = grank < 0.5
    chosen = best & (rank < 2.0)
    w = jnp.where(chosen, scores, 0.0)
    gates = w / jnp.sum(w, axis=0, keepdims=True)
    gid = jnp.sum(jnp.where(best & (in_group == 0), group.astype(F32), 0.0), axis=0, keepdims=True)
    bits = jnp.sum(jnp.where(chosen, jnp.left_shift(1, in_group).astype(F32), 0.0), axis=0, keepdims=True)
    pair = sum(jnp.where(bits == float((1 << a) + (1 << b)), float(k), 0.0) for k, (a, b) in enumerate(PAIRS))
    bucket = gid * len(PAIRS) + pair
    tokens = scores.shape[1]
    return jnp.concatenate([gates, jnp.broadcast_to(bucket, (8, tokens)),
                            jnp.zeros((LANES - N_EXPERTS - 8, tokens), F32)], axis=0)


def _outproj_kernel(*refs, n_stream, t0):
    stream_refs = refs[:n_stream]
    tile = pl.program_id(1) + t0
    mix_ref, w_ref, mod_ref, g_ref, wrh_ref, wrl_ref, rb_ref, x1_ref, h2_ref, gates_ref, bucket_ref = refs[n_stream:]
    for i in range(mix_ref.shape[0]):
        x1 = _stream_rows(stream_refs, i, tile) + mod_ref[i, 0, 2:3, :] * _dot(mix_ref[i], w_ref[...])
        x1_ref[i] = x1
        h2 = _rms(x1, g_ref[...]) * (1.0 + mod_ref[i, 0, 4:5, :]) + mod_ref[i, 0, 3:4, :]
        h2_ref[i] = _pack_bf16_pairs(h2)
        hi, lo = _split(h2)
        logits = _dot_nt(wrh_ref[...], hi) + _dot_nt(wrl_ref[...], hi) + _dot_nt(wrh_ref[...], lo)
        routed = _route(_sigmoid(logits), rb_ref[...])
        gates_ref[i] = routed.T
        bucket_ref[i] = routed[GROUP_LANE:GROUP_LANE + 8]


def _outproj(stream, mix, w, mod, g, wrh, wrl, rb, t0):
    batch, t_len, d = _stream_dims(stream)
    nt = t_len // TILE
    nb = 2 * PAIR
    t_out = (nt - t0) * TILE
    const = lambda b, t: (0, 0)
    tok_in = lambda b, t: (b, t + t0, 0)
    tok_out = lambda b, t: (b, t, 0)
    return pl.pallas_call(
        functools.partial(_outproj_kernel, n_stream=len(stream), t0=t0),
        grid=(batch // nb, nt - t0),
        in_specs=_stream_specs(stream, nb, lambda t: t + t0) + [
                  pl.BlockSpec((nb, TILE, Q_COLS), tok_in),
                  pl.BlockSpec(w.shape, const),
                  pl.BlockSpec((nb, 1, 8, d), lambda b, t: (b, jnp.minimum(t + t0, 1), 0, 0)),
                  pl.BlockSpec((1, d), const),
                  pl.BlockSpec((N_EXPERTS, d), const),
                  pl.BlockSpec((N_EXPERTS, d), const),
                  pl.BlockSpec((N_EXPERTS, 1), const)],
        out_specs=[pl.BlockSpec((nb, TILE, d), tok_out),
                   pl.BlockSpec((nb, TILE, d // 2), tok_out),
                   pl.BlockSpec((nb, TILE, LANES), tok_out),
                   pl.BlockSpec((nb, 8, TILE), tok_out)],
        out_shape=[jax.ShapeDtypeStruct((batch, t_out, d), F32),
                   jax.ShapeDtypeStruct((batch, t_out, d // 2), jnp.uint32),
                   jax.ShapeDtypeStruct((batch, t_out, LANES), F32),
                   jax.ShapeDtypeStruct((batch, (nt - t0) * 8, TILE), F32)],
        compiler_params=pltpu.CompilerParams(dimension_semantics=("parallel", "parallel"),
                                             vmem_limit_bytes=VMEM_LIMIT),
        name="outproj",
    )(*stream, mix, w, mod, g, wrh, wrl, rb)


def _sc_workers():
    info = plsc.get_sparse_core_info()
    return info.num_cores, info.num_subcores


def _sc_dispatch(hpk, gates, pos2d, n_out):
    n, w = hpk.shape
    gw = gates.shape[1]
    nc, ns = _sc_workers()
    steps = n // (nc * ns * SC_WIN)
    mesh = plsc.VectorSubcoreMesh(core_axis_name="c", subcore_axis_name="s")

    @functools.partial(
        pl.kernel, mesh=mesh,
        out_type=(jax.ShapeDtypeStruct((n_out, w), hpk.dtype), jax.ShapeDtypeStruct((n_out, gw), gates.dtype)),
        scratch_types=[pltpu.VMEM((1, SC_WIN), jnp.int32), pltpu.VMEM((SC_WIN, w), hpk.dtype),
                       pltpu.VMEM((SC_WIN, gw), gates.dtype), pltpu.SemaphoreType.DMA, pltpu.SemaphoreType.DMA],
        name="sc_dispatch")
    def k(h_hbm, g_hbm, pos_hbm, hs_hbm, gs_hbm, idx_v, rows_v, grow_v, sem_in, sem_out):
        wid = lax.axis_index("s") * nc + lax.axis_index("c")

        @pl.loop(0, steps)
        def _(j):
            blk = wid * steps + j
            off = pl.multiple_of(blk * SC_WIN, SC_WIN)
            loads = [pltpu.async_copy(pos_hbm.at[pl.ds(blk, 1)], idx_v, sem_in),
                     pltpu.async_copy(h_hbm.at[pl.ds(off, SC_WIN)], rows_v, sem_in),
                     pltpu.async_copy(g_hbm.at[pl.ds(off, SC_WIN)], grow_v, sem_in)]
            for cp in loads:
                cp.wait()
            stores = [pltpu.async_copy(rows_v, hs_hbm.at[idx_v.at[0]], sem_out),
                      pltpu.async_copy(grow_v, gs_hbm.at[idx_v.at[0]], sem_out)]
            for cp in stores:
                cp.wait()

    return k(hpk, gates, pos2d)


def _sc_combine(ys, pos2d, n):
    w = ys.shape[1]
    nc, ns = _sc_workers()
    steps = n // (nc * ns * SC_WIN)
    mesh = plsc.VectorSubcoreMesh(core_axis_name="c", subcore_axis_name="s")

    @functools.partial(
        pl.kernel, mesh=mesh,
        out_type=jax.ShapeDtypeStruct((n, w), ys.dtype),
        scratch_types=[pltpu.VMEM((1, SC_WIN), jnp.int32), pltpu.VMEM((SC_WIN, w), ys.dtype)],
        name="sc_combine")
    def k(y_hbm, pos_hbm, o_hbm, idx_v, rows_v):
        wid = lax.axis_index("s") * nc + lax.axis_index("c")

        @pl.loop(0, steps)
        def _(j):
            blk = wid * steps + j
            off = pl.multiple_of(blk * SC_WIN, SC_WIN)
            pltpu.sync_copy(pos_hbm.at[pl.ds(blk, 1)], idx_v)
            pltpu.sync_copy(y_hbm.at[idx_v.at[0]], rows_v)
            pltpu.sync_copy(rows_v, o_hbm.at[pl.ds(off, SC_WIN)])

    return k(ys, pos2d)


PLAN_ROWS = 256


def _plan_kernel(g_ref, pos_ref, tg_ref, tk_ref, slots_ref):
    _plan_finish(g_ref, pos_ref, tg_ref, tk_ref, slots_ref)


def _plan_finish(g_ref, pos_ref, tg_ref, tk_ref, slots_ref):
    g = g_ref[...]
    rows = g.shape[0]
    n_groups = N_EXPERTS // EXPERTS_PER_GROUP
    before_lane = (lax.broadcasted_iota(jnp.int32, (LANES, LANES), 0)
                   < lax.broadcasted_iota(jnp.int32, (LANES, LANES), 1)).astype(BF16)
    before_row = (lax.broadcasted_iota(jnp.int32, (rows, rows), 1)
                  < lax.broadcasted_iota(jnp.int32, (rows, rows), 0)).astype(BF16)
    tile_row = lax.broadcasted_iota(jnp.int32, (1, LANES), 1).astype(F32) * FFN_TILE
    expert = lax.broadcasted_iota(jnp.int32, (N_EXPERTS, 1), 0)
    pos = jnp.zeros_like(g)
    start = jnp.zeros((1, 1), F32)
    tile_group = jnp.zeros((1, LANES), F32)
    need = jnp.zeros((N_EXPERTS, LANES), F32)
    for grp in range(n_groups):
        for k, (ea, eb) in enumerate(PAIRS):
            m = jnp.where(g == grp * len(PAIRS) + k, 1.0, 0.0)
            in_row = _dot(m.astype(BF16), before_lane)
            row_total = jnp.sum(m, axis=1, keepdims=True)
            rows_before = _dot(before_row, jnp.broadcast_to(row_total, m.shape).astype(BF16))
            pos = pos + m * (start + in_row + rows_before)
            end = start + jnp.sum(row_total, axis=0, keepdims=True)
            holds = jnp.where((tile_row < end) & (tile_row + FFN_TILE > start) & (end > start), 1.0, 0.0)
            uses = jnp.where((expert == grp * EXPERTS_PER_GROUP + ea) | (expert == grp * EXPERTS_PER_GROUP + eb),
                             1.0, 0.0)
            need = need + uses * holds
            start = end
        start = jnp.floor((start + (FFN_TILE - 1)) * (1.0 / FFN_TILE)) * FFN_TILE
        if grp < n_groups - 1:
            tile_group = tile_group + jnp.where(tile_row >= start, 1.0, 0.0)
    pos_ref[...] = pos.astype(jnp.int32)
    tg_ref[...] = tile_group.astype(jnp.int32)
    count = jnp.zeros((1, LANES), F32)
    slots = [tile_group * EXPERTS_PER_GROUP] * EXPERTS_PER_GROUP
    for e in range(N_EXPERTS):
        used = need[e:e + 1] > 0.0
        slots = [jnp.where(used & (count == k), float(e), slot) for k, slot in enumerate(slots)]
        count = count + jnp.where(used, 1.0, 0.0)
    tk_ref[...] = count.astype(jnp.int32)
    slots_ref[...] = jnp.concatenate(slots + [jnp.zeros((8 - EXPERTS_PER_GROUP, LANES), F32)], axis=0).astype(jnp.int32)


def _bucket_plan(bucket, n, n_pad):
    g = bucket.reshape(bucket.shape[0], -1, 8, TILE)[:, :, 0, :].reshape(n // LANES, LANES)
    g = jnp.pad(g, ((0, PLAN_ROWS - n // LANES), (0, 0)), constant_values=-1.0)
    pos, tile_group, tile_k, slots = pl.pallas_call(
        _plan_kernel,
        out_shape=[jax.ShapeDtypeStruct((PLAN_ROWS, LANES), jnp.int32),
                   jax.ShapeDtypeStruct((1, LANES), jnp.int32),
                   jax.ShapeDtypeStruct((1, LANES), jnp.int32),
                   jax.ShapeDtypeStruct((8, LANES), jnp.int32)],
        name="plan",
    )(g)
    n_tiles = n_pad // FFN_TILE
    return (pos[:n // LANES].reshape(n // SC_WIN, SC_WIN), tile_group[0, :n_tiles], tile_k[0, :n_tiles],
            [slots[k, :n_tiles] for k in range(EXPERTS_PER_GROUP)])


def _ffn_kernel(tg_ref, tk_ref, s0_ref, s1_ref, s2_ref, s3_ref, hs_ref, gs_ref, wg_ref, wu_ref, wd_ref, ys_ref):
    i = pl.program_id(0)
    slot_refs = (s0_ref, s1_ref, s2_ref, s3_ref)

    def run_experts(n_experts):
        def body():
            lo, hi = _unpack_bf16_pairs(hs_ref[...])
            lo = lo.astype(BF16)
            hi = hi.astype(BF16)
            half = lo.shape[1]
            gs = gs_ref[...]
            lane = lax.broadcasted_iota(jnp.int32, gs.shape, 1)
            y = None
            for k in range(n_experts):
                e = slot_refs[k][i]
                local = e - tg_ref[i] * EXPERTS_PER_GROUP
                wg, wu, wd = wg_ref.at[local], wu_ref.at[local], wd_ref.at[local]
                a = _dot(lo, wg[:half, :]) + _dot(hi, wg[half:, :])
                u = _dot(lo, wu[:half, :]) + _dot(hi, wu[half:, :])
                gate = jnp.sum(jnp.where(lane == e, gs, 0.0), axis=-1, keepdims=True)
                part = _dot(((a * _sigmoid(a)) * u * gate).astype(BF16), wd[...])
                y = part if y is None else y + part
            ys_ref[...] = _pack_bf16_pairs(y)
        return body

    for n_experts in range(2, EXPERTS_PER_GROUP + 1):
        pl.when(tk_ref[i] == n_experts)(run_experts(n_experts))


def _ffn(tile_group, tile_k, slots, hs, gs, wg, wu, wd):
    n_pad, half = hs.shape
    grp = lambda i, tg, *_: (tg[i], 0, 0)
    row = lambda i, *_: (i, 0)
    wspec = lambda w: pl.BlockSpec((EXPERTS_PER_GROUP,) + w.shape[1:], grp)
    return pl.pallas_call(
        _ffn_kernel,
        grid_spec=pltpu.PrefetchScalarGridSpec(
            num_scalar_prefetch=2 + EXPERTS_PER_GROUP,
            grid=(n_pad // FFN_TILE,),
            in_specs=[pl.BlockSpec((FFN_TILE, half), row),
                      pl.BlockSpec((FFN_TILE, LANES), row),
                      wspec(wg), wspec(wu), wspec(wd)],
            out_specs=pl.BlockSpec((FFN_TILE, half), row)),
        out_shape=jax.ShapeDtypeStruct((n_pad, half), jnp.uint32),
        compiler_params=pltpu.CompilerParams(dimension_semantics=("parallel",), vmem_limit_bytes=VMEM_LIMIT),
        name="ffn",
    )(tile_group, tile_k, *slots, hs, gs, wg, wu, wd)


def _residual_kernel(x_ref, y_ref, mod_ref, fg_ref, o_ref, *, final_norm):
    for i in range(x_ref.shape[0]):
        lo, hi = _unpack_bf16_pairs(y_ref[i])
        y = x_ref[i] + mod_ref[i, 0, 5:6, :] * jnp.concatenate([lo, hi], axis=1)
        if final_norm:
            y = _rms(y, fg_ref[...])
        o_ref[i] = y


def _residual(x1, yg, mod, fg, t0, final_norm):
    batch, t_out, d = x1.shape
    nb = PAIR
    tok = lambda b, t: (b, t, 0)
    return pl.pallas_call(
        functools.partial(_residual_kernel, final_norm=final_norm),
        grid=(batch // nb, t_out // TILE),
        in_specs=[pl.BlockSpec((nb, TILE, d), tok),
                  pl.BlockSpec((nb, TILE, d // 2), tok),
                  pl.BlockSpec((nb, 1, 8, d), lambda b, t: (b, jnp.minimum(t + t0, 1), 0, 0)),
                  pl.BlockSpec((1, d), lambda b, t: (0, 0))],
        out_specs=pl.BlockSpec((nb, TILE, d), tok),
        out_shape=jax.ShapeDtypeStruct((batch, t_out, d), F32),
        compiler_params=pltpu.CompilerParams(dimension_semantics=("parallel", "parallel"),
                                             vmem_limit_bytes=VMEM_LIMIT),
        name="residual",
    )(x1, yg, mod, fg)


def _moe(h2, gates, bucket, wg, wu, wd):
    batch, t_out, half = h2.shape
    n = batch * t_out
    n_pad = n + (N_EXPERTS // EXPERTS_PER_GROUP) * FFN_TILE
    assert n % LANES == 0 and n // LANES <= PLAN_ROWS and n_pad // FFN_TILE <= LANES
    pos2d, tile_group, tile_k, slots = _bucket_plan(bucket, n, n_pad)
    hs, gs = _sc_dispatch(h2.reshape(n, half), gates.reshape(n, LANES), pos2d, n_pad)
    ys = _ffn(tile_group, tile_k, slots, hs, gs, wg, wu, wd)
    return _sc_combine(ys, pos2d, n).reshape(batch, t_out, half)


def _rope_tables(seq, ctx_len):
    rows = seq // GRID_W
    row_pos = jnp.repeat(jnp.arange(rows, dtype=F32), GRID_W)
    col_pos = jnp.tile(jnp.arange(GRID_W, dtype=F32), rows)
    axis_dim = HEAD_DIM // 2
    inv_freq = ROPE_BASE ** (-jnp.arange(0, axis_dim, 2, dtype=F32) / axis_dim)
    ang_r = row_pos[:, None] * inv_freq[None, :]
    ang_c = col_pos[:, None] * inv_freq[None, :]
    z = jnp.zeros_like(ang_r)
    cos = jnp.concatenate([jnp.cos(ang_r)] * 2 + [jnp.cos(ang_c)] * 2, axis=-1)
    sin_a = jnp.concatenate([-jnp.sin(ang_r), z, -jnp.sin(ang_c), z], axis=-1)
    sin_b = jnp.concatenate([z, jnp.sin(ang_r), z, jnp.sin(ang_c)], axis=-1)

    def full(tab, fill):
        tab = jnp.tile(tab, (1, LANES // HEAD_DIM))
        return jnp.concatenate([jnp.full((ctx_len, LANES), fill, F32), tab], axis=0)

    return full(cos, 1.0), full(sin_a, 0.0), full(sin_b, 0.0)


def kernel(x, c, ctx, c_ctx, w_ada, b_ada, norm1_g, norm2_g, w_in, w_out, lam_q1, lam_k1, lam_q2, lam_k2,
           subln_g, q_norm_g, k_norm_g, sink, w_router, router_bias, w_gate, w_up, w_down, final_g):
    batch, seq, d = x.shape
    ctx_len = ctx.shape[1]
    depth = w_in.shape[0]
    sc_rows = SC_WIN * math.prod(_sc_workers())
    assert ctx_len == TILE and seq % TILE == 0 and batch % PAIR == 0 and batch <= 15
    assert (batch * seq) % sc_rows == 0 and (batch * (seq + ctx_len)) % sc_rows == 0

    cpad = jnp.zeros((16, d), F32).at[:batch].set(c).at[batch].set(c_ctx)
    mod_all, w_in_next = _adaln(cpad, w_ada, b_ada, w_in)
    cos, sin_a, sin_b = _rope_tables(seq, ctx_len)
    mseg = jnp.asarray(np.kron(np.eye(2), np.ones((HEAD_DIM, HEAD_DIM))), BF16)
    wr = w_router.T
    wrh = wr.astype(BF16)
    wrl = (wr - wrh.astype(F32)).astype(BF16)
    rb = router_bias.reshape(N_EXPERTS, 1)

    stream = (ctx, x)

    for l in range(depth):
        last = l == depth - 1
        lambda_init = 0.8 - 0.6 * math.exp(-0.3 * l)
        m6 = mod_all[l].reshape(16, 6, d)
        m8 = jnp.concatenate([m6, jnp.zeros((16, 2, d), F32)], axis=1)
        mod = jnp.stack([jnp.broadcast_to(m8[batch], (batch, 8, d)), m8[:batch]], axis=1)

        w_in_p = w_in_next
        dense = [(w_out, l)] + ([] if last else [(w_in, l + 1)])
        gq = jnp.tile(q_norm_g[l], 2).reshape(1, LANES)
        gk = jnp.tile(k_norm_g[l], 2).reshape(1, LANES)
        q, k, v, *cast = _inproj(stream, mod, norm1_g[l].reshape(1, d), w_in_p, cos, sin_a, sin_b, gq, gk, mseg,
                                 (w_gate, w_up, w_down), l, dense)
        experts, w_out_p = cast[:3], cast[3]
        w_in_next = None if last else cast[4]

        t0 = 1 if last else 0
        lamv = jnp.stack([lam_q1[l], lam_k1[l], lam_q2[l], lam_k2[l]])
        mix = _attention(sink[l], lamv, subln_g[l].reshape(1, LANES), q, k, v, t0, lambda_init)
        x1, h2, gates, bucket = _outproj(stream, mix, w_out_p, mod, norm2_g[l].reshape(1, d), wrh, wrl, rb, t0)
        stream = (x1, _moe(h2, gates, bucket, *experts), mod)
    return _residual(*stream, final_g.reshape(1, d), 1, final_norm=True)
```
